```python
import jax, jax.numpy as jnp
from jax import lax
import numpy as np

D_MODEL = 1024
BATCH = 8
SEQ = 4096
DEPTH = 2

CHUNK = 64
MIX_WIDTH = D_MODEL
GMLP_WIDTH = MIX_WIDTH // 2
GMLP_GROUPS = 4
GMLP_GROUP_CH = GMLP_WIDTH // GMLP_GROUPS
GMLP_BLOCK = 128
FOX_WIDTH = MIX_WIDTH - GMLP_WIDTH
FOX_HEAD_DIM = 64
FOX_HEADS = FOX_WIDTH // FOX_HEAD_DIM
FOX_Q_BLOCK = 128
IN_COLS = 2 * GMLP_WIDTH + 3 * FOX_WIDTH + FOX_HEADS
PLE_DIM = 256
N_EXPERTS = 32
TOP_K = 4
D_EXPERT = D_MODEL
SWIGLU_LIMIT = 7.0
SWIGLU_ALPHA = 1.702
EXPERT_ROWS = 256
RMS_EPS = 1e-5
LN_EPS = 1e-5

kernel_name = "hybrid_gmlp_fox_moe_streaming_encoder"


def rms_norm(x, g):
    xf = x.astype(jnp.float32)
    y = xf * lax.rsqrt(jnp.mean(xf * xf, axis=-1, keepdims=True) + RMS_EPS)
    return (y * g.astype(jnp.float32)).astype(x.dtype)


def layer_norm(x, g, b):
    xf = x.astype(jnp.float32)
    mu = jnp.mean(xf, axis=-1, keepdims=True)
    var = jnp.mean(jnp.square(xf - mu), axis=-1, keepdims=True)
    y = (xf - mu) * lax.rsqrt(var + LN_EPS)
    return (y * g.astype(jnp.float32) + b.astype(jnp.float32)).astype(x.dtype)


def spatial_gating(z_a, ln_g, ln_b, w_s, b_s):
    B, S, _ = z_a.shape
    z = jax.nn.gelu(z_a, approximate=False)
    u, v = jnp.split(z, 2, axis=-1)
    v = layer_norm(v, ln_g, ln_b)
    v = v.reshape(B, S // GMLP_BLOCK, GMLP_BLOCK, GMLP_GROUPS, GMLP_GROUP_CH)
    pos = jnp.arange(GMLP_BLOCK)
    allowed = (pos[None, :] // CHUNK) <= (pos[:, None] // CHUNK)
    ws = jnp.where(allowed[None], w_s, jnp.zeros_like(w_s))
    sv = jnp.einsum('gij,bnjgc->bnigc', ws, v) + b_s.T[None, None, :, :, None]
    return u * sv.reshape(B, S, GMLP_WIDTH)


def forgetting_attention(q, k, v, f_logit, b_f):
    B, S, _ = q.shape
    def heads(t):
        return t.reshape(B, S, FOX_HEADS, FOX_HEAD_DIM).transpose(0, 2, 1, 3)
    q, k, v = heads(q), heads(k), heads(v)
    log_f = jax.nn.log_sigmoid((f_logit + b_f).astype(jnp.float32))
    c = jnp.cumsum(log_f, axis=1).transpose(0, 2, 1)
    scale = FOX_HEAD_DIM ** -0.5
    neg = jnp.finfo(jnp.float32).min
    outs = []
    for qb in range(S // FOX_Q_BLOCK):
        s0 = qb * FOX_Q_BLOCK
        end = s0 + FOX_Q_BLOCK
        logits = jnp.einsum('bhqd,bhkd->bhqk', q[:, :, s0:end], k[:, :, :end]).astype(jnp.float32) * scale
        logits = logits + c[:, :, s0:end, None] - c[:, :, None, :end]
        causal = jnp.arange(end)[None, :] <= (s0 + jnp.arange(FOX_Q_BLOCK))[:, None]
        logits = jnp.where(causal, logits, neg)
        w = jax.nn.softmax(logits, axis=-1).astype(v.dtype)
        outs.append(jnp.einsum('bhqk,bhkd->bhqd', w, v[:, :, :end]))
    o = jnp.concatenate(outs, axis=2)
    return o.transpose(0, 2, 1, 3).reshape(B, S, FOX_WIDTH)


def moe_ffn(xn, w_router, b_router, w_gu, b_gu, w_dn, b_dn):
    B, S, D = xn.shape
    T = B * S
    xt = xn.reshape(T, D)
    logits = (xt @ w_router + b_router).astype(jnp.float32)
    top_v, top_i = lax.top_k(logits, TOP_K)
    gates = jax.nn.softmax(top_v, axis=-1)
    A = T * TOP_K
    flat_e = top_i.reshape(A).astype(jnp.int32)
    flat_tok = jnp.arange(A, dtype=jnp.int32) // TOP_K
    order = jnp.argsort(flat_e, stable=True)
    sorted_e = flat_e[order]
    counts = jnp.bincount(flat_e, length=N_EXPERTS).astype(jnp.int32)
    starts = jnp.cumsum(counts) - counts
    padded = (counts + EXPERT_ROWS - 1) // EXPERT_ROWS * EXPERT_ROWS
    pad_ends = jnp.cumsum(padded)
    pad_starts = pad_ends - padded
    rank = jnp.arange(A, dtype=jnp.int32) - starts[sorted_e]
    dest_sorted = pad_starts[sorted_e] + rank
    n_blocks = -(-A // EXPERT_ROWS) + N_EXPERTS
    P = n_blocks * EXPERT_ROWS
    row_tok = jnp.full((P,), T, dtype=jnp.int32).at[dest_sorted].set(flat_tok[order])
    x_pad = jnp.concatenate([xt, jnp.zeros((1, D), xt.dtype)], axis=0)
    xs = x_pad[row_tok].reshape(n_blocks, EXPERT_ROWS, D)
    block_start = jnp.arange(n_blocks, dtype=jnp.int32) * EXPERT_ROWS
    block_e = jnp.minimum(jnp.searchsorted(pad_ends, block_start, side='right'), N_EXPERTS - 1).astype(jnp.int32)

    def expert_block(args):
        xb, e = args
        h = xb @ w_gu[e] + b_gu[e]
        g, lin = jnp.split(h, 2, axis=-1)
        g = jnp.minimum(g, SWIGLU_LIMIT)
        lin = jnp.clip(lin, -SWIGLU_LIMIT, SWIGLU_LIMIT)
        act = g * jax.nn.sigmoid(SWIGLU_ALPHA * g) * (lin + 1.0)
        return act @ w_dn[e] + b_dn[e]

    ys = lax.map(expert_block, (xs, block_e)).reshape(P, D)
    dest = jnp.zeros((A,), jnp.int32).at[order].set(dest_sorted)
    y = jnp.sum(ys[dest].reshape(T, TOP_K, D) * gates[..., None].astype(ys.dtype), axis=1)
    return y.reshape(B, S, D)


def setup_inputs(seed: int = 0) -> dict:
    key = jax.random.key(seed)
    ks = jax.random.split(key, 24)
    f32 = jnp.float32
    def nrm(k, shape, scale):
        return jax.random.normal(k, shape, f32) * scale
    L, D, E, F = DEPTH, D_MODEL, N_EXPERTS, D_EXPERT
    return {
        "x": nrm(ks[0], (BATCH, SEQ, D), 1.0),
        "p": nrm(ks[1], (DEPTH, BATCH, SEQ, PLE_DIM), 1.0),
        "g_mix": 1.0 + nrm(ks[2], (L, D), 0.05),
        "w_in": nrm(ks[3], (L, D, IN_COLS), D ** -0.5),
        "ln_g": 1.0 + nrm(ks[4], (L, GMLP_WIDTH), 0.05),
        "ln_b": nrm(ks[5], (L, GMLP_WIDTH), 0.02),
        "w_s": nrm(ks[6], (L, GMLP_GROUPS, GMLP_BLOCK, GMLP_BLOCK), GMLP_BLOCK ** -0.5),
        "b_s": 1.0 + nrm(ks[7], (L, GMLP_GROUPS, GMLP_BLOCK), 0.1),
        "b_f": 4.0 + nrm(ks[8], (L, FOX_HEADS), 0.5),
        "g_a": 1.0 + nrm(ks[9], (L, GMLP_WIDTH), 0.05),
        "g_b": 1.0 + nrm(ks[10], (L, FOX_WIDTH), 0.05),
        "w_out": nrm(ks[11], (L, MIX_WIDTH, D), MIX_WIDTH ** -0.5),
        "g_moe": 1.0 + nrm(ks[12], (L, D), 0.05),
        "w_router": nrm(ks[13], (L, D, E), D ** -0.5),
        "b_router": nrm(ks[14], (L, E), 0.01),
        "w_gu": nrm(ks[15], (L, E, D, 2 * F), D ** -0.5),
        "b_gu": nrm(ks[16], (L, E, 2 * F), 0.02),
        "w_dn": nrm(ks[17], (L, E, F, D), F ** -0.5),
        "b_dn": nrm(ks[18], (L, E, D), 0.02),
        "g_ple": 1.0 + nrm(ks[19], (L, D), 0.05),
        "w_ple_gate": nrm(ks[20], (L, D, D), D ** -0.5),
        "w_ple": nrm(ks[21], (L, PLE_DIM, D), PLE_DIM ** -0.5),
        "g_final": 1.0 + nrm(ks[22], (D,), 0.05),
    }


def reference(x, p, g_mix, w_in, ln_g, ln_b, w_s, b_s, b_f, g_a, g_b, w_out,
              g_moe, w_router, b_router, w_gu, b_gu, w_dn, b_dn,
              g_ple, w_ple_gate, w_ple, g_final):
    h = x
    c1 = 2 * GMLP_WIDTH
    c2 = c1 + FOX_WIDTH
    c3 = c2 + FOX_WIDTH
    c4 = c3 + FOX_WIDTH
    for i in range(DEPTH):
        xn = rms_norm(h, g_mix[i])
        z = xn @ w_in[i]
        y_a = spatial_gating(z[..., :c1], ln_g[i], ln_b[i], w_s[i], b_s[i])
        y_b = forgetting_attention(z[..., c1:c2], z[..., c2:c3], z[..., c3:c4], z[..., c4:], b_f[i])
        y = jnp.concatenate([rms_norm(y_a, g_a[i]), rms_norm(y_b, g_b[i])], axis=-1) @ w_out[i]
        h = h + y
        h = h + moe_ffn(rms_norm(h, g_moe[i]), w_router[i], b_router[i], w_gu[i], b_gu[i], w_dn[i], b_dn[i])
        gate = jax.nn.sigmoid(rms_norm(h, g_ple[i]) @ w_ple_gate[i])
        h = h + gate * (p[i] @ w_ple[i])
    return rms_norm(h, g_final)
```

```python
import functools

import jax
import jax.numpy as jnp
from jax import lax
from jax.experimental import pallas as pl
from jax.experimental.pallas import tpu as pltpu
from jax.experimental.pallas import tpu_sc as plsc

F32 = jnp.float32
BF16 = jnp.bfloat16
I32 = jnp.int32
U32 = jnp.uint32

CHUNK = 64
GMLP_WIDTH = 512
GMLP_GROUPS = 4
GMLP_GROUP_CH = 128
GMLP_BLOCK = 128
FOX_WIDTH = 512
FOX_HEAD_DIM = 64
FOX_HEADS = 8
N_EXPERTS = 32
TOP_K = 4
EXPERT_ROWS = 256
SWIGLU_LIMIT = 7.0
SWIGLU_ALPHA = 1.702
RMS_EPS = 1e-5
LN_EPS = 1e-5

HEAD_PAD = 128
BIAS_PARTS = 3
ROW_TILE = 512
ATTN_TILE = 512
SC_WINDOW = 128
SC_ROW_WORDS = 256
VMEM_LIMIT = 56 * 1024 * 1024
NEG_BIG = -1e30

_NT = (((1,), (1,)), ((), ()))
_TN = (((0,), (0,)), ((), ()))


def _rms(x, g):
    return x * lax.rsqrt(jnp.mean(x * x, axis=-1, keepdims=True) + RMS_EPS) * g


def _split3(c):
    a1 = c.astype(BF16).astype(F32)
    r1 = c - a1
    a2 = r1.astype(BF16).astype(F32)
    a3 = (r1 - a2).astype(BF16).astype(F32)
    return a1, a2, a3


def _pack_rows(x):
    n = x.shape[1] // 2
    lo = lax.bitcast_convert_type(x[:, :n].astype(BF16).astype(F32), U32) >> 16
    hi = lax.bitcast_convert_type(x[:, n:].astype(BF16).astype(F32), U32) & jnp.uint32(0xFFFF0000)
    return lo | hi


def _unpack_rows(u):
    lo = lax.bitcast_convert_type(u << 16, F32)
    hi = lax.bitcast_convert_type(u & jnp.uint32(0xFFFF0000), F32)
    return jnp.concatenate([lo, hi], axis=1)


def _mix_in_kernel(h_ref, gmix_ref, wuv_ref, wqt_ref, wk_ref, wvt_ref, wft_ref, bf_ref, lng_ref, lnb_ref,
                   ws_ref, bs_ref, ga_ref, tri_ref, selk_ref, onesk_ref,
                   ya_ref, qt_ref, k_ref, vt_ref, carry_ref, ya_scr):
    tm = h_ref.shape[0]

    @pl.when(pl.program_id(1) == 0)
    def _():
        carry_ref[...] = jnp.zeros_like(carry_ref)

    xn = _rms(h_ref[...], gmix_ref[...]).astype(BF16)

    z = jnp.dot(xn, wuv_ref[...], preferred_element_type=F32)
    z = 0.5 * z * (1.0 + lax.erf(z * (0.5 ** 0.5)))
    u = z[:, :GMLP_WIDTH]
    v = z[:, GMLP_WIDTH:]
    mu = jnp.mean(v, axis=-1, keepdims=True)
    vc = v - mu
    var = jnp.mean(vc * vc, axis=-1, keepdims=True)
    vn = (vc * lax.rsqrt(var + LN_EPS) * lng_ref[...] + lnb_ref[...]).astype(BF16)
    pi = lax.broadcasted_iota(I32, (GMLP_BLOCK, GMLP_BLOCK), 0) // CHUNK
    pj = lax.broadcasted_iota(I32, (GMLP_BLOCK, GMLP_BLOCK), 1) // CHUNK
    for g in range(GMLP_GROUPS):
        wm = jnp.where(pj <= pi, ws_ref[g], 0.0).astype(BF16)
        cs = slice(g * GMLP_GROUP_CH, (g + 1) * GMLP_GROUP_CH)
        for n in range(tm // GMLP_BLOCK):
            rs = slice(n * GMLP_BLOCK, (n + 1) * GMLP_BLOCK)
            sv = jnp.dot(wm, vn[rs, cs], preferred_element_type=F32) + bs_ref[g]
            ya_scr[rs, cs] = u[rs, cs] * sv
    ya_ref[...] = _rms(ya_scr[...], ga_ref[...]).astype(BF16)

    qt = lax.dot_general(wqt_ref[...], xn, _NT, preferred_element_type=F32)
    vt_ref[0] = lax.dot_general(wvt_ref[...], xn, _NT, preferred_element_type=F32).astype(BF16)
    kp = jnp.dot(xn, wk_ref[...], preferred_element_type=F32)

    ft = lax.dot_general(wft_ref[...], xn, _NT, preferred_element_type=F32) + bf_ref[...]
    lf = jnp.minimum(ft, 0.0) - jnp.log1p(jnp.exp(-jnp.abs(ft)))
    zpad = jnp.zeros((32 - BIAS_PARTS * FOX_HEADS, tm), F32)
    stack = jnp.concatenate(list(_split3(lf)) + [zpad], axis=0).astype(BF16)
    cum = jnp.dot(stack, tri_ref[...], preferred_element_type=F32)
    c = cum[0:8] + cum[8:16] + cum[16:24] + carry_ref[:, 0:1]
    carry_ref[...] = jnp.broadcast_to(c[:, tm - 1:tm], carry_ref.shape)
    a1, a2, a3 = _split3(c)

    si = lax.broadcasted_iota(I32, (8, tm), 0)
    zrows = jnp.zeros((HEAD_PAD - FOX_HEAD_DIM - 8, tm), F32)
    for hd in range(FOX_HEADS):
        b1 = jnp.broadcast_to(a1[hd:hd + 1], (8, tm))
        b2 = jnp.broadcast_to(a2[hd:hd + 1], (8, tm))
        b3 = jnp.broadcast_to(a3[hd:hd + 1], (8, tm))
        ext = jnp.where(si == 0, b1, jnp.where(si == 1, b2, jnp.where(si == 2, b3,
                        jnp.where(si < 2 * BIAS_PARTS, 1.0, 0.0))))
        blk = jnp.concatenate([qt[hd * FOX_HEAD_DIM:(hd + 1) * FOX_HEAD_DIM], ext, zrows], axis=0)
        qt_ref[0, hd * HEAD_PAD:(hd + 1) * HEAD_PAD, :] = blk.astype(BF16)

    cstack = jnp.concatenate([a1, a2, a3, zpad], axis=0).astype(BF16)
    kext = lax.dot_general(cstack, selk_ref[...], _TN, preferred_element_type=F32)
    k_ref[...] = (kp + kext + onesk_ref[...]).astype(BF16)


def _mix_in(h, B, S, prm):
    T, D = h.shape
    tm = min(ROW_TILE, S)
    ns = S // tm
    H = FOX_HEADS
    const = lambda shape: pl.BlockSpec(shape, lambda b, s: (0,) * len(shape))
    return pl.pallas_call(
        _mix_in_kernel,
        grid=(B, ns),
        in_specs=[
            pl.BlockSpec((tm, D), lambda b, s: (b * ns + s, 0)),
            const((1, D)),
            const((D, 2 * GMLP_WIDTH)),
            const((FOX_WIDTH, D)),
            const((D, H * HEAD_PAD)),
            const((FOX_WIDTH, D)),
            const((H, D)),
            const((H, 1)),
            const((1, GMLP_WIDTH)),
            const((1, GMLP_WIDTH)),
            const((GMLP_GROUPS, GMLP_BLOCK, GMLP_BLOCK)),
            const((GMLP_GROUPS, GMLP_BLOCK, 1)),
            const((1, GMLP_WIDTH)),
            const((tm, tm)),
            const((32, H * HEAD_PAD)),
            const((1, H * HEAD_PAD)),
        ],
        out_specs=[
            pl.BlockSpec((tm, GMLP_WIDTH), lambda b, s: (b * ns + s, 0)),
            pl.BlockSpec((1, H * HEAD_PAD, tm), lambda b, s: (b, 0, s)),
            pl.BlockSpec((tm, H * HEAD_PAD), lambda b, s: (b * ns + s, 0)),
            pl.BlockSpec((1, FOX_WIDTH, tm), lambda b, s: (b, 0, s)),
        ],
        out_shape=[
            jax.ShapeDtypeStruct((T, GMLP_WIDTH), BF16),
            jax.ShapeDtypeStruct((B, H * HEAD_PAD, S), BF16),
            jax.ShapeDtypeStruct((T, H * HEAD_PAD), BF16),
            jax.ShapeDtypeStruct((B, FOX_WIDTH, S), BF16),
        ],
        scratch_shapes=[pltpu.VMEM((H, 128), F32), pltpu.VMEM((tm, GMLP_WIDTH), F32)],
        compiler_params=pltpu.CompilerParams(
            dimension_semantics=("arbitrary", "arbitrary"), vmem_limit_bytes=VMEM_LIMIT),
        name="mix_in",
    )(h, prm["g_mix"], prm["wuv"], prm["wqt"], prm["wk"], prm["wvt"], prm["wft"], prm["b_f"],
      prm["ln_g"], prm["ln_b"], prm["w_s"], prm["b_s"], prm["g_a"], prm["tri_incl"], prm["selk"], prm["onesk"])


def _fox_attn_kernel(qt_ref, k_ref, vt_ref, ot_ref):
    S = k_ref.shape[0]
    t = min(ATTN_TILE, S)
    nq = S // t

    def block(q, ks, carry, masked):
        m, l, acc = carry
        k = k_ref[pl.ds(ks, t), :]
        s = jnp.dot(k, q, preferred_element_type=F32)
        if masked:
            kpos = lax.broadcasted_iota(I32, (t, t), 0)
            qpos = lax.broadcasted_iota(I32, (t, t), 1)
            s = jnp.where(kpos <= qpos, s, NEG_BIG)
        m_new = jnp.maximum(m, jnp.max(s, axis=0, keepdims=True))
        alpha = jnp.exp(m - m_new)
        p = jnp.exp(s - m_new)
        l = alpha * l + jnp.sum(p, axis=0, keepdims=True)
        v = vt_ref[0, :, pl.ds(ks, t)]
        acc = alpha * acc + jnp.dot(v, p.astype(BF16), preferred_element_type=F32)
        return m_new, l, acc

    for qi in range(nq):
        q = qt_ref[0, :, qi * t:(qi + 1) * t]
        carry = (jnp.full((1, t), NEG_BIG, F32), jnp.zeros((1, t), F32), jnp.zeros((FOX_HEAD_DIM, t), F32))
        if qi > 0:
            carry = lax.fori_loop(
                0, qi, lambda j, c: block(q, pl.multiple_of(j * t, t), c, False), carry)
        m, l, acc = block(q, qi * t, carry, True)
        ot_ref[0, :, qi * t:(qi + 1) * t] = (acc / l).astype(BF16)


def _fox_attn(qt, kx, vt, B, S):
    H = FOX_HEADS
    return pl.pallas_call(
        _fox_attn_kernel,
        grid=(B, H),
        in_specs=[
            pl.BlockSpec((1, HEAD_PAD, S), lambda b, h: (b, h, 0)),
            pl.BlockSpec((S, HEAD_PAD), lambda b, h: (b, h)),
            pl.BlockSpec((1, FOX_HEAD_DIM, S), lambda b, h: (b, h, 0)),
        ],
        out_specs=pl.BlockSpec((1, FOX_HEAD_DIM, S), lambda b, h: (b, h, 0)),
        out_shape=jax.ShapeDtypeStruct((B, FOX_WIDTH, S), BF16),
        compiler_params=pltpu.CompilerParams(
            dimension_semantics=("parallel", "parallel"), vmem_limit_bytes=VMEM_LIMIT),
        name="fox_attn",
    )(qt, kx, vt)


def _out_router_kernel(h_ref, ya_ref, ybt_ref, woa_ref, wob_ref, gb_ref, gmoe_ref, wrt_ref, br_ref, tri_ref,
                       h1_ref, xp_ref, tope_ref, rank_ref, gates_ref, counts_ref, carry_ref):
    tm = h_ref.shape[0]

    @pl.when(pl.program_id(0) == 0)
    def _():
        carry_ref[...] = jnp.zeros_like(carry_ref)

    yb = ybt_ref[0].astype(F32)
    ybn = (yb * lax.rsqrt(jnp.mean(yb * yb, axis=0, keepdims=True) + RMS_EPS) * gb_ref[...]).astype(BF16)
    y = jnp.dot(ya_ref[...], woa_ref[...], preferred_element_type=F32)
    y = y + lax.dot_general(ybn, wob_ref[...], _TN, preferred_element_type=F32)
    h1 = h_ref[...] + y
    h1_ref[...] = h1
    xn = _rms(h1, gmoe_ref[...])
    xp_ref[...] = _pack_rows(xn)

    logits = lax.dot_general(wrt_ref[...], xn.astype(BF16), _NT, preferred_element_type=F32) + br_ref[...]
    ri = lax.broadcasted_iota(I32, (N_EXPERTS, tm), 0)
    vals, idxs = [], []
    l = logits
    for _ in range(TOP_K):
        m = jnp.max(l, axis=0, keepdims=True)
        idx = jnp.min(jnp.where(l == m, ri, N_EXPERTS), axis=0, keepdims=True)
        vals.append(m)
        idxs.append(idx)
        l = jnp.where(ri == idx, -jnp.inf, l)
    es = [jnp.exp(vk - vals[0]) for vk in vals]
    den = es[0] + es[1] + es[2] + es[3]
    zrow = jnp.zeros((8 - TOP_K, tm), F32)
    gates_ref[...] = jnp.concatenate([e / den for e in es] + [zrow], axis=0).T

    sel = jnp.zeros((N_EXPERTS, tm), F32)
    for idx in idxs:
        sel = sel + jnp.where(ri == idx, 1.0, 0.0)
    cnt = jnp.dot(sel.astype(BF16), tri_ref[...], preferred_element_type=F32) + carry_ref[:, 0:1]
    ranks = [jnp.sum(jnp.where(ri == idx, cnt, 0.0), axis=0, keepdims=True) for idx in idxs]
    zi = jnp.zeros((8 - TOP_K, tm), I32)
    rank_ref[...] = jnp.concatenate([r.astype(I32) for r in ranks] + [zi], axis=0)
    tope_ref[...] = jnp.concatenate(idxs + [zi], axis=0)
    total = carry_ref[...] + jnp.sum(sel, axis=1, keepdims=True)
    carry_ref[...] = total
    counts_ref[...] = total.astype(I32)


def _out_router(h, ya, ybt, B, S, prm, tri_strict):
    T, D = h.shape
    tm = min(ROW_TILE, S)
    ns = S // tm
    E = N_EXPERTS
    const = lambda shape: pl.BlockSpec(shape, lambda i: (0,) * len(shape))
    return pl.pallas_call(
        _out_router_kernel,
        grid=(T // tm,),
        in_specs=[
            pl.BlockSpec((tm, D), lambda i: (i, 0)),
            pl.BlockSpec((tm, GMLP_WIDTH), lambda i: (i, 0)),
            pl.BlockSpec((1, FOX_WIDTH, tm), lambda i: (i // ns, 0, i % ns)),
            const((GMLP_WIDTH, D)),
            const((FOX_WIDTH, D)),
            const((FOX_WIDTH, 1)),
            const((1, D)),
            const((E, D)),
            const((E, 1)),
            const((tm, tm)),
        ],
        out_specs=[
            pl.BlockSpec((tm, D), lambda i: (i, 0)),
            pl.BlockSpec((tm, D // 2), lambda i: (i, 0)),
            pl.BlockSpec((8, tm), lambda i: (0, i)),
            pl.BlockSpec((8, tm), lambda i: (0, i)),
            pl.BlockSpec((tm, 8), lambda i: (i, 0)),
            pl.BlockSpec((E, 128), lambda i: (0, 0)),
        ],
        out_shape=[
            jax.ShapeDtypeStruct((T, D), F32),
            jax.ShapeDtypeStruct((T, D // 2), U32),
            jax.ShapeDtypeStruct((8, T), I32),
            jax.ShapeDtypeStruct((8, T), I32),
            jax.ShapeDtypeStruct((T, 8), F32),
            jax.ShapeDtypeStruct((E, 128), I32),
        ],
        scratch_shapes=[pltpu.VMEM((E, 128), F32)],
        compiler_params=pltpu.CompilerParams(
            dimension_semantics=("arbitrary",), vmem_limit_bytes=VMEM_LIMIT),
        name="out_router",
    )(h, ya, ybt, prm["woa"], prm["wob"], prm["g_b"], prm["g_moe"], prm["wrt"], prm["b_router"], tri_strict)


def _dest_kernel(ps_ref, tope_ref, rank_ref, dest_ref):
    e_sel = tope_ref[...]
    d = rank_ref[...]
    for e in range(N_EXPERTS):
        d = d + jnp.where(e_sel == e, ps_ref[e], 0)
    dest_ref[...] = d


def _dest_rows(pad_starts, tope, rank):
    T = tope.shape[1]
    tc = min(4096, T)
    return pl.pallas_call(
        _dest_kernel,
        grid_spec=pltpu.PrefetchScalarGridSpec(
            num_scalar_prefetch=1,
            grid=(T // tc,),
            in_specs=[pl.BlockSpec((8, tc), lambda i, ps: (0, i)),
                      pl.BlockSpec((8, tc), lambda i, ps: (0, i))],
            out_specs=pl.BlockSpec((8, tc), lambda i, ps: (0, i)),
        ),
        out_shape=jax.ShapeDtypeStruct((8, T), I32),
        name="dest_rows",
    )(pad_starts, tope, rank)


def _sc_mesh():
    return plsc.VectorSubcoreMesh(core_axis_name="c", subcore_axis_name="s")


def _sc_scatter_rows(x, idx, n_out):
    R, W = x.shape
    N = idx.shape[0]
    nsrc = R // SC_WINDOW

    @functools.partial(pl.kernel, out_type=jax.ShapeDtypeStruct((n_out, W), x.dtype), mesh=_sc_mesh())
    def k(x_hbm, i_hbm, o_hbm):
        def body(x_vmem, i_vmem):
            pltpu.sync_copy(x_vmem, o_hbm.at[i_vmem.at[0]])

        pltpu.emit_pipeline(
            body,
            grid=(N // SC_WINDOW,),
            in_specs=[pl.BlockSpec((SC_WINDOW, W), lambda i: (i % nsrc, 0)),
                      pl.BlockSpec((1, SC_WINDOW), lambda i: (0, i))],
            out_specs=[],
            core_axis_name=("c", "s"),
            dimension_semantics=(pltpu.PARALLEL,),
        )(x_hbm, i_hbm)

    return k(x, idx.reshape(1, N))


def _sc_gather_rows(table, idx):
    V, W = table.shape
    N = idx.shape[0]

    @functools.partial(pl.kernel, out_type=jax.ShapeDtypeStruct((N, W), table.dtype), mesh=_sc_mesh())
    def k(t_hbm, i_hbm, o_hbm):
        def body(i_vmem, o_vmem):
            pltpu.sync_copy(t_hbm.at[i_vmem.at[0]], o_vmem)

        pltpu.emit_pipeline(
            body,
            grid=(N // SC_WINDOW,),
            in_specs=[pl.BlockSpec((1, SC_WINDOW), lambda i: (0, i))],
            out_specs=[pl.BlockSpec((SC_WINDOW, W), lambda i: (i, 0))],
            core_axis_name=("c", "s"),
            dimension_semantics=(pltpu.PARALLEL,),
        )(i_hbm, o_hbm)

    return k(table, idx.reshape(1, N))


def _expert_kernel(be_ref, nv_ref, xs_ref, wgu_ref, bgu_ref, wdn_ref, bdn_ref, ys_ref):
    nv = nv_ref[pl.program_id(0)]
    F = wdn_ref.shape[1]

    @pl.when(nv > 0)
    def _():
        x = _unpack_rows(xs_ref[...])
        rows = lax.broadcasted_iota(I32, x.shape, 0)
        x = jnp.where(rows < nv, x, 0.0).astype(BF16)
        hgu = jnp.dot(x, wgu_ref[0], preferred_element_type=F32) + bgu_ref[0]
        g = jnp.minimum(hgu[:, :F], SWIGLU_LIMIT)
        lin = jnp.clip(hgu[:, F:], -SWIGLU_LIMIT, SWIGLU_LIMIT)
        act = g * jax.nn.sigmoid(SWIGLU_ALPHA * g) * (lin + 1.0)
        y = jnp.dot(act.astype(BF16), wdn_ref[0], preferred_element_type=F32) + bdn_ref[0]
        ys_ref[...] = _pack_rows(y)

    @pl.when(nv == 0)
    def _():
        ys_ref[...] = jnp.zeros_like(ys_ref)


def _experts(block_e, block_nv, xs, prm):
    P, Dh = xs.shape
    D = 2 * Dh
    nb = P // EXPERT_ROWS
    F = prm["wdn"].shape[1]
    return pl.pallas_call(
        _expert_kernel,
        grid_spec=pltpu.PrefetchScalarGridSpec(
            num_scalar_prefetch=2,
            grid=(nb,),
            in_specs=[
                pl.BlockSpec((EXPERT_ROWS, Dh), lambda i, be, nv: (i, 0)),
                pl.BlockSpec((1, D, 2 * F), lambda i, be, nv: (be[i], 0, 0)),
                pl.BlockSpec((1, 1, 2 * F), lambda i, be, nv: (be[i], 0, 0)),
                pl.BlockSpec((1, F, D), lambda i, be, nv: (be[i], 0, 0)),
                pl.BlockSpec((1, 1, D), lambda i, be, nv: (be[i], 0, 0)),
            ],
            out_specs=pl.BlockSpec((EXPERT_ROWS, Dh), lambda i, be, nv: (i, 0)),
        ),
        out_shape=jax.ShapeDtypeStruct((P, Dh), U32),
        compiler_params=pltpu.CompilerParams(
            dimension_semantics=("arbitrary",), vmem_limit_bytes=VMEM_LIMIT),
        name="experts",
    )(block_e, block_nv, xs, prm["wgu"], prm["b_gu"], prm["wdn"], prm["b_dn"])


def _combine_ple_kernel(h1_ref, yg_ref, gates_ref, gple_ref, wpg_ref, p_ref, wple_ref, gfin_ref, o_ref, *, final):
    h2 = h1_ref[...]
    gates = gates_ref[...]
    for k in range(TOP_K):
        h2 = h2 + gates[:, k:k + 1] * _unpack_rows(yg_ref[k])
    xn = _rms(h2, gple_ref[...]).astype(BF16)
    gate = jax.nn.sigmoid(jnp.dot(xn, wpg_ref[...], preferred_element_type=F32))
    pe = jnp.dot(p_ref[...].astype(BF16), wple_ref[...], preferred_element_type=F32)
    h3 = h2 + gate * pe
    if final:
        h3 = _rms(h3, gfin_ref[...])
    o_ref[...] = h3


def _combine_ple(h1, yg, gates, p, prm, g_final, final):
    T, D = h1.shape
    tm = min(ROW_TILE, T)
    PD = p.shape[1]
    const = lambda shape: pl.BlockSpec(shape, lambda i: (0,) * len(shape))
    return pl.pallas_call(
        functools.partial(_combine_ple_kernel, final=final),
        grid=(T // tm,),
        in_specs=[
            pl.BlockSpec((tm, D), lambda i: (i, 0)),
            pl.BlockSpec((TOP_K, tm, D // 2), lambda i: (0, i, 0)),
            pl.BlockSpec((tm, 8), lambda i: (i, 0)),
            const((1, D)),
            const((D, D)),
            pl.BlockSpec((tm, PD), lambda i: (i, 0)),
            const((PD, D)),
            const((1, D)),
        ],
        out_specs=pl.BlockSpec((tm, D), lambda i: (i, 0)),
        out_shape=jax.ShapeDtypeStruct((T, D), F32),
        compiler_params=pltpu.CompilerParams(
            dimension_semantics=("parallel",), vmem_limit_bytes=VMEM_LIMIT),
        name="combine_ple",
    )(h1, yg, gates, prm["g_ple"], prm["wpg"], p, prm["wple"], g_final)


def _layer_params(i, w):
    D = w["w_in"].shape[1]
    H = FOX_HEADS
    w_in = w["w_in"][i]
    c1 = 2 * GMLP_WIDTH
    c2, c3, c4 = c1 + FOX_WIDTH, c1 + 2 * FOX_WIDTH, c1 + 3 * FOX_WIDTH
    wk = jnp.zeros((D, H, HEAD_PAD), F32).at[:, :, :FOX_HEAD_DIM].set(
        w_in[:, c2:c3].reshape(D, H, FOX_HEAD_DIM)).reshape(D, H * HEAD_PAD)
    return {
        "g_mix": w["g_mix"][i].reshape(1, D),
        "wuv": w_in[:, :c1].astype(BF16),
        "wqt": (w_in[:, c1:c2] * (FOX_HEAD_DIM ** -0.5)).T.astype(BF16),
        "wk": wk.astype(BF16),
        "wvt": w_in[:, c3:c4].T.astype(BF16),
        "wft": w_in[:, c4:].T.astype(BF16),
        "b_f": w["b_f"][i].reshape(H, 1),
        "ln_g": w["ln_g"][i].reshape(1, GMLP_WIDTH),
        "ln_b": w["ln_b"][i].reshape(1, GMLP_WIDTH),
        "w_s": w["w_s"][i],
        "b_s": w["b_s"][i].reshape(GMLP_GROUPS, GMLP_BLOCK, 1),
        "g_a": w["g_a"][i].reshape(1, GMLP_WIDTH),
        "woa": w["w_out"][i][:GMLP_WIDTH].astype(BF16),
        "wob": w["w_out"][i][GMLP_WIDTH:].astype(BF16),
        "g_b": w["g_b"][i].reshape(FOX_WIDTH, 1),
        "g_moe": w["g_moe"][i].reshape(1, D),
        "wrt": w["w_router"][i].T.astype(BF16),
        "b_router": w["b_router"][i].reshape(N_EXPERTS, 1),
        "wgu": w["w_gu"][i].astype(BF16),
        "b_gu": w["b_gu"][i].reshape(N_EXPERTS, 1, -1),
        "wdn": w["w_dn"][i].astype(BF16),
        "b_dn": w["b_dn"][i].reshape(N_EXPERTS, 1, -1),
        "g_ple": w["g_ple"][i].reshape(1, D),
        "wpg": w["w_ple_gate"][i].astype(BF16),
        "wple": w["w_ple"][i].astype(BF16),
    }


def _bias_constants(tm):
    H = FOX_HEADS
    j = jnp.arange(tm)
    tri_incl = (j[:, None] <= j[None, :]).astype(BF16)
    tri_strict = (j[:, None] < j[None, :]).astype(BF16)
    part = jnp.arange(BIAS_PARTS)[:, None]
    head = jnp.arange(H)[None, :]
    rows = (part * H + head).reshape(-1)
    cols = (head * HEAD_PAD + FOX_HEAD_DIM + BIAS_PARTS + part).reshape(-1)
    selk = jnp.zeros((32, H * HEAD_PAD), F32).at[rows, cols].set(-1.0).astype(BF16)
    one_cols = (jnp.arange(H)[:, None] * HEAD_PAD + FOX_HEAD_DIM + jnp.arange(BIAS_PARTS)[None, :]).reshape(-1)
    onesk = jnp.zeros((1, H * HEAD_PAD), F32).at[0, one_cols].set(1.0)
    return tri_incl, tri_strict, selk, onesk


def _moe_dispatch_plan(counts, n_blocks):
    padded = (counts + EXPERT_ROWS - 1) // EXPERT_ROWS * EXPERT_ROWS
    pad_ends = jnp.cumsum(padded)
    pad_starts = (pad_ends - padded).astype(I32)
    block_start = jnp.arange(n_blocks, dtype=I32) * EXPERT_ROWS
    block_e = jnp.minimum(jnp.searchsorted(pad_ends, block_start, side="right"), N_EXPERTS - 1).astype(I32)
    seg_end = pad_starts[block_e] + counts[block_e]
    block_nv = jnp.clip(seg_end - block_start, 0, EXPERT_ROWS)
    block_nv = jnp.where(block_start < pad_ends[-1], block_nv, 0).astype(I32)
    return pad_starts, block_e, block_nv


def kernel(x, p, g_mix, w_in, ln_g, ln_b, w_s, b_s, b_f, g_a, g_b, w_out, g_moe, w_router, b_router,
           w_gu, b_gu, w_dn, b_dn, g_ple, w_ple_gate, w_ple, g_final):
    B, S, D = x.shape
    T = B * S
    depth = w_in.shape[0]
    w = dict(g_mix=g_mix, w_in=w_in, ln_g=ln_g, ln_b=ln_b, w_s=w_s, b_s=b_s, b_f=b_f, g_a=g_a, g_b=g_b,
             w_out=w_out, g_moe=g_moe, w_router=w_router, b_router=b_router, w_gu=w_gu, b_gu=b_gu,
             w_dn=w_dn, b_dn=b_dn, g_ple=g_ple, w_ple_gate=w_ple_gate, w_ple=w_ple)
    tri_incl, tri_strict, selk, onesk = _bias_constants(min(ROW_TILE, S))
    n_blocks = -(-(T * TOP_K) // EXPERT_ROWS) + N_EXPERTS
    P = n_blocks * EXPERT_ROWS
    halves = (D // 2) // SC_ROW_WORDS
    sub = jnp.arange(halves, dtype=I32)

    h = x.reshape(T, D)
    for i in range(depth):
        prm = _layer_params(i, w)
        prm.update(tri_incl=tri_incl, selk=selk, onesk=onesk)
        ya, qt, kx, vt = _mix_in(h, B, S, prm)
        ybt = _fox_attn(qt, kx, vt, B, S)
        h1, xp, tope, rank, gates, counts = _out_router(h, ya, ybt, B, S, prm, tri_strict)

        pad_starts, block_e, block_nv = _moe_dispatch_plan(counts[:, 0], n_blocks)
        dest = _dest_rows(pad_starts, tope, rank)[:TOP_K]
        idx = (dest[:, :, None] * halves + sub).reshape(-1)
        xs = _sc_scatter_rows(xp.reshape(T * halves, SC_ROW_WORDS), idx, P * halves)
        ys = _experts(block_e, block_nv, xs.reshape(P, D // 2), prm)
        yg = _sc_gather_rows(ys.reshape(P * halves, SC_ROW_WORDS), idx)
        h = _combine_ple(h1, yg.reshape(TOP_K, T, D // 2), gates, p[i].reshape(T, -1), prm,
                         g_final.reshape(1, D), final=(i == depth - 1))
    return h.reshape(B, S, D)
```

```python
import functools

import jax
import jax.numpy as jnp
from jax import lax
from jax.experimental import pallas as pl
from jax.experimental.pallas import tpu as pltpu
from jax.experimental.pallas import tpu_sc as plsc

F32 = jnp.float32
BF16 = jnp.bfloat16
I32 = jnp.int32
U32 = jnp.uint32

CHUNK = 64
GMLP_WIDTH = 512
GMLP_GROUPS = 4
GMLP_GROUP_CH = 128
GMLP_BLOCK = 128
FOX_WIDTH = 512
FOX_HEAD_DIM = 64
FOX_HEADS = 8
N_EXPERTS = 32
TOP_K = 4
EXPERT_ROWS = 256
SWIGLU_LIMIT = 7.0
SWIGLU_ALPHA = 1.702
RMS_EPS = 1e-5
LN_EPS = 1e-5

HEAD_PAD = 128
BIAS_PARTS = 3
ROW_TILE = 512
ATTN_TILE = 512
SC_WINDOW = 128
SC_ROW_WORDS = 256
VMEM_LIMIT = 56 * 1024 * 1024
NEG_BIG = -1e30

_NT = (((1,), (1,)), ((), ()))
_TN = (((0,), (0,)), ((), ()))


def _rms(x, g):
    return x * lax.rsqrt(jnp.mean(x * x, axis=-1, keepdims=True) + RMS_EPS) * g


def _split3(c):
    a1 = c.astype(BF16).astype(F32)
    r1 = c - a1
    a2 = r1.astype(BF16).astype(F32)
    a3 = (r1 - a2).astype(BF16).astype(F32)
    return a1, a2, a3


def _pack_rows(x):
    n = x.shape[1] // 2
    lo = lax.bitcast_convert_type(x[:, :n].astype(BF16).astype(F32), U32) >> 16
    hi = lax.bitcast_convert_type(x[:, n:].astype(BF16).astype(F32), U32) & jnp.uint32(0xFFFF0000)
    return lo | hi


def _unpack_rows(u):
    lo = lax.bitcast_convert_type(u << 16, F32)
    hi = lax.bitcast_convert_type(u & jnp.uint32(0xFFFF0000), F32)
    return jnp.concatenate([lo, hi], axis=1)


def _store_planes(ref, u):
    for j in range(ref.shape[0]):
        ref[j] = u[:, j * SC_ROW_WORDS:(j + 1) * SC_ROW_WORDS]


def _load_planes(ref):
    return jnp.concatenate([ref[j] for j in range(ref.shape[0])], axis=1)


def _mix_in_kernel(h_ref, gmix_ref, wuv_ref, wqt_ref, wk_ref, wvt_ref, wft_ref, bf_ref, lng_ref, lnb_ref,
                   ws_ref, bs_ref, ga_ref, tri_ref, selk_ref, onesk_ref,
                   ya_ref, qt_ref, k_ref, vt_ref, carry_ref, ya_scr):
    tm = h_ref.shape[0]

    @pl.when(pl.program_id(1) == 0)
    def _():
        carry_ref[...] = jnp.zeros_like(carry_ref)

    xn = _rms(h_ref[...], gmix_ref[...]).astype(BF16)

    z = jnp.dot(xn, wuv_ref[...], preferred_element_type=F32)
    z = 0.5 * z * (1.0 + lax.erf(z * (0.5 ** 0.5)))
    u = z[:, :GMLP_WIDTH]
    v = z[:, GMLP_WIDTH:]
    mu = jnp.mean(v, axis=-1, keepdims=True)
    vc = v - mu
    var = jnp.mean(vc * vc, axis=-1, keepdims=True)
    vn = (vc * lax.rsqrt(var + LN_EPS) * lng_ref[...] + lnb_ref[...]).astype(BF16)
    pi = lax.broadcasted_iota(I32, (GMLP_BLOCK, GMLP_BLOCK), 0) // CHUNK
    pj = lax.broadcasted_iota(I32, (GMLP_BLOCK, GMLP_BLOCK), 1) // CHUNK
    for g in range(GMLP_GROUPS):
        wm = jnp.where(pj <= pi, ws_ref[g], 0.0).astype(BF16)
        cs = slice(g * GMLP_GROUP_CH, (g + 1) * GMLP_GROUP_CH)
        for n in range(tm // GMLP_BLOCK):
            rs = slice(n * GMLP_BLOCK, (n + 1) * GMLP_BLOCK)
            sv = jnp.dot(wm, vn[rs, cs], preferred_element_type=F32) + bs_ref[g]
            ya_scr[rs, cs] = u[rs, cs] * sv
    ya_ref[...] = _rms(ya_scr[...], ga_ref[...]).astype(BF16)

    qt = lax.dot_general(wqt_ref[...], xn, _NT, preferred_element_type=F32)
    vt_ref[0] = lax.dot_general(wvt_ref[...], xn, _NT, preferred_element_type=F32).astype(BF16)
    kp = jnp.dot(xn, wk_ref[...], preferred_element_type=F32)

    ft = lax.dot_general(wft_ref[...], xn, _NT, preferred_element_type=F32) + bf_ref[...]
    lf = jnp.minimum(ft, 0.0) - jnp.log1p(jnp.exp(-jnp.abs(ft)))
    zpad = jnp.zeros((32 - BIAS_PARTS * FOX_HEADS, tm), F32)
    stack = jnp.concatenate(list(_split3(lf)) + [zpad], axis=0).astype(BF16)
    cum = jnp.dot(stack, tri_ref[...], preferred_element_type=F32)
    c = cum[0:8] + cum[8:16] + cum[16:24] + carry_ref[:, 0:1]
    carry_ref[...] = jnp.broadcast_to(c[:, tm - 1:tm], carry_ref.shape)
    a1, a2, a3 = _split3(c)

    si = lax.broadcasted_iota(I32, (8, tm), 0)
    zrows = jnp.zeros((HEAD_PAD - FOX_HEAD_DIM - 8, tm), F32)
    for hd in range(FOX_HEADS):
        b1 = jnp.broadcast_to(a1[hd:hd + 1], (8, tm))
        b2 = jnp.broadcast_to(a2[hd:hd + 1], (8, tm))
        b3 = jnp.broadcast_to(a3[hd:hd + 1], (8, tm))
        ext = jnp.where(si == 0, b1, jnp.where(si == 1, b2, jnp.where(si == 2, b3,
                        jnp.where(si < 2 * BIAS_PARTS, 1.0, 0.0))))
        blk = jnp.concatenate([qt[hd * FOX_HEAD_DIM:(hd + 1) * FOX_HEAD_DIM], ext, zrows], axis=0)
        qt_ref[0, hd * HEAD_PAD:(hd + 1) * HEAD_PAD, :] = blk.astype(BF16)

    cstack = jnp.concatenate([a1, a2, a3, zpad], axis=0).astype(BF16)
    kext = lax.dot_general(cstack, selk_ref[...], _TN, preferred_element_type=F32)
    k_ref[...] = (kp + kext + onesk_ref[...]).astype(BF16)


def _mix_in(h, B, S, prm):
    T, D = h.shape
    tm = min(ROW_TILE, S)
    ns = S // tm
    H = FOX_HEADS
    const = lambda shape: pl.BlockSpec(shape, lambda b, s: (0,) * len(shape))
    return pl.pallas_call(
        _mix_in_kernel,
        grid=(B, ns),
        in_specs=[
            pl.BlockSpec((tm, D), lambda b, s: (b * ns + s, 0)),
            const((1, D)),
            const((D, 2 * GMLP_WIDTH)),
            const((FOX_WIDTH, D)),
            const((D, H * HEAD_PAD)),
            const((FOX_WIDTH, D)),
            const((H, D)),
            const((H, 1)),
            const((1, GMLP_WIDTH)),
            const((1, GMLP_WIDTH)),
            const((GMLP_GROUPS, GMLP_BLOCK, GMLP_BLOCK)),
            const((GMLP_GROUPS, GMLP_BLOCK, 1)),
            const((1, GMLP_WIDTH)),
            const((tm, tm)),
            const((32, H * HEAD_PAD)),
            const((1, H * HEAD_PAD)),
        ],
        out_specs=[
            pl.BlockSpec((tm, GMLP_WIDTH), lambda b, s: (b * ns + s, 0)),
            pl.BlockSpec((1, H * HEAD_PAD, tm), lambda b, s: (b, 0, s)),
            pl.BlockSpec((tm, H * HEAD_PAD), lambda b, s: (b * ns + s, 0)),
            pl.BlockSpec((1, FOX_WIDTH, tm), lambda b, s: (b, 0, s)),
        ],
        out_shape=[
            jax.ShapeDtypeStruct((T, GMLP_WIDTH), BF16),
            jax.ShapeDtypeStruct((B, H * HEAD_PAD, S), BF16),
            jax.ShapeDtypeStruct((T, H * HEAD_PAD), BF16),
            jax.ShapeDtypeStruct((B, FOX_WIDTH, S), BF16),
        ],
        scratch_shapes=[pltpu.VMEM((H, 128), F32), pltpu.VMEM((tm, GMLP_WIDTH), F32)],
        compiler_params=pltpu.CompilerParams(
            dimension_semantics=("arbitrary", "arbitrary"), vmem_limit_bytes=VMEM_LIMIT),
        name="mix_in",
    )(h, prm["g_mix"], prm["wuv"], prm["wqt"], prm["wk"], prm["wvt"], prm["wft"], prm["b_f"],
      prm["ln_g"], prm["ln_b"], prm["w_s"], prm["b_s"], prm["g_a"], prm["tri_incl"], prm["selk"], prm["onesk"])


def _fox_attn_kernel(qt_ref, k_ref, vt_ref, ot_ref):
    S = k_ref.shape[0]
    t = min(ATTN_TILE, S)
    nq = S // t

    def block(q, ks, carry, masked):
        m, l, acc = carry
        k = k_ref[pl.ds(ks, t), :]
        s = jnp.dot(k, q, preferred_element_type=F32)
        if masked:
            kpos = lax.broadcasted_iota(I32, (t, t), 0)
            qpos = lax.broadcasted_iota(I32, (t, t), 1)
            s = jnp.where(kpos <= qpos, s, NEG_BIG)
        m_new = jnp.maximum(m, jnp.max(s, axis=0, keepdims=True))
        alpha = jnp.exp(m - m_new)
        p = jnp.exp(s - m_new)
        l = alpha * l + jnp.sum(p, axis=0, keepdims=True)
        v = vt_ref[0, :, pl.ds(ks, t)]
        acc = alpha * acc + jnp.dot(v, p.astype(BF16), preferred_element_type=F32)
        return m_new, l, acc

    for qi in range(nq):
        q = qt_ref[0, :, qi * t:(qi + 1) * t]
        carry = (jnp.full((1, t), NEG_BIG, F32), jnp.zeros((1, t), F32), jnp.zeros((FOX_HEAD_DIM, t), F32))
        if qi > 0:
            carry = lax.fori_loop(
                0, qi, lambda j, c: block(q, pl.multiple_of(j * t, t), c, False), carry)
        m, l, acc = block(q, qi * t, carry, True)
        ot_ref[0, :, qi * t:(qi + 1) * t] = (acc / l).astype(BF16)


def _fox_attn(qt, kx, vt, B, S):
    H = FOX_HEADS
    return pl.pallas_call(
        _fox_attn_kernel,
        grid=(B, H),
        in_specs=[
            pl.BlockSpec((1, HEAD_PAD, S), lambda b, h: (b, h, 0)),
            pl.BlockSpec((S, HEAD_PAD), lambda b, h: (b, h)),
            pl.BlockSpec((1, FOX_HEAD_DIM, S), lambda b, h: (b, h, 0)),
        ],
        out_specs=pl.BlockSpec((1, FOX_HEAD_DIM, S), lambda b, h: (b, h, 0)),
        out_shape=jax.ShapeDtypeStruct((B, FOX_WIDTH, S), BF16),
        compiler_params=pltpu.CompilerParams(
            dimension_semantics=("parallel", "parallel"), vmem_limit_bytes=VMEM_LIMIT),
        name="fox_attn",
    )(qt, kx, vt)


def _out_router_kernel(h_ref, ya_ref, ybt_ref, woa_ref, wob_ref, gb_ref, gmoe_ref, wrt_ref, br_ref, tri_ref,
                       h1_ref, xp_ref, tope_ref, rank_ref, gates_ref, counts_ref, carry_ref):
    tm = h_ref.shape[0]

    @pl.when(pl.program_id(0) == 0)
    def _():
        carry_ref[...] = jnp.zeros_like(carry_ref)

    yb = ybt_ref[0].astype(F32)
    ybn = (yb * lax.rsqrt(jnp.mean(yb * yb, axis=0, keepdims=True) + RMS_EPS) * gb_ref[...]).astype(BF16)
    y = jnp.dot(ya_ref[...], woa_ref[...], preferred_element_type=F32)
    y = y + lax.dot_general(ybn, wob_ref[...], _TN, preferred_element_type=F32)
    h1 = h_ref[...] + y
    h1_ref[...] = h1
    xn = _rms(h1, gmoe_ref[...])
    _store_planes(xp_ref, _pack_rows(xn))

    logits = lax.dot_general(wrt_ref[...], xn.astype(BF16), _NT, preferred_element_type=F32) + br_ref[...]
    ri = lax.broadcasted_iota(I32, (N_EXPERTS, tm), 0)
    vals, idxs = [], []
    l = logits
    for _ in range(TOP_K):
        m = jnp.max(l, axis=0, keepdims=True)
        idx = jnp.min(jnp.where(l == m, ri, N_EXPERTS), axis=0, keepdims=True)
        vals.append(m)
        idxs.append(idx)
        l = jnp.where(ri == idx, -jnp.inf, l)
    es = [jnp.exp(vk - vals[0]) for vk in vals]
    den = es[0] + es[1] + es[2] + es[3]
    zrow = jnp.zeros((8 - TOP_K, tm), F32)
    gates_ref[...] = jnp.concatenate([e / den for e in es] + [zrow], axis=0).T

    sel = jnp.zeros((N_EXPERTS, tm), F32)
    for idx in idxs:
        sel = sel + jnp.where(ri == idx, 1.0, 0.0)
    cnt = jnp.dot(sel.astype(BF16), tri_ref[...], preferred_element_type=F32) + carry_ref[:, 0:1]
    ranks = [jnp.sum(jnp.where(ri == idx, cnt, 0.0), axis=0, keepdims=True) for idx in idxs]
    zi = jnp.zeros((8 - TOP_K, tm), I32)
    rank_ref[...] = jnp.concatenate([r.astype(I32) for r in ranks] + [zi], axis=0)
    tope_ref[...] = jnp.concatenate(idxs + [zi], axis=0)
    total = carry_ref[...] + jnp.sum(sel, axis=1, keepdims=True)
    carry_ref[...] = total
    counts_ref[...] = total.astype(I32)


def _out_router(h, ya, ybt, B, S, prm, tri_strict):
    T, D = h.shape
    tm = min(ROW_TILE, S)
    ns = S // tm
    E = N_EXPERTS
    planes = (D // 2) // SC_ROW_WORDS
    const = lambda shape: pl.BlockSpec(shape, lambda i: (0,) * len(shape))
    return pl.pallas_call(
        _out_router_kernel,
        grid=(T // tm,),
        in_specs=[
            pl.BlockSpec((tm, D), lambda i: (i, 0)),
            pl.BlockSpec((tm, GMLP_WIDTH), lambda i: (i, 0)),
            pl.BlockSpec((1, FOX_WIDTH, tm), lambda i: (i // ns, 0, i % ns)),
            const((GMLP_WIDTH, D)),
            const((FOX_WIDTH, D)),
            const((FOX_WIDTH, 1)),
            const((1, D)),
            const((E, D)),
            const((E, 1)),
            const((tm, tm)),
        ],
        out_specs=[
            pl.BlockSpec((tm, D), lambda i: (i, 0)),
            pl.BlockSpec((planes, tm, SC_ROW_WORDS), lambda i: (0, i, 0)),
            pl.BlockSpec((8, tm), lambda i: (0, i)),
            pl.BlockSpec((8, tm), lambda i: (0, i)),
            pl.BlockSpec((tm, 8), lambda i: (i, 0)),
            pl.BlockSpec((E, 128), lambda i: (0, 0)),
        ],
        out_shape=[
            jax.ShapeDtypeStruct((T, D), F32),
            jax.ShapeDtypeStruct((planes, T, SC_ROW_WORDS), U32),
            jax.ShapeDtypeStruct((8, T), I32),
            jax.ShapeDtypeStruct((8, T), I32),
            jax.ShapeDtypeStruct((T, 8), F32),
            jax.ShapeDtypeStruct((E, 128), I32),
        ],
        scratch_shapes=[pltpu.VMEM((E, 128), F32)],
        compiler_params=pltpu.CompilerParams(
            dimension_semantics=("arbitrary",), vmem_limit_bytes=VMEM_LIMIT),
        name="out_router",
    )(h, ya, ybt, prm["woa"], prm["wob"], prm["g_b"], prm["g_moe"], prm["wrt"], prm["b_router"], tri_strict)


def _dest_kernel(ps_ref, tope_ref, rank_ref, dest_ref):
    e_sel = tope_ref[...]
    d = rank_ref[...]
    for e in range(N_EXPERTS):
        d = d + jnp.where(e_sel == e, ps_ref[e], 0)
    dest_ref[...] = d


def _dest_rows(pad_starts, tope, rank):
    T = tope.shape[1]
    tc = min(4096, T)
    return pl.pallas_call(
        _dest_kernel,
        grid_spec=pltpu.PrefetchScalarGridSpec(
            num_scalar_prefetch=1,
            grid=(T // tc,),
            in_specs=[pl.BlockSpec((8, tc), lambda i, ps: (0, i)),
                      pl.BlockSpec((8, tc), lambda i, ps: (0, i))],
            out_specs=pl.BlockSpec((8, tc), lambda i, ps: (0, i)),
        ),
        out_shape=jax.ShapeDtypeStruct((8, T), I32),
        name="dest_rows",
    )(pad_starts, tope, rank)


def _sc_mesh():
    return plsc.VectorSubcoreMesh(core_axis_name="c", subcore_axis_name="s")


def _sc_scatter_rows(x, idx, n_out):
    R, W = x.shape
    N = idx.shape[0]
    nsrc = R // SC_WINDOW

    @functools.partial(pl.kernel, out_type=jax.ShapeDtypeStruct((n_out, W), x.dtype), mesh=_sc_mesh())
    def k(x_hbm, i_hbm, o_hbm):
        def body(x_vmem, i_vmem):
            pltpu.sync_copy(x_vmem, o_hbm.at[i_vmem.at[0]])

        pltpu.emit_pipeline(
            body,
            grid=(N // SC_WINDOW,),
            in_specs=[pl.BlockSpec((SC_WINDOW, W), lambda i: (i % nsrc, 0)),
                      pl.BlockSpec((1, SC_WINDOW), lambda i: (0, i))],
            out_specs=[],
            core_axis_name=("c", "s"),
            dimension_semantics=(pltpu.PARALLEL,),
        )(x_hbm, i_hbm)

    return k(x, idx.reshape(1, N))


def _sc_gather_rows(table, idx):
    V, W = table.shape
    N = idx.shape[0]

    @functools.partial(pl.kernel, out_type=jax.ShapeDtypeStruct((N, W), table.dtype), mesh=_sc_mesh())
    def k(t_hbm, i_hbm, o_hbm):
        def body(i_vmem, o_vmem):
            pltpu.sync_copy(t_hbm.at[i_vmem.at[0]], o_vmem)

        pltpu.emit_pipeline(
            body,
            grid=(N // SC_WINDOW,),
            in_specs=[pl.BlockSpec((1, SC_WINDOW), lambda i: (0, i))],
            out_specs=[pl.BlockSpec((SC_WINDOW, W), lambda i: (i, 0))],
            core_axis_name=("c", "s"),
            dimension_semantics=(pltpu.PARALLEL,),
        )(i_hbm, o_hbm)

    return k(table, idx.reshape(1, N))


def _expert_kernel(be_ref, nv_ref, xs_ref, wgu_ref, bgu_ref, wdn_ref, bdn_ref, ys_ref, wgu_bf, wdn_bf):
    i = pl.program_id(0)
    nv = nv_ref[i]
    F = wdn_ref.shape[0]

    @pl.when((i == 0) | (be_ref[i] != be_ref[jnp.maximum(i - 1, 0)]))
    def _():
        wgu_bf[...] = wgu_ref[...].astype(BF16)
        wdn_bf[...] = wdn_ref[...].astype(BF16)

    @pl.when(nv > 0)
    def _():
        x = _unpack_rows(_load_planes(xs_ref))
        rows = lax.broadcasted_iota(I32, x.shape, 0)
        x = jnp.where(rows < nv, x, 0.0).astype(BF16)
        hgu = jnp.dot(x, wgu_bf[...], preferred_element_type=F32) + bgu_ref[...]
        g = jnp.minimum(hgu[:, :F], SWIGLU_LIMIT)
        lin = jnp.clip(hgu[:, F:], -SWIGLU_LIMIT, SWIGLU_LIMIT)
        act = g * jax.nn.sigmoid(SWIGLU_ALPHA * g) * (lin + 1.0)
        y = jnp.dot(act.astype(BF16), wdn_bf[...], preferred_element_type=F32) + bdn_ref[...]
        _store_planes(ys_ref, _pack_rows(y))

    @pl.when(nv == 0)
    def _():
        ys_ref[...] = jnp.zeros_like(ys_ref)


def _experts(layer, block_e, block_nv, xs, w_gu, b_gu, w_dn, b_dn):
    planes, P, _ = xs.shape
    _, E, D, F2 = w_gu.shape
    F = F2 // 2
    nb = P // EXPERT_ROWS
    return pl.pallas_call(
        _expert_kernel,
        grid_spec=pltpu.PrefetchScalarGridSpec(
            num_scalar_prefetch=2,
            grid=(nb,),
            in_specs=[
                pl.BlockSpec((planes, EXPERT_ROWS, SC_ROW_WORDS), lambda i, be, nv: (0, i, 0)),
                pl.BlockSpec((None, None, D, F2), lambda i, be, nv: (layer, be[i], 0, 0)),
                pl.BlockSpec((None, None, 1, F2), lambda i, be, nv: (layer, be[i], 0, 0)),
                pl.BlockSpec((None, None, F, D), lambda i, be, nv: (layer, be[i], 0, 0)),
                pl.BlockSpec((None, None, 1, D), lambda i, be, nv: (layer, be[i], 0, 0)),
            ],
            out_specs=pl.BlockSpec((planes, EXPERT_ROWS, SC_ROW_WORDS), lambda i, be, nv: (0, i, 0)),
            scratch_shapes=[pltpu.VMEM((D, F2), BF16), pltpu.VMEM((F, D), BF16)],
        ),
        out_shape=jax.ShapeDtypeStruct((planes, P, SC_ROW_WORDS), U32),
        compiler_params=pltpu.CompilerParams(
            dimension_semantics=("arbitrary",), vmem_limit_bytes=VMEM_LIMIT),
        name="experts",
    )(block_e, block_nv, xs, w_gu, b_gu.reshape(-1, E, 1, F2), w_dn, b_dn.reshape(-1, E, 1, D))


def _combine_ple_kernel(h1_ref, yg_ref, gates_ref, gple_ref, wpg_ref, p_ref, wple_ref, gfin_ref, o_ref, *, final):
    h2 = h1_ref[...]
    gates = gates_ref[...]
    for k in range(TOP_K):
        h2 = h2 + gates[:, k:k + 1] * _unpack_rows(_load_planes(yg_ref.at[k]))
    xn = _rms(h2, gple_ref[...]).astype(BF16)
    gate = jax.nn.sigmoid(jnp.dot(xn, wpg_ref[...], preferred_element_type=F32))
    pe = jnp.dot(p_ref[...].astype(BF16), wple_ref[...], preferred_element_type=F32)
    h3 = h2 + gate * pe
    if final:
        h3 = _rms(h3, gfin_ref[...])
    o_ref[...] = h3


def _combine_ple(layer, h1, yg, gates, p, prm, g_final, final):
    T, D = h1.shape
    tm = min(ROW_TILE, T)
    PD = p.shape[2]
    planes = yg.shape[1]
    const = lambda shape: pl.BlockSpec(shape, lambda i: (0,) * len(shape))
    return pl.pallas_call(
        functools.partial(_combine_ple_kernel, final=final),
        grid=(T // tm,),
        in_specs=[
            pl.BlockSpec((tm, D), lambda i: (i, 0)),
            pl.BlockSpec((TOP_K, planes, tm, SC_ROW_WORDS), lambda i: (0, 0, i, 0)),
            pl.BlockSpec((tm, 8), lambda i: (i, 0)),
            const((1, D)),
            const((D, D)),
            pl.BlockSpec((None, tm, PD), lambda i: (layer, i, 0)),
            const((PD, D)),
            const((1, D)),
        ],
        out_specs=pl.BlockSpec((tm, D), lambda i: (i, 0)),
        out_shape=jax.ShapeDtypeStruct((T, D), F32),
        compiler_params=pltpu.CompilerParams(
            dimension_semantics=("parallel",), vmem_limit_bytes=VMEM_LIMIT),
        name="combine_ple",
    )(h1, yg, gates, prm["g_ple"], prm["wpg"], p, prm["wple"], g_final)


def _layer_params(i, w):
    D = w["w_in"].shape[1]
    H = FOX_HEADS
    w_in = w["w_in"][i]
    c1 = 2 * GMLP_WIDTH
    c2, c3, c4 = c1 + FOX_WIDTH, c1 + 2 * FOX_WIDTH, c1 + 3 * FOX_WIDTH
    wk = jnp.zeros((D, H, HEAD_PAD), F32).at[:, :, :FOX_HEAD_DIM].set(
        w_in[:, c2:c3].reshape(D, H, FOX_HEAD_DIM)).reshape(D, H * HEAD_PAD)
    return {
        "g_mix": w["g_mix"][i].reshape(1, D),
        "wuv": w_in[:, :c1].astype(BF16),
        "wqt": (w_in[:, c1:c2] * (FOX_HEAD_DIM ** -0.5)).T.astype(BF16),
        "wk": wk.astype(BF16),
        "wvt": w_in[:, c3:c4].T.astype(BF16),
        "wft": w_in[:, c4:].T.astype(BF16),
        "b_f": w["b_f"][i].reshape(H, 1),
        "ln_g": w["ln_g"][i].reshape(1, GMLP_WIDTH),
        "ln_b": w["ln_b"][i].reshape(1, GMLP_WIDTH),
        "w_s": w["w_s"][i],
        "b_s": w["b_s"][i].reshape(GMLP_GROUPS, GMLP_BLOCK, 1),
        "g_a": w["g_a"][i].reshape(1, GMLP_WIDTH),
        "woa": w["w_out"][i][:GMLP_WIDTH].astype(BF16),
        "wob": w["w_out"][i][GMLP_WIDTH:].astype(BF16),
        "g_b": w["g_b"][i].reshape(FOX_WIDTH, 1),
        "g_moe": w["g_moe"][i].reshape(1, D),
        "wrt": w["w_router"][i].T.astype(BF16),
        "b_router": w["b_router"][i].reshape(N_EXPERTS, 1),
        "g_ple": w["g_ple"][i].reshape(1, D),
        "wpg": w["w_ple_gate"][i].astype(BF16),
        "wple": w["w_ple"][i].astype(BF16),
    }


def _bias_constants(tm):
    H = FOX_HEADS
    j = jnp.arange(tm)
    tri_incl = (j[:, None] <= j[None, :]).astype(BF16)
    tri_strict = (j[:, None] < j[None, :]).astype(BF16)
    part = jnp.arange(BIAS_PARTS)[:, None]
    head = jnp.arange(H)[None, :]
    rows = (part * H + head).reshape(-1)
    cols = (head * HEAD_PAD + FOX_HEAD_DIM + BIAS_PARTS + part).reshape(-1)
    selk = jnp.zeros((32, H * HEAD_PAD), F32).at[rows, cols].set(-1.0).astype(BF16)
    one_cols = (jnp.arange(H)[:, None] * HEAD_PAD + FOX_HEAD_DIM + jnp.arange(BIAS_PARTS)[None, :]).reshape(-1)
    onesk = jnp.zeros((1, H * HEAD_PAD), F32).at[0, one_cols].set(1.0)
    return tri_incl, tri_strict, selk, onesk


def _moe_dispatch_plan(counts, n_blocks):
    padded = (counts + EXPERT_ROWS - 1) // EXPERT_ROWS * EXPERT_ROWS
    pad_ends = jnp.cumsum(padded)
    pad_starts = (pad_ends - padded).astype(I32)
    block_start = jnp.arange(n_blocks, dtype=I32) * EXPERT_ROWS
    n_before = jnp.sum((pad_ends[None, :] <= block_start[:, None]).astype(I32), axis=1)
    block_e = jnp.minimum(n_before, N_EXPERTS - 1).astype(I32)
    own = block_e[:, None] == jnp.arange(N_EXPERTS, dtype=I32)[None, :]
    seg_end = jnp.sum(jnp.where(own, (pad_starts + counts)[None, :], 0), axis=1)
    block_nv = jnp.clip(seg_end - block_start, 0, EXPERT_ROWS)
    block_nv = jnp.where(block_start < pad_ends[-1], block_nv, 0).astype(I32)
    return pad_starts, block_e, block_nv


def kernel(x, p, g_mix, w_in, ln_g, ln_b, w_s, b_s, b_f, g_a, g_b, w_out, g_moe, w_router, b_router,
           w_gu, b_gu, w_dn, b_dn, g_ple, w_ple_gate, w_ple, g_final):
    B, S, D = x.shape
    T = B * S
    depth = w_in.shape[0]
    w = dict(g_mix=g_mix, w_in=w_in, ln_g=ln_g, ln_b=ln_b, w_s=w_s, b_s=b_s, b_f=b_f, g_a=g_a, g_b=g_b,
             w_out=w_out, g_moe=g_moe, w_router=w_router, b_router=b_router, w_gu=w_gu, b_gu=b_gu,
             w_dn=w_dn, b_dn=b_dn, g_ple=g_ple, w_ple_gate=w_ple_gate, w_ple=w_ple)
    tri_incl, tri_strict, selk, onesk = _bias_constants(min(ROW_TILE, S))
    n_blocks = -(-(T * TOP_K) // EXPERT_ROWS) + N_EXPERTS
    P = n_blocks * EXPERT_ROWS
    planes = (D // 2) // SC_ROW_WORDS
    plane_base = (jnp.arange(planes, dtype=I32) * P)[None, :, None]
    p_rows = p.reshape(depth, T, p.shape[-1])

    h = x.reshape(T, D)
    for i in range(depth):
        prm = _layer_params(i, w)
        prm.update(tri_incl=tri_incl, selk=selk, onesk=onesk)
        ya, qt, kx, vt = _mix_in(h, B, S, prm)
        ybt = _fox_attn(qt, kx, vt, B, S)
        h1, xp, tope, rank, gates, counts = _out_router(h, ya, ybt, B, S, prm, tri_strict)

        pad_starts, block_e, block_nv = _moe_dispatch_plan(counts[:, 0], n_blocks)
        dest = _dest_rows(pad_starts, tope, rank)[:TOP_K]
        idx = (dest[:, None, :] + plane_base).reshape(-1)
        xs = _sc_scatter_rows(xp.reshape(planes * T, SC_ROW_WORDS), idx, planes * P)
        ys = _experts(i, block_e, block_nv, xs.reshape(planes, P, SC_ROW_WORDS), w_gu, b_gu, w_dn, b_dn)
        yg = _sc_gather_rows(ys.reshape(planes * P, SC_ROW_WORDS), idx)
        h = _combine_ple(i, h1, yg.reshape(TOP_K, planes, T, SC_ROW_WORDS), gates, p_rows, prm,
                         g_final.reshape(1, D), final=(i == depth - 1))
    return h.reshape(B, S, D)
```

```python
import functools

import jax
import jax.numpy as jnp
from jax import lax
from jax.experimental import pallas as pl
from jax.experimental.pallas import tpu as pltpu
from jax.experimental.pallas import tpu_sc as plsc

F32 = jnp.float32
BF16 = jnp.bfloat16
I32 = jnp.int32
U32 = jnp.uint32

CHUNK = 64
GMLP_WIDTH = 512
GMLP_GROUPS = 4
GMLP_GROUP_CH = 128
GMLP_BLOCK = 128
FOX_WIDTH = 512
FOX_HEAD_DIM = 64
FOX_HEADS = 8
N_EXPERTS = 32
TOP_K = 4
EXPERT_ROWS = 256
SWIGLU_LIMIT = 7.0
SWIGLU_ALPHA = 1.702
RMS_EPS = 1e-5
LN_EPS = 1e-5

HEAD_PAD = 128
BIAS_PARTS = 3
ROW_TILE = 512
ATTN_TILE = 512
ATTN_HEADS = 2
ACC_ROWS = 80
LOG2E = 1.4426950408889634
SC_WINDOW = 128
SC_ROW_WORDS = 256
VMEM_LIMIT = 56 * 1024 * 1024
NEG_BIG = -1e30

_NT = (((1,), (1,)), ((), ()))
_TN = (((0,), (0,)), ((), ()))


def _rms(x, g):
    return x * lax.rsqrt(jnp.mean(x * x, axis=-1, keepdims=True) + RMS_EPS) * g


def _split3(c):
    a1 = c.astype(BF16).astype(F32)
    r1 = c - a1
    a2 = r1.astype(BF16).astype(F32)
    a3 = (r1 - a2).astype(BF16).astype(F32)
    return a1, a2, a3


def _pack_rows(x):
    n = x.shape[1] // 2
    lo = lax.bitcast_convert_type(x[:, :n].astype(BF16).astype(F32), U32) >> 16
    hi = lax.bitcast_convert_type(x[:, n:].astype(BF16).astype(F32), U32) & jnp.uint32(0xFFFF0000)
    return lo | hi


def _unpack_rows(u):
    lo = lax.bitcast_convert_type(u << 16, F32)
    hi = lax.bitcast_convert_type(u & jnp.uint32(0xFFFF0000), F32)
    return jnp.concatenate([lo, hi], axis=1)


def _store_planes(ref, u):
    for j in range(ref.shape[0]):
        ref[j] = u[:, j * SC_ROW_WORDS:(j + 1) * SC_ROW_WORDS]


def _load_planes(ref):
    return jnp.concatenate([ref[j] for j in range(ref.shape[0])], axis=1)


def _mix_in_kernel(h_ref, gmix_ref, wuv_ref, wqt_ref, wk_ref, wvt_ref, wft_ref, bf_ref, lng_ref, lnb_ref,
                   ws_ref, bs_ref, ga_ref, tri_ref, selk_ref, onesk_ref,
                   ya_ref, qt_ref, k_ref, vt_ref, carry_ref, ya_scr):
    tm = h_ref.shape[0]

    @pl.when(pl.program_id(1) == 0)
    def _():
        carry_ref[...] = jnp.zeros_like(carry_ref)

    xn = _rms(h_ref[...], gmix_ref[...]).astype(BF16)

    z = jnp.dot(xn, wuv_ref[...], preferred_element_type=F32)
    z = 0.5 * z * (1.0 + lax.erf(z * (0.5 ** 0.5)))
    u = z[:, :GMLP_WIDTH]
    v = z[:, GMLP_WIDTH:]
    mu = jnp.mean(v, axis=-1, keepdims=True)
    vc = v - mu
    var = jnp.mean(vc * vc, axis=-1, keepdims=True)
    vn = (vc * lax.rsqrt(var + LN_EPS) * lng_ref[...] + lnb_ref[...]).astype(BF16)
    pi = lax.broadcasted_iota(I32, (GMLP_BLOCK, GMLP_BLOCK), 0) // CHUNK
    pj = lax.broadcasted_iota(I32, (GMLP_BLOCK, GMLP_BLOCK), 1) // CHUNK
    for g in range(GMLP_GROUPS):
        wm = jnp.where(pj <= pi, ws_ref[g], 0.0).astype(BF16)
        cs = slice(g * GMLP_GROUP_CH, (g + 1) * GMLP_GROUP_CH)
        for n in range(tm // GMLP_BLOCK):
            rs = slice(n * GMLP_BLOCK, (n + 1) * GMLP_BLOCK)
            sv = jnp.dot(wm, vn[rs, cs], preferred_element_type=F32) + bs_ref[g]
            ya_scr[rs, cs] = u[rs, cs] * sv
    ya_ref[...] = _rms(ya_scr[...], ga_ref[...]).astype(BF16)

    qt = lax.dot_general(wqt_ref[...], xn, _NT, preferred_element_type=F32)
    vt_ref[0] = lax.dot_general(wvt_ref[...], xn, _NT, preferred_element_type=F32).astype(BF16)
    kp = jnp.dot(xn, wk_ref[...], preferred_element_type=F32)

    ft = lax.dot_general(wft_ref[...], xn, _NT, preferred_element_type=F32) + bf_ref[...]
    lf = (jnp.minimum(ft, 0.0) - jnp.log1p(jnp.exp(-jnp.abs(ft)))) * LOG2E
    zpad = jnp.zeros((32 - BIAS_PARTS * FOX_HEADS, tm), F32)
    stack = jnp.concatenate(list(_split3(lf)) + [zpad], axis=0).astype(BF16)
    cum = jnp.dot(stack, tri_ref[...], preferred_element_type=F32)
    c = cum[0:8] + cum[8:16] + cum[16:24] + carry_ref[:, 0:1]
    carry_ref[...] = jnp.broadcast_to(c[:, tm - 1:tm], carry_ref.shape)
    a1, a2, a3 = _split3(c)

    si = lax.broadcasted_iota(I32, (8, tm), 0)
    zrows = jnp.zeros((HEAD_PAD - FOX_HEAD_DIM - 8, tm), F32)
    for hd in range(FOX_HEADS):
        b1 = jnp.broadcast_to(a1[hd:hd + 1], (8, tm))
        b2 = jnp.broadcast_to(a2[hd:hd + 1], (8, tm))
        b3 = jnp.broadcast_to(a3[hd:hd + 1], (8, tm))
        ext = jnp.where(si == 0, b1, jnp.where(si == 1, b2, jnp.where(si == 2, b3,
                        jnp.where(si < 2 * BIAS_PARTS, 1.0, 0.0))))
        blk = jnp.concatenate([qt[hd * FOX_HEAD_DIM:(hd + 1) * FOX_HEAD_DIM], ext, zrows], axis=0)
        qt_ref[0, hd * HEAD_PAD:(hd + 1) * HEAD_PAD, :] = blk.astype(BF16)

    cstack = jnp.concatenate([a1, a2, a3, zpad], axis=0).astype(BF16)
    kext = lax.dot_general(cstack, selk_ref[...], _TN, preferred_element_type=F32)
    k_ref[...] = (kp + kext + onesk_ref[...]).astype(BF16)


def _mix_in(h, B, S, prm):
    T, D = h.shape
    tm = min(ROW_TILE, S)
    ns = S // tm
    H = FOX_HEADS
    const = lambda shape: pl.BlockSpec(shape, lambda b, s: (0,) * len(shape))
    return pl.pallas_call(
        _mix_in_kernel,
        grid=(B, ns),
        in_specs=[
            pl.BlockSpec((tm, D), lambda b, s: (b * ns + s, 0)),
            const((1, D)),
            const((D, 2 * GMLP_WIDTH)),
            const((FOX_WIDTH, D)),
            const((D, H * HEAD_PAD)),
            const((FOX_WIDTH, D)),
            const((H, D)),
            const((H, 1)),
            const((1, GMLP_WIDTH)),
            const((1, GMLP_WIDTH)),
            const((GMLP_GROUPS, GMLP_BLOCK, GMLP_BLOCK)),
            const((GMLP_GROUPS, GMLP_BLOCK, 1)),
            const((1, GMLP_WIDTH)),
            const((tm, tm)),
            const((32, H * HEAD_PAD)),
            const((1, H * HEAD_PAD)),
        ],
        out_specs=[
            pl.BlockSpec((tm, GMLP_WIDTH), lambda b, s: (b * ns + s, 0)),
            pl.BlockSpec((1, H * HEAD_PAD, tm), lambda b, s: (b, 0, s)),
            pl.BlockSpec((tm, H * HEAD_PAD), lambda b, s: (b * ns + s, 0)),
            pl.BlockSpec((1, FOX_WIDTH, tm), lambda b, s: (b, 0, s)),
        ],
        out_shape=[
            jax.ShapeDtypeStruct((T, GMLP_WIDTH), BF16),
            jax.ShapeDtypeStruct((B, H * HEAD_PAD, S), BF16),
            jax.ShapeDtypeStruct((T, H * HEAD_PAD), BF16),
            jax.ShapeDtypeStruct((B, FOX_WIDTH, S), BF16),
        ],
        scratch_shapes=[pltpu.VMEM((H, 128), F32), pltpu.VMEM((tm, GMLP_WIDTH), F32)],
        compiler_params=pltpu.CompilerParams(
            dimension_semantics=("arbitrary", "arbitrary"), vmem_limit_bytes=VMEM_LIMIT),
        name="mix_in",
    )(h, prm["g_mix"], prm["wuv"], prm["wqt"], prm["wk"], prm["wvt"], prm["wft"], prm["b_f"],
      prm["ln_g"], prm["ln_b"], prm["w_s"], prm["b_s"], prm["g_a"], prm["tri_incl"], prm["selk"], prm["onesk"])


def _fox_attn_kernel(qt_ref, k_ref, vt_ref, ot_ref):
    S = k_ref.shape[0]
    t = min(ATTN_TILE, S)
    nq = S // t
    ones_rows = jnp.ones((ACC_ROWS - FOX_HEAD_DIM, t), BF16)
    causal = lax.broadcasted_iota(I32, (t, t), 0) <= lax.broadcasted_iota(I32, (t, t), 1)
    pairs = [(qi, kj) for qi in range(nq) for kj in range(qi + 1)]

    def scores(qi, kj):
        out = []
        for hh in range(ATTN_HEADS):
            hs = slice(hh * HEAD_PAD, (hh + 1) * HEAD_PAD)
            s = jnp.dot(k_ref[kj * t:(kj + 1) * t, hs], qt_ref[0, hs, qi * t:(qi + 1) * t],
                        preferred_element_type=F32)
            out.append(jnp.where(causal, s, NEG_BIG) if kj == qi else s)
        return out

    s_next = scores(*pairs[0])
    state = None
    for n, (qi, kj) in enumerate(pairs):
        s_cur = s_next
        if n + 1 < len(pairs):
            s_next = scores(*pairs[n + 1])
        new_state = []
        for hh in range(ATTN_HEADS):
            s = s_cur[hh]
            v = vt_ref[0, hh * FOX_HEAD_DIM:(hh + 1) * FOX_HEAD_DIM, kj * t:(kj + 1) * t]
            v_aug = jnp.concatenate([v, ones_rows], axis=0)
            if kj == 0:
                m_new = jnp.max(s, axis=0, keepdims=True)
                acc = jnp.dot(v_aug, jnp.exp2(s - m_new).astype(BF16), preferred_element_type=F32)
            else:
                m, acc = state[hh]
                m_new = jnp.maximum(m, jnp.max(s, axis=0, keepdims=True))
                p = jnp.exp2(s - m_new).astype(BF16)
                acc = jnp.exp2(m - m_new) * acc + jnp.dot(v_aug, p, preferred_element_type=F32)
            new_state.append((m_new, acc))
        state = new_state
        if kj == qi:
            for hh in range(ATTN_HEADS):
                acc = state[hh][1]
                o = acc[:FOX_HEAD_DIM] / acc[FOX_HEAD_DIM:FOX_HEAD_DIM + 1]
                ot_ref[0, hh * FOX_HEAD_DIM:(hh + 1) * FOX_HEAD_DIM, qi * t:(qi + 1) * t] = o.astype(BF16)


def _fox_attn(qt, kx, vt, B, S):
    G = ATTN_HEADS
    return pl.pallas_call(
        _fox_attn_kernel,
        grid=(B, FOX_HEADS // G),
        in_specs=[
            pl.BlockSpec((1, G * HEAD_PAD, S), lambda b, h: (b, h, 0)),
            pl.BlockSpec((S, G * HEAD_PAD), lambda b, h: (b, h)),
            pl.BlockSpec((1, G * FOX_HEAD_DIM, S), lambda b, h: (b, h, 0)),
        ],
        out_specs=pl.BlockSpec((1, G * FOX_HEAD_DIM, S), lambda b, h: (b, h, 0)),
        out_shape=jax.ShapeDtypeStruct((B, FOX_WIDTH, S), BF16),
        compiler_params=pltpu.CompilerParams(
            dimension_semantics=("parallel", "parallel"), vmem_limit_bytes=VMEM_LIMIT),
        name="fox_attn",
    )(qt, kx, vt)


def _out_router_kernel(h_ref, ya_ref, ybt_ref, woa_ref, wob_ref, gb_ref, gmoe_ref, wrt_ref, br_ref, tri_ref,
                       h1_ref, xp_ref, tope_ref, rank_ref, gates_ref, counts_ref, carry_ref):
    tm = h_ref.shape[0]

    @pl.when(pl.program_id(0) == 0)
    def _():
        carry_ref[...] = jnp.zeros_like(carry_ref)

    yb = ybt_ref[0].astype(F32)
    ybn = (yb * lax.rsqrt(jnp.mean(yb * yb, axis=0, keepdims=True) + RMS_EPS) * gb_ref[...]).astype(BF16)
    y = jnp.dot(ya_ref[...], woa_ref[...], preferred_element_type=F32)
    y = y + lax.dot_general(ybn, wob_ref[...], _TN, preferred_element_type=F32)
    h1 = h_ref[...] + y
    h1_ref[...] = h1
    xn = _rms(h1, gmoe_ref[...])
    _store_planes(xp_ref, _pack_rows(xn))

    logits = lax.dot_general(wrt_ref[...], xn.astype(BF16), _NT, preferred_element_type=F32) + br_ref[...]
    ri = lax.broadcasted_iota(I32, (N_EXPERTS, tm), 0)
    vals, idxs = [], []
    l = logits
    for _ in range(TOP_K):
        m = jnp.max(l, axis=0, keepdims=True)
        idx = jnp.min(jnp.where(l == m, ri, N_EXPERTS), axis=0, keepdims=True)
        vals.append(m)
        idxs.append(idx)
        l = jnp.where(ri == idx, -jnp.inf, l)
    es = [jnp.exp(vk - vals[0]) for vk in vals]
    den = es[0] + es[1] + es[2] + es[3]
    zrow = jnp.zeros((8 - TOP_K, tm), F32)
    gates_ref[...] = jnp.concatenate([e / den for e in es] + [zrow], axis=0).T

    sel = jnp.zeros((N_EXPERTS, tm), F32)
    for idx in idxs:
        sel = sel + jnp.where(ri == idx, 1.0, 0.0)
    cnt = jnp.dot(sel.astype(BF16), tri_ref[...], preferred_element_type=F32) + carry_ref[:, 0:1]
    ranks = [jnp.sum(jnp.where(ri == idx, cnt, 0.0), axis=0, keepdims=True) for idx in idxs]
    zi = jnp.zeros((8 - TOP_K, tm), I32)
    rank_ref[...] = jnp.concatenate([r.astype(I32) for r in ranks] + [zi], axis=0)
    tope_ref[...] = jnp.concatenate(idxs + [zi], axis=0)
    total = carry_ref[...] + jnp.sum(sel, axis=1, keepdims=True)
    carry_ref[...] = total
    counts_ref[...] = total.astype(I32)


def _out_router(h, ya, ybt, B, S, prm, tri_strict):
    T, D = h.shape
    tm = min(ROW_TILE, S)
    ns = S // tm
    E = N_EXPERTS
    planes = (D // 2) // SC_ROW_WORDS
    const = lambda shape: pl.BlockSpec(shape, lambda i: (0,) * len(shape))
    return pl.pallas_call(
        _out_router_kernel,
        grid=(T // tm,),
        in_specs=[
            pl.BlockSpec((tm, D), lambda i: (i, 0)),
            pl.BlockSpec((tm, GMLP_WIDTH), lambda i: (i, 0)),
            pl.BlockSpec((1, FOX_WIDTH, tm), lambda i: (i // ns, 0, i % ns)),
            const((GMLP_WIDTH, D)),
            const((FOX_WIDTH, D)),
            const((FOX_WIDTH, 1)),
            const((1, D)),
            const((E, D)),
            const((E, 1)),
            const((tm, tm)),
        ],
        out_specs=[
            pl.BlockSpec((tm, D), lambda i: (i, 0)),
            pl.BlockSpec((planes, tm, SC_ROW_WORDS), lambda i: (0, i, 0)),
            pl.BlockSpec((8, tm), lambda i: (0, i)),
            pl.BlockSpec((8, tm), lambda i: (0, i)),
            pl.BlockSpec((tm, 8), lambda i: (i, 0)),
            pl.BlockSpec((E, 128), lambda i: (0, 0)),
        ],
        out_shape=[
            jax.ShapeDtypeStruct((T, D), F32),
            jax.ShapeDtypeStruct((planes, T, SC_ROW_WORDS), U32),
            jax.ShapeDtypeStruct((8, T), I32),
            jax.ShapeDtypeStruct((8, T), I32),
            jax.ShapeDtypeStruct((T, 8), F32),
            jax.ShapeDtypeStruct((E, 128), I32),
        ],
        scratch_shapes=[pltpu.VMEM((E, 128), F32)],
        compiler_params=pltpu.CompilerParams(
            dimension_semantics=("arbitrary",), vmem_limit_bytes=VMEM_LIMIT),
        name="out_router",
    )(h, ya, ybt, prm["woa"], prm["wob"], prm["g_b"], prm["g_moe"], prm["wrt"], prm["b_router"], tri_strict)


def _dest_kernel(ps_ref, tope_ref, rank_ref, dest_ref):
    e_sel = tope_ref[...]
    d = rank_ref[...]
    for e in range(N_EXPERTS):
        d = d + jnp.where(e_sel == e, ps_ref[e], 0)
    dest_ref[...] = d


def _dest_rows(pad_starts, tope, rank):
    T = tope.shape[1]
    tc = min(4096, T)
    return pl.pallas_call(
        _dest_kernel,
        grid_spec=pltpu.PrefetchScalarGridSpec(
            num_scalar_prefetch=1,
            grid=(T // tc,),
            in_specs=[pl.BlockSpec((8, tc), lambda i, ps: (0, i)),
                      pl.BlockSpec((8, tc), lambda i, ps: (0, i))],
            out_specs=pl.BlockSpec((8, tc), lambda i, ps: (0, i)),
        ),
        out_shape=jax.ShapeDtypeStruct((8, T), I32),
        name="dest_rows",
    )(pad_starts, tope, rank)


def _sc_mesh():
    return plsc.VectorSubcoreMesh(core_axis_name="c", subcore_axis_name="s")


def _sc_scatter_rows(x, idx, n_out):
    R, W = x.shape
    N = idx.shape[0]
    nsrc = R // SC_WINDOW

    @functools.partial(pl.kernel, out_type=jax.ShapeDtypeStruct((n_out, W), x.dtype), mesh=_sc_mesh())
    def k(x_hbm, i_hbm, o_hbm):
        def body(x_vmem, i_vmem):
            pltpu.sync_copy(x_vmem, o_hbm.at[i_vmem.at[0]])

        pltpu.emit_pipeline(
            body,
            grid=(N // SC_WINDOW,),
            in_specs=[pl.BlockSpec((SC_WINDOW, W), lambda i: (i % nsrc, 0)),
                      pl.BlockSpec((1, SC_WINDOW), lambda i: (0, i))],
            out_specs=[],
            core_axis_name=("c", "s"),
            dimension_semantics=(pltpu.PARALLEL,),
        )(x_hbm, i_hbm)

    return k(x, idx.reshape(1, N))


def _sc_gather_rows(table, idx):
    V, W = table.shape
    N = idx.shape[0]

    @functools.partial(pl.kernel, out_type=jax.ShapeDtypeStruct((N, W), table.dtype), mesh=_sc_mesh())
    def k(t_hbm, i_hbm, o_hbm):
        def body(i_vmem, o_vmem):
            pltpu.sync_copy(t_hbm.at[i_vmem.at[0]], o_vmem)

        pltpu.emit_pipeline(
            body,
            grid=(N // SC_WINDOW,),
            in_specs=[pl.BlockSpec((1, SC_WINDOW), lambda i: (0, i))],
            out_specs=[pl.BlockSpec((SC_WINDOW, W), lambda i: (i, 0))],
            core_axis_name=("c", "s"),
            dimension_semantics=(pltpu.PARALLEL,),
        )(i_hbm, o_hbm)

    return k(table, idx.reshape(1, N))


def _expert_kernel(be_ref, nv_ref, xs_ref, wgu_ref, bgu_ref, wdn_ref, bdn_ref, ys_ref, wgu_bf, wdn_bf):
    i = pl.program_id(0)
    nv = nv_ref[i]
    F = wdn_ref.shape[0]

    @pl.when((i == 0) | (be_ref[i] != be_ref[jnp.maximum(i - 1, 0)]))
    def _():
        wgu_bf[...] = wgu_ref[...].astype(BF16)
        wdn_bf[...] = wdn_ref[...].astype(BF16)

    @pl.when(nv > 0)
    def _():
        x = _unpack_rows(_load_planes(xs_ref))
        rows = lax.broadcasted_iota(I32, x.shape, 0)
        x = jnp.where(rows < nv, x, 0.0).astype(BF16)
        hgu = jnp.dot(x, wgu_bf[...], preferred_element_type=F32) + bgu_ref[...]
        g = jnp.minimum(hgu[:, :F], SWIGLU_LIMIT)
        lin = jnp.clip(hgu[:, F:], -SWIGLU_LIMIT, SWIGLU_LIMIT)
        act = g * jax.nn.sigmoid(SWIGLU_ALPHA * g) * (lin + 1.0)
        y = jnp.dot(act.astype(BF16), wdn_bf[...], preferred_element_type=F32) + bdn_ref[...]
        _store_planes(ys_ref, _pack_rows(y))

    @pl.when(nv == 0)
    def _():
        ys_ref[...] = jnp.zeros_like(ys_ref)


def _experts(layer, block_e, block_nv, xs, w_gu, b_gu, w_dn, b_dn):
    planes, P, _ = xs.shape
    _, E, D, F2 = w_gu.shape
    F = F2 // 2
    nb = P // EXPERT_ROWS
    return pl.pallas_call(
        _expert_kernel,
        grid_spec=pltpu.PrefetchScalarGridSpec(
            num_scalar_prefetch=2,
            grid=(nb,),
            in_specs=[
                pl.BlockSpec((planes, EXPERT_ROWS, SC_ROW_WORDS), lambda i, be, nv: (0, i, 0)),
                pl.BlockSpec((None, None, D, F2), lambda i, be, nv: (layer, be[i], 0, 0)),
                pl.BlockSpec((None, None, 1, F2), lambda i, be, nv: (layer, be[i], 0, 0)),
                pl.BlockSpec((None, None, F, D), lambda i, be, nv: (layer, be[i], 0, 0)),
                pl.BlockSpec((None, None, 1, D), lambda i, be, nv: (layer, be[i], 0, 0)),
            ],
            out_specs=pl.BlockSpec((planes, EXPERT_ROWS, SC_ROW_WORDS), lambda i, be, nv: (0, i, 0)),
            scratch_shapes=[pltpu.VMEM((D, F2), BF16), pltpu.VMEM((F, D), BF16)],
        ),
        out_shape=jax.ShapeDtypeStruct((planes, P, SC_ROW_WORDS), U32),
        compiler_params=pltpu.CompilerParams(
            dimension_semantics=("arbitrary",), vmem_limit_bytes=VMEM_LIMIT),
        name="experts",
    )(block_e, block_nv, xs, w_gu, b_gu.reshape(-1, E, 1, F2), w_dn, b_dn.reshape(-1, E, 1, D))


def _combine_ple_kernel(h1_ref, yg_ref, gates_ref, gple_ref, wpg_ref, p_ref, wple_ref, gfin_ref, o_ref, *, final):
    h2 = h1_ref[...]
    gates = gates_ref[...]
    for k in range(TOP_K):
        h2 = h2 + gates[:, k:k + 1] * _unpack_rows(_load_planes(yg_ref.at[k]))
    xn = _rms(h2, gple_ref[...]).astype(BF16)
    gate = jax.nn.sigmoid(jnp.dot(xn, wpg_ref[...], preferred_element_type=F32))
    pe = jnp.dot(p_ref[...].astype(BF16), wple_ref[...], preferred_element_type=F32)
    h3 = h2 + gate * pe
    if final:
        h3 = _rms(h3, gfin_ref[...])
    o_ref[...] = h3


def _combine_ple(layer, h1, yg, gates, p, prm, g_final, final):
    T, D = h1.shape
    tm = min(ROW_TILE, T)
    PD = p.shape[2]
    planes = yg.shape[1]
    const = lambda shape: pl.BlockSpec(shape, lambda i: (0,) * len(shape))
    return pl.pallas_call(
        functools.partial(_combine_ple_kernel, final=final),
        grid=(T // tm,),
        in_specs=[
            pl.BlockSpec((tm, D), lambda i: (i, 0)),
            pl.BlockSpec((TOP_K, planes, tm, SC_ROW_WORDS), lambda i: (0, 0, i, 0)),
            pl.BlockSpec((tm, 8), lambda i: (i, 0)),
            const((1, D)),
            const((D, D)),
            pl.BlockSpec((None, tm, PD), lambda i: (layer, i, 0)),
            const((PD, D)),
            const((1, D)),
        ],
        out_specs=pl.BlockSpec((tm, D), lambda i: (i, 0)),
        out_shape=jax.ShapeDtypeStruct((T, D), F32),
        compiler_params=pltpu.CompilerParams(
            dimension_semantics=("parallel",), vmem_limit_bytes=VMEM_LIMIT),
        name="combine_ple",
    )(h1, yg, gates, prm["g_ple"], prm["wpg"], p, prm["wple"], g_final)


def _layer_params(i, w):
    D = w["w_in"].shape[1]
    H = FOX_HEADS
    w_in = w["w_in"][i]
    c1 = 2 * GMLP_WIDTH
    c2, c3, c4 = c1 + FOX_WIDTH, c1 + 2 * FOX_WIDTH, c1 + 3 * FOX_WIDTH
    wk = jnp.zeros((D, H, HEAD_PAD), F32).at[:, :, :FOX_HEAD_DIM].set(
        w_in[:, c2:c3].reshape(D, H, FOX_HEAD_DIM)).reshape(D, H * HEAD_PAD)
    return {
        "g_mix": w["g_mix"][i].reshape(1, D),
        "wuv": w_in[:, :c1].astype(BF16),
        "wqt": (w_in[:, c1:c2] * (FOX_HEAD_DIM ** -0.5 * LOG2E)).T.astype(BF16),
        "wk": wk.astype(BF16),
        "wvt": w_in[:, c3:c4].T.astype(BF16),
        "wft": w_in[:, c4:].T.astype(BF16),
        "b_f": w["b_f"][i].reshape(H, 1),
        "ln_g": w["ln_g"][i].reshape(1, GMLP_WIDTH),
        "ln_b": w["ln_b"][i].reshape(1, GMLP_WIDTH),
        "w_s": w["w_s"][i],
        "b_s": w["b_s"][i].reshape(GMLP_GROUPS, GMLP_BLOCK, 1),
        "g_a": w["g_a"][i].reshape(1, GMLP_WIDTH),
        "woa": w["w_out"][i][:GMLP_WIDTH].astype(BF16),
        "wob": w["w_out"][i][GMLP_WIDTH:].astype(BF16),
        "g_b": w["g_b"][i].reshape(FOX_WIDTH, 1),
        "g_moe": w["g_moe"][i].reshape(1, D),
        "wrt": w["w_router"][i].T.astype(BF16),
        "b_router": w["b_router"][i].reshape(N_EXPERTS, 1),
        "g_ple": w["g_ple"][i].reshape(1, D),
        "wpg": w["w_ple_gate"][i].astype(BF16),
        "wple": w["w_ple"][i].astype(BF16),
    }


def _bias_constants(tm):
    H = FOX_HEADS
    j = jnp.arange(tm)
    tri_incl = (j[:, None] <= j[None, :]).astype(BF16)
    tri_strict = (j[:, None] < j[None, :]).astype(BF16)
    part = jnp.arange(BIAS_PARTS)[:, None]
    head = jnp.arange(H)[None, :]
    rows = (part * H + head).reshape(-1)
    cols = (head * HEAD_PAD + FOX_HEAD_DIM + BIAS_PARTS + part).reshape(-1)
    selk = jnp.zeros((32, H * HEAD_PAD), F32).at[rows, cols].set(-1.0).astype(BF16)
    one_cols = (jnp.arange(H)[:, None] * HEAD_PAD + FOX_HEAD_DIM + jnp.arange(BIAS_PARTS)[None, :]).reshape(-1)
    onesk = jnp.zeros((1, H * HEAD_PAD), F32).at[0, one_cols].set(1.0)
    return tri_incl, tri_strict, selk, onesk


def _moe_dispatch_plan(counts, n_blocks):
    padded = (counts + EXPERT_ROWS - 1) // EXPERT_ROWS * EXPERT_ROWS
    pad_ends = jnp.cumsum(padded)
    pad_starts = (pad_ends - padded).astype(I32)
    block_start = jnp.arange(n_blocks, dtype=I32) * EXPERT_ROWS
    n_before = jnp.sum((pad_ends[None, :] <= block_start[:, None]).astype(I32), axis=1)
    block_e = jnp.minimum(n_before, N_EXPERTS - 1).astype(I32)
    own = block_e[:, None] == jnp.arange(N_EXPERTS, dtype=I32)[None, :]
    seg_end = jnp.sum(jnp.where(own, (pad_starts + counts)[None, :], 0), axis=1)
    block_nv = jnp.clip(seg_end - block_start, 0, EXPERT_ROWS)
    block_nv = jnp.where(block_start < pad_ends[-1], block_nv, 0).astype(I32)
    return pad_starts, block_e, block_nv


def kernel(x, p, g_mix, w_in, ln_g, ln_b, w_s, b_s, b_f, g_a, g_b, w_out, g_moe, w_router, b_router,
           w_gu, b_gu, w_dn, b_dn, g_ple, w_ple_gate, w_ple, g_final):
    B, S, D = x.shape
    T = B * S
    depth = w_in.shape[0]
    w = dict(g_mix=g_mix, w_in=w_in, ln_g=ln_g, ln_b=ln_b, w_s=w_s, b_s=b_s, b_f=b_f, g_a=g_a, g_b=g_b,
             w_out=w_out, g_moe=g_moe, w_router=w_router, b_router=b_router, w_gu=w_gu, b_gu=b_gu,
             w_dn=w_dn, b_dn=b_dn, g_ple=g_ple, w_ple_gate=w_ple_gate, w_ple=w_ple)
    tri_incl, tri_strict, selk, onesk = _bias_constants(min(ROW_TILE, S))
    n_blocks = -(-(T * TOP_K) // EXPERT_ROWS) + N_EXPERTS
    P = n_blocks * EXPERT_ROWS
    planes = (D // 2) // SC_ROW_WORDS
    plane_base = (jnp.arange(planes, dtype=I32) * P)[None, :, None]
    p_rows = p.reshape(depth, T, p.shape[-1])

    h = x.reshape(T, D)
    for i in range(depth):
        prm = _layer_params(i, w)
        prm.update(tri_incl=tri_incl, selk=selk, onesk=onesk)
        ya, qt, kx, vt = _mix_in(h, B, S, prm)
        ybt = _fox_attn(qt, kx, vt, B, S)
        h1, xp, tope, rank, gates, counts = _out_router(h, ya, ybt, B, S, prm, tri_strict)

        pad_starts, block_e, block_nv = _moe_dispatch_plan(counts[:, 0], n_blocks)
        dest = _dest_rows(pad_starts, tope, rank)[:TOP_K]
        idx = (dest[:, None, :] + plane_base).reshape(-1)
        xs = _sc_scatter_rows(xp.reshape(planes * T, SC_ROW_WORDS), idx, planes * P)
        ys = _experts(i, block_e, block_nv, xs.reshape(planes, P, SC_ROW_WORDS), w_gu, b_gu, w_dn, b_dn)
        yg = _sc_gather_rows(ys.reshape(planes * P, SC_ROW_WORDS), idx)
        h = _combine_ple(i, h1, yg.reshape(TOP_K, planes, T, SC_ROW_WORDS), gates, p_rows, prm,
                         g_final.reshape(1, D), final=(i == depth - 1))
    return h.reshape(B, S, D)
```

```python
import functools

import jax
import jax.numpy as jnp
from jax import lax
from jax.experimental import pallas as pl
from jax.experimental.pallas import tpu as pltpu
from jax.experimental.pallas import tpu_sc as plsc

F32 = jnp.float32
BF16 = jnp.bfloat16
I32 = jnp.int32
U32 = jnp.uint32

CHUNK = 64
GMLP_WIDTH = 512
GMLP_GROUPS = 4
GMLP_GROUP_CH = 128
GMLP_BLOCK = 128
FOX_WIDTH = 512
FOX_HEAD_DIM = 64
FOX_HEADS = 8
N_EXPERTS = 32
TOP_K = 4
EXPERT_ROWS = 256
SWIGLU_LIMIT = 7.0
SWIGLU_ALPHA = 1.702
RMS_EPS = 1e-5
LN_EPS = 1e-5

HEAD_PAD = 128
BIAS_PARTS = 3
ROW_TILE = 512
ATTN_TILE = 512
ATTN_HEADS = 2
ACC_ROWS = 80
LOG2E = 1.4426950408889634
SC_WINDOW = 128
SC_ROW_WORDS = 256
MOE_GROUPS = 2
VMEM_LIMIT = 56 * 1024 * 1024
NEG_BIG = -1e30

_NT = (((1,), (1,)), ((), ()))
_TN = (((0,), (0,)), ((), ()))


def _rms(x, g):
    return x * lax.rsqrt(jnp.mean(x * x, axis=-1, keepdims=True) + RMS_EPS) * g


def _split3(c):
    a1 = c.astype(BF16).astype(F32)
    r1 = c - a1
    a2 = r1.astype(BF16).astype(F32)
    a3 = (r1 - a2).astype(BF16).astype(F32)
    return a1, a2, a3


def _pack_rows(x):
    n = x.shape[1] // 2
    lo = lax.bitcast_convert_type(x[:, :n].astype(BF16).astype(F32), U32) >> 16
    hi = lax.bitcast_convert_type(x[:, n:].astype(BF16).astype(F32), U32) & jnp.uint32(0xFFFF0000)
    return lo | hi


def _unpack_rows(u):
    lo = lax.bitcast_convert_type(u << 16, F32)
    hi = lax.bitcast_convert_type(u & jnp.uint32(0xFFFF0000), F32)
    return jnp.concatenate([lo, hi], axis=1)


def _store_planes(ref, u):
    for j in range(ref.shape[0]):
        ref[j] = u[:, j * SC_ROW_WORDS:(j + 1) * SC_ROW_WORDS]


def _load_planes(ref):
    return jnp.concatenate([ref[j] for j in range(ref.shape[0])], axis=1)


def _mix_in_kernel(h_ref, gmix_ref, wuv_ref, wqt_ref, wk_ref, wvt_ref, wft_ref, bf_ref, lng_ref, lnb_ref,
                   ws_ref, bs_ref, ga_ref, tri_ref, selk_ref, onesk_ref,
                   ya_ref, qt_ref, k_ref, vt_ref, carry_ref, ya_scr):
    tm = h_ref.shape[0]

    @pl.when(pl.program_id(1) == 0)
    def _():
        carry_ref[...] = jnp.zeros_like(carry_ref)

    xn = _rms(h_ref[...], gmix_ref[...]).astype(BF16)

    z = jnp.dot(xn, wuv_ref[...], preferred_element_type=F32)
    z = 0.5 * z * (1.0 + lax.erf(z * (0.5 ** 0.5)))
    u = z[:, :GMLP_WIDTH]
    v = z[:, GMLP_WIDTH:]
    mu = jnp.mean(v, axis=-1, keepdims=True)
    vc = v - mu
    var = jnp.mean(vc * vc, axis=-1, keepdims=True)
    vn = (vc * lax.rsqrt(var + LN_EPS) * lng_ref[...] + lnb_ref[...]).astype(BF16)
    pi = lax.broadcasted_iota(I32, (GMLP_BLOCK, GMLP_BLOCK), 0) // CHUNK
    pj = lax.broadcasted_iota(I32, (GMLP_BLOCK, GMLP_BLOCK), 1) // CHUNK
    for g in range(GMLP_GROUPS):
        wm = jnp.where(pj <= pi, ws_ref[g], 0.0).astype(BF16)
        cs = slice(g * GMLP_GROUP_CH, (g + 1) * GMLP_GROUP_CH)
        for n in range(tm // GMLP_BLOCK):
            rs = slice(n * GMLP_BLOCK, (n + 1) * GMLP_BLOCK)
            sv = jnp.dot(wm, vn[rs, cs], preferred_element_type=F32) + bs_ref[g]
            ya_scr[rs, cs] = u[rs, cs] * sv
    ya_ref[...] = _rms(ya_scr[...], ga_ref[...]).astype(BF16)

    qt = lax.dot_general(wqt_ref[...], xn, _NT, preferred_element_type=F32)
    vt_ref[0] = lax.dot_general(wvt_ref[...], xn, _NT, preferred_element_type=F32).astype(BF16)
    kp = jnp.dot(xn, wk_ref[...], preferred_element_type=F32)

    ft = lax.dot_general(wft_ref[...], xn, _NT, preferred_element_type=F32) + bf_ref[...]
    lf = (jnp.minimum(ft, 0.0) - jnp.log1p(jnp.exp(-jnp.abs(ft)))) * LOG2E
    zpad = jnp.zeros((32 - BIAS_PARTS * FOX_HEADS, tm), F32)
    stack = jnp.concatenate(list(_split3(lf)) + [zpad], axis=0).astype(BF16)
    cum = jnp.dot(stack, tri_ref[...], preferred_element_type=F32)
    c = cum[0:8] + cum[8:16] + cum[16:24] + carry_ref[:, 0:1]
    carry_ref[...] = jnp.broadcast_to(c[:, tm - 1:tm], carry_ref.shape)
    a1, a2, a3 = _split3(c)

    si = lax.broadcasted_iota(I32, (8, tm), 0)
    zrows = jnp.zeros((HEAD_PAD - FOX_HEAD_DIM - 8, tm), F32)
    for hd in range(FOX_HEADS):
        b1 = jnp.broadcast_to(a1[hd:hd + 1], (8, tm))
        b2 = jnp.broadcast_to(a2[hd:hd + 1], (8, tm))
        b3 = jnp.broadcast_to(a3[hd:hd + 1], (8, tm))
        ext = jnp.where(si == 0, b1, jnp.where(si == 1, b2, jnp.where(si == 2, b3,
                        jnp.where(si < 2 * BIAS_PARTS, 1.0, 0.0))))
        blk = jnp.concatenate([qt[hd * FOX_HEAD_DIM:(hd + 1) * FOX_HEAD_DIM], ext, zrows], axis=0)
        qt_ref[0, hd * HEAD_PAD:(hd + 1) * HEAD_PAD, :] = blk.astype(BF16)

    cstack = jnp.concatenate([a1, a2, a3, zpad], axis=0).astype(BF16)
    kext = lax.dot_general(cstack, selk_ref[...], _TN, preferred_element_type=F32)
    k_ref[...] = (kp + kext + onesk_ref[...]).astype(BF16)


def _mix_in(h, B, S, prm):
    T, D = h.shape
    tm = min(ROW_TILE, S)
    ns = S // tm
    H = FOX_HEADS
    const = lambda shape: pl.BlockSpec(shape, lambda b, s: (0,) * len(shape))
    return pl.pallas_call(
        _mix_in_kernel,
        grid=(B, ns),
        in_specs=[
            pl.BlockSpec((tm, D), lambda b, s: (b * ns + s, 0)),
            const((1, D)),
            const((D, 2 * GMLP_WIDTH)),
            const((FOX_WIDTH, D)),
            const((D, H * HEAD_PAD)),
            const((FOX_WIDTH, D)),
            const((H, D)),
            const((H, 1)),
            const((1, GMLP_WIDTH)),
            const((1, GMLP_WIDTH)),
            const((GMLP_GROUPS, GMLP_BLOCK, GMLP_BLOCK)),
            const((GMLP_GROUPS, GMLP_BLOCK, 1)),
            const((1, GMLP_WIDTH)),
            const((tm, tm)),
            const((32, H * HEAD_PAD)),
            const((1, H * HEAD_PAD)),
        ],
        out_specs=[
            pl.BlockSpec((tm, GMLP_WIDTH), lambda b, s: (b * ns + s, 0)),
            pl.BlockSpec((1, H * HEAD_PAD, tm), lambda b, s: (b, 0, s)),
            pl.BlockSpec((tm, H * HEAD_PAD), lambda b, s: (b * ns + s, 0)),
            pl.BlockSpec((1, FOX_WIDTH, tm), lambda b, s: (b, 0, s)),
        ],
        out_shape=[
            jax.ShapeDtypeStruct((T, GMLP_WIDTH), BF16),
            jax.ShapeDtypeStruct((B, H * HEAD_PAD, S), BF16),
            jax.ShapeDtypeStruct((T, H * HEAD_PAD), BF16),
            jax.ShapeDtypeStruct((B, FOX_WIDTH, S), BF16),
        ],
        scratch_shapes=[pltpu.VMEM((H, 128), F32), pltpu.VMEM((tm, GMLP_WIDTH), F32)],
        compiler_params=pltpu.CompilerParams(
            dimension_semantics=("arbitrary", "arbitrary"), vmem_limit_bytes=VMEM_LIMIT),
        name="mix_in",
    )(h, prm["g_mix"], prm["wuv"], prm["wqt"], prm["wk"], prm["wvt"], prm["wft"], prm["b_f"],
      prm["ln_g"], prm["ln_b"], prm["w_s"], prm["b_s"], prm["g_a"], prm["tri_incl"], prm["selk"], prm["onesk"])


def _fox_attn_kernel(qt_ref, k_ref, vt_ref, ot_ref):
    S = k_ref.shape[0]
    t = min(ATTN_TILE, S)
    nq = S // t
    ones_rows = jnp.ones((ACC_ROWS - FOX_HEAD_DIM, t), BF16)
    causal = lax.broadcasted_iota(I32, (t, t), 0) <= lax.broadcasted_iota(I32, (t, t), 1)
    pairs = [(qi, kj) for qi in range(nq) for kj in range(qi + 1)]

    def scores(qi, kj):
        out = []
        for hh in range(ATTN_HEADS):
            hs = slice(hh * HEAD_PAD, (hh + 1) * HEAD_PAD)
            s = jnp.dot(k_ref[kj * t:(kj + 1) * t, hs], qt_ref[0, hs, qi * t:(qi + 1) * t],
                        preferred_element_type=F32)
            out.append(jnp.where(causal, s, NEG_BIG) if kj == qi else s)
        return out

    s_next = scores(*pairs[0])
    state = None
    for n, (qi, kj) in enumerate(pairs):
        s_cur = s_next
        if n + 1 < len(pairs):
            s_next = scores(*pairs[n + 1])
        new_state = []
        for hh in range(ATTN_HEADS):
            s = s_cur[hh]
            v = vt_ref[0, hh * FOX_HEAD_DIM:(hh + 1) * FOX_HEAD_DIM, kj * t:(kj + 1) * t]
            v_aug = jnp.concatenate([v, ones_rows], axis=0)
            if kj == 0:
                m_new = jnp.max(s, axis=0, keepdims=True)
                acc = jnp.dot(v_aug, jnp.exp2(s - m_new).astype(BF16), preferred_element_type=F32)
            else:
                m, acc = state[hh]
                m_new = jnp.maximum(m, jnp.max(s, axis=0, keepdims=True))
                p = jnp.exp2(s - m_new).astype(BF16)
                acc = jnp.exp2(m - m_new) * acc + jnp.dot(v_aug, p, preferred_element_type=F32)
            new_state.append((m_new, acc))
        state = new_state
        if kj == qi:
            for hh in range(ATTN_HEADS):
                acc = state[hh][1]
                o = acc[:FOX_HEAD_DIM] / acc[FOX_HEAD_DIM:FOX_HEAD_DIM + 1]
                ot_ref[0, hh * FOX_HEAD_DIM:(hh + 1) * FOX_HEAD_DIM, qi * t:(qi + 1) * t] = o.astype(BF16)


def _fox_attn(qt, kx, vt, B, S):
    G = ATTN_HEADS
    return pl.pallas_call(
        _fox_attn_kernel,
        grid=(B, FOX_HEADS // G),
        in_specs=[
            pl.BlockSpec((1, G * HEAD_PAD, S), lambda b, h: (b, h, 0)),
            pl.BlockSpec((S, G * HEAD_PAD), lambda b, h: (b, h)),
            pl.BlockSpec((1, G * FOX_HEAD_DIM, S), lambda b, h: (b, h, 0)),
        ],
        out_specs=pl.BlockSpec((1, G * FOX_HEAD_DIM, S), lambda b, h: (b, h, 0)),
        out_shape=jax.ShapeDtypeStruct((B, FOX_WIDTH, S), BF16),
        compiler_params=pltpu.CompilerParams(
            dimension_semantics=("parallel", "parallel"), vmem_limit_bytes=VMEM_LIMIT),
        name="fox_attn",
    )(qt, kx, vt)


def _out_router_kernel(h_ref, ya_ref, ybt_ref, woa_ref, wob_ref, gb_ref, gmoe_ref, wrt_ref, br_ref, tri_ref,
                       h1_ref, xp_ref, tope_ref, rank_ref, gates_ref, counts_ref, carry_ref):
    tm = h_ref.shape[0]

    @pl.when(pl.program_id(0) == 0)
    def _():
        carry_ref[...] = jnp.zeros_like(carry_ref)

    yb = ybt_ref[0].astype(F32)
    ybn = (yb * lax.rsqrt(jnp.mean(yb * yb, axis=0, keepdims=True) + RMS_EPS) * gb_ref[...]).astype(BF16)
    y = jnp.dot(ya_ref[...], woa_ref[...], preferred_element_type=F32)
    y = y + lax.dot_general(ybn, wob_ref[...], _TN, preferred_element_type=F32)
    h1 = h_ref[...] + y
    h1_ref[...] = h1
    xn = _rms(h1, gmoe_ref[...])
    _store_planes(xp_ref, _pack_rows(xn))

    logits = lax.dot_general(wrt_ref[...], xn.astype(BF16), _NT, preferred_element_type=F32) + br_ref[...]
    ri = lax.broadcasted_iota(I32, (N_EXPERTS, tm), 0)
    vals, idxs = [], []
    l = logits
    for _ in range(TOP_K):
        m = jnp.max(l, axis=0, keepdims=True)
        idx = jnp.min(jnp.where(l == m, ri, N_EXPERTS), axis=0, keepdims=True)
        vals.append(m)
        idxs.append(idx)
        l = jnp.where(ri == idx, -jnp.inf, l)
    es = [jnp.exp(vk - vals[0]) for vk in vals]
    den = es[0] + es[1] + es[2] + es[3]
    zrow = jnp.zeros((8 - TOP_K, tm), F32)
    gates_ref[...] = jnp.concatenate([e / den for e in es] + [zrow], axis=0).T

    sel = jnp.zeros((N_EXPERTS, tm), F32)
    for idx in idxs:
        sel = sel + jnp.where(ri == idx, 1.0, 0.0)
    cnt = jnp.dot(sel.astype(BF16), tri_ref[...], preferred_element_type=F32) + carry_ref[:, 0:1]
    ranks = [jnp.sum(jnp.where(ri == idx, cnt, 0.0), axis=0, keepdims=True) for idx in idxs]
    zi = jnp.zeros((8 - TOP_K, tm), I32)
    rank_ref[...] = jnp.concatenate([r.astype(I32) for r in ranks] + [zi], axis=0)
    tope_ref[...] = jnp.concatenate(idxs + [zi], axis=0)
    total = carry_ref[...] + jnp.sum(sel, axis=1, keepdims=True)
    carry_ref[...] = total
    counts_ref[...] = total.astype(I32)


def _out_router(h, ya, ybt, B, S, prm, tri_strict):
    T, D = h.shape
    tm = min(ROW_TILE, S)
    ns = S // tm
    E = N_EXPERTS
    planes = (D // 2) // SC_ROW_WORDS
    spg = (T // tm) // MOE_GROUPS
    const = lambda shape: pl.BlockSpec(shape, lambda i: (0,) * len(shape))
    return pl.pallas_call(
        _out_router_kernel,
        grid=(T // tm,),
        in_specs=[
            pl.BlockSpec((tm, D), lambda i: (i, 0)),
            pl.BlockSpec((tm, GMLP_WIDTH), lambda i: (i, 0)),
            pl.BlockSpec((1, FOX_WIDTH, tm), lambda i: (i // ns, 0, i % ns)),
            const((GMLP_WIDTH, D)),
            const((FOX_WIDTH, D)),
            const((FOX_WIDTH, 1)),
            const((1, D)),
            const((E, D)),
            const((E, 1)),
            const((tm, tm)),
        ],
        out_specs=[
            pl.BlockSpec((tm, D), lambda i: (i, 0)),
            pl.BlockSpec((planes, tm, SC_ROW_WORDS), lambda i: (0, i, 0)),
            pl.BlockSpec((8, tm), lambda i: (0, i)),
            pl.BlockSpec((8, tm), lambda i: (0, i)),
            pl.BlockSpec((tm, 8), lambda i: (i, 0)),
            pl.BlockSpec((None, E, 128), lambda i: (i // spg, 0, 0)),
        ],
        out_shape=[
            jax.ShapeDtypeStruct((T, D), F32),
            jax.ShapeDtypeStruct((planes, T, SC_ROW_WORDS), U32),
            jax.ShapeDtypeStruct((8, T), I32),
            jax.ShapeDtypeStruct((8, T), I32),
            jax.ShapeDtypeStruct((T, 8), F32),
            jax.ShapeDtypeStruct((MOE_GROUPS, E, 128), I32),
        ],
        scratch_shapes=[pltpu.VMEM((E, 128), F32)],
        compiler_params=pltpu.CompilerParams(
            dimension_semantics=("arbitrary",), vmem_limit_bytes=VMEM_LIMIT),
        name="out_router",
    )(h, ya, ybt, prm["woa"], prm["wob"], prm["g_b"], prm["g_moe"], prm["wrt"], prm["b_router"], tri_strict)


def _dest_kernel(ps_ref, tope_ref, rank_ref, dest_ref, *, steps_per_group):
    base = (pl.program_id(0) // steps_per_group) * N_EXPERTS
    e_sel = tope_ref[...]
    d = rank_ref[...]
    for e in range(N_EXPERTS):
        d = d + jnp.where(e_sel == e, ps_ref[base + e], 0)
    dest_ref[...] = d


def _dest_rows(row_offset, tope, rank):
    T = tope.shape[1]
    tc = min(4096, T // MOE_GROUPS)
    return pl.pallas_call(
        functools.partial(_dest_kernel, steps_per_group=(T // MOE_GROUPS) // tc),
        grid_spec=pltpu.PrefetchScalarGridSpec(
            num_scalar_prefetch=1,
            grid=(T // tc,),
            in_specs=[pl.BlockSpec((8, tc), lambda i, ps: (0, i)),
                      pl.BlockSpec((8, tc), lambda i, ps: (0, i))],
            out_specs=pl.BlockSpec((8, tc), lambda i, ps: (0, i)),
        ),
        out_shape=jax.ShapeDtypeStruct((8, T), I32),
        name="dest_rows",
    )(row_offset, tope, rank)


def _sc_mesh():
    return plsc.VectorSubcoreMesh(core_axis_name="c", subcore_axis_name="s")


def _sc_scatter_rows(x, idx, n_out, src_block):
    R, W = x.shape
    N = idx.shape[0]

    @functools.partial(pl.kernel, out_type=jax.ShapeDtypeStruct((n_out, W), x.dtype), mesh=_sc_mesh())
    def k(x_hbm, i_hbm, o_hbm):
        def body(x_vmem, i_vmem):
            pltpu.sync_copy(x_vmem, o_hbm.at[i_vmem.at[0]])

        pltpu.emit_pipeline(
            body,
            grid=(N // SC_WINDOW,),
            in_specs=[pl.BlockSpec((SC_WINDOW, W), lambda i: (src_block(i), 0)),
                      pl.BlockSpec((1, SC_WINDOW), lambda i: (0, i))],
            out_specs=[],
            core_axis_name=("c", "s"),
            dimension_semantics=(pltpu.PARALLEL,),
        )(x_hbm, i_hbm)

    return k(x, idx.reshape(1, N))


def _sc_gather_rows(table, idx):
    V, W = table.shape
    N = idx.shape[0]

    @functools.partial(pl.kernel, out_type=jax.ShapeDtypeStruct((N, W), table.dtype), mesh=_sc_mesh())
    def k(t_hbm, i_hbm, o_hbm):
        def body(i_vmem, o_vmem):
            pltpu.sync_copy(t_hbm.at[i_vmem.at[0]], o_vmem)

        pltpu.emit_pipeline(
            body,
            grid=(N // SC_WINDOW,),
            in_specs=[pl.BlockSpec((1, SC_WINDOW), lambda i: (0, i))],
            out_specs=[pl.BlockSpec((SC_WINDOW, W), lambda i: (i, 0))],
            core_axis_name=("c", "s"),
            dimension_semantics=(pltpu.PARALLEL,),
        )(i_hbm, o_hbm)

    return k(table, idx.reshape(1, N))


def _expert_kernel(be_ref, nv_ref, xs_ref, wgu_ref, bgu_ref, wdn_ref, bdn_ref, ys_ref, wgu_bf, wdn_bf):
    i = pl.program_id(0)
    nv = nv_ref[i]
    F = wdn_ref.shape[0]

    @pl.when((i == 0) | (be_ref[i] != be_ref[jnp.maximum(i - 1, 0)]))
    def _():
        wgu_bf[...] = wgu_ref[...].astype(BF16)
        wdn_bf[...] = wdn_ref[...].astype(BF16)

    @pl.when(nv > 0)
    def _():
        x = _unpack_rows(_load_planes(xs_ref))
        rows = lax.broadcasted_iota(I32, x.shape, 0)
        x = jnp.where(rows < nv, x, 0.0).astype(BF16)
        hgu = jnp.dot(x, wgu_bf[...], preferred_element_type=F32) + bgu_ref[...]
        g = jnp.minimum(hgu[:, :F], SWIGLU_LIMIT)
        lin = jnp.clip(hgu[:, F:], -SWIGLU_LIMIT, SWIGLU_LIMIT)
        act = g * jax.nn.sigmoid(SWIGLU_ALPHA * g) * (lin + 1.0)
        y = jnp.dot(act.astype(BF16), wdn_bf[...], preferred_element_type=F32) + bdn_ref[...]
        _store_planes(ys_ref, _pack_rows(y))

    @pl.when(nv == 0)
    def _():
        ys_ref[...] = jnp.zeros_like(ys_ref)


def _experts(layer, block_e, block_nv, xs, w_gu, b_gu, w_dn, b_dn):
    planes, P, _ = xs.shape
    _, E, D, F2 = w_gu.shape
    F = F2 // 2
    nb = P // EXPERT_ROWS
    return pl.pallas_call(
        _expert_kernel,
        grid_spec=pltpu.PrefetchScalarGridSpec(
            num_scalar_prefetch=2,
            grid=(nb,),
            in_specs=[
                pl.BlockSpec((planes, EXPERT_ROWS, SC_ROW_WORDS), lambda i, be, nv: (0, i, 0)),
                pl.BlockSpec((None, None, D, F2), lambda i, be, nv: (layer, be[i], 0, 0)),
                pl.BlockSpec((None, None, 1, F2), lambda i, be, nv: (layer, be[i], 0, 0)),
                pl.BlockSpec((None, None, F, D), lambda i, be, nv: (layer, be[i], 0, 0)),
                pl.BlockSpec((None, None, 1, D), lambda i, be, nv: (layer, be[i], 0, 0)),
            ],
            out_specs=pl.BlockSpec((planes, EXPERT_ROWS, SC_ROW_WORDS), lambda i, be, nv: (0, i, 0)),
            scratch_shapes=[pltpu.VMEM((D, F2), BF16), pltpu.VMEM((F, D), BF16)],
        ),
        out_shape=jax.ShapeDtypeStruct((planes, P, SC_ROW_WORDS), U32),
        compiler_params=pltpu.CompilerParams(
            dimension_semantics=("arbitrary",), vmem_limit_bytes=VMEM_LIMIT),
        name="experts",
    )(block_e, block_nv, xs, w_gu, b_gu.reshape(-1, E, 1, F2), w_dn, b_dn.reshape(-1, E, 1, D))


def _combine_ple_kernel(h1_ref, yg_ref, gates_ref, gple_ref, wpg_ref, p_ref, wple_ref, gfin_ref, *rest, final):
    o_ref = rest[-1]
    h2 = h1_ref[...]
    gates = gates_ref[...]
    for k in range(TOP_K):
        h2 = h2 + gates[:, k:k + 1] * _unpack_rows(_load_planes(yg_ref.at[k]))
    xn = _rms(h2, gple_ref[...]).astype(BF16)
    gate = jax.nn.sigmoid(jnp.dot(xn, wpg_ref[...], preferred_element_type=F32))
    pe = jnp.dot(p_ref[...].astype(BF16), wple_ref[...], preferred_element_type=F32)
    h3 = h2 + gate * pe
    if final:
        h3 = _rms(h3, gfin_ref[...])
    o_ref[...] = h3


def _combine_ple(layer, group, h1, yg, gates, p, prm, g_final, final, h_prev):
    T, D = h1.shape
    Tg = yg.shape[2]
    tm = min(ROW_TILE, Tg)
    PD = p.shape[2]
    planes = yg.shape[1]
    off = group * (Tg // tm)
    const = lambda shape: pl.BlockSpec(shape, lambda i: (0,) * len(shape))
    in_specs = [
        pl.BlockSpec((tm, D), lambda i: (off + i, 0)),
        pl.BlockSpec((TOP_K, planes, tm, SC_ROW_WORDS), lambda i: (0, 0, i, 0)),
        pl.BlockSpec((tm, 8), lambda i: (off + i, 0)),
        const((1, D)),
        const((D, D)),
        pl.BlockSpec((None, tm, PD), lambda i: (layer, off + i, 0)),
        const((PD, D)),
        const((1, D)),
    ]
    args = [h1, yg, gates, prm["g_ple"], prm["wpg"], p, prm["wple"], g_final]
    aliases = {}
    if h_prev is not None:
        in_specs.append(pl.BlockSpec(memory_space=pl.ANY))
        args.append(h_prev)
        aliases = {len(args) - 1: 0}
    return pl.pallas_call(
        functools.partial(_combine_ple_kernel, final=final),
        grid=(Tg // tm,),
        in_specs=in_specs,
        out_specs=pl.BlockSpec((tm, D), lambda i: (off + i, 0)),
        out_shape=jax.ShapeDtypeStruct((T, D), F32),
        input_output_aliases=aliases,
        compiler_params=pltpu.CompilerParams(
            dimension_semantics=("parallel",), vmem_limit_bytes=VMEM_LIMIT),
        name="combine_ple",
    )(*args)


def _layer_params(i, w):
    D = w["w_in"].shape[1]
    H = FOX_HEADS
    w_in = w["w_in"][i]
    c1 = 2 * GMLP_WIDTH
    c2, c3, c4 = c1 + FOX_WIDTH, c1 + 2 * FOX_WIDTH, c1 + 3 * FOX_WIDTH
    wk = jnp.zeros((D, H, HEAD_PAD), F32).at[:, :, :FOX_HEAD_DIM].set(
        w_in[:, c2:c3].reshape(D, H, FOX_HEAD_DIM)).reshape(D, H * HEAD_PAD)
    return {
        "g_mix": w["g_mix"][i].reshape(1, D),
        "wuv": w_in[:, :c1].astype(BF16),
        "wqt": (w_in[:, c1:c2] * (FOX_HEAD_DIM ** -0.5 * LOG2E)).T.astype(BF16),
        "wk": wk.astype(BF16),
        "wvt": w_in[:, c3:c4].T.astype(BF16),
        "wft": w_in[:, c4:].T.astype(BF16),
        "b_f": w["b_f"][i].reshape(H, 1),
        "ln_g": w["ln_g"][i].reshape(1, GMLP_WIDTH),
        "ln_b": w["ln_b"][i].reshape(1, GMLP_WIDTH),
        "w_s": w["w_s"][i],
        "b_s": w["b_s"][i].reshape(GMLP_GROUPS, GMLP_BLOCK, 1),
        "g_a": w["g_a"][i].reshape(1, GMLP_WIDTH),
        "woa": w["w_out"][i][:GMLP_WIDTH].astype(BF16),
        "wob": w["w_out"][i][GMLP_WIDTH:].astype(BF16),
        "g_b": w["g_b"][i].reshape(FOX_WIDTH, 1),
        "g_moe": w["g_moe"][i].reshape(1, D),
        "wrt": w["w_router"][i].T.astype(BF16),
        "b_router": w["b_router"][i].reshape(N_EXPERTS, 1),
        "g_ple": w["g_ple"][i].reshape(1, D),
        "wpg": w["w_ple_gate"][i].astype(BF16),
        "wple": w["w_ple"][i].astype(BF16),
    }


def _bias_constants(tm):
    H = FOX_HEADS
    j = jnp.arange(tm)
    tri_incl = (j[:, None] <= j[None, :]).astype(BF16)
    tri_strict = (j[:, None] < j[None, :]).astype(BF16)
    part = jnp.arange(BIAS_PARTS)[:, None]
    head = jnp.arange(H)[None, :]
    rows = (part * H + head).reshape(-1)
    cols = (head * HEAD_PAD + FOX_HEAD_DIM + BIAS_PARTS + part).reshape(-1)
    selk = jnp.zeros((32, H * HEAD_PAD), F32).at[rows, cols].set(-1.0).astype(BF16)
    one_cols = (jnp.arange(H)[:, None] * HEAD_PAD + FOX_HEAD_DIM + jnp.arange(BIAS_PARTS)[None, :]).reshape(-1)
    onesk = jnp.zeros((1, H * HEAD_PAD), F32).at[0, one_cols].set(1.0)
    return tri_incl, tri_strict, selk, onesk


def _moe_dispatch_plan(counts, n_blocks):
    padded = (counts + EXPERT_ROWS - 1) // EXPERT_ROWS * EXPERT_ROWS
    pad_ends = jnp.cumsum(padded)
    pad_starts = (pad_ends - padded).astype(I32)
    block_start = jnp.arange(n_blocks, dtype=I32) * EXPERT_ROWS
    n_before = jnp.sum((pad_ends[None, :] <= block_start[:, None]).astype(I32), axis=1)
    block_e = jnp.minimum(n_before, N_EXPERTS - 1).astype(I32)
    own = block_e[:, None] == jnp.arange(N_EXPERTS, dtype=I32)[None, :]
    seg_end = jnp.sum(jnp.where(own, (pad_starts + counts)[None, :], 0), axis=1)
    block_nv = jnp.clip(seg_end - block_start, 0, EXPERT_ROWS)
    block_nv = jnp.where(block_start < pad_ends[-1], block_nv, 0).astype(I32)
    return pad_starts, block_e, block_nv


def kernel(x, p, g_mix, w_in, ln_g, ln_b, w_s, b_s, b_f, g_a, g_b, w_out, g_moe, w_router, b_router,
           w_gu, b_gu, w_dn, b_dn, g_ple, w_ple_gate, w_ple, g_final):
    B, S, D = x.shape
    T = B * S
    depth = w_in.shape[0]
    w = dict(g_mix=g_mix, w_in=w_in, ln_g=ln_g, ln_b=ln_b, w_s=w_s, b_s=b_s, b_f=b_f, g_a=g_a, g_b=g_b,
             w_out=w_out, g_moe=g_moe, w_router=w_router, b_router=b_router, w_gu=w_gu, b_gu=b_gu,
             w_dn=w_dn, b_dn=b_dn, g_ple=g_ple, w_ple_gate=w_ple_gate, w_ple=w_ple)
    tri_incl, tri_strict, selk, onesk = _bias_constants(min(ROW_TILE, S))
    G = MOE_GROUPS
    Tg = T // G
    n_blocks = -(-(Tg * TOP_K) // EXPERT_ROWS) + N_EXPERTS
    P = n_blocks * EXPERT_ROWS
    planes = (D // 2) // SC_ROW_WORDS
    plane_base = (jnp.arange(planes, dtype=I32) * P)[None, :, None]
    p_rows = p.reshape(depth, T, p.shape[-1])
    wpg = Tg // SC_WINDOW

    h = x.reshape(T, D)
    for i in range(depth):
        prm = _layer_params(i, w)
        prm.update(tri_incl=tri_incl, selk=selk, onesk=onesk)
        ya, qt, kx, vt = _mix_in(h, B, S, prm)
        ybt = _fox_attn(qt, kx, vt, B, S)
        h1, xp, tope, rank, gates, counts = _out_router(h, ya, ybt, B, S, prm, tri_strict)

        cum = counts[:, :, 0]
        before = jnp.concatenate([jnp.zeros((1, N_EXPERTS), I32), cum[:-1]], axis=0)
        plans = [_moe_dispatch_plan(cum[g] - before[g], n_blocks) for g in range(G)]
        row_offset = jnp.stack([plans[g][0] - before[g] for g in range(G)]).reshape(-1)
        dest = _dest_rows(row_offset, tope, rank)[:TOP_K]
        xp_rows = xp.reshape(planes * T, SC_ROW_WORDS)
        h_out = None
        for g in range(G):
            _, block_e, block_nv = plans[g]
            idx = (dest[:, None, g * Tg:(g + 1) * Tg] + plane_base).reshape(-1)
            src = lambda j, g=g: ((j // wpg) % planes) * (T // SC_WINDOW) + g * wpg + j % wpg
            xs = _sc_scatter_rows(xp_rows, idx, planes * P, src)
            ys = _experts(i, block_e, block_nv, xs.reshape(planes, P, SC_ROW_WORDS), w_gu, b_gu, w_dn, b_dn)
            yg = _sc_gather_rows(ys.reshape(planes * P, SC_ROW_WORDS), idx)
            h_out = _combine_ple(i, g, h1, yg.reshape(TOP_K, planes, Tg, SC_ROW_WORDS), gates, p_rows, prm,
                                 g_final.reshape(1, D), i == depth - 1, h_out)
        h = h_out
    return h.reshape(B, S, D)
```

```python
import functools

import jax
import jax.numpy as jnp
from jax import lax
from jax.experimental import pallas as pl
from jax.experimental.pallas import tpu as pltpu
from jax.experimental.pallas import tpu_sc as plsc

F32 = jnp.float32
BF16 = jnp.bfloat16
I32 = jnp.int32
U32 = jnp.uint32

CHUNK = 64
GMLP_WIDTH = 512
GMLP_GROUPS = 4
GMLP_GROUP_CH = 128
GMLP_BLOCK = 128
FOX_WIDTH = 512
FOX_HEAD_DIM = 64
FOX_HEADS = 8
N_EXPERTS = 32
TOP_K = 4
EXPERT_ROWS = 256
SWIGLU_LIMIT = 7.0
SWIGLU_ALPHA = 1.702
RMS_EPS = 1e-5
LN_EPS = 1e-5

HEAD_PAD = 128
BIAS_PARTS = 3
ROW_TILE = 512
ATTN_TILE = 512
ATTN_HEADS = 2
ACC_ROWS = 80
LOG2E = 1.4426950408889634
SC_WINDOW = 128
SC_ROW_WORDS = 256
MOE_GROUPS = 2
VMEM_LIMIT = 56 * 1024 * 1024
NEG_BIG = -1e30

_NT = (((1,), (1,)), ((), ()))
_TN = (((0,), (0,)), ((), ()))


def _rms(x, g):
    return x * lax.rsqrt(jnp.mean(x * x, axis=-1, keepdims=True) + RMS_EPS) * g


def _split3(c):
    a1 = c.astype(BF16).astype(F32)
    r1 = c - a1
    a2 = r1.astype(BF16).astype(F32)
    a3 = (r1 - a2).astype(BF16).astype(F32)
    return a1, a2, a3


def _pack_rows(x):
    n = x.shape[1] // 2
    lo = lax.bitcast_convert_type(x[:, :n].astype(BF16).astype(F32), U32) >> 16
    hi = lax.bitcast_convert_type(x[:, n:].astype(BF16).astype(F32), U32) & jnp.uint32(0xFFFF0000)
    return lo | hi


def _unpack_rows(u):
    lo = lax.bitcast_convert_type(u << 16, F32)
    hi = lax.bitcast_convert_type(u & jnp.uint32(0xFFFF0000), F32)
    return jnp.concatenate([lo, hi], axis=1)


def _store_planes(ref, u):
    for j in range(ref.shape[0]):
        ref[j] = u[:, j * SC_ROW_WORDS:(j + 1) * SC_ROW_WORDS]


def _load_planes(ref):
    return jnp.concatenate([ref[j] for j in range(ref.shape[0])], axis=1)


def _mix_in_kernel(h_ref, gmix_ref, wuv_ref, wqt_ref, wk_ref, wvt_ref, wft_ref, bf_ref, lng_ref, lnb_ref,
                   ws_ref, bs_ref, ga_ref, tri_ref, selk_ref, onesk_ref,
                   ya_ref, qt_ref, k_ref, vt_ref, carry_ref, ya_scr):
    tm = h_ref.shape[0]

    @pl.when(pl.program_id(1) == 0)
    def _():
        carry_ref[...] = jnp.zeros_like(carry_ref)

    xn = _rms(h_ref[...], gmix_ref[...]).astype(BF16)

    z = jnp.dot(xn, wuv_ref[...], preferred_element_type=F32)
    z = 0.5 * z * (1.0 + lax.erf(z * (0.5 ** 0.5)))
    u = z[:, :GMLP_WIDTH]
    v = z[:, GMLP_WIDTH:]
    mu = jnp.mean(v, axis=-1, keepdims=True)
    vc = v - mu
    var = jnp.mean(vc * vc, axis=-1, keepdims=True)
    vn = (vc * lax.rsqrt(var + LN_EPS) * lng_ref[...] + lnb_ref[...]).astype(BF16)
    pi = lax.broadcasted_iota(I32, (GMLP_BLOCK, GMLP_BLOCK), 0) // CHUNK
    pj = lax.broadcasted_iota(I32, (GMLP_BLOCK, GMLP_BLOCK), 1) // CHUNK
    for g in range(GMLP_GROUPS):
        wm = jnp.where(pj <= pi, ws_ref[g], 0.0).astype(BF16)
        cs = slice(g * GMLP_GROUP_CH, (g + 1) * GMLP_GROUP_CH)
        for n in range(tm // GMLP_BLOCK):
            rs = slice(n * GMLP_BLOCK, (n + 1) * GMLP_BLOCK)
            sv = jnp.dot(wm, vn[rs, cs], preferred_element_type=F32) + bs_ref[g]
            ya_scr[rs, cs] = u[rs, cs] * sv
    ya_ref[...] = _rms(ya_scr[...], ga_ref[...]).astype(BF16)

    qt = lax.dot_general(wqt_ref[...], xn, _NT, preferred_element_type=F32)
    vt_ref[0] = lax.dot_general(wvt_ref[...], xn, _NT, preferred_element_type=F32).astype(BF16)
    kp = jnp.dot(xn, wk_ref[...], preferred_element_type=F32)

    ft = lax.dot_general(wft_ref[...], xn, _NT, preferred_element_type=F32) + bf_ref[...]
    lf = (jnp.minimum(ft, 0.0) - jnp.log1p(jnp.exp(-jnp.abs(ft)))) * LOG2E
    zpad = jnp.zeros((32 - BIAS_PARTS * FOX_HEADS, tm), F32)
    stack = jnp.concatenate(list(_split3(lf)) + [zpad], axis=0).astype(BF16)
    cum = jnp.dot(stack, tri_ref[...], preferred_element_type=F32)
    c = cum[0:8] + cum[8:16] + cum[16:24] + carry_ref[:, 0:1]
    carry_ref[...] = jnp.broadcast_to(c[:, tm - 1:tm], carry_ref.shape)
    a1, a2, a3 = _split3(c)

    si = lax.broadcasted_iota(I32, (8, tm), 0)
    zrows = jnp.zeros((HEAD_PAD - FOX_HEAD_DIM - 8, tm), F32)
    for hd in range(FOX_HEADS):
        b1 = jnp.broadcast_to(a1[hd:hd + 1], (8, tm))
        b2 = jnp.broadcast_to(a2[hd:hd + 1], (8, tm))
        b3 = jnp.broadcast_to(a3[hd:hd + 1], (8, tm))
        ext = jnp.where(si == 0, b1, jnp.where(si == 1, b2, jnp.where(si == 2, b3,
                        jnp.where(si < 2 * BIAS_PARTS, 1.0, 0.0))))
        blk = jnp.concatenate([qt[hd * FOX_HEAD_DIM:(hd + 1) * FOX_HEAD_DIM], ext, zrows], axis=0)
        qt_ref[0, hd * HEAD_PAD:(hd + 1) * HEAD_PAD, :] = blk.astype(BF16)

    cstack = jnp.concatenate([a1, a2, a3, zpad], axis=0).astype(BF16)
    kext = lax.dot_general(cstack, selk_ref[...], _TN, preferred_element_type=F32)
    k_ref[...] = (kp + kext + onesk_ref[...]).astype(BF16)


def _mix_in(h, B, S, prm):
    T, D = h.shape
    tm = min(ROW_TILE, S)
    ns = S // tm
    H = FOX_HEADS
    const = lambda shape: pl.BlockSpec(shape, lambda b, s: (0,) * len(shape))
    return pl.pallas_call(
        _mix_in_kernel,
        grid=(B, ns),
        in_specs=[
            pl.BlockSpec((tm, D), lambda b, s: (b * ns + s, 0)),
            const((1, D)),
            const((D, 2 * GMLP_WIDTH)),
            const((FOX_WIDTH, D)),
            const((D, H * HEAD_PAD)),
            const((FOX_WIDTH, D)),
            const((H, D)),
            const((H, 1)),
            const((1, GMLP_WIDTH)),
            const((1, GMLP_WIDTH)),
            const((GMLP_GROUPS, GMLP_BLOCK, GMLP_BLOCK)),
            const((GMLP_GROUPS, GMLP_BLOCK, 1)),
            const((1, GMLP_WIDTH)),
            const((tm, tm)),
            const((32, H * HEAD_PAD)),
            const((1, H * HEAD_PAD)),
        ],
        out_specs=[
            pl.BlockSpec((tm, GMLP_WIDTH), lambda b, s: (b * ns + s, 0)),
            pl.BlockSpec((1, H * HEAD_PAD, tm), lambda b, s: (b, 0, s)),
            pl.BlockSpec((tm, H * HEAD_PAD), lambda b, s: (b * ns + s, 0)),
            pl.BlockSpec((1, FOX_WIDTH, tm), lambda b, s: (b, 0, s)),
        ],
        out_shape=[
            jax.ShapeDtypeStruct((T, GMLP_WIDTH), BF16),
            jax.ShapeDtypeStruct((B, H * HEAD_PAD, S), BF16),
            jax.ShapeDtypeStruct((T, H * HEAD_PAD), BF16),
            jax.ShapeDtypeStruct((B, FOX_WIDTH, S), BF16),
        ],
        scratch_shapes=[pltpu.VMEM((H, 128), F32), pltpu.VMEM((tm, GMLP_WIDTH), F32)],
        compiler_params=pltpu.CompilerParams(
            dimension_semantics=("arbitrary", "arbitrary"), vmem_limit_bytes=VMEM_LIMIT),
        name="mix_in",
    )(h, prm["g_mix"], prm["wuv"], prm["wqt"], prm["wk"], prm["wvt"], prm["wft"], prm["b_f"],
      prm["ln_g"], prm["ln_b"], prm["w_s"], prm["b_s"], prm["g_a"], prm["tri_incl"], prm["selk"], prm["onesk"])


def _fox_attn_kernel(qt_ref, k_ref, vt_ref, ot_ref):
    S = k_ref.shape[0]
    t = min(ATTN_TILE, S)
    nq = S // t
    ones_rows = jnp.ones((ACC_ROWS - FOX_HEAD_DIM, t), BF16)
    causal = lax.broadcasted_iota(I32, (t, t), 0) <= lax.broadcasted_iota(I32, (t, t), 1)
    pairs = [(qi, kj) for qi in range(nq) for kj in range(qi + 1)]

    def scores(qi, kj):
        out = []
        for hh in range(ATTN_HEADS):
            hs = slice(hh * HEAD_PAD, (hh + 1) * HEAD_PAD)
            s = jnp.dot(k_ref[kj * t:(kj + 1) * t, hs], qt_ref[0, hs, qi * t:(qi + 1) * t],
                        preferred_element_type=F32)
            out.append(jnp.where(causal, s, NEG_BIG) if kj == qi else s)
        return out

    s_next = scores(*pairs[0])
    state = None
    for n, (qi, kj) in enumerate(pairs):
        s_cur = s_next
        if n + 1 < len(pairs):
            s_next = scores(*pairs[n + 1])
        new_state = []
        for hh in range(ATTN_HEADS):
            s = s_cur[hh]
            v = vt_ref[0, hh * FOX_HEAD_DIM:(hh + 1) * FOX_HEAD_DIM, kj * t:(kj + 1) * t]
            v_aug = jnp.concatenate([v, ones_rows], axis=0)
            if kj == 0:
                m_new = jnp.max(s, axis=0, keepdims=True)
                acc = jnp.dot(v_aug, jnp.exp2(s - m_new).astype(BF16), preferred_element_type=F32)
            else:
                m, acc = state[hh]
                m_new = jnp.maximum(m, jnp.max(s, axis=0, keepdims=True))
                p = jnp.exp2(s - m_new).astype(BF16)
                acc = jnp.exp2(m - m_new) * acc + jnp.dot(v_aug, p, preferred_element_type=F32)
            new_state.append((m_new, acc))
        state = new_state
        if kj == qi:
            for hh in range(ATTN_HEADS):
                acc = state[hh][1]
                o = acc[:FOX_HEAD_DIM] / acc[FOX_HEAD_DIM:FOX_HEAD_DIM + 1]
                ot_ref[0, hh * FOX_HEAD_DIM:(hh + 1) * FOX_HEAD_DIM, qi * t:(qi + 1) * t] = o.astype(BF16)


def _fox_attn(qt, kx, vt, B, S):
    G = ATTN_HEADS
    return pl.pallas_call(
        _fox_attn_kernel,
        grid=(B, FOX_HEADS // G),
        in_specs=[
            pl.BlockSpec((1, G * HEAD_PAD, S), lambda b, h: (b, h, 0)),
            pl.BlockSpec((S, G * HEAD_PAD), lambda b, h: (b, h)),
            pl.BlockSpec((1, G * FOX_HEAD_DIM, S), lambda b, h: (b, h, 0)),
        ],
        out_specs=pl.BlockSpec((1, G * FOX_HEAD_DIM, S), lambda b, h: (b, h, 0)),
        out_shape=jax.ShapeDtypeStruct((B, FOX_WIDTH, S), BF16),
        compiler_params=pltpu.CompilerParams(
            dimension_semantics=("parallel", "parallel"), vmem_limit_bytes=VMEM_LIMIT),
        name="fox_attn",
    )(qt, kx, vt)


def _out_router_kernel(h_ref, ya_ref, ybt_ref, woa_ref, wob_ref, gb_ref, gmoe_ref, wrt_ref, br_ref, tri_ref,
                       h1_ref, xp_ref, tope_ref, rank_ref, gates_ref, counts_ref, carry_ref):
    tm = h_ref.shape[0]

    @pl.when(pl.program_id(0) == 0)
    def _():
        carry_ref[...] = jnp.zeros_like(carry_ref)

    yb = ybt_ref[0].astype(F32)
    ybn = (yb * lax.rsqrt(jnp.mean(yb * yb, axis=0, keepdims=True) + RMS_EPS) * gb_ref[...]).astype(BF16)
    y = jnp.dot(ya_ref[...], woa_ref[...], preferred_element_type=F32)
    y = y + lax.dot_general(ybn, wob_ref[...], _TN, preferred_element_type=F32)
    h1 = h_ref[...] + y
    h1_ref[...] = h1
    xn = _rms(h1, gmoe_ref[...])
    _store_planes(xp_ref, _pack_rows(xn))

    logits = lax.dot_general(wrt_ref[...], xn.astype(BF16), _NT, preferred_element_type=F32) + br_ref[...]
    ri = lax.broadcasted_iota(I32, (N_EXPERTS, tm), 0)
    vals, idxs = [], []
    l = logits
    for _ in range(TOP_K):
        m = jnp.max(l, axis=0, keepdims=True)
        idx = jnp.min(jnp.where(l == m, ri, N_EXPERTS), axis=0, keepdims=True)
        vals.append(m)
        idxs.append(idx)
        l = jnp.where(ri == idx, -jnp.inf, l)
    es = [jnp.exp(vk - vals[0]) for vk in vals]
    den = es[0] + es[1] + es[2] + es[3]
    zrow = jnp.zeros((8 - TOP_K, tm), F32)
    gates_ref[...] = jnp.concatenate([e / den for e in es] + [zrow], axis=0).T

    sel = jnp.zeros((N_EXPERTS, tm), F32)
    for idx in idxs:
        sel = sel + jnp.where(ri == idx, 1.0, 0.0)
    cnt = jnp.dot(sel.astype(BF16), tri_ref[...], preferred_element_type=F32) + carry_ref[:, 0:1]
    ranks = [jnp.sum(jnp.where(ri == idx, cnt, 0.0), axis=0, keepdims=True) for idx in idxs]
    zi = jnp.zeros((8 - TOP_K, tm), I32)
    rank_ref[...] = jnp.concatenate([r.astype(I32) for r in ranks] + [zi], axis=0)
    tope_ref[...] = jnp.concatenate(idxs + [zi], axis=0)
    total = carry_ref[...] + jnp.sum(sel, axis=1, keepdims=True)
    carry_ref[...] = total
    counts_ref[...] = total.astype(I32)


def _out_router(h, ya, ybt, B, S, prm, tri_strict):
    T, D = h.shape
    tm = min(ROW_TILE, S)
    ns = S // tm
    E = N_EXPERTS
    planes = (D // 2) // SC_ROW_WORDS
    spg = (T // tm) // MOE_GROUPS
    const = lambda shape: pl.BlockSpec(shape, lambda i: (0,) * len(shape))
    return pl.pallas_call(
        _out_router_kernel,
        grid=(T // tm,),
        in_specs=[
            pl.BlockSpec((tm, D), lambda i: (i, 0)),
            pl.BlockSpec((tm, GMLP_WIDTH), lambda i: (i, 0)),
            pl.BlockSpec((1, FOX_WIDTH, tm), lambda i: (i // ns, 0, i % ns)),
            const((GMLP_WIDTH, D)),
            const((FOX_WIDTH, D)),
            const((FOX_WIDTH, 1)),
            const((1, D)),
            const((E, D)),
            const((E, 1)),
            const((tm, tm)),
        ],
        out_specs=[
            pl.BlockSpec((tm, D), lambda i: (i, 0)),
            pl.BlockSpec((planes, tm, SC_ROW_WORDS), lambda i: (0, i, 0)),
            pl.BlockSpec((8, tm), lambda i: (0, i)),
            pl.BlockSpec((8, tm), lambda i: (0, i)),
            pl.BlockSpec((tm, 8), lambda i: (i, 0)),
            pl.BlockSpec((None, E, 128), lambda i: (i // spg, 0, 0)),
        ],
        out_shape=[
            jax.ShapeDtypeStruct((T, D), F32),
            jax.ShapeDtypeStruct((planes, T, SC_ROW_WORDS), U32),
            jax.ShapeDtypeStruct((8, T), I32),
            jax.ShapeDtypeStruct((8, T), I32),
            jax.ShapeDtypeStruct((T, 8), F32),
            jax.ShapeDtypeStruct((MOE_GROUPS, E, 128), I32),
        ],
        scratch_shapes=[pltpu.VMEM((E, 128), F32)],
        compiler_params=pltpu.CompilerParams(
            dimension_semantics=("arbitrary",), vmem_limit_bytes=VMEM_LIMIT),
        name="out_router",
    )(h, ya, ybt, prm["woa"], prm["wob"], prm["g_b"], prm["g_moe"], prm["wrt"], prm["b_router"], tri_strict)


def _dest_kernel(ps_ref, tope_ref, rank_ref, dest_ref, *, steps_per_group):
    base = (pl.program_id(0) // steps_per_group) * N_EXPERTS
    e_sel = tope_ref[...]
    d = rank_ref[...]
    for e in range(N_EXPERTS):
        d = d + jnp.where(e_sel == e, ps_ref[base + e], 0)
    dest_ref[...] = d


def _dest_rows(row_offset, tope, rank):
    T = tope.shape[1]
    tc = min(4096, T // MOE_GROUPS)
    return pl.pallas_call(
        functools.partial(_dest_kernel, steps_per_group=(T // MOE_GROUPS) // tc),
        grid_spec=pltpu.PrefetchScalarGridSpec(
            num_scalar_prefetch=1,
            grid=(T // tc,),
            in_specs=[pl.BlockSpec((8, tc), lambda i, ps: (0, i)),
                      pl.BlockSpec((8, tc), lambda i, ps: (0, i))],
            out_specs=pl.BlockSpec((8, tc), lambda i, ps: (0, i)),
        ),
        out_shape=jax.ShapeDtypeStruct((8, T), I32),
        name="dest_rows",
    )(row_offset, tope, rank)


def _sc_mesh():
    return plsc.VectorSubcoreMesh(core_axis_name="c", subcore_axis_name="s")


def _sc_scatter_rows(x, idx, n_out, src_block):
    R, W = x.shape
    N = idx.shape[0]

    @functools.partial(pl.kernel, out_type=jax.ShapeDtypeStruct((n_out, W), x.dtype), mesh=_sc_mesh())
    def k(x_hbm, i_hbm, o_hbm):
        def body(x_vmem, i_vmem):
            pltpu.sync_copy(x_vmem, o_hbm.at[i_vmem.at[0]])

        pltpu.emit_pipeline(
            body,
            grid=(N // SC_WINDOW,),
            in_specs=[pl.BlockSpec((SC_WINDOW, W), lambda i: (src_block(i), 0)),
                      pl.BlockSpec((1, SC_WINDOW), lambda i: (0, i))],
            out_specs=[],
            core_axis_name=("c", "s"),
            dimension_semantics=(pltpu.PARALLEL,),
        )(x_hbm, i_hbm)

    return k(x, idx.reshape(1, N))


def _sc_gather_rows(table, idx):
    V, W = table.shape
    N = idx.shape[0]

    @functools.partial(pl.kernel, out_type=jax.ShapeDtypeStruct((N, W), table.dtype), mesh=_sc_mesh())
    def k(t_hbm, i_hbm, o_hbm):
        def body(i_vmem, o_vmem):
            pltpu.sync_copy(t_hbm.at[i_vmem.at[0]], o_vmem)

        pltpu.emit_pipeline(
            body,
            grid=(N // SC_WINDOW,),
            in_specs=[pl.BlockSpec((1, SC_WINDOW), lambda i: (0, i))],
            out_specs=[pl.BlockSpec((SC_WINDOW, W), lambda i: (i, 0))],
            core_axis_name=("c", "s"),
            dimension_semantics=(pltpu.PARALLEL,),
        )(i_hbm, o_hbm)

    return k(table, idx.reshape(1, N))


def _expert_kernel(be_ref, nv_ref, fs_ref, nxt_ref, xs_ref, wgu_hbm, bgu_ref, wdn_hbm, bdn_ref, ys_ref,
                   wgu_f32, wdn_f32, wgu_bf, wdn_bf, sem, *, layer):
    i = pl.program_id(0)
    nv = nv_ref[i]
    fs = fs_ref[i]
    F = wdn_bf.shape[0]

    def weight_copies(e, slot):
        return (pltpu.make_async_copy(wgu_hbm.at[layer, e], wgu_f32.at[slot], sem.at[slot, 0]),
                pltpu.make_async_copy(wdn_hbm.at[layer, e], wdn_f32.at[slot], sem.at[slot, 1]))

    @pl.when(i == 0)
    def _():
        for cp in weight_copies(be_ref[0], 0):
            cp.start()

    @pl.when(fs >= 0)
    def _():
        for cp in weight_copies(be_ref[i], fs):
            cp.wait()
        wgu_bf[...] = wgu_f32[fs].astype(BF16)
        wdn_bf[...] = wdn_f32[fs].astype(BF16)
        nxt = nxt_ref[i]

        @pl.when(nxt >= 0)
        def _():
            for cp in weight_copies(nxt, 1 - fs):
                cp.start()

    @pl.when(nv > 0)
    def _():
        x = _unpack_rows(_load_planes(xs_ref))
        rows = lax.broadcasted_iota(I32, x.shape, 0)
        x = jnp.where(rows < nv, x, 0.0).astype(BF16)
        hgu = jnp.dot(x, wgu_bf[...], preferred_element_type=F32) + bgu_ref[...]
        g = jnp.minimum(hgu[:, :F], SWIGLU_LIMIT)
        lin = jnp.clip(hgu[:, F:], -SWIGLU_LIMIT, SWIGLU_LIMIT)
        act = g * jax.nn.sigmoid(SWIGLU_ALPHA * g) * (lin + 1.0)
        y = jnp.dot(act.astype(BF16), wdn_bf[...], preferred_element_type=F32) + bdn_ref[...]
        _store_planes(ys_ref, _pack_rows(y))

    @pl.when(nv == 0)
    def _():
        ys_ref[...] = jnp.zeros_like(ys_ref)


def _experts(layer, plan, xs, w_gu, b_gu, w_dn, b_dn):
    planes, P, _ = xs.shape
    _, E, D, F2 = w_gu.shape
    F = F2 // 2
    nb = P // EXPERT_ROWS
    by_block = lambda i, be, nv, fs, nxt: (0, i, 0)
    by_expert = lambda i, be, nv, fs, nxt: (layer, be[i], 0, 0)
    return pl.pallas_call(
        functools.partial(_expert_kernel, layer=layer),
        grid_spec=pltpu.PrefetchScalarGridSpec(
            num_scalar_prefetch=4,
            grid=(nb,),
            in_specs=[
                pl.BlockSpec((planes, EXPERT_ROWS, SC_ROW_WORDS), by_block),
                pl.BlockSpec(memory_space=pl.ANY),
                pl.BlockSpec((None, None, 1, F2), by_expert),
                pl.BlockSpec(memory_space=pl.ANY),
                pl.BlockSpec((None, None, 1, D), by_expert),
            ],
            out_specs=pl.BlockSpec((planes, EXPERT_ROWS, SC_ROW_WORDS), by_block),
            scratch_shapes=[
                pltpu.VMEM((2, D, F2), F32), pltpu.VMEM((2, F, D), F32),
                pltpu.VMEM((D, F2), BF16), pltpu.VMEM((F, D), BF16),
                pltpu.SemaphoreType.DMA((2, 2)),
            ],
        ),
        out_shape=jax.ShapeDtypeStruct((planes, P, SC_ROW_WORDS), U32),
        compiler_params=pltpu.CompilerParams(
            dimension_semantics=("arbitrary",), vmem_limit_bytes=VMEM_LIMIT),
        name="experts",
    )(plan["block_e"], plan["block_nv"], plan["first_slot"], plan["next_e"],
      xs, w_gu, b_gu.reshape(-1, E, 1, F2), w_dn, b_dn.reshape(-1, E, 1, D))


def _combine_ple_kernel(h1_ref, yg_ref, gates_ref, gple_ref, wpg_ref, p_ref, wple_ref, gfin_ref, *rest, final):
    o_ref = rest[-1]
    h2 = h1_ref[...]
    gates = gates_ref[...]
    for k in range(TOP_K):
        h2 = h2 + gates[:, k:k + 1] * _unpack_rows(_load_planes(yg_ref.at[k]))
    xn = _rms(h2, gple_ref[...]).astype(BF16)
    gate = jax.nn.sigmoid(jnp.dot(xn, wpg_ref[...], preferred_element_type=F32))
    pe = jnp.dot(p_ref[...].astype(BF16), wple_ref[...], preferred_element_type=F32)
    h3 = h2 + gate * pe
    if final:
        h3 = _rms(h3, gfin_ref[...])
    o_ref[...] = h3


def _combine_ple(layer, group, h1, yg, gates, p, prm, g_final, final, h_prev):
    T, D = h1.shape
    Tg = yg.shape[2]
    tm = min(ROW_TILE, Tg)
    PD = p.shape[2]
    planes = yg.shape[1]
    off = group * (Tg // tm)
    const = lambda shape: pl.BlockSpec(shape, lambda i: (0,) * len(shape))
    in_specs = [
        pl.BlockSpec((tm, D), lambda i: (off + i, 0)),
        pl.BlockSpec((TOP_K, planes, tm, SC_ROW_WORDS), lambda i: (0, 0, i, 0)),
        pl.BlockSpec((tm, 8), lambda i: (off + i, 0)),
        const((1, D)),
        const((D, D)),
        pl.BlockSpec((None, tm, PD), lambda i: (layer, off + i, 0)),
        const((PD, D)),
        const((1, D)),
    ]
    args = [h1, yg, gates, prm["g_ple"], prm["wpg"], p, prm["wple"], g_final]
    aliases = {}
    if h_prev is not None:
        in_specs.append(pl.BlockSpec(memory_space=pl.ANY))
        args.append(h_prev)
        aliases = {len(args) - 1: 0}
    return pl.pallas_call(
        functools.partial(_combine_ple_kernel, final=final),
        grid=(Tg // tm,),
        in_specs=in_specs,
        out_specs=pl.BlockSpec((tm, D), lambda i: (off + i, 0)),
        out_shape=jax.ShapeDtypeStruct((T, D), F32),
        input_output_aliases=aliases,
        compiler_params=pltpu.CompilerParams(
            dimension_semantics=("parallel",), vmem_limit_bytes=VMEM_LIMIT),
        name="combine_ple",
    )(*args)


def _layer_params(i, w):
    D = w["w_in"].shape[1]
    H = FOX_HEADS
    w_in = w["w_in"][i]
    c1 = 2 * GMLP_WIDTH
    c2, c3, c4 = c1 + FOX_WIDTH, c1 + 2 * FOX_WIDTH, c1 + 3 * FOX_WIDTH
    wk = jnp.zeros((D, H, HEAD_PAD), F32).at[:, :, :FOX_HEAD_DIM].set(
        w_in[:, c2:c3].reshape(D, H, FOX_HEAD_DIM)).reshape(D, H * HEAD_PAD)
    return {
        "g_mix": w["g_mix"][i].reshape(1, D),
        "wuv": w_in[:, :c1].astype(BF16),
        "wqt": (w_in[:, c1:c2] * (FOX_HEAD_DIM ** -0.5 * LOG2E)).T.astype(BF16),
        "wk": wk.astype(BF16),
        "wvt": w_in[:, c3:c4].T.astype(BF16),
        "wft": w_in[:, c4:].T.astype(BF16),
        "b_f": w["b_f"][i].reshape(H, 1),
        "ln_g": w["ln_g"][i].reshape(1, GMLP_WIDTH),
        "ln_b": w["ln_b"][i].reshape(1, GMLP_WIDTH),
        "w_s": w["w_s"][i],
        "b_s": w["b_s"][i].reshape(GMLP_GROUPS, GMLP_BLOCK, 1),
        "g_a": w["g_a"][i].reshape(1, GMLP_WIDTH),
        "woa": w["w_out"][i][:GMLP_WIDTH].astype(BF16),
        "wob": w["w_out"][i][GMLP_WIDTH:].astype(BF16),
        "g_b": w["g_b"][i].reshape(FOX_WIDTH, 1),
        "g_moe": w["g_moe"][i].reshape(1, D),
        "wrt": w["w_router"][i].T.astype(BF16),
        "b_router": w["b_router"][i].reshape(N_EXPERTS, 1),
        "g_ple": w["g_ple"][i].reshape(1, D),
        "wpg": w["w_ple_gate"][i].astype(BF16),
        "wple": w["w_ple"][i].astype(BF16),
    }


def _bias_constants(tm):
    H = FOX_HEADS
    j = jnp.arange(tm)
    tri_incl = (j[:, None] <= j[None, :]).astype(BF16)
    tri_strict = (j[:, None] < j[None, :]).astype(BF16)
    part = jnp.arange(BIAS_PARTS)[:, None]
    head = jnp.arange(H)[None, :]
    rows = (part * H + head).reshape(-1)
    cols = (head * HEAD_PAD + FOX_HEAD_DIM + BIAS_PARTS + part).reshape(-1)
    selk = jnp.zeros((32, H * HEAD_PAD), F32).at[rows, cols].set(-1.0).astype(BF16)
    one_cols = (jnp.arange(H)[:, None] * HEAD_PAD + FOX_HEAD_DIM + jnp.arange(BIAS_PARTS)[None, :]).reshape(-1)
    onesk = jnp.zeros((1, H * HEAD_PAD), F32).at[0, one_cols].set(1.0)
    return tri_incl, tri_strict, selk, onesk


def _moe_dispatch_plan(counts, n_blocks):
    padded = (counts + EXPERT_ROWS - 1) // EXPERT_ROWS * EXPERT_ROWS
    pad_ends = jnp.cumsum(padded)
    pad_starts = (pad_ends - padded).astype(I32)
    block_start = jnp.arange(n_blocks, dtype=I32) * EXPERT_ROWS
    n_before = jnp.sum((pad_ends[None, :] <= block_start[:, None]).astype(I32), axis=1)
    block_e = jnp.minimum(n_before, N_EXPERTS - 1).astype(I32)
    own = block_e[:, None] == jnp.arange(N_EXPERTS, dtype=I32)[None, :]
    seg_end = jnp.sum(jnp.where(own, (pad_starts + counts)[None, :], 0), axis=1)
    block_nv = jnp.clip(seg_end - block_start, 0, EXPERT_ROWS)
    block_nv = jnp.where(block_start < pad_ends[-1], block_nv, 0).astype(I32)
    prev_e = jnp.concatenate([jnp.full((1,), -1, I32), block_e[:-1]])
    first = (block_nv > 0) & (block_e != prev_e)
    run = jnp.cumsum(first.astype(I32)) - 1
    first_slot = jnp.where(first, run % 2, -1).astype(I32)
    ar = jnp.arange(N_EXPERTS, dtype=I32)
    later = (ar[None, :] > ar[:, None]) & (counts > 0)[None, :]
    next_of = jnp.min(jnp.where(later, ar[None, :], N_EXPERTS), axis=1)
    next_of = jnp.where(next_of < N_EXPERTS, next_of, -1)
    next_e = jnp.sum(jnp.where(own, next_of[None, :], 0), axis=1).astype(I32)
    return dict(pad_starts=pad_starts, block_e=block_e, block_nv=block_nv, first_slot=first_slot, next_e=next_e)


def kernel(x, p, g_mix, w_in, ln_g, ln_b, w_s, b_s, b_f, g_a, g_b, w_out, g_moe, w_router, b_router,
           w_gu, b_gu, w_dn, b_dn, g_ple, w_ple_gate, w_ple, g_final):
    B, S, D = x.shape
    T = B * S
    depth = w_in.shape[0]
    w = dict(g_mix=g_mix, w_in=w_in, ln_g=ln_g, ln_b=ln_b, w_s=w_s, b_s=b_s, b_f=b_f, g_a=g_a, g_b=g_b,
             w_out=w_out, g_moe=g_moe, w_router=w_router, b_router=b_router, w_gu=w_gu, b_gu=b_gu,
             w_dn=w_dn, b_dn=b_dn, g_ple=g_ple, w_ple_gate=w_ple_gate, w_ple=w_ple)
    tri_incl, tri_strict, selk, onesk = _bias_constants(min(ROW_TILE, S))
    G = MOE_GROUPS
    Tg = T // G
    n_blocks = -(-(Tg * TOP_K) // EXPERT_ROWS) + N_EXPERTS
    P = n_blocks * EXPERT_ROWS
    planes = (D // 2) // SC_ROW_WORDS
    plane_base = (jnp.arange(planes, dtype=I32) * P)[None, :, None]
    p_rows = p.reshape(depth, T, p.shape[-1])
    wpg = Tg // SC_WINDOW

    h = x.reshape(T, D)
    for i in range(depth):
        prm = _layer_params(i, w)
        prm.update(tri_incl=tri_incl, selk=selk, onesk=onesk)
        ya, qt, kx, vt = _mix_in(h, B, S, prm)
        ybt = _fox_attn(qt, kx, vt, B, S)
        h1, xp, tope, rank, gates, counts = _out_router(h, ya, ybt, B, S, prm, tri_strict)

        cum = counts[:, :, 0]
        before = jnp.concatenate([jnp.zeros((1, N_EXPERTS), I32), cum[:-1]], axis=0)
        plans = [_moe_dispatch_plan(cum[g] - before[g], n_blocks) for g in range(G)]
        row_offset = jnp.stack([plans[g]["pad_starts"] - before[g] for g in range(G)]).reshape(-1)
        dest = _dest_rows(row_offset, tope, rank)[:TOP_K]
        xp_rows = xp.reshape(planes * T, SC_ROW_WORDS)
        h_out = None
        for g in range(G):
            idx = (dest[:, None, g * Tg:(g + 1) * Tg] + plane_base).reshape(-1)
            src = lambda j, g=g: ((j // wpg) % planes) * (T // SC_WINDOW) + g * wpg + j % wpg
            xs = _sc_scatter_rows(xp_rows, idx, planes * P, src)
            ys = _experts(i, plans[g], xs.reshape(planes, P, SC_ROW_WORDS), w_gu, b_gu, w_dn, b_dn)
            yg = _sc_gather_rows(ys.reshape(planes * P, SC_ROW_WORDS), idx)
            h_out = _combine_ple(i, g, h1, yg.reshape(TOP_K, planes, Tg, SC_ROW_WORDS), gates, p_rows, prm,
                                 g_final.reshape(1, D), i == depth - 1, h_out)
        h = h_out
    return h.reshape(B, S, D)
```

```python
import functools

import jax
import jax.numpy as jnp
from jax import lax
from jax.experimental import pallas as pl
from jax.experimental.pallas import tpu as pltpu
from jax.experimental.pallas import tpu_sc as plsc

F32 = jnp.float32
BF16 = jnp.bfloat16
I32 = jnp.int32
U32 = jnp.uint32

CHUNK = 64
GMLP_WIDTH = 512
GMLP_GROUPS = 4
GMLP_GROUP_CH = 128
GMLP_BLOCK = 128
FOX_WIDTH = 512
FOX_HEAD_DIM = 64
FOX_HEADS = 8
N_EXPERTS = 32
TOP_K = 4
EXPERT_ROWS = 256
SWIGLU_LIMIT = 7.0
SWIGLU_ALPHA = 1.702
RMS_EPS = 1e-5
LN_EPS = 1e-5

HEAD_PAD = 128
BIAS_PARTS = 3
ROW_TILE = 512
ATTN_TILE = 512
ATTN_HEADS = 2
ACC_ROWS = 80
LOG2E = 1.4426950408889634
SC_WINDOW = 128
SC_ROW_WORDS = 256
MOE_GROUPS = 2
VMEM_LIMIT = 56 * 1024 * 1024
NEG_BIG = -1e30

_NT = (((1,), (1,)), ((), ()))
_TN = (((0,), (0,)), ((), ()))


def _rms(x, g):
    return x * lax.rsqrt(jnp.mean(x * x, axis=-1, keepdims=True) + RMS_EPS) * g


def _split3(c):
    a1 = c.astype(BF16).astype(F32)
    r1 = c - a1
    a2 = r1.astype(BF16).astype(F32)
    a3 = (r1 - a2).astype(BF16).astype(F32)
    return a1, a2, a3


def _pack_rows(x):
    n = x.shape[1] // 2
    lo = lax.bitcast_convert_type(x[:, :n].astype(BF16).astype(F32), U32) >> 16
    hi = lax.bitcast_convert_type(x[:, n:].astype(BF16).astype(F32), U32) & jnp.uint32(0xFFFF0000)
    return lo | hi


def _unpack_rows(u):
    lo = lax.bitcast_convert_type(u << 16, F32)
    hi = lax.bitcast_convert_type(u & jnp.uint32(0xFFFF0000), F32)
    return jnp.concatenate([lo, hi], axis=1)


def _store_planes(ref, u):
    for j in range(ref.shape[0]):
        ref[j] = u[:, j * SC_ROW_WORDS:(j + 1) * SC_ROW_WORDS]


def _load_planes(ref):
    return jnp.concatenate([ref[j] for j in range(ref.shape[0])], axis=1)


def _mix_in_kernel(h_ref, gmix_ref, wuv_ref, wqt_ref, wk_ref, wvt_ref, wft_ref, bf_ref, lng_ref, lnb_ref,
                   ws_ref, bs_ref, ga_ref, tri_ref, selk_ref, onesk_ref,
                   ya_ref, qt_ref, k_ref, vt_ref, carry_ref, ya_scr):
    tm = h_ref.shape[0]

    @pl.when(pl.program_id(1) == 0)
    def _():
        carry_ref[...] = jnp.zeros_like(carry_ref)

    xn = _rms(h_ref[...], gmix_ref[...]).astype(BF16)

    z = jnp.dot(xn, wuv_ref[...], preferred_element_type=F32)
    qt = lax.dot_general(wqt_ref[...], xn, _NT, preferred_element_type=F32)
    vt_ref[0] = lax.dot_general(wvt_ref[...], xn, _NT, preferred_element_type=F32).astype(BF16)
    kp = jnp.dot(xn, wk_ref[...], preferred_element_type=F32)
    ft = lax.dot_general(wft_ref[...], xn, _NT, preferred_element_type=F32) + bf_ref[...]

    z = 0.5 * z * (1.0 + lax.erf(z * (0.5 ** 0.5)))
    u = z[:, :GMLP_WIDTH]
    v = z[:, GMLP_WIDTH:]
    mu = jnp.mean(v, axis=-1, keepdims=True)
    vc = v - mu
    var = jnp.mean(vc * vc, axis=-1, keepdims=True)
    vn = (vc * lax.rsqrt(var + LN_EPS) * lng_ref[...] + lnb_ref[...]).astype(BF16)
    pi = lax.broadcasted_iota(I32, (GMLP_BLOCK, GMLP_BLOCK), 0) // CHUNK
    pj = lax.broadcasted_iota(I32, (GMLP_BLOCK, GMLP_BLOCK), 1) // CHUNK
    for g in range(GMLP_GROUPS):
        wm = jnp.where(pj <= pi, ws_ref[g], 0.0).astype(BF16)
        cs = slice(g * GMLP_GROUP_CH, (g + 1) * GMLP_GROUP_CH)
        for n in range(tm // GMLP_BLOCK):
            rs = slice(n * GMLP_BLOCK, (n + 1) * GMLP_BLOCK)
            sv = jnp.dot(wm, vn[rs, cs], preferred_element_type=F32) + bs_ref[g]
            ya_scr[rs, cs] = u[rs, cs] * sv
    ya_ref[...] = _rms(ya_scr[...], ga_ref[...]).astype(BF16)

    lf = (jnp.minimum(ft, 0.0) - jnp.log1p(jnp.exp(-jnp.abs(ft)))) * LOG2E
    zpad = jnp.zeros((32 - BIAS_PARTS * FOX_HEADS, tm), F32)
    stack = jnp.concatenate(list(_split3(lf)) + [zpad], axis=0).astype(BF16)
    cum = jnp.dot(stack, tri_ref[...], preferred_element_type=F32)
    c = cum[0:8] + cum[8:16] + cum[16:24] + carry_ref[:, 0:1]
    carry_ref[...] = jnp.broadcast_to(c[:, tm - 1:tm], carry_ref.shape)
    a1, a2, a3 = _split3(c)

    si = lax.broadcasted_iota(I32, (8, tm), 0)
    zrows = jnp.zeros((HEAD_PAD - FOX_HEAD_DIM - 8, tm), F32)
    for hd in range(FOX_HEADS):
        b1 = jnp.broadcast_to(a1[hd:hd + 1], (8, tm))
        b2 = jnp.broadcast_to(a2[hd:hd + 1], (8, tm))
        b3 = jnp.broadcast_to(a3[hd:hd + 1], (8, tm))
        ext = jnp.where(si == 0, b1, jnp.where(si == 1, b2, jnp.where(si == 2, b3,
                        jnp.where(si < 2 * BIAS_PARTS, 1.0, 0.0))))
        blk = jnp.concatenate([qt[hd * FOX_HEAD_DIM:(hd + 1) * FOX_HEAD_DIM], ext, zrows], axis=0)
        qt_ref[0, hd * HEAD_PAD:(hd + 1) * HEAD_PAD, :] = blk.astype(BF16)

    cstack = jnp.concatenate([a1, a2, a3, zpad], axis=0).astype(BF16)
    kext = lax.dot_general(cstack, selk_ref[...], _TN, preferred_element_type=F32)
    k_ref[...] = (kp + kext + onesk_ref[...]).astype(BF16)


def _mix_in(h, B, S, prm):
    T, D = h.shape
    tm = min(ROW_TILE, S)
    ns = S // tm
    H = FOX_HEADS
    const = lambda shape: pl.BlockSpec(shape, lambda b, s: (0,) * len(shape))
    return pl.pallas_call(
        _mix_in_kernel,
        grid=(B, ns),
        in_specs=[
            pl.BlockSpec((tm, D), lambda b, s: (b * ns + s, 0)),
            const((1, D)),
            const((D, 2 * GMLP_WIDTH)),
            const((FOX_WIDTH, D)),
            const((D, H * HEAD_PAD)),
            const((FOX_WIDTH, D)),
            const((H, D)),
            const((H, 1)),
            const((1, GMLP_WIDTH)),
            const((1, GMLP_WIDTH)),
            const((GMLP_GROUPS, GMLP_BLOCK, GMLP_BLOCK)),
            const((GMLP_GROUPS, GMLP_BLOCK, 1)),
            const((1, GMLP_WIDTH)),
            const((tm, tm)),
            const((32, H * HEAD_PAD)),
            const((1, H * HEAD_PAD)),
        ],
        out_specs=[
            pl.BlockSpec((tm, GMLP_WIDTH), lambda b, s: (b * ns + s, 0)),
            pl.BlockSpec((1, H * HEAD_PAD, tm), lambda b, s: (b, 0, s)),
            pl.BlockSpec((tm, H * HEAD_PAD), lambda b, s: (b * ns + s, 0)),
            pl.BlockSpec((1, FOX_WIDTH, tm), lambda b, s: (b, 0, s)),
        ],
        out_shape=[
            jax.ShapeDtypeStruct((T, GMLP_WIDTH), BF16),
            jax.ShapeDtypeStruct((B, H * HEAD_PAD, S), BF16),
            jax.ShapeDtypeStruct((T, H * HEAD_PAD), BF16),
            jax.ShapeDtypeStruct((B, FOX_WIDTH, S), BF16),
        ],
        scratch_shapes=[pltpu.VMEM((H, 128), F32), pltpu.VMEM((tm, GMLP_WIDTH), F32)],
        compiler_params=pltpu.CompilerParams(
            dimension_semantics=("arbitrary", "arbitrary"), vmem_limit_bytes=VMEM_LIMIT),
        name="mix_in",
    )(h, prm["g_mix"], prm["wuv"], prm["wqt"], prm["wk"], prm["wvt"], prm["wft"], prm["b_f"],
      prm["ln_g"], prm["ln_b"], prm["w_s"], prm["b_s"], prm["g_a"], prm["tri_incl"], prm["selk"], prm["onesk"])


def _fox_attn_kernel(qt_ref, k_ref, vt_ref, ot_ref):
    S = k_ref.shape[0]
    t = min(ATTN_TILE, S)
    nq = S // t
    ones_rows = jnp.ones((ACC_ROWS - FOX_HEAD_DIM, t), BF16)
    causal = lax.broadcasted_iota(I32, (t, t), 0) <= lax.broadcasted_iota(I32, (t, t), 1)
    pairs = [(qi, kj) for qi in range(nq) for kj in range(qi + 1)]

    def scores(qi, kj):
        out = []
        for hh in range(ATTN_HEADS):
            hs = slice(hh * HEAD_PAD, (hh + 1) * HEAD_PAD)
            s = jnp.dot(k_ref[kj * t:(kj + 1) * t, hs], qt_ref[0, hs, qi * t:(qi + 1) * t],
                        preferred_element_type=F32)
            out.append(jnp.where(causal, s, NEG_BIG) if kj == qi else s)
        return out

    s_next = scores(*pairs[0])
    state = None
    for n, (qi, kj) in enumerate(pairs):
        s_cur = s_next
        if n + 1 < len(pairs):
            s_next = scores(*pairs[n + 1])
        new_state = []
        for hh in range(ATTN_HEADS):
            s = s_cur[hh]
            v = vt_ref[0, hh * FOX_HEAD_DIM:(hh + 1) * FOX_HEAD_DIM, kj * t:(kj + 1) * t]
            v_aug = jnp.concatenate([v, ones_rows], axis=0)
            if kj == 0:
                m_new = jnp.max(s, axis=0, keepdims=True)
                acc = jnp.dot(v_aug, jnp.exp2(s - m_new).astype(BF16), preferred_element_type=F32)
            else:
                m, acc = state[hh]
                m_new = jnp.maximum(m, jnp.max(s, axis=0, keepdims=True))
                p = jnp.exp2(s - m_new).astype(BF16)
                acc = jnp.exp2(m - m_new) * acc + jnp.dot(v_aug, p, preferred_element_type=F32)
            new_state.append((m_new, acc))
        state = new_state
        if kj == qi:
            for hh in range(ATTN_HEADS):
                acc = state[hh][1]
                o = acc[:FOX_HEAD_DIM] / acc[FOX_HEAD_DIM:FOX_HEAD_DIM + 1]
                ot_ref[0, hh * FOX_HEAD_DIM:(hh + 1) * FOX_HEAD_DIM, qi * t:(qi + 1) * t] = o.astype(BF16)


def _fox_attn(qt, kx, vt, B, S):
    G = ATTN_HEADS
    return pl.pallas_call(
        _fox_attn_kernel,
        grid=(B, FOX_HEADS // G),
        in_specs=[
            pl.BlockSpec((1, G * HEAD_PAD, S), lambda b, h: (b, h, 0)),
            pl.BlockSpec((S, G * HEAD_PAD), lambda b, h: (b, h)),
            pl.BlockSpec((1, G * FOX_HEAD_DIM, S), lambda b, h: (b, h, 0)),
        ],
        out_specs=pl.BlockSpec((1, G * FOX_HEAD_DIM, S), lambda b, h: (b, h, 0)),
        out_shape=jax.ShapeDtypeStruct((B, FOX_WIDTH, S), BF16),
        compiler_params=pltpu.CompilerParams(
            dimension_semantics=("parallel", "parallel"), vmem_limit_bytes=VMEM_LIMIT),
        name="fox_attn",
    )(qt, kx, vt)


def _out_router_kernel(h_ref, ya_ref, ybt_ref, woa_ref, wob_ref, gb_ref, gmoe_ref, wrt_ref, br_ref, tri_ref,
                       h1_ref, xp_ref, tope_ref, rank_ref, gates_ref, counts_ref, carry_ref):
    tm = h_ref.shape[0]

    @pl.when(pl.program_id(0) == 0)
    def _():
        carry_ref[...] = jnp.zeros_like(carry_ref)

    yb = ybt_ref[0].astype(F32)
    ybn = (yb * lax.rsqrt(jnp.mean(yb * yb, axis=0, keepdims=True) + RMS_EPS) * gb_ref[...]).astype(BF16)
    y = jnp.dot(ya_ref[...], woa_ref[...], preferred_element_type=F32)
    y = y + lax.dot_general(ybn, wob_ref[...], _TN, preferred_element_type=F32)
    h1 = h_ref[...] + y
    h1_ref[...] = h1
    xn = _rms(h1, gmoe_ref[...])
    _store_planes(xp_ref, _pack_rows(xn))

    logits = lax.dot_general(wrt_ref[...], xn.astype(BF16), _NT, preferred_element_type=F32) + br_ref[...]
    ri = lax.broadcasted_iota(I32, (N_EXPERTS, tm), 0)
    vals, idxs = [], []
    l = logits
    for _ in range(TOP_K):
        m = jnp.max(l, axis=0, keepdims=True)
        idx = jnp.min(jnp.where(l == m, ri, N_EXPERTS), axis=0, keepdims=True)
        vals.append(m)
        idxs.append(idx)
        l = jnp.where(ri == idx, -jnp.inf, l)
    es = [jnp.exp(vk - vals[0]) for vk in vals]
    den = es[0] + es[1] + es[2] + es[3]
    zrow = jnp.zeros((8 - TOP_K, tm), F32)
    gates_ref[...] = jnp.concatenate([e / den for e in es] + [zrow], axis=0).T

    sel = jnp.zeros((N_EXPERTS, tm), F32)
    for idx in idxs:
        sel = sel + jnp.where(ri == idx, 1.0, 0.0)
    cnt = jnp.dot(sel.astype(BF16), tri_ref[...], preferred_element_type=F32) + carry_ref[:, 0:1]
    ranks = [jnp.sum(jnp.where(ri == idx, cnt, 0.0), axis=0, keepdims=True) for idx in idxs]
    zi = jnp.zeros((8 - TOP_K, tm), I32)
    rank_ref[...] = jnp.concatenate([r.astype(I32) for r in ranks] + [zi], axis=0)
    tope_ref[...] = jnp.concatenate(idxs + [zi], axis=0)
    total = carry_ref[...] + jnp.sum(sel, axis=1, keepdims=True)
    carry_ref[...] = total
    counts_ref[...] = total.astype(I32)


def _out_router(group, h, ya, ybt, B, S, prm, tri_strict):
    T, D = h.shape
    Tg = T // MOE_GROUPS
    tm = min(ROW_TILE, S)
    ns = S // tm
    E = N_EXPERTS
    planes = (D // 2) // SC_ROW_WORDS
    off = group * (Tg // tm)
    const = lambda shape: pl.BlockSpec(shape, lambda i: (0,) * len(shape))
    return pl.pallas_call(
        _out_router_kernel,
        grid=(Tg // tm,),
        in_specs=[
            pl.BlockSpec((tm, D), lambda i: (off + i, 0)),
            pl.BlockSpec((tm, GMLP_WIDTH), lambda i: (off + i, 0)),
            pl.BlockSpec((1, FOX_WIDTH, tm), lambda i: ((off + i) // ns, 0, (off + i) % ns)),
            const((GMLP_WIDTH, D)),
            const((FOX_WIDTH, D)),
            const((FOX_WIDTH, 1)),
            const((1, D)),
            const((E, D)),
            const((E, 1)),
            const((tm, tm)),
        ],
        out_specs=[
            pl.BlockSpec((tm, D), lambda i: (i, 0)),
            pl.BlockSpec((planes, tm, SC_ROW_WORDS), lambda i: (0, i, 0)),
            pl.BlockSpec((8, tm), lambda i: (0, i)),
            pl.BlockSpec((8, tm), lambda i: (0, i)),
            pl.BlockSpec((tm, 8), lambda i: (i, 0)),
            const((E, 128)),
        ],
        out_shape=[
            jax.ShapeDtypeStruct((Tg, D), F32),
            jax.ShapeDtypeStruct((planes, Tg, SC_ROW_WORDS), U32),
            jax.ShapeDtypeStruct((8, Tg), I32),
            jax.ShapeDtypeStruct((8, Tg), I32),
            jax.ShapeDtypeStruct((Tg, 8), F32),
            jax.ShapeDtypeStruct((E, 128), I32),
        ],
        scratch_shapes=[pltpu.VMEM((E, 128), F32)],
        compiler_params=pltpu.CompilerParams(
            dimension_semantics=("arbitrary",), vmem_limit_bytes=VMEM_LIMIT),
        name="out_router",
    )(h, ya, ybt, prm["woa"], prm["wob"], prm["g_b"], prm["g_moe"], prm["wrt"], prm["b_router"], tri_strict)


def _dest_kernel(ps_ref, tope_ref, rank_ref, dest_ref):
    e_sel = tope_ref[...]
    d = rank_ref[...]
    for e in range(N_EXPERTS):
        d = d + jnp.where(e_sel == e, ps_ref[e], 0)
    dest_ref[...] = d


def _dest_rows(pad_starts, tope, rank):
    T = tope.shape[1]
    tc = min(4096, T)
    return pl.pallas_call(
        _dest_kernel,
        grid_spec=pltpu.PrefetchScalarGridSpec(
            num_scalar_prefetch=1,
            grid=(T // tc,),
            in_specs=[pl.BlockSpec((8, tc), lambda i, ps: (0, i)),
                      pl.BlockSpec((8, tc), lambda i, ps: (0, i))],
            out_specs=pl.BlockSpec((8, tc), lambda i, ps: (0, i)),
        ),
        out_shape=jax.ShapeDtypeStruct((8, T), I32),
        name="dest_rows",
    )(pad_starts, tope, rank)


def _sc_mesh():
    return plsc.VectorSubcoreMesh(core_axis_name="c", subcore_axis_name="s")


def _sc_scatter_rows(x, idx, n_out):
    R, W = x.shape
    N = idx.shape[0]
    nsrc = R // SC_WINDOW

    @functools.partial(pl.kernel, out_type=jax.ShapeDtypeStruct((n_out, W), x.dtype), mesh=_sc_mesh())
    def k(x_hbm, i_hbm, o_hbm):
        def body(x_vmem, i_vmem):
            pltpu.sync_copy(x_vmem, o_hbm.at[i_vmem.at[0]])

        pltpu.emit_pipeline(
            body,
            grid=(N // SC_WINDOW,),
            in_specs=[pl.BlockSpec((SC_WINDOW, W), lambda i: (i % nsrc, 0)),
                      pl.BlockSpec((1, SC_WINDOW), lambda i: (0, i))],
            out_specs=[],
            core_axis_name=("c", "s"),
            dimension_semantics=(pltpu.PARALLEL,),
        )(x_hbm, i_hbm)

    return k(x, idx.reshape(1, N))


def _sc_gather_rows(table, idx):
    V, W = table.shape
    N = idx.shape[0]

    @functools.partial(pl.kernel, out_type=jax.ShapeDtypeStruct((N, W), table.dtype), mesh=_sc_mesh())
    def k(t_hbm, i_hbm, o_hbm):
        def body(i_vmem, o_vmem):
            pltpu.sync_copy(t_hbm.at[i_vmem.at[0]], o_vmem)

        pltpu.emit_pipeline(
            body,
            grid=(N // SC_WINDOW,),
            in_specs=[pl.BlockSpec((1, SC_WINDOW), lambda i: (0, i))],
            out_specs=[pl.BlockSpec((SC_WINDOW, W), lambda i: (i, 0))],
            core_axis_name=("c", "s"),
            dimension_semantics=(pltpu.PARALLEL,),
        )(i_hbm, o_hbm)

    return k(table, idx.reshape(1, N))


def _expert_kernel(be_ref, nv_ref, fs_ref, nxt_ref, xs_ref, wgu_hbm, bgu_ref, wdn_hbm, bdn_ref, ys_ref,
                   wgu_f32, wdn_f32, wgu_bf, wdn_bf, sem, *, layer):
    i = pl.program_id(0)
    nv = nv_ref[i]
    fs = fs_ref[i]
    F = wdn_bf.shape[0]

    def weight_copies(e, slot):
        return (pltpu.make_async_copy(wgu_hbm.at[layer, e], wgu_f32.at[slot], sem.at[slot, 0]),
                pltpu.make_async_copy(wdn_hbm.at[layer, e], wdn_f32.at[slot], sem.at[slot, 1]))

    @pl.when(i == 0)
    def _():
        for cp in weight_copies(be_ref[0], 0):
            cp.start()

    @pl.when(fs >= 0)
    def _():
        for cp in weight_copies(be_ref[i], fs):
            cp.wait()
        wgu_bf[...] = wgu_f32[fs].astype(BF16)
        wdn_bf[...] = wdn_f32[fs].astype(BF16)
        nxt = nxt_ref[i]

        @pl.when(nxt >= 0)
        def _():
            for cp in weight_copies(nxt, 1 - fs):
                cp.start()

    @pl.when(nv > 0)
    def _():
        x = _unpack_rows(_load_planes(xs_ref))
        rows = lax.broadcasted_iota(I32, x.shape, 0)
        x = jnp.where(rows < nv, x, 0.0).astype(BF16)
        hgu = jnp.dot(x, wgu_bf[...], preferred_element_type=F32) + bgu_ref[...]
        g = jnp.minimum(hgu[:, :F], SWIGLU_LIMIT)
        lin = jnp.clip(hgu[:, F:], -SWIGLU_LIMIT, SWIGLU_LIMIT)
        act = g * jax.nn.sigmoid(SWIGLU_ALPHA * g) * (lin + 1.0)
        y = jnp.dot(act.astype(BF16), wdn_bf[...], preferred_element_type=F32) + bdn_ref[...]
        _store_planes(ys_ref, _pack_rows(y))

    @pl.when(nv == 0)
    def _():
        ys_ref[...] = jnp.zeros_like(ys_ref)


def _experts(layer, plan, xs, w_gu, b_gu, w_dn, b_dn):
    planes, P, _ = xs.shape
    _, E, D, F2 = w_gu.shape
    F = F2 // 2
    nb = P // EXPERT_ROWS
    by_block = lambda i, be, nv, fs, nxt: (0, i, 0)
    by_expert = lambda i, be, nv, fs, nxt: (layer, be[i], 0, 0)
    return pl.pallas_call(
        functools.partial(_expert_kernel, layer=layer),
        grid_spec=pltpu.PrefetchScalarGridSpec(
            num_scalar_prefetch=4,
            grid=(nb,),
            in_specs=[
                pl.BlockSpec((planes, EXPERT_ROWS, SC_ROW_WORDS), by_block),
                pl.BlockSpec(memory_space=pl.ANY),
                pl.BlockSpec((None, None, 1, F2), by_expert),
                pl.BlockSpec(memory_space=pl.ANY),
                pl.BlockSpec((None, None, 1, D), by_expert),
            ],
            out_specs=pl.BlockSpec((planes, EXPERT_ROWS, SC_ROW_WORDS), by_block),
            scratch_shapes=[
                pltpu.VMEM((2, D, F2), F32), pltpu.VMEM((2, F, D), F32),
                pltpu.VMEM((D, F2), BF16), pltpu.VMEM((F, D), BF16),
                pltpu.SemaphoreType.DMA((2, 2)),
            ],
        ),
        out_shape=jax.ShapeDtypeStruct((planes, P, SC_ROW_WORDS), U32),
        compiler_params=pltpu.CompilerParams(
            dimension_semantics=("arbitrary",), vmem_limit_bytes=VMEM_LIMIT),
        name="experts",
    )(plan["block_e"], plan["block_nv"], plan["first_slot"], plan["next_e"],
      xs, w_gu, b_gu.reshape(-1, E, 1, F2), w_dn, b_dn.reshape(-1, E, 1, D))


def _combine_ple_kernel(h1_ref, yg_ref, gates_ref, gple_ref, wpg_ref, p_ref, wple_ref, gfin_ref, *rest, final):
    o_ref = rest[-1]
    h2 = h1_ref[...]
    gates = gates_ref[...]
    for k in range(TOP_K):
        h2 = h2 + gates[:, k:k + 1] * _unpack_rows(_load_planes(yg_ref.at[k]))
    xn = _rms(h2, gple_ref[...]).astype(BF16)
    gate = jax.nn.sigmoid(jnp.dot(xn, wpg_ref[...], preferred_element_type=F32))
    pe = jnp.dot(p_ref[...].astype(BF16), wple_ref[...], preferred_element_type=F32)
    h3 = h2 + gate * pe
    if final:
        h3 = _rms(h3, gfin_ref[...])
    o_ref[...] = h3


def _combine_ple(layer, group, h1, yg, gates, p, prm, g_final, final, h_prev):
    Tg, D = h1.shape
    T = Tg * MOE_GROUPS
    tm = min(ROW_TILE, Tg)
    PD = p.shape[2]
    planes = yg.shape[1]
    off = group * (Tg // tm)
    const = lambda shape: pl.BlockSpec(shape, lambda i: (0,) * len(shape))
    in_specs = [
        pl.BlockSpec((tm, D), lambda i: (i, 0)),
        pl.BlockSpec((TOP_K, planes, tm, SC_ROW_WORDS), lambda i: (0, 0, i, 0)),
        pl.BlockSpec((tm, 8), lambda i: (i, 0)),
        const((1, D)),
        const((D, D)),
        pl.BlockSpec((None, tm, PD), lambda i: (layer, off + i, 0)),
        const((PD, D)),
        const((1, D)),
    ]
    args = [h1, yg, gates, prm["g_ple"], prm["wpg"], p, prm["wple"], g_final]
    aliases = {}
    if h_prev is not None:
        in_specs.append(pl.BlockSpec(memory_space=pl.ANY))
        args.append(h_prev)
        aliases = {len(args) - 1: 0}
    return pl.pallas_call(
        functools.partial(_combine_ple_kernel, final=final),
        grid=(Tg // tm,),
        in_specs=in_specs,
        out_specs=pl.BlockSpec((tm, D), lambda i: (off + i, 0)),
        out_shape=jax.ShapeDtypeStruct((T, D), F32),
        input_output_aliases=aliases,
        compiler_params=pltpu.CompilerParams(
            dimension_semantics=("parallel",), vmem_limit_bytes=VMEM_LIMIT),
        name="combine_ple",
    )(*args)


def _layer_params(i, w):
    D = w["w_in"].shape[1]
    H = FOX_HEADS
    w_in = w["w_in"][i]
    c1 = 2 * GMLP_WIDTH
    c2, c3, c4 = c1 + FOX_WIDTH, c1 + 2 * FOX_WIDTH, c1 + 3 * FOX_WIDTH
    wk = jnp.zeros((D, H, HEAD_PAD), F32).at[:, :, :FOX_HEAD_DIM].set(
        w_in[:, c2:c3].reshape(D, H, FOX_HEAD_DIM)).reshape(D, H * HEAD_PAD)
    return {
        "g_mix": w["g_mix"][i].reshape(1, D),
        "wuv": w_in[:, :c1].astype(BF16),
        "wqt": (w_in[:, c1:c2] * (FOX_HEAD_DIM ** -0.5 * LOG2E)).T.astype(BF16),
        "wk": wk.astype(BF16),
        "wvt": w_in[:, c3:c4].T.astype(BF16),
        "wft": w_in[:, c4:].T.astype(BF16),
        "b_f": w["b_f"][i].reshape(H, 1),
        "ln_g": w["ln_g"][i].reshape(1, GMLP_WIDTH),
        "ln_b": w["ln_b"][i].reshape(1, GMLP_WIDTH),
        "w_s": w["w_s"][i],
        "b_s": w["b_s"][i].reshape(GMLP_GROUPS, GMLP_BLOCK, 1),
        "g_a": w["g_a"][i].reshape(1, GMLP_WIDTH),
        "woa": w["w_out"][i][:GMLP_WIDTH].astype(BF16),
        "wob": w["w_out"][i][GMLP_WIDTH:].astype(BF16),
        "g_b": w["g_b"][i].reshape(FOX_WIDTH, 1),
        "g_moe": w["g_moe"][i].reshape(1, D),
        "wrt": w["w_router"][i].T.astype(BF16),
        "b_router": w["b_router"][i].reshape(N_EXPERTS, 1),
        "g_ple": w["g_ple"][i].reshape(1, D),
        "wpg": w["w_ple_gate"][i].astype(BF16),
        "wple": w["w_ple"][i].astype(BF16),
    }


def _bias_constants(tm):
    H = FOX_HEADS
    j = jnp.arange(tm)
    tri_incl = (j[:, None] <= j[None, :]).astype(BF16)
    tri_strict = (j[:, None] < j[None, :]).astype(BF16)
    part = jnp.arange(BIAS_PARTS)[:, None]
    head = jnp.arange(H)[None, :]
    rows = (part * H + head).reshape(-1)
    cols = (head * HEAD_PAD + FOX_HEAD_DIM + BIAS_PARTS + part).reshape(-1)
    selk = jnp.zeros((32, H * HEAD_PAD), F32).at[rows, cols].set(-1.0).astype(BF16)
    one_cols = (jnp.arange(H)[:, None] * HEAD_PAD + FOX_HEAD_DIM + jnp.arange(BIAS_PARTS)[None, :]).reshape(-1)
    onesk = jnp.zeros((1, H * HEAD_PAD), F32).at[0, one_cols].set(1.0)
    return tri_incl, tri_strict, selk, onesk


def _moe_dispatch_plan(counts, n_blocks):
    padded = (counts + EXPERT_ROWS - 1) // EXPERT_ROWS * EXPERT_ROWS
    pad_ends = jnp.cumsum(padded)
    pad_starts = (pad_ends - padded).astype(I32)
    block_start = jnp.arange(n_blocks, dtype=I32) * EXPERT_ROWS
    n_before = jnp.sum((pad_ends[None, :] <= block_start[:, None]).astype(I32), axis=1)
    block_e = jnp.minimum(n_before, N_EXPERTS - 1).astype(I32)
    own = block_e[:, None] == jnp.arange(N_EXPERTS, dtype=I32)[None, :]
    seg_end = jnp.sum(jnp.where(own, (pad_starts + counts)[None, :], 0), axis=1)
    block_nv = jnp.clip(seg_end - block_start, 0, EXPERT_ROWS)
    block_nv = jnp.where(block_start < pad_ends[-1], block_nv, 0).astype(I32)
    prev_e = jnp.concatenate([jnp.full((1,), -1, I32), block_e[:-1]])
    first = (block_nv > 0) & (block_e != prev_e)
    run = jnp.cumsum(first.astype(I32)) - 1
    first_slot = jnp.where(first, run % 2, -1).astype(I32)
    ar = jnp.arange(N_EXPERTS, dtype=I32)
    later = (ar[None, :] > ar[:, None]) & (counts > 0)[None, :]
    next_of = jnp.min(jnp.where(later, ar[None, :], N_EXPERTS), axis=1)
    next_of = jnp.where(next_of < N_EXPERTS, next_of, -1)
    next_e = jnp.sum(jnp.where(own, next_of[None, :], 0), axis=1).astype(I32)
    return dict(pad_starts=pad_starts, block_e=block_e, block_nv=block_nv, first_slot=first_slot, next_e=next_e)


def kernel(x, p, g_mix, w_in, ln_g, ln_b, w_s, b_s, b_f, g_a, g_b, w_out, g_moe, w_router, b_router,
           w_gu, b_gu, w_dn, b_dn, g_ple, w_ple_gate, w_ple, g_final):
    B, S, D = x.shape
    T = B * S
    depth = w_in.shape[0]
    w = dict(g_mix=g_mix, w_in=w_in, ln_g=ln_g, ln_b=ln_b, w_s=w_s, b_s=b_s, b_f=b_f, g_a=g_a, g_b=g_b,
             w_out=w_out, g_moe=g_moe, w_router=w_router, b_router=b_router, w_gu=w_gu, b_gu=b_gu,
             w_dn=w_dn, b_dn=b_dn, g_ple=g_ple, w_ple_gate=w_ple_gate, w_ple=w_ple)
    tri_incl, tri_strict, selk, onesk = _bias_constants(min(ROW_TILE, S))
    G = MOE_GROUPS
    Tg = T // G
    n_blocks = -(-(Tg * TOP_K) // EXPERT_ROWS) + N_EXPERTS
    P = n_blocks * EXPERT_ROWS
    planes = (D // 2) // SC_ROW_WORDS
    plane_base = (jnp.arange(planes, dtype=I32) * P)[None, :, None]
    p_rows = p.reshape(depth, T, p.shape[-1])

    h = x.reshape(T, D)
    for i in range(depth):
        prm = _layer_params(i, w)
        prm.update(tri_incl=tri_incl, selk=selk, onesk=onesk)
        ya, qt, kx, vt = _mix_in(h, B, S, prm)
        ybt = _fox_attn(qt, kx, vt, B, S)
        routed = []
        for g in range(G):
            h1, xp, tope, rank, gates, counts = _out_router(g, h, ya, ybt, B, S, prm, tri_strict)
            plan = _moe_dispatch_plan(counts[:, 0], n_blocks)
            dest = _dest_rows(plan["pad_starts"], tope, rank)[:TOP_K]
            idx = (dest[:, None, :] + plane_base).reshape(-1)
            xs = _sc_scatter_rows(xp.reshape(planes * Tg, SC_ROW_WORDS), idx, planes * P)
            routed.append((h1, gates, plan, idx, xs))
        h_out = None
        for g, (h1, gates, plan, idx, xs) in enumerate(routed):
            ys = _experts(i, plan, xs.reshape(planes, P, SC_ROW_WORDS), w_gu, b_gu, w_dn, b_dn)
            yg = _sc_gather_rows(ys.reshape(planes * P, SC_ROW_WORDS), idx)
            h_out = _combine_ple(i, g, h1, yg.reshape(TOP_K, planes, Tg, SC_ROW_WORDS), gates, p_rows, prm,
                                 g_final.reshape(1, D), i == depth - 1, h_out)
        h = h_out
    return h.reshape(B, S, D)
```

```python
import functools

import jax
import jax.numpy as jnp
from jax import lax
from jax.experimental import pallas as pl
from jax.experimental.pallas import tpu as pltpu
from jax.experimental.pallas import tpu_sc as plsc

F32 = jnp.float32
BF16 = jnp.bfloat16
I32 = jnp.int32
U32 = jnp.uint32

CHUNK = 64
GMLP_WIDTH = 512
GMLP_GROUPS = 4
GMLP_GROUP_CH = 128
GMLP_BLOCK = 128
FOX_WIDTH = 512
FOX_HEAD_DIM = 64
FOX_HEADS = 8
N_EXPERTS = 32
TOP_K = 4
EXPERT_ROWS = 256
SWIGLU_LIMIT = 7.0
SWIGLU_ALPHA = 1.702
RMS_EPS = 1e-5
LN_EPS = 1e-5

HEAD_PAD = 128
BIAS_PARTS = 3
ROW_TILE = 512
ATTN_TILE = 512
ATTN_HEADS = 2
ACC_ROWS = 80
LOG2E = 1.4426950408889634
SC_WINDOW = 128
SC_ROW_WORDS = 256
MOE_GROUPS = 2
VMEM_LIMIT = 56 * 1024 * 1024
NEG_BIG = -1e30

_NT = (((1,), (1,)), ((), ()))
_TN = (((0,), (0,)), ((), ()))


def _rms(x, g):
    return x * lax.rsqrt(jnp.mean(x * x, axis=-1, keepdims=True) + RMS_EPS) * g


def _split3(c):
    a1 = c.astype(BF16).astype(F32)
    r1 = c - a1
    a2 = r1.astype(BF16).astype(F32)
    a3 = (r1 - a2).astype(BF16).astype(F32)
    return a1, a2, a3


def _pack_rows(x):
    n = x.shape[1] // 2
    lo = lax.bitcast_convert_type(x[:, :n].astype(BF16).astype(F32), U32) >> 16
    hi = lax.bitcast_convert_type(x[:, n:].astype(BF16).astype(F32), U32) & jnp.uint32(0xFFFF0000)
    return lo | hi


def _unpack_rows(u):
    lo = lax.bitcast_convert_type(u << 16, F32)
    hi = lax.bitcast_convert_type(u & jnp.uint32(0xFFFF0000), F32)
    return jnp.concatenate([lo, hi], axis=1)


def _store_planes(ref, u):
    for j in range(ref.shape[0]):
        ref[j] = u[:, j * SC_ROW_WORDS:(j + 1) * SC_ROW_WORDS]


def _load_planes(ref):
    return jnp.concatenate([ref[j] for j in range(ref.shape[0])], axis=1)


def _mix_in_kernel(h_ref, gmix_ref, wuv_ref, wqt_ref, wk_ref, wvt_ref, wft_ref, bf_ref, lng_ref, lnb_ref,
                   ws_ref, bs_ref, ga_ref, tri_ref, selk_ref, onesk_ref,
                   ya_ref, qt_ref, k_ref, vt_ref, carry_ref, ya_scr):
    tm = h_ref.shape[0]

    @pl.when(pl.program_id(1) == 0)
    def _():
        carry_ref[...] = jnp.zeros_like(carry_ref)

    xn = _rms(h_ref[...], gmix_ref[...]).astype(BF16)

    z = jnp.dot(xn, wuv_ref[...], preferred_element_type=F32)
    qt = lax.dot_general(wqt_ref[...], xn, _NT, preferred_element_type=F32)
    vt_ref[0] = lax.dot_general(wvt_ref[...], xn, _NT, preferred_element_type=F32).astype(BF16)
    kp = jnp.dot(xn, wk_ref[...], preferred_element_type=F32)
    ft = lax.dot_general(wft_ref[...], xn, _NT, preferred_element_type=F32) + bf_ref[...]

    z = 0.5 * z * (1.0 + lax.erf(z * (0.5 ** 0.5)))
    u = z[:, :GMLP_WIDTH]
    v = z[:, GMLP_WIDTH:]
    mu = jnp.mean(v, axis=-1, keepdims=True)
    vc = v - mu
    var = jnp.mean(vc * vc, axis=-1, keepdims=True)
    vn = (vc * lax.rsqrt(var + LN_EPS) * lng_ref[...] + lnb_ref[...]).astype(BF16)
    pi = lax.broadcasted_iota(I32, (GMLP_BLOCK, GMLP_BLOCK), 0) // CHUNK
    pj = lax.broadcasted_iota(I32, (GMLP_BLOCK, GMLP_BLOCK), 1) // CHUNK
    for g in range(GMLP_GROUPS):
        wm = jnp.where(pj <= pi, ws_ref[g], 0.0).astype(BF16)
        cs = slice(g * GMLP_GROUP_CH, (g + 1) * GMLP_GROUP_CH)
        for n in range(tm // GMLP_BLOCK):
            rs = slice(n * GMLP_BLOCK, (n + 1) * GMLP_BLOCK)
            sv = jnp.dot(wm, vn[rs, cs], preferred_element_type=F32) + bs_ref[g]
            ya_scr[rs, cs] = u[rs, cs] * sv
    ya_ref[...] = _rms(ya_scr[...], ga_ref[...]).astype(BF16)

    lf = (jnp.minimum(ft, 0.0) - jnp.log1p(jnp.exp(-jnp.abs(ft)))) * LOG2E
    zpad = jnp.zeros((32 - BIAS_PARTS * FOX_HEADS, tm), F32)
    stack = jnp.concatenate(list(_split3(lf)) + [zpad], axis=0).astype(BF16)
    cum = jnp.dot(stack, tri_ref[...], preferred_element_type=F32)
    c = cum[0:8] + cum[8:16] + cum[16:24] + carry_ref[:, 0:1]
    carry_ref[...] = jnp.broadcast_to(c[:, tm - 1:tm], carry_ref.shape)
    a1, a2, a3 = _split3(c)

    si = lax.broadcasted_iota(I32, (8, tm), 0)
    zrows = jnp.zeros((HEAD_PAD - FOX_HEAD_DIM - 8, tm), F32)
    for hd in range(FOX_HEADS):
        b1 = jnp.broadcast_to(a1[hd:hd + 1], (8, tm))
        b2 = jnp.broadcast_to(a2[hd:hd + 1], (8, tm))
        b3 = jnp.broadcast_to(a3[hd:hd + 1], (8, tm))
        ext = jnp.where(si == 0, b1, jnp.where(si == 1, b2, jnp.where(si == 2, b3,
                        jnp.where(si < 2 * BIAS_PARTS, 1.0, 0.0))))
        blk = jnp.concatenate([qt[hd * FOX_HEAD_DIM:(hd + 1) * FOX_HEAD_DIM], ext, zrows], axis=0)
        qt_ref[0, hd * HEAD_PAD:(hd + 1) * HEAD_PAD, :] = blk.astype(BF16)

    cstack = jnp.concatenate([a1, a2, a3, zpad], axis=0).astype(BF16)
    kext = lax.dot_general(cstack, selk_ref[...], _TN, preferred_element_type=F32)
    k_ref[...] = (kp + kext + onesk_ref[...]).astype(BF16)


def _mix_in(h, B, S, prm):
    T, D = h.shape
    tm = min(ROW_TILE, S)
    ns = S // tm
    H = FOX_HEADS
    const = lambda shape: pl.BlockSpec(shape, lambda b, s: (0,) * len(shape))
    return pl.pallas_call(
        _mix_in_kernel,
        grid=(B, ns),
        in_specs=[
            pl.BlockSpec((tm, D), lambda b, s: (b * ns + s, 0)),
            const((1, D)),
            const((D, 2 * GMLP_WIDTH)),
            const((FOX_WIDTH, D)),
            const((D, H * HEAD_PAD)),
            const((FOX_WIDTH, D)),
            const((H, D)),
            const((H, 1)),
            const((1, GMLP_WIDTH)),
            const((1, GMLP_WIDTH)),
            const((GMLP_GROUPS, GMLP_BLOCK, GMLP_BLOCK)),
            const((GMLP_GROUPS, GMLP_BLOCK, 1)),
            const((1, GMLP_WIDTH)),
            const((tm, tm)),
            const((32, H * HEAD_PAD)),
            const((1, H * HEAD_PAD)),
        ],
        out_specs=[
            pl.BlockSpec((tm, GMLP_WIDTH), lambda b, s: (b * ns + s, 0)),
            pl.BlockSpec((1, H * HEAD_PAD, tm), lambda b, s: (b, 0, s)),
            pl.BlockSpec((tm, H * HEAD_PAD), lambda b, s: (b * ns + s, 0)),
            pl.BlockSpec((1, FOX_WIDTH, tm), lambda b, s: (b, 0, s)),
        ],
        out_shape=[
            jax.ShapeDtypeStruct((T, GMLP_WIDTH), BF16),
            jax.ShapeDtypeStruct((B, H * HEAD_PAD, S), BF16),
            jax.ShapeDtypeStruct((T, H * HEAD_PAD), BF16),
            jax.ShapeDtypeStruct((B, FOX_WIDTH, S), BF16),
        ],
        scratch_shapes=[pltpu.VMEM((H, 128), F32), pltpu.VMEM((tm, GMLP_WIDTH), F32)],
        compiler_params=pltpu.CompilerParams(
            dimension_semantics=("arbitrary", "arbitrary"), vmem_limit_bytes=VMEM_LIMIT),
        name="mix_in",
    )(h, prm["g_mix"], prm["wuv"], prm["wqt"], prm["wk"], prm["wvt"], prm["wft"], prm["b_f"],
      prm["ln_g"], prm["ln_b"], prm["w_s"], prm["b_s"], prm["g_a"], prm["tri_incl"], prm["selk"], prm["onesk"])


def _fox_attn_kernel(qt_ref, k_ref, vt_ref, ot_ref):
    S = k_ref.shape[0]
    t = min(ATTN_TILE, S)
    nq = S // t
    ones_rows = jnp.ones((ACC_ROWS - FOX_HEAD_DIM, t), BF16)
    causal = lax.broadcasted_iota(I32, (t, t), 0) <= lax.broadcasted_iota(I32, (t, t), 1)
    pairs = [(qi, kj) for qi in range(nq) for kj in range(qi + 1)]

    def scores(qi, kj):
        out = []
        for hh in range(ATTN_HEADS):
            hs = slice(hh * HEAD_PAD, (hh + 1) * HEAD_PAD)
            s = jnp.dot(k_ref[kj * t:(kj + 1) * t, hs], qt_ref[0, hs, qi * t:(qi + 1) * t],
                        preferred_element_type=F32)
            out.append(jnp.where(causal, s, NEG_BIG) if kj == qi else s)
        return out

    s_next = scores(*pairs[0])
    state = None
    for n, (qi, kj) in enumerate(pairs):
        s_cur = s_next
        if n + 1 < len(pairs):
            s_next = scores(*pairs[n + 1])
        new_state = []
        for hh in range(ATTN_HEADS):
            s = s_cur[hh]
            v = vt_ref[0, hh * FOX_HEAD_DIM:(hh + 1) * FOX_HEAD_DIM, kj * t:(kj + 1) * t]
            v_aug = jnp.concatenate([v, ones_rows], axis=0)
            if kj == 0:
                m_new = jnp.max(s, axis=0, keepdims=True)
                acc = jnp.dot(v_aug, jnp.exp2(s - m_new).astype(BF16), preferred_element_type=F32)
            else:
                m, acc = state[hh]
                m_new = jnp.maximum(m, jnp.max(s, axis=0, keepdims=True))
                p = jnp.exp2(s - m_new).astype(BF16)
                acc = jnp.exp2(m - m_new) * acc + jnp.dot(v_aug, p, preferred_element_type=F32)
            new_state.append((m_new, acc))
        state = new_state
        if kj == qi:
            for hh in range(ATTN_HEADS):
                acc = state[hh][1]
                o = acc[:FOX_HEAD_DIM] / acc[FOX_HEAD_DIM:FOX_HEAD_DIM + 1]
                ot_ref[0, hh * FOX_HEAD_DIM:(hh + 1) * FOX_HEAD_DIM, qi * t:(qi + 1) * t] = o.astype(BF16)


def _fox_attn(qt, kx, vt, B, S):
    G = ATTN_HEADS
    return pl.pallas_call(
        _fox_attn_kernel,
        grid=(B, FOX_HEADS // G),
        in_specs=[
            pl.BlockSpec((1, G * HEAD_PAD, S), lambda b, h: (b, h, 0)),
            pl.BlockSpec((S, G * HEAD_PAD), lambda b, h: (b, h)),
            pl.BlockSpec((1, G * FOX_HEAD_DIM, S), lambda b, h: (b, h, 0)),
        ],
        out_specs=pl.BlockSpec((1, G * FOX_HEAD_DIM, S), lambda b, h: (b, h, 0)),
        out_shape=jax.ShapeDtypeStruct((B, FOX_WIDTH, S), BF16),
        compiler_params=pltpu.CompilerParams(
            dimension_semantics=("parallel", "parallel"), vmem_limit_bytes=VMEM_LIMIT),
        name="fox_attn",
    )(qt, kx, vt)


def _out_router_kernel(h_ref, ya_ref, ybt_ref, woa_ref, wob_ref, gb_ref, gmoe_ref, wrt_ref, br_ref, tri_ref,
                       after_ref, h1_ref, xp_ref, tope_ref, rank_ref, gates_ref, counts_ref, carry_ref):
    del after_ref
    tm = h_ref.shape[0]

    @pl.when(pl.program_id(0) == 0)
    def _():
        carry_ref[...] = jnp.zeros_like(carry_ref)

    yb = ybt_ref[0].astype(F32)
    ybn = (yb * lax.rsqrt(jnp.mean(yb * yb, axis=0, keepdims=True) + RMS_EPS) * gb_ref[...]).astype(BF16)
    y = jnp.dot(ya_ref[...], woa_ref[...], preferred_element_type=F32)
    y = y + lax.dot_general(ybn, wob_ref[...], _TN, preferred_element_type=F32)
    h1 = h_ref[...] + y
    h1_ref[...] = h1
    xn = _rms(h1, gmoe_ref[...])
    _store_planes(xp_ref, _pack_rows(xn))

    logits = lax.dot_general(wrt_ref[...], xn.astype(BF16), _NT, preferred_element_type=F32) + br_ref[...]
    ri = lax.broadcasted_iota(I32, (N_EXPERTS, tm), 0)
    vals, idxs = [], []
    l = logits
    for _ in range(TOP_K):
        m = jnp.max(l, axis=0, keepdims=True)
        idx = jnp.min(jnp.where(l == m, ri, N_EXPERTS), axis=0, keepdims=True)
        vals.append(m)
        idxs.append(idx)
        l = jnp.where(ri == idx, -jnp.inf, l)
    es = [jnp.exp(vk - vals[0]) for vk in vals]
    den = es[0] + es[1] + es[2] + es[3]
    zrow = jnp.zeros((8 - TOP_K, tm), F32)
    gates_ref[...] = jnp.concatenate([e / den for e in es] + [zrow], axis=0).T

    sel = jnp.zeros((N_EXPERTS, tm), F32)
    for idx in idxs:
        sel = sel + jnp.where(ri == idx, 1.0, 0.0)
    cnt = jnp.dot(sel.astype(BF16), tri_ref[...], preferred_element_type=F32) + carry_ref[:, 0:1]
    ranks = [jnp.sum(jnp.where(ri == idx, cnt, 0.0), axis=0, keepdims=True) for idx in idxs]
    zi = jnp.zeros((8 - TOP_K, tm), I32)
    rank_ref[...] = jnp.concatenate([r.astype(I32) for r in ranks] + [zi], axis=0)
    tope_ref[...] = jnp.concatenate(idxs + [zi], axis=0)
    total = carry_ref[...] + jnp.sum(sel, axis=1, keepdims=True)
    carry_ref[...] = total
    counts_ref[...] = total.astype(I32)


def _out_router(group, h, ya, ybt, B, S, prm, tri_strict, after):
    T, D = h.shape
    Tg = T // MOE_GROUPS
    tm = min(ROW_TILE, S)
    ns = S // tm
    E = N_EXPERTS
    planes = (D // 2) // SC_ROW_WORDS
    off = group * (Tg // tm)
    const = lambda shape: pl.BlockSpec(shape, lambda i: (0,) * len(shape))
    return pl.pallas_call(
        _out_router_kernel,
        grid=(Tg // tm,),
        in_specs=[
            pl.BlockSpec((tm, D), lambda i: (off + i, 0)),
            pl.BlockSpec((tm, GMLP_WIDTH), lambda i: (off + i, 0)),
            pl.BlockSpec((1, FOX_WIDTH, tm), lambda i: ((off + i) // ns, 0, (off + i) % ns)),
            const((GMLP_WIDTH, D)),
            const((FOX_WIDTH, D)),
            const((FOX_WIDTH, 1)),
            const((1, D)),
            const((E, D)),
            const((E, 1)),
            const((tm, tm)),
            pl.BlockSpec(memory_space=pl.ANY),
        ],
        out_specs=[
            pl.BlockSpec((tm, D), lambda i: (i, 0)),
            pl.BlockSpec((planes, tm, SC_ROW_WORDS), lambda i: (0, i, 0)),
            pl.BlockSpec((8, tm), lambda i: (0, i)),
            pl.BlockSpec((8, tm), lambda i: (0, i)),
            pl.BlockSpec((tm, 8), lambda i: (i, 0)),
            const((E, 128)),
        ],
        out_shape=[
            jax.ShapeDtypeStruct((Tg, D), F32),
            jax.ShapeDtypeStruct((planes, Tg, SC_ROW_WORDS), U32),
            jax.ShapeDtypeStruct((8, Tg), I32),
            jax.ShapeDtypeStruct((8, Tg), I32),
            jax.ShapeDtypeStruct((Tg, 8), F32),
            jax.ShapeDtypeStruct((E, 128), I32),
        ],
        scratch_shapes=[pltpu.VMEM((E, 128), F32)],
        compiler_params=pltpu.CompilerParams(
            dimension_semantics=("arbitrary",), vmem_limit_bytes=VMEM_LIMIT),
        name="out_router",
    )(h, ya, ybt, prm["woa"], prm["wob"], prm["g_b"], prm["g_moe"], prm["wrt"], prm["b_router"], tri_strict, after)


def _dest_kernel(ps_ref, tope_ref, rank_ref, dest_ref):
    e_sel = tope_ref[...]
    d = rank_ref[...]
    for e in range(N_EXPERTS):
        d = d + jnp.where(e_sel == e, ps_ref[e], 0)
    dest_ref[...] = d


def _dest_rows(pad_starts, tope, rank):
    T = tope.shape[1]
    tc = min(4096, T)
    return pl.pallas_call(
        _dest_kernel,
        grid_spec=pltpu.PrefetchScalarGridSpec(
            num_scalar_prefetch=1,
            grid=(T // tc,),
            in_specs=[pl.BlockSpec((8, tc), lambda i, ps: (0, i)),
                      pl.BlockSpec((8, tc), lambda i, ps: (0, i))],
            out_specs=pl.BlockSpec((8, tc), lambda i, ps: (0, i)),
        ),
        out_shape=jax.ShapeDtypeStruct((8, T), I32),
        name="dest_rows",
    )(pad_starts, tope, rank)


def _sc_mesh():
    return plsc.VectorSubcoreMesh(core_axis_name="c", subcore_axis_name="s")


def _sc_scatter_rows(x, idx, n_out):
    R, W = x.shape
    N = idx.shape[0]
    nsrc = R // SC_WINDOW

    @functools.partial(pl.kernel, out_type=jax.ShapeDtypeStruct((n_out, W), x.dtype), mesh=_sc_mesh())
    def k(x_hbm, i_hbm, o_hbm):
        def body(x_vmem, i_vmem):
            pltpu.sync_copy(x_vmem, o_hbm.at[i_vmem.at[0]])

        pltpu.emit_pipeline(
            body,
            grid=(N // SC_WINDOW,),
            in_specs=[pl.BlockSpec((SC_WINDOW, W), lambda i: (i % nsrc, 0)),
                      pl.BlockSpec((1, SC_WINDOW), lambda i: (0, i))],
            out_specs=[],
            core_axis_name=("c", "s"),
            dimension_semantics=(pltpu.PARALLEL,),
        )(x_hbm, i_hbm)

    return k(x, idx.reshape(1, N))


def _sc_gather_rows(table, idx):
    V, W = table.shape
    N = idx.shape[0]

    @functools.partial(pl.kernel, out_type=jax.ShapeDtypeStruct((N, W), table.dtype), mesh=_sc_mesh())
    def k(t_hbm, i_hbm, o_hbm):
        def body(i_vmem, o_vmem):
            pltpu.sync_copy(t_hbm.at[i_vmem.at[0]], o_vmem)

        pltpu.emit_pipeline(
            body,
            grid=(N // SC_WINDOW,),
            in_specs=[pl.BlockSpec((1, SC_WINDOW), lambda i: (0, i))],
            out_specs=[pl.BlockSpec((SC_WINDOW, W), lambda i: (i, 0))],
            core_axis_name=("c", "s"),
            dimension_semantics=(pltpu.PARALLEL,),
        )(i_hbm, o_hbm)

    return k(table, idx.reshape(1, N))


def _expert_kernel(be_ref, nv_ref, fs_ref, nxt_ref, xs_ref, wgu_hbm, bgu_ref, wdn_hbm, bdn_ref, ys_ref,
                   wgu_f32, wdn_f32, wgu_bf, wdn_bf, sem, *, layer):
    i = pl.program_id(0)
    nv = nv_ref[i]
    fs = fs_ref[i]
    F = wdn_bf.shape[0]

    def weight_copies(e, slot):
        return (pltpu.make_async_copy(wgu_hbm.at[layer, e], wgu_f32.at[slot], sem.at[slot, 0]),
                pltpu.make_async_copy(wdn_hbm.at[layer, e], wdn_f32.at[slot], sem.at[slot, 1]))

    @pl.when(i == 0)
    def _():
        for cp in weight_copies(be_ref[0], 0):
            cp.start()

    @pl.when(fs >= 0)
    def _():
        for cp in weight_copies(be_ref[i], fs):
            cp.wait()
        wgu_bf[...] = wgu_f32[fs].astype(BF16)
        wdn_bf[...] = wdn_f32[fs].astype(BF16)
        nxt = nxt_ref[i]

        @pl.when(nxt >= 0)
        def _():
            for cp in weight_copies(nxt, 1 - fs):
                cp.start(priority=1)

    @pl.when(nv > 0)
    def _():
        x = _unpack_rows(_load_planes(xs_ref))
        rows = lax.broadcasted_iota(I32, x.shape, 0)
        x = jnp.where(rows < nv, x, 0.0).astype(BF16)
        hgu = jnp.dot(x, wgu_bf[...], preferred_element_type=F32) + bgu_ref[...]
        g = jnp.minimum(hgu[:, :F], SWIGLU_LIMIT)
        lin = jnp.clip(hgu[:, F:], -SWIGLU_LIMIT, SWIGLU_LIMIT)
        act = g * jax.nn.sigmoid(SWIGLU_ALPHA * g) * (lin + 1.0)
        y = jnp.dot(act.astype(BF16), wdn_bf[...], preferred_element_type=F32) + bdn_ref[...]
        _store_planes(ys_ref, _pack_rows(y))

    @pl.when(nv == 0)
    def _():
        ys_ref[...] = jnp.zeros_like(ys_ref)


def _experts(layer, plan, xs, w_gu, b_gu, w_dn, b_dn):
    planes, P, _ = xs.shape
    _, E, D, F2 = w_gu.shape
    F = F2 // 2
    nb = P // EXPERT_ROWS
    by_block = lambda i, be, nv, fs, nxt: (0, i, 0)
    by_expert = lambda i, be, nv, fs, nxt: (layer, be[i], 0, 0)
    return pl.pallas_call(
        functools.partial(_expert_kernel, layer=layer),
        grid_spec=pltpu.PrefetchScalarGridSpec(
            num_scalar_prefetch=4,
            grid=(nb,),
            in_specs=[
                pl.BlockSpec((planes, EXPERT_ROWS, SC_ROW_WORDS), by_block),
                pl.BlockSpec(memory_space=pl.ANY),
                pl.BlockSpec((None, None, 1, F2), by_expert),
                pl.BlockSpec(memory_space=pl.ANY),
                pl.BlockSpec((None, None, 1, D), by_expert),
            ],
            out_specs=pl.BlockSpec((planes, EXPERT_ROWS, SC_ROW_WORDS), by_block),
            scratch_shapes=[
                pltpu.VMEM((2, D, F2), F32), pltpu.VMEM((2, F, D), F32),
                pltpu.VMEM((D, F2), BF16), pltpu.VMEM((F, D), BF16),
                pltpu.SemaphoreType.DMA((2, 2)),
            ],
        ),
        out_shape=jax.ShapeDtypeStruct((planes, P, SC_ROW_WORDS), U32),
        compiler_params=pltpu.CompilerParams(
            dimension_semantics=("arbitrary",), vmem_limit_bytes=VMEM_LIMIT),
        name="experts",
    )(plan["block_e"], plan["block_nv"], plan["first_slot"], plan["next_e"],
      xs, w_gu, b_gu.reshape(-1, E, 1, F2), w_dn, b_dn.reshape(-1, E, 1, D))


def _combine_ple_kernel(h1_ref, yg_ref, gates_ref, gple_ref, wpg_ref, p_ref, wple_ref, gfin_ref, *rest, final):
    o_ref = rest[-1]
    h2 = h1_ref[...]
    gates = gates_ref[...]
    for k in range(TOP_K):
        h2 = h2 + gates[:, k:k + 1] * _unpack_rows(_load_planes(yg_ref.at[k]))
    xn = _rms(h2, gple_ref[...]).astype(BF16)
    gate = jax.nn.sigmoid(jnp.dot(xn, wpg_ref[...], preferred_element_type=F32))
    pe = jnp.dot(p_ref[...].astype(BF16), wple_ref[...], preferred_element_type=F32)
    h3 = h2 + gate * pe
    if final:
        h3 = _rms(h3, gfin_ref[...])
    o_ref[...] = h3


def _combine_ple(layer, group, h1, yg, gates, p, prm, g_final, final, h_prev):
    Tg, D = h1.shape
    T = Tg * MOE_GROUPS
    tm = min(ROW_TILE, Tg)
    PD = p.shape[2]
    planes = yg.shape[1]
    off = group * (Tg // tm)
    const = lambda shape: pl.BlockSpec(shape, lambda i: (0,) * len(shape))
    in_specs = [
        pl.BlockSpec((tm, D), lambda i: (i, 0)),
        pl.BlockSpec((TOP_K, planes, tm, SC_ROW_WORDS), lambda i: (0, 0, i, 0)),
        pl.BlockSpec((tm, 8), lambda i: (i, 0)),
        const((1, D)),
        const((D, D)),
        pl.BlockSpec((None, tm, PD), lambda i: (layer, off + i, 0)),
        const((PD, D)),
        const((1, D)),
    ]
    args = [h1, yg, gates, prm["g_ple"], prm["wpg"], p, prm["wple"], g_final]
    aliases = {}
    if h_prev is not None:
        in_specs.append(pl.BlockSpec(memory_space=pl.ANY))
        args.append(h_prev)
        aliases = {len(args) - 1: 0}
    return pl.pallas_call(
        functools.partial(_combine_ple_kernel, final=final),
        grid=(Tg // tm,),
        in_specs=in_specs,
        out_specs=pl.BlockSpec((tm, D), lambda i: (off + i, 0)),
        out_shape=jax.ShapeDtypeStruct((T, D), F32),
        input_output_aliases=aliases,
        compiler_params=pltpu.CompilerParams(
            dimension_semantics=("parallel",), vmem_limit_bytes=VMEM_LIMIT),
        name="combine_ple",
    )(*args)


def _layer_params(i, w):
    D = w["w_in"].shape[1]
    H = FOX_HEADS
    w_in = w["w_in"][i]
    c1 = 2 * GMLP_WIDTH
    c2, c3, c4 = c1 + FOX_WIDTH, c1 + 2 * FOX_WIDTH, c1 + 3 * FOX_WIDTH
    wk = jnp.zeros((D, H, HEAD_PAD), F32).at[:, :, :FOX_HEAD_DIM].set(
        w_in[:, c2:c3].reshape(D, H, FOX_HEAD_DIM)).reshape(D, H * HEAD_PAD)
    return {
        "g_mix": w["g_mix"][i].reshape(1, D),
        "wuv": w_in[:, :c1].astype(BF16),
        "wqt": (w_in[:, c1:c2] * (FOX_HEAD_DIM ** -0.5 * LOG2E)).T.astype(BF16),
        "wk": wk.astype(BF16),
        "wvt": w_in[:, c3:c4].T.astype(BF16),
        "wft": w_in[:, c4:].T.astype(BF16),
        "b_f": w["b_f"][i].reshape(H, 1),
        "ln_g": w["ln_g"][i].reshape(1, GMLP_WIDTH),
        "ln_b": w["ln_b"][i].reshape(1, GMLP_WIDTH),
        "w_s": w["w_s"][i],
        "b_s": w["b_s"][i].reshape(GMLP_GROUPS, GMLP_BLOCK, 1),
        "g_a": w["g_a"][i].reshape(1, GMLP_WIDTH),
        "woa": w["w_out"][i][:GMLP_WIDTH].astype(BF16),
        "wob": w["w_out"][i][GMLP_WIDTH:].astype(BF16),
        "g_b": w["g_b"][i].reshape(FOX_WIDTH, 1),
        "g_moe": w["g_moe"][i].reshape(1, D),
        "wrt": w["w_router"][i].T.astype(BF16),
        "b_router": w["b_router"][i].reshape(N_EXPERTS, 1),
        "g_ple": w["g_ple"][i].reshape(1, D),
        "wpg": w["w_ple_gate"][i].astype(BF16),
        "wple": w["w_ple"][i].astype(BF16),
    }


def _bias_constants(tm):
    H = FOX_HEADS
    j = jnp.arange(tm)
    tri_incl = (j[:, None] <= j[None, :]).astype(BF16)
    tri_strict = (j[:, None] < j[None, :]).astype(BF16)
    part = jnp.arange(BIAS_PARTS)[:, None]
    head = jnp.arange(H)[None, :]
    rows = (part * H + head).reshape(-1)
    cols = (head * HEAD_PAD + FOX_HEAD_DIM + BIAS_PARTS + part).reshape(-1)
    selk = jnp.zeros((32, H * HEAD_PAD), F32).at[rows, cols].set(-1.0).astype(BF16)
    one_cols = (jnp.arange(H)[:, None] * HEAD_PAD + FOX_HEAD_DIM + jnp.arange(BIAS_PARTS)[None, :]).reshape(-1)
    onesk = jnp.zeros((1, H * HEAD_PAD), F32).at[0, one_cols].set(1.0)
    return tri_incl, tri_strict, selk, onesk


def _moe_dispatch_plan(counts, n_blocks):
    padded = (counts + EXPERT_ROWS - 1) // EXPERT_ROWS * EXPERT_ROWS
    pad_ends = jnp.cumsum(padded)
    pad_starts = (pad_ends - padded).astype(I32)
    block_start = jnp.arange(n_blocks, dtype=I32) * EXPERT_ROWS
    n_before = jnp.sum((pad_ends[None, :] <= block_start[:, None]).astype(I32), axis=1)
    block_e = jnp.minimum(n_before, N_EXPERTS - 1).astype(I32)
    own = block_e[:, None] == jnp.arange(N_EXPERTS, dtype=I32)[None, :]
    seg_end = jnp.sum(jnp.where(own, (pad_starts + counts)[None, :], 0), axis=1)
    block_nv = jnp.clip(seg_end - block_start, 0, EXPERT_ROWS)
    block_nv = jnp.where(block_start < pad_ends[-1], block_nv, 0).astype(I32)
    prev_e = jnp.concatenate([jnp.full((1,), -1, I32), block_e[:-1]])
    first = (block_nv > 0) & (block_e != prev_e)
    run = jnp.cumsum(first.astype(I32)) - 1
    first_slot = jnp.where(first, run % 2, -1).astype(I32)
    ar = jnp.arange(N_EXPERTS, dtype=I32)
    later = (ar[None, :] > ar[:, None]) & (counts > 0)[None, :]
    next_of = jnp.min(jnp.where(later, ar[None, :], N_EXPERTS), axis=1)
    next_of = jnp.where(next_of < N_EXPERTS, next_of, -1)
    next_e = jnp.sum(jnp.where(own, next_of[None, :], 0), axis=1).astype(I32)
    return dict(pad_starts=pad_starts, block_e=block_e, block_nv=block_nv, first_slot=first_slot, next_e=next_e)


def kernel(x, p, g_mix, w_in, ln_g, ln_b, w_s, b_s, b_f, g_a, g_b, w_out, g_moe, w_router, b_router,
           w_gu, b_gu, w_dn, b_dn, g_ple, w_ple_gate, w_ple, g_final):
    B, S, D = x.shape
    T = B * S
    depth = w_in.shape[0]
    w = dict(g_mix=g_mix, w_in=w_in, ln_g=ln_g, ln_b=ln_b, w_s=w_s, b_s=b_s, b_f=b_f, g_a=g_a, g_b=g_b,
             w_out=w_out, g_moe=g_moe, w_router=w_router, b_router=b_router, w_gu=w_gu, b_gu=b_gu,
             w_dn=w_dn, b_dn=b_dn, g_ple=g_ple, w_ple_gate=w_ple_gate, w_ple=w_ple)
    tri_incl, tri_strict, selk, onesk = _bias_constants(min(ROW_TILE, S))
    G = MOE_GROUPS
    Tg = T // G
    n_blocks = -(-(Tg * TOP_K) // EXPERT_ROWS) + N_EXPERTS
    P = n_blocks * EXPERT_ROWS
    planes = (D // 2) // SC_ROW_WORDS
    plane_base = (jnp.arange(planes, dtype=I32) * P)[None, :, None]
    p_rows = p.reshape(depth, T, p.shape[-1])

    h = x.reshape(T, D)
    for i in range(depth):
        prm = _layer_params(i, w)
        prm.update(tri_incl=tri_incl, selk=selk, onesk=onesk)
        ya, qt, kx, vt = _mix_in(h, B, S, prm)
        ybt = _fox_attn(qt, kx, vt, B, S)
        routed = []
        idx = jnp.zeros((8, 128), I32)
        for g in range(G):
            h1, xp, tope, rank, gates, counts = _out_router(g, h, ya, ybt, B, S, prm, tri_strict, idx)
            plan = _moe_dispatch_plan(counts[:, 0], n_blocks)
            dest = _dest_rows(plan["pad_starts"], tope, rank)[:TOP_K]
            idx = (dest[:, None, :] + plane_base).reshape(-1)
            xs = _sc_scatter_rows(xp.reshape(planes * Tg, SC_ROW_WORDS), idx, planes * P)
            routed.append((h1, gates, plan, idx, xs))
        h_out = None
        for g, (h1, gates, plan, idx, xs) in enumerate(routed):
            ys = _experts(i, plan, xs.reshape(planes, P, SC_ROW_WORDS), w_gu, b_gu, w_dn, b_dn)
            yg = _sc_gather_rows(ys.reshape(planes * P, SC_ROW_WORDS), idx)
            h_out = _combine_ple(i, g, h1, yg.reshape(TOP_K, planes, Tg, SC_ROW_WORDS), gates, p_rows, prm,
                                 g_final.reshape(1, D), i == depth - 1, h_out)
        h = h_out
    return h.reshape(B, S, D)
```

```python
import functools

import jax
import jax.numpy as jnp
from jax import lax
from jax.experimental import pallas as pl
from jax.experimental.pallas import tpu as pltpu
from jax.experimental.pallas import tpu_sc as plsc

F32 = jnp.float32
BF16 = jnp.bfloat16
I32 = jnp.int32
U32 = jnp.uint32

CHUNK = 64
GMLP_WIDTH = 512
GMLP_GROUPS = 4
GMLP_GROUP_CH = 128
GMLP_BLOCK = 128
FOX_WIDTH = 512
FOX_HEAD_DIM = 64
FOX_HEADS = 8
N_EXPERTS = 32
TOP_K = 4
EXPERT_ROWS = 256
SWIGLU_LIMIT = 7.0
SWIGLU_ALPHA = 1.702
RMS_EPS = 1e-5
LN_EPS = 1e-5

HEAD_PAD = 128
BIAS_PARTS = 3
ROW_TILE = 512
ATTN_TILE = 512
ATTN_HEADS = 2
ACC_ROWS = 80
LOG2E = 1.4426950408889634
SC_WINDOW = 128
SC_ROW_WORDS = 256
EXPERT_STEP_BLOCKS = 4
MOE_GROUPS = 2
VMEM_LIMIT = 56 * 1024 * 1024
NEG_BIG = -1e30

_NT = (((1,), (1,)), ((), ()))
_TN = (((0,), (0,)), ((), ()))


def _rms(x, g):
    return x * lax.rsqrt(jnp.mean(x * x, axis=-1, keepdims=True) + RMS_EPS) * g


def _split3(c):
    a1 = c.astype(BF16).astype(F32)
    r1 = c - a1
    a2 = r1.astype(BF16).astype(F32)
    a3 = (r1 - a2).astype(BF16).astype(F32)
    return a1, a2, a3


def _pack_rows(x):
    n = x.shape[1] // 2
    lo = lax.bitcast_convert_type(x[:, :n].astype(BF16).astype(F32), U32) >> 16
    hi = lax.bitcast_convert_type(x[:, n:].astype(BF16).astype(F32), U32) & jnp.uint32(0xFFFF0000)
    return lo | hi


def _unpack_rows(u):
    lo = lax.bitcast_convert_type(u << 16, F32)
    hi = lax.bitcast_convert_type(u & jnp.uint32(0xFFFF0000), F32)
    return jnp.concatenate([lo, hi], axis=1)


def _store_planes(ref, u):
    for j in range(ref.shape[0]):
        ref[j] = u[:, j * SC_ROW_WORDS:(j + 1) * SC_ROW_WORDS]


def _load_planes(ref):
    return jnp.concatenate([ref[j] for j in range(ref.shape[0])], axis=1)


def _mix_in_kernel(h_ref, gmix_ref, wuv_ref, wqt_ref, wk_ref, wvt_ref, wft_ref, bf_ref, lng_ref, lnb_ref,
                   ws_ref, bs_ref, ga_ref, tri_ref, selk_ref, onesk_ref,
                   ya_ref, qt_ref, k_ref, vt_ref, carry_ref, ya_scr):
    tm = h_ref.shape[0]

    @pl.when(pl.program_id(1) == 0)
    def _():
        carry_ref[...] = jnp.zeros_like(carry_ref)

    xn = _rms(h_ref[...], gmix_ref[...]).astype(BF16)

    z = jnp.dot(xn, wuv_ref[...], preferred_element_type=F32)
    qt = lax.dot_general(wqt_ref[...], xn, _NT, preferred_element_type=F32)
    vt_ref[0] = lax.dot_general(wvt_ref[...], xn, _NT, preferred_element_type=F32).astype(BF16)
    kp = jnp.dot(xn, wk_ref[...], preferred_element_type=F32)
    ft = lax.dot_general(wft_ref[...], xn, _NT, preferred_element_type=F32) + bf_ref[...]

    z = 0.5 * z * (1.0 + lax.erf(z * (0.5 ** 0.5)))
    u = z[:, :GMLP_WIDTH]
    v = z[:, GMLP_WIDTH:]
    mu = jnp.mean(v, axis=-1, keepdims=True)
    vc = v - mu
    var = jnp.mean(vc * vc, axis=-1, keepdims=True)
    vn = (vc * lax.rsqrt(var + LN_EPS) * lng_ref[...] + lnb_ref[...]).astype(BF16)
    pi = lax.broadcasted_iota(I32, (GMLP_BLOCK, GMLP_BLOCK), 0) // CHUNK
    pj = lax.broadcasted_iota(I32, (GMLP_BLOCK, GMLP_BLOCK), 1) // CHUNK
    for g in range(GMLP_GROUPS):
        wm = jnp.where(pj <= pi, ws_ref[g], 0.0).astype(BF16)
        cs = slice(g * GMLP_GROUP_CH, (g + 1) * GMLP_GROUP_CH)
        for n in range(tm // GMLP_BLOCK):
            rs = slice(n * GMLP_BLOCK, (n + 1) * GMLP_BLOCK)
            sv = jnp.dot(wm, vn[rs, cs], preferred_element_type=F32) + bs_ref[g]
            ya_scr[rs, cs] = u[rs, cs] * sv
    ya_ref[...] = _rms(ya_scr[...], ga_ref[...]).astype(BF16)

    lf = (jnp.minimum(ft, 0.0) - jnp.log1p(jnp.exp(-jnp.abs(ft)))) * LOG2E
    zpad = jnp.zeros((32 - BIAS_PARTS * FOX_HEADS, tm), F32)
    stack = jnp.concatenate(list(_split3(lf)) + [zpad], axis=0).astype(BF16)
    cum = jnp.dot(stack, tri_ref[...], preferred_element_type=F32)
    c = cum[0:8] + cum[8:16] + cum[16:24] + carry_ref[:, 0:1]
    carry_ref[...] = jnp.broadcast_to(c[:, tm - 1:tm], carry_ref.shape)
    a1, a2, a3 = _split3(c)

    si = lax.broadcasted_iota(I32, (8, tm), 0)
    zrows = jnp.zeros((HEAD_PAD - FOX_HEAD_DIM - 8, tm), F32)
    for hd in range(FOX_HEADS):
        b1 = jnp.broadcast_to(a1[hd:hd + 1], (8, tm))
        b2 = jnp.broadcast_to(a2[hd:hd + 1], (8, tm))
        b3 = jnp.broadcast_to(a3[hd:hd + 1], (8, tm))
        ext = jnp.where(si == 0, b1, jnp.where(si == 1, b2, jnp.where(si == 2, b3,
                        jnp.where(si < 2 * BIAS_PARTS, 1.0, 0.0))))
        blk = jnp.concatenate([qt[hd * FOX_HEAD_DIM:(hd + 1) * FOX_HEAD_DIM], ext, zrows], axis=0)
        qt_ref[0, hd * HEAD_PAD:(hd + 1) * HEAD_PAD, :] = blk.astype(BF16)

    cstack = jnp.concatenate([a1, a2, a3, zpad], axis=0).astype(BF16)
    kext = lax.dot_general(cstack, selk_ref[...], _TN, preferred_element_type=F32)
    k_ref[...] = (kp + kext + onesk_ref[...]).astype(BF16)


def _mix_in(h, B, S, prm):
    T, D = h.shape
    tm = min(ROW_TILE, S)
    ns = S // tm
    H = FOX_HEADS
    const = lambda shape: pl.BlockSpec(shape, lambda b, s: (0,) * len(shape))
    return pl.pallas_call(
        _mix_in_kernel,
        grid=(B, ns),
        in_specs=[
            pl.BlockSpec((tm, D), lambda b, s: (b * ns + s, 0)),
            const((1, D)),
            const((D, 2 * GMLP_WIDTH)),
            const((FOX_WIDTH, D)),
            const((D, H * HEAD_PAD)),
            const((FOX_WIDTH, D)),
            const((H, D)),
            const((H, 1)),
            const((1, GMLP_WIDTH)),
            const((1, GMLP_WIDTH)),
            const((GMLP_GROUPS, GMLP_BLOCK, GMLP_BLOCK)),
            const((GMLP_GROUPS, GMLP_BLOCK, 1)),
            const((1, GMLP_WIDTH)),
            const((tm, tm)),
            const((32, H * HEAD_PAD)),
            const((1, H * HEAD_PAD)),
        ],
        out_specs=[
            pl.BlockSpec((tm, GMLP_WIDTH), lambda b, s: (b * ns + s, 0)),
            pl.BlockSpec((1, H * HEAD_PAD, tm), lambda b, s: (b, 0, s)),
            pl.BlockSpec((tm, H * HEAD_PAD), lambda b, s: (b * ns + s, 0)),
            pl.BlockSpec((1, FOX_WIDTH, tm), lambda b, s: (b, 0, s)),
        ],
        out_shape=[
            jax.ShapeDtypeStruct((T, GMLP_WIDTH), BF16),
            jax.ShapeDtypeStruct((B, H * HEAD_PAD, S), BF16),
            jax.ShapeDtypeStruct((T, H * HEAD_PAD), BF16),
            jax.ShapeDtypeStruct((B, FOX_WIDTH, S), BF16),
        ],
        scratch_shapes=[pltpu.VMEM((H, 128), F32), pltpu.VMEM((tm, GMLP_WIDTH), F32)],
        compiler_params=pltpu.CompilerParams(
            dimension_semantics=("arbitrary", "arbitrary"), vmem_limit_bytes=VMEM_LIMIT),
        name="mix_in",
    )(h, prm["g_mix"], prm["wuv"], prm["wqt"], prm["wk"], prm["wvt"], prm["wft"], prm["b_f"],
      prm["ln_g"], prm["ln_b"], prm["w_s"], prm["b_s"], prm["g_a"], prm["tri_incl"], prm["selk"], prm["onesk"])


def _fox_attn_kernel(qt_ref, k_ref, vt_ref, ot_ref):
    S = k_ref.shape[0]
    t = min(ATTN_TILE, S)
    nq = S // t
    ones_rows = jnp.ones((ACC_ROWS - FOX_HEAD_DIM, t), BF16)
    causal = lax.broadcasted_iota(I32, (t, t), 0) <= lax.broadcasted_iota(I32, (t, t), 1)
    pairs = [(qi, kj) for qi in range(nq) for kj in range(qi + 1)]

    def scores(qi, kj):
        out = []
        for hh in range(ATTN_HEADS):
            hs = slice(hh * HEAD_PAD, (hh + 1) * HEAD_PAD)
            s = jnp.dot(k_ref[kj * t:(kj + 1) * t, hs], qt_ref[0, hs, qi * t:(qi + 1) * t],
                        preferred_element_type=F32)
            out.append(jnp.where(causal, s, NEG_BIG) if kj == qi else s)
        return out

    s_next = scores(*pairs[0])
    state = None
    for n, (qi, kj) in enumerate(pairs):
        s_cur = s_next
        if n + 1 < len(pairs):
            s_next = scores(*pairs[n + 1])
        new_state = []
        for hh in range(ATTN_HEADS):
            s = s_cur[hh]
            v = vt_ref[0, hh * FOX_HEAD_DIM:(hh + 1) * FOX_HEAD_DIM, kj * t:(kj + 1) * t]
            v_aug = jnp.concatenate([v, ones_rows], axis=0)
            if kj == 0:
                m_new = jnp.max(s, axis=0, keepdims=True)
                acc = jnp.dot(v_aug, jnp.exp2(s - m_new).astype(BF16), preferred_element_type=F32)
            else:
                m, acc = state[hh]
                m_new = jnp.maximum(m, jnp.max(s, axis=0, keepdims=True))
                p = jnp.exp2(s - m_new).astype(BF16)
                acc = jnp.exp2(m - m_new) * acc + jnp.dot(v_aug, p, preferred_element_type=F32)
            new_state.append((m_new, acc))
        state = new_state
        if kj == qi:
            for hh in range(ATTN_HEADS):
                acc = state[hh][1]
                o = acc[:FOX_HEAD_DIM] / acc[FOX_HEAD_DIM:FOX_HEAD_DIM + 1]
                ot_ref[0, hh * FOX_HEAD_DIM:(hh + 1) * FOX_HEAD_DIM, qi * t:(qi + 1) * t] = o.astype(BF16)


def _fox_attn(qt, kx, vt, B, S):
    G = ATTN_HEADS
    return pl.pallas_call(
        _fox_attn_kernel,
        grid=(B, FOX_HEADS // G),
        in_specs=[
            pl.BlockSpec((1, G * HEAD_PAD, S), lambda b, h: (b, h, 0)),
            pl.BlockSpec((S, G * HEAD_PAD), lambda b, h: (b, h)),
            pl.BlockSpec((1, G * FOX_HEAD_DIM, S), lambda b, h: (b, h, 0)),
        ],
        out_specs=pl.BlockSpec((1, G * FOX_HEAD_DIM, S), lambda b, h: (b, h, 0)),
        out_shape=jax.ShapeDtypeStruct((B, FOX_WIDTH, S), BF16),
        compiler_params=pltpu.CompilerParams(
            dimension_semantics=("parallel", "parallel"), vmem_limit_bytes=VMEM_LIMIT),
        name="fox_attn",
    )(qt, kx, vt)


def _out_router_kernel(h_ref, ya_ref, ybt_ref, woa_ref, wob_ref, gb_ref, gmoe_ref, wrt_ref, br_ref, tri_ref,
                       after_ref, h1_ref, xp_ref, tope_ref, rank_ref, gates_ref, counts_ref, carry_ref):
    del after_ref
    tm = h_ref.shape[0]

    @pl.when(pl.program_id(0) == 0)
    def _():
        carry_ref[...] = jnp.zeros_like(carry_ref)

    yb = ybt_ref[0].astype(F32)
    ybn = (yb * lax.rsqrt(jnp.mean(yb * yb, axis=0, keepdims=True) + RMS_EPS) * gb_ref[...]).astype(BF16)
    y = jnp.dot(ya_ref[...], woa_ref[...], preferred_element_type=F32)
    y = y + lax.dot_general(ybn, wob_ref[...], _TN, preferred_element_type=F32)
    h1 = h_ref[...] + y
    h1_ref[...] = h1
    xn = _rms(h1, gmoe_ref[...])
    _store_planes(xp_ref, _pack_rows(xn))

    logits = lax.dot_general(wrt_ref[...], xn.astype(BF16), _NT, preferred_element_type=F32) + br_ref[...]
    ri = lax.broadcasted_iota(I32, (N_EXPERTS, tm), 0)
    vals, idxs = [], []
    l = logits
    for _ in range(TOP_K):
        m = jnp.max(l, axis=0, keepdims=True)
        idx = jnp.min(jnp.where(l == m, ri, N_EXPERTS), axis=0, keepdims=True)
        vals.append(m)
        idxs.append(idx)
        l = jnp.where(ri == idx, -jnp.inf, l)
    es = [jnp.exp(vk - vals[0]) for vk in vals]
    den = es[0] + es[1] + es[2] + es[3]
    zrow = jnp.zeros((8 - TOP_K, tm), F32)
    gates_ref[...] = jnp.concatenate([e / den for e in es] + [zrow], axis=0).T

    sel = jnp.zeros((N_EXPERTS, tm), F32)
    for idx in idxs:
        sel = sel + jnp.where(ri == idx, 1.0, 0.0)
    cnt = jnp.dot(sel.astype(BF16), tri_ref[...], preferred_element_type=F32) + carry_ref[:, 0:1]
    ranks = [jnp.sum(jnp.where(ri == idx, cnt, 0.0), axis=0, keepdims=True) for idx in idxs]
    zi = jnp.zeros((8 - TOP_K, tm), I32)
    rank_ref[...] = jnp.concatenate([r.astype(I32) for r in ranks] + [zi], axis=0)
    tope_ref[...] = jnp.concatenate(idxs + [zi], axis=0)
    total = carry_ref[...] + jnp.sum(sel, axis=1, keepdims=True)
    carry_ref[...] = total
    counts_ref[...] = total.astype(I32)


def _out_router(group, h, ya, ybt, B, S, prm, tri_strict, after):
    T, D = h.shape
    Tg = T // MOE_GROUPS
    tm = min(ROW_TILE, S)
    ns = S // tm
    E = N_EXPERTS
    planes = (D // 2) // SC_ROW_WORDS
    off = group * (Tg // tm)
    const = lambda shape: pl.BlockSpec(shape, lambda i: (0,) * len(shape))
    return pl.pallas_call(
        _out_router_kernel,
        grid=(Tg // tm,),
        in_specs=[
            pl.BlockSpec((tm, D), lambda i: (off + i, 0)),
            pl.BlockSpec((tm, GMLP_WIDTH), lambda i: (off + i, 0)),
            pl.BlockSpec((1, FOX_WIDTH, tm), lambda i: ((off + i) // ns, 0, (off + i) % ns)),
            const((GMLP_WIDTH, D)),
            const((FOX_WIDTH, D)),
            const((FOX_WIDTH, 1)),
            const((1, D)),
            const((E, D)),
            const((E, 1)),
            const((tm, tm)),
            pl.BlockSpec(memory_space=pl.ANY),
        ],
        out_specs=[
            pl.BlockSpec((tm, D), lambda i: (i, 0)),
            pl.BlockSpec((planes, tm, SC_ROW_WORDS), lambda i: (0, i, 0)),
            pl.BlockSpec((8, tm), lambda i: (0, i)),
            pl.BlockSpec((8, tm), lambda i: (0, i)),
            pl.BlockSpec((tm, 8), lambda i: (i, 0)),
            const((E, 128)),
        ],
        out_shape=[
            jax.ShapeDtypeStruct((Tg, D), F32),
            jax.ShapeDtypeStruct((planes, Tg, SC_ROW_WORDS), U32),
            jax.ShapeDtypeStruct((8, Tg), I32),
            jax.ShapeDtypeStruct((8, Tg), I32),
            jax.ShapeDtypeStruct((Tg, 8), F32),
            jax.ShapeDtypeStruct((E, 128), I32),
        ],
        scratch_shapes=[pltpu.VMEM((E, 128), F32)],
        compiler_params=pltpu.CompilerParams(
            dimension_semantics=("arbitrary",), vmem_limit_bytes=VMEM_LIMIT),
        name="out_router",
    )(h, ya, ybt, prm["woa"], prm["wob"], prm["g_b"], prm["g_moe"], prm["wrt"], prm["b_router"], tri_strict, after)


def _dest_kernel(ps_ref, tope_ref, rank_ref, dest_ref):
    e_sel = tope_ref[...]
    d = rank_ref[...]
    for e in range(N_EXPERTS):
        d = d + jnp.where(e_sel == e, ps_ref[e], 0)
    dest_ref[...] = d


def _dest_rows(pad_starts, tope, rank):
    T = tope.shape[1]
    tc = min(4096, T)
    return pl.pallas_call(
        _dest_kernel,
        grid_spec=pltpu.PrefetchScalarGridSpec(
            num_scalar_prefetch=1,
            grid=(T // tc,),
            in_specs=[pl.BlockSpec((8, tc), lambda i, ps: (0, i)),
                      pl.BlockSpec((8, tc), lambda i, ps: (0, i))],
            out_specs=pl.BlockSpec((8, tc), lambda i, ps: (0, i)),
        ),
        out_shape=jax.ShapeDtypeStruct((8, T), I32),
        name="dest_rows",
    )(pad_starts, tope, rank)


def _sc_mesh():
    return plsc.VectorSubcoreMesh(core_axis_name="c", subcore_axis_name="s")


def _sc_scatter_rows(x, idx, n_out):
    R, W = x.shape
    N = idx.shape[0]
    nsrc = R // SC_WINDOW

    @functools.partial(pl.kernel, out_type=jax.ShapeDtypeStruct((n_out, W), x.dtype), mesh=_sc_mesh())
    def k(x_hbm, i_hbm, o_hbm):
        def body(x_vmem, i_vmem):
            pltpu.sync_copy(x_vmem, o_hbm.at[i_vmem.at[0]])

        pltpu.emit_pipeline(
            body,
            grid=(N // SC_WINDOW,),
            in_specs=[pl.BlockSpec((SC_WINDOW, W), lambda i: (i % nsrc, 0)),
                      pl.BlockSpec((1, SC_WINDOW), lambda i: (0, i))],
            out_specs=[],
            core_axis_name=("c", "s"),
            dimension_semantics=(pltpu.PARALLEL,),
        )(x_hbm, i_hbm)

    return k(x, idx.reshape(1, N))


def _sc_gather_rows(table, idx):
    V, W = table.shape
    N = idx.shape[0]

    @functools.partial(pl.kernel, out_type=jax.ShapeDtypeStruct((N, W), table.dtype), mesh=_sc_mesh())
    def k(t_hbm, i_hbm, o_hbm):
        def body(i_vmem, o_vmem):
            pltpu.sync_copy(t_hbm.at[i_vmem.at[0]], o_vmem)

        pltpu.emit_pipeline(
            body,
            grid=(N // SC_WINDOW,),
            in_specs=[pl.BlockSpec((1, SC_WINDOW), lambda i: (0, i))],
            out_specs=[pl.BlockSpec((SC_WINDOW, W), lambda i: (i, 0))],
            core_axis_name=("c", "s"),
            dimension_semantics=(pltpu.PARALLEL,),
        )(i_hbm, o_hbm)

    return k(table, idx.reshape(1, N))


def _expert_kernel(be_ref, nv_ref, fs_ref, nxt_ref, xs_ref, wgu_hbm, bgu_ref, wdn_hbm, bdn_ref, ys_ref,
                   wgu_f32, wdn_f32, wgu_bf, wdn_bf, sem, *, layer):
    F = wdn_bf.shape[0]

    def weight_copies(e, slot):
        return (pltpu.make_async_copy(wgu_hbm.at[layer, e], wgu_f32.at[slot], sem.at[slot, 0]),
                pltpu.make_async_copy(wdn_hbm.at[layer, e], wdn_f32.at[slot], sem.at[slot, 1]))

    @pl.when(pl.program_id(0) == 0)
    def _():
        for cp in weight_copies(be_ref[0], 0):
            cp.start()

    for j in range(EXPERT_STEP_BLOCKS):
        b = pl.program_id(0) * EXPERT_STEP_BLOCKS + j
        rs = slice(j * EXPERT_ROWS, (j + 1) * EXPERT_ROWS)
        e = be_ref[b]
        nv = nv_ref[b]
        fs = fs_ref[b]

        @pl.when(fs >= 0)
        def _():
            for cp in weight_copies(e, fs):
                cp.wait()
            wgu_bf[...] = wgu_f32[fs].astype(BF16)
            wdn_bf[...] = wdn_f32[fs].astype(BF16)
            nxt = nxt_ref[b]

            @pl.when(nxt >= 0)
            def _():
                for cp in weight_copies(nxt, 1 - fs):
                    cp.start(priority=1)

        @pl.when(nv > 0)
        def _():
            x = _unpack_rows(jnp.concatenate([xs_ref[pn, rs, :] for pn in range(xs_ref.shape[0])], axis=1))
            rows = lax.broadcasted_iota(I32, x.shape, 0)
            x = jnp.where(rows < nv, x, 0.0).astype(BF16)
            hgu = jnp.dot(x, wgu_bf[...], preferred_element_type=F32) + bgu_ref[e]
            g = jnp.minimum(hgu[:, :F], SWIGLU_LIMIT)
            lin = jnp.clip(hgu[:, F:], -SWIGLU_LIMIT, SWIGLU_LIMIT)
            act = g * jax.nn.sigmoid(SWIGLU_ALPHA * g) * (lin + 1.0)
            y = _pack_rows(jnp.dot(act.astype(BF16), wdn_bf[...], preferred_element_type=F32) + bdn_ref[e])
            for pn in range(ys_ref.shape[0]):
                ys_ref[pn, rs, :] = y[:, pn * SC_ROW_WORDS:(pn + 1) * SC_ROW_WORDS]

        @pl.when(nv == 0)
        def _():
            ys_ref[:, rs, :] = jnp.zeros((ys_ref.shape[0], EXPERT_ROWS, SC_ROW_WORDS), U32)


def _experts(layer, plan, xs, w_gu, b_gu, w_dn, b_dn):
    planes, P, _ = xs.shape
    _, E, D, F2 = w_gu.shape
    F = F2 // 2
    step_rows = EXPERT_STEP_BLOCKS * EXPERT_ROWS
    by_step = lambda i, be, nv, fs, nxt: (0, i, 0)
    layer_all = lambda i, be, nv, fs, nxt: (layer, 0, 0, 0)
    return pl.pallas_call(
        functools.partial(_expert_kernel, layer=layer),
        grid_spec=pltpu.PrefetchScalarGridSpec(
            num_scalar_prefetch=4,
            grid=(P // step_rows,),
            in_specs=[
                pl.BlockSpec((planes, step_rows, SC_ROW_WORDS), by_step),
                pl.BlockSpec(memory_space=pl.ANY),
                pl.BlockSpec((None, E, 1, F2), layer_all),
                pl.BlockSpec(memory_space=pl.ANY),
                pl.BlockSpec((None, E, 1, D), layer_all),
            ],
            out_specs=pl.BlockSpec((planes, step_rows, SC_ROW_WORDS), by_step),
            scratch_shapes=[
                pltpu.VMEM((2, D, F2), F32), pltpu.VMEM((2, F, D), F32),
                pltpu.VMEM((D, F2), BF16), pltpu.VMEM((F, D), BF16),
                pltpu.SemaphoreType.DMA((2, 2)),
            ],
        ),
        out_shape=jax.ShapeDtypeStruct((planes, P, SC_ROW_WORDS), U32),
        compiler_params=pltpu.CompilerParams(
            dimension_semantics=("arbitrary",), vmem_limit_bytes=VMEM_LIMIT),
        name="experts",
    )(plan["block_e"], plan["block_nv"], plan["first_slot"], plan["next_e"],
      xs, w_gu, b_gu.reshape(-1, E, 1, F2), w_dn, b_dn.reshape(-1, E, 1, D))


def _combine_ple_kernel(h1_ref, yg_ref, gates_ref, gple_ref, wpg_ref, p_ref, wple_ref, gfin_ref, *rest, final):
    o_ref = rest[-1]
    h2 = h1_ref[...]
    gates = gates_ref[...]
    for k in range(TOP_K):
        h2 = h2 + gates[:, k:k + 1] * _unpack_rows(_load_planes(yg_ref.at[k]))
    xn = _rms(h2, gple_ref[...]).astype(BF16)
    gate = jax.nn.sigmoid(jnp.dot(xn, wpg_ref[...], preferred_element_type=F32))
    pe = jnp.dot(p_ref[...].astype(BF16), wple_ref[...], preferred_element_type=F32)
    h3 = h2 + gate * pe
    if final:
        h3 = _rms(h3, gfin_ref[...])
    o_ref[...] = h3


def _combine_ple(layer, group, h1, yg, gates, p, prm, g_final, final, h_prev):
    Tg, D = h1.shape
    T = Tg * MOE_GROUPS
    tm = min(ROW_TILE, Tg)
    PD = p.shape[2]
    planes = yg.shape[1]
    off = group * (Tg // tm)
    const = lambda shape: pl.BlockSpec(shape, lambda i: (0,) * len(shape))
    in_specs = [
        pl.BlockSpec((tm, D), lambda i: (i, 0)),
        pl.BlockSpec((TOP_K, planes, tm, SC_ROW_WORDS), lambda i: (0, 0, i, 0)),
        pl.BlockSpec((tm, 8), lambda i: (i, 0)),
        const((1, D)),
        const((D, D)),
        pl.BlockSpec((None, tm, PD), lambda i: (layer, off + i, 0)),
        const((PD, D)),
        const((1, D)),
    ]
    args = [h1, yg, gates, prm["g_ple"], prm["wpg"], p, prm["wple"], g_final]
    aliases = {}
    if h_prev is not None:
        in_specs.append(pl.BlockSpec(memory_space=pl.ANY))
        args.append(h_prev)
        aliases = {len(args) - 1: 0}
    return pl.pallas_call(
        functools.partial(_combine_ple_kernel, final=final),
        grid=(Tg // tm,),
        in_specs=in_specs,
        out_specs=pl.BlockSpec((tm, D), lambda i: (off + i, 0)),
        out_shape=jax.ShapeDtypeStruct((T, D), F32),
        input_output_aliases=aliases,
        compiler_params=pltpu.CompilerParams(
            dimension_semantics=("parallel",), vmem_limit_bytes=VMEM_LIMIT),
        name="combine_ple",
    )(*args)


def _layer_params(i, w):
    D = w["w_in"].shape[1]
    H = FOX_HEADS
    w_in = w["w_in"][i]
    c1 = 2 * GMLP_WIDTH
    c2, c3, c4 = c1 + FOX_WIDTH, c1 + 2 * FOX_WIDTH, c1 + 3 * FOX_WIDTH
    wk = jnp.zeros((D, H, HEAD_PAD), F32).at[:, :, :FOX_HEAD_DIM].set(
        w_in[:, c2:c3].reshape(D, H, FOX_HEAD_DIM)).reshape(D, H * HEAD_PAD)
    return {
        "g_mix": w["g_mix"][i].reshape(1, D),
        "wuv": w_in[:, :c1].astype(BF16),
        "wqt": (w_in[:, c1:c2] * (FOX_HEAD_DIM ** -0.5 * LOG2E)).T.astype(BF16),
        "wk": wk.astype(BF16),
        "wvt": w_in[:, c3:c4].T.astype(BF16),
        "wft": w_in[:, c4:].T.astype(BF16),
        "b_f": w["b_f"][i].reshape(H, 1),
        "ln_g": w["ln_g"][i].reshape(1, GMLP_WIDTH),
        "ln_b": w["ln_b"][i].reshape(1, GMLP_WIDTH),
        "w_s": w["w_s"][i],
        "b_s": w["b_s"][i].reshape(GMLP_GROUPS, GMLP_BLOCK, 1),
        "g_a": w["g_a"][i].reshape(1, GMLP_WIDTH),
        "woa": w["w_out"][i][:GMLP_WIDTH].astype(BF16),
        "wob": w["w_out"][i][GMLP_WIDTH:].astype(BF16),
        "g_b": w["g_b"][i].reshape(FOX_WIDTH, 1),
        "g_moe": w["g_moe"][i].reshape(1, D),
        "wrt": w["w_router"][i].T.astype(BF16),
        "b_router": w["b_router"][i].reshape(N_EXPERTS, 1),
        "g_ple": w["g_ple"][i].reshape(1, D),
        "wpg": w["w_ple_gate"][i].astype(BF16),
        "wple": w["w_ple"][i].astype(BF16),
    }


def _bias_constants(tm):
    H = FOX_HEADS
    j = jnp.arange(tm)
    tri_incl = (j[:, None] <= j[None, :]).astype(BF16)
    tri_strict = (j[:, None] < j[None, :]).astype(BF16)
    part = jnp.arange(BIAS_PARTS)[:, None]
    head = jnp.arange(H)[None, :]
    rows = (part * H + head).reshape(-1)
    cols = (head * HEAD_PAD + FOX_HEAD_DIM + BIAS_PARTS + part).reshape(-1)
    selk = jnp.zeros((32, H * HEAD_PAD), F32).at[rows, cols].set(-1.0).astype(BF16)
    one_cols = (jnp.arange(H)[:, None] * HEAD_PAD + FOX_HEAD_DIM + jnp.arange(BIAS_PARTS)[None, :]).reshape(-1)
    onesk = jnp.zeros((1, H * HEAD_PAD), F32).at[0, one_cols].set(1.0)
    return tri_incl, tri_strict, selk, onesk


def _moe_dispatch_plan(counts, n_blocks):
    padded = (counts + EXPERT_ROWS - 1) // EXPERT_ROWS * EXPERT_ROWS
    pad_ends = jnp.cumsum(padded)
    pad_starts = (pad_ends - padded).astype(I32)
    block_start = jnp.arange(n_blocks, dtype=I32) * EXPERT_ROWS
    n_before = jnp.sum((pad_ends[None, :] <= block_start[:, None]).astype(I32), axis=1)
    block_e = jnp.minimum(n_before, N_EXPERTS - 1).astype(I32)
    own = block_e[:, None] == jnp.arange(N_EXPERTS, dtype=I32)[None, :]
    seg_end = jnp.sum(jnp.where(own, (pad_starts + counts)[None, :], 0), axis=1)
    block_nv = jnp.clip(seg_end - block_start, 0, EXPERT_ROWS)
    block_nv = jnp.where(block_start < pad_ends[-1], block_nv, 0).astype(I32)
    prev_e = jnp.concatenate([jnp.full((1,), -1, I32), block_e[:-1]])
    first = (block_nv > 0) & (block_e != prev_e)
    run = jnp.cumsum(first.astype(I32)) - 1
    first_slot = jnp.where(first, run % 2, -1).astype(I32)
    ar = jnp.arange(N_EXPERTS, dtype=I32)
    later = (ar[None, :] > ar[:, None]) & (counts > 0)[None, :]
    next_of = jnp.min(jnp.where(later, ar[None, :], N_EXPERTS), axis=1)
    next_of = jnp.where(next_of < N_EXPERTS, next_of, -1)
    next_e = jnp.sum(jnp.where(own, next_of[None, :], 0), axis=1).astype(I32)
    return dict(pad_starts=pad_starts, block_e=block_e, block_nv=block_nv, first_slot=first_slot, next_e=next_e)


def kernel(x, p, g_mix, w_in, ln_g, ln_b, w_s, b_s, b_f, g_a, g_b, w_out, g_moe, w_router, b_router,
           w_gu, b_gu, w_dn, b_dn, g_ple, w_ple_gate, w_ple, g_final):
    B, S, D = x.shape
    T = B * S
    depth = w_in.shape[0]
    w = dict(g_mix=g_mix, w_in=w_in, ln_g=ln_g, ln_b=ln_b, w_s=w_s, b_s=b_s, b_f=b_f, g_a=g_a, g_b=g_b,
             w_out=w_out, g_moe=g_moe, w_router=w_router, b_router=b_router, w_gu=w_gu, b_gu=b_gu,
             w_dn=w_dn, b_dn=b_dn, g_ple=g_ple, w_ple_gate=w_ple_gate, w_ple=w_ple)
    tri_incl, tri_strict, selk, onesk = _bias_constants(min(ROW_TILE, S))
    G = MOE_GROUPS
    Tg = T // G
    n_blocks = -(-(Tg * TOP_K) // EXPERT_ROWS) + N_EXPERTS
    n_blocks = -(-n_blocks // EXPERT_STEP_BLOCKS) * EXPERT_STEP_BLOCKS
    P = n_blocks * EXPERT_ROWS
    planes = (D // 2) // SC_ROW_WORDS
    plane_base = (jnp.arange(planes, dtype=I32) * P)[None, :, None]
    p_rows = p.reshape(depth, T, p.shape[-1])

    h = x.reshape(T, D)
    for i in range(depth):
        prm = _layer_params(i, w)
        prm.update(tri_incl=tri_incl, selk=selk, onesk=onesk)
        ya, qt, kx, vt = _mix_in(h, B, S, prm)
        ybt = _fox_attn(qt, kx, vt, B, S)
        routed = []
        idx = jnp.zeros((8, 128), I32)
        for g in range(G):
            h1, xp, tope, rank, gates, counts = _out_router(g, h, ya, ybt, B, S, prm, tri_strict, idx)
            plan = _moe_dispatch_plan(counts[:, 0], n_blocks)
            dest = _dest_rows(plan["pad_starts"], tope, rank)[:TOP_K]
            idx = (dest[:, None, :] + plane_base).reshape(-1)
            xs = _sc_scatter_rows(xp.reshape(planes * Tg, SC_ROW_WORDS), idx, planes * P)
            routed.append((h1, gates, plan, idx, xs))
        h_out = None
        for g, (h1, gates, plan, idx, xs) in enumerate(routed):
            ys = _experts(i, plan, xs.reshape(planes, P, SC_ROW_WORDS), w_gu, b_gu, w_dn, b_dn)
            yg = _sc_gather_rows(ys.reshape(planes * P, SC_ROW_WORDS), idx)
            h_out = _combine_ple(i, g, h1, yg.reshape(TOP_K, planes, Tg, SC_ROW_WORDS), gates, p_rows, prm,
                                 g_final.reshape(1, D), i == depth - 1, h_out)
        h = h_out
    return h.reshape(B, S, D)
```

```python
import functools

import jax
import jax.numpy as jnp
from jax import lax
from jax.experimental import pallas as pl
from jax.experimental.pallas import tpu as pltpu
from jax.experimental.pallas import tpu_sc as plsc

F32 = jnp.float32
BF16 = jnp.bfloat16
I32 = jnp.int32
U32 = jnp.uint32

CHUNK = 64
GMLP_WIDTH = 512
GMLP_GROUPS = 4
GMLP_GROUP_CH = 128
GMLP_BLOCK = 128
FOX_WIDTH = 512
FOX_HEAD_DIM = 64
FOX_HEADS = 8
N_EXPERTS = 32
TOP_K = 4
EXPERT_ROWS = 256
SWIGLU_LIMIT = 7.0
SWIGLU_ALPHA = 1.702
RMS_EPS = 1e-5
LN_EPS = 1e-5

HEAD_PAD = 128
BIAS_PARTS = 3
ROW_TILE = 512
ATTN_TILE = 512
ATTN_HEADS = 2
ACC_ROWS = 80
LOG2E = 1.4426950408889634
SC_WINDOW = 128
SC_ROW_WORDS = 256
EXPERT_STEP_BLOCKS = 4
MOE_GROUPS = 2
VMEM_LIMIT = 56 * 1024 * 1024
NEG_BIG = -1e30

_NT = (((1,), (1,)), ((), ()))
_TN = (((0,), (0,)), ((), ()))


def _rms(x, g):
    return x * lax.rsqrt(jnp.mean(x * x, axis=-1, keepdims=True) + RMS_EPS) * g


def _split3(c):
    a1 = c.astype(BF16).astype(F32)
    r1 = c - a1
    a2 = r1.astype(BF16).astype(F32)
    a3 = (r1 - a2).astype(BF16).astype(F32)
    return a1, a2, a3


def _pack_rows(x):
    n = x.shape[1] // 2
    lo = lax.bitcast_convert_type(x[:, :n].astype(BF16).astype(F32), U32) >> 16
    hi = lax.bitcast_convert_type(x[:, n:].astype(BF16).astype(F32), U32) & jnp.uint32(0xFFFF0000)
    return lo | hi


def _unpack_rows(u):
    lo = lax.bitcast_convert_type(u << 16, F32)
    hi = lax.bitcast_convert_type(u & jnp.uint32(0xFFFF0000), F32)
    return jnp.concatenate([lo, hi], axis=1)


def _store_planes(ref, u):
    for j in range(ref.shape[0]):
        ref[j] = u[:, j * SC_ROW_WORDS:(j + 1) * SC_ROW_WORDS]


def _load_planes(ref):
    return jnp.concatenate([ref[j] for j in range(ref.shape[0])], axis=1)


def _mix_in_kernel(h_ref, gmix_ref, wuv_ref, wqt_ref, wk_ref, wvt_ref, wft_ref, bf_ref, lng_ref, lnb_ref,
                   ws_ref, bs_ref, ga_ref, tri_ref, selk_ref, onesk_ref, after_ref,
                   ya_ref, qt_ref, k_ref, vt_ref, carry_ref, ya_scr):
    del after_ref
    tm = h_ref.shape[0]

    @pl.when(pl.program_id(1) == 0)
    def _():
        carry_ref[...] = jnp.zeros_like(carry_ref)

    xn = _rms(h_ref[...], gmix_ref[...]).astype(BF16)

    z = jnp.dot(xn, wuv_ref[...], preferred_element_type=F32)
    qt = lax.dot_general(wqt_ref[...], xn, _NT, preferred_element_type=F32)
    vt_ref[0] = lax.dot_general(wvt_ref[...], xn, _NT, preferred_element_type=F32).astype(BF16)
    kp = jnp.dot(xn, wk_ref[...], preferred_element_type=F32)
    ft = lax.dot_general(wft_ref[...], xn, _NT, preferred_element_type=F32) + bf_ref[...]

    z = 0.5 * z * (1.0 + lax.erf(z * (0.5 ** 0.5)))
    u = z[:, :GMLP_WIDTH]
    v = z[:, GMLP_WIDTH:]
    mu = jnp.mean(v, axis=-1, keepdims=True)
    vc = v - mu
    var = jnp.mean(vc * vc, axis=-1, keepdims=True)
    vn = (vc * lax.rsqrt(var + LN_EPS) * lng_ref[...] + lnb_ref[...]).astype(BF16)
    pi = lax.broadcasted_iota(I32, (GMLP_BLOCK, GMLP_BLOCK), 0) // CHUNK
    pj = lax.broadcasted_iota(I32, (GMLP_BLOCK, GMLP_BLOCK), 1) // CHUNK
    for g in range(GMLP_GROUPS):
        wm = jnp.where(pj <= pi, ws_ref[g], 0.0).astype(BF16)
        cs = slice(g * GMLP_GROUP_CH, (g + 1) * GMLP_GROUP_CH)
        for n in range(tm // GMLP_BLOCK):
            rs = slice(n * GMLP_BLOCK, (n + 1) * GMLP_BLOCK)
            sv = jnp.dot(wm, vn[rs, cs], preferred_element_type=F32) + bs_ref[g]
            ya_scr[rs, cs] = u[rs, cs] * sv
    ya_ref[...] = _rms(ya_scr[...], ga_ref[...]).astype(BF16)

    lf = (jnp.minimum(ft, 0.0) - jnp.log1p(jnp.exp(-jnp.abs(ft)))) * LOG2E
    zpad = jnp.zeros((32 - BIAS_PARTS * FOX_HEADS, tm), F32)
    stack = jnp.concatenate(list(_split3(lf)) + [zpad], axis=0).astype(BF16)
    cum = jnp.dot(stack, tri_ref[...], preferred_element_type=F32)
    c = cum[0:8] + cum[8:16] + cum[16:24] + carry_ref[:, 0:1]
    carry_ref[...] = jnp.broadcast_to(c[:, tm - 1:tm], carry_ref.shape)
    a1, a2, a3 = _split3(c)

    si = lax.broadcasted_iota(I32, (8, tm), 0)
    zrows = jnp.zeros((HEAD_PAD - FOX_HEAD_DIM - 8, tm), F32)
    for hd in range(FOX_HEADS):
        b1 = jnp.broadcast_to(a1[hd:hd + 1], (8, tm))
        b2 = jnp.broadcast_to(a2[hd:hd + 1], (8, tm))
        b3 = jnp.broadcast_to(a3[hd:hd + 1], (8, tm))
        ext = jnp.where(si == 0, b1, jnp.where(si == 1, b2, jnp.where(si == 2, b3,
                        jnp.where(si < 2 * BIAS_PARTS, 1.0, 0.0))))
        blk = jnp.concatenate([qt[hd * FOX_HEAD_DIM:(hd + 1) * FOX_HEAD_DIM], ext, zrows], axis=0)
        qt_ref[0, hd * HEAD_PAD:(hd + 1) * HEAD_PAD, :] = blk.astype(BF16)

    cstack = jnp.concatenate([a1, a2, a3, zpad], axis=0).astype(BF16)
    kext = lax.dot_general(cstack, selk_ref[...], _TN, preferred_element_type=F32)
    k_ref[...] = (kp + kext + onesk_ref[...]).astype(BF16)


def _mix_in(group, h, B, S, prm, after):
    T, D = h.shape
    Bg = B // MOE_GROUPS
    Tg = Bg * S
    tm = min(ROW_TILE, S)
    ns = S // tm
    H = FOX_HEADS
    const = lambda shape: pl.BlockSpec(shape, lambda b, s: (0,) * len(shape))
    return pl.pallas_call(
        _mix_in_kernel,
        grid=(Bg, ns),
        in_specs=[
            pl.BlockSpec((tm, D), lambda b, s: ((group * Bg + b) * ns + s, 0)),
            const((1, D)),
            const((D, 2 * GMLP_WIDTH)),
            const((FOX_WIDTH, D)),
            const((D, H * HEAD_PAD)),
            const((FOX_WIDTH, D)),
            const((H, D)),
            const((H, 1)),
            const((1, GMLP_WIDTH)),
            const((1, GMLP_WIDTH)),
            const((GMLP_GROUPS, GMLP_BLOCK, GMLP_BLOCK)),
            const((GMLP_GROUPS, GMLP_BLOCK, 1)),
            const((1, GMLP_WIDTH)),
            const((tm, tm)),
            const((32, H * HEAD_PAD)),
            const((1, H * HEAD_PAD)),
            pl.BlockSpec(memory_space=pl.ANY),
        ],
        out_specs=[
            pl.BlockSpec((tm, GMLP_WIDTH), lambda b, s: (b * ns + s, 0)),
            pl.BlockSpec((1, H * HEAD_PAD, tm), lambda b, s: (b, 0, s)),
            pl.BlockSpec((tm, H * HEAD_PAD), lambda b, s: (b * ns + s, 0)),
            pl.BlockSpec((1, FOX_WIDTH, tm), lambda b, s: (b, 0, s)),
        ],
        out_shape=[
            jax.ShapeDtypeStruct((Tg, GMLP_WIDTH), BF16),
            jax.ShapeDtypeStruct((Bg, H * HEAD_PAD, S), BF16),
            jax.ShapeDtypeStruct((Tg, H * HEAD_PAD), BF16),
            jax.ShapeDtypeStruct((Bg, FOX_WIDTH, S), BF16),
        ],
        scratch_shapes=[pltpu.VMEM((H, 128), F32), pltpu.VMEM((tm, GMLP_WIDTH), F32)],
        compiler_params=pltpu.CompilerParams(
            dimension_semantics=("arbitrary", "arbitrary"), vmem_limit_bytes=VMEM_LIMIT),
        name="mix_in",
    )(h, prm["g_mix"], prm["wuv"], prm["wqt"], prm["wk"], prm["wvt"], prm["wft"], prm["b_f"],
      prm["ln_g"], prm["ln_b"], prm["w_s"], prm["b_s"], prm["g_a"], prm["tri_incl"], prm["selk"], prm["onesk"],
      after)


def _fox_attn_kernel(qt_ref, k_ref, vt_ref, ot_ref):
    S = k_ref.shape[0]
    t = min(ATTN_TILE, S)
    nq = S // t
    ones_rows = jnp.ones((ACC_ROWS - FOX_HEAD_DIM, t), BF16)
    causal = lax.broadcasted_iota(I32, (t, t), 0) <= lax.broadcasted_iota(I32, (t, t), 1)
    pairs = [(qi, kj) for qi in range(nq) for kj in range(qi + 1)]

    def scores(qi, kj):
        out = []
        for hh in range(ATTN_HEADS):
            hs = slice(hh * HEAD_PAD, (hh + 1) * HEAD_PAD)
            s = jnp.dot(k_ref[kj * t:(kj + 1) * t, hs], qt_ref[0, hs, qi * t:(qi + 1) * t],
                        preferred_element_type=F32)
            out.append(jnp.where(causal, s, NEG_BIG) if kj == qi else s)
        return out

    s_next = scores(*pairs[0])
    state = None
    for n, (qi, kj) in enumerate(pairs):
        s_cur = s_next
        if n + 1 < len(pairs):
            s_next = scores(*pairs[n + 1])
        new_state = []
        for hh in range(ATTN_HEADS):
            s = s_cur[hh]
            v = vt_ref[0, hh * FOX_HEAD_DIM:(hh + 1) * FOX_HEAD_DIM, kj * t:(kj + 1) * t]
            v_aug = jnp.concatenate([v, ones_rows], axis=0)
            if kj == 0:
                m_new = jnp.max(s, axis=0, keepdims=True)
                acc = jnp.dot(v_aug, jnp.exp2(s - m_new).astype(BF16), preferred_element_type=F32)
            else:
                m, acc = state[hh]
                m_new = jnp.maximum(m, jnp.max(s, axis=0, keepdims=True))
                p = jnp.exp2(s - m_new).astype(BF16)
                acc = jnp.exp2(m - m_new) * acc + jnp.dot(v_aug, p, preferred_element_type=F32)
            new_state.append((m_new, acc))
        state = new_state
        if kj == qi:
            for hh in range(ATTN_HEADS):
                acc = state[hh][1]
                o = acc[:FOX_HEAD_DIM] / acc[FOX_HEAD_DIM:FOX_HEAD_DIM + 1]
                ot_ref[0, hh * FOX_HEAD_DIM:(hh + 1) * FOX_HEAD_DIM, qi * t:(qi + 1) * t] = o.astype(BF16)


def _fox_attn(qt, kx, vt, B, S):
    G = ATTN_HEADS
    return pl.pallas_call(
        _fox_attn_kernel,
        grid=(B, FOX_HEADS // G),
        in_specs=[
            pl.BlockSpec((1, G * HEAD_PAD, S), lambda b, h: (b, h, 0)),
            pl.BlockSpec((S, G * HEAD_PAD), lambda b, h: (b, h)),
            pl.BlockSpec((1, G * FOX_HEAD_DIM, S), lambda b, h: (b, h, 0)),
        ],
        out_specs=pl.BlockSpec((1, G * FOX_HEAD_DIM, S), lambda b, h: (b, h, 0)),
        out_shape=jax.ShapeDtypeStruct((B, FOX_WIDTH, S), BF16),
        compiler_params=pltpu.CompilerParams(
            dimension_semantics=("parallel", "parallel"), vmem_limit_bytes=VMEM_LIMIT),
        name="fox_attn",
    )(qt, kx, vt)


def _out_router_kernel(h_ref, ya_ref, ybt_ref, woa_ref, wob_ref, gb_ref, gmoe_ref, wrt_ref, br_ref, tri_ref,
                       h1_ref, xp_ref, tope_ref, rank_ref, gates_ref, counts_ref, carry_ref):
    tm = h_ref.shape[0]

    @pl.when(pl.program_id(0) == 0)
    def _():
        carry_ref[...] = jnp.zeros_like(carry_ref)

    yb = ybt_ref[0].astype(F32)
    ybn = (yb * lax.rsqrt(jnp.mean(yb * yb, axis=0, keepdims=True) + RMS_EPS) * gb_ref[...]).astype(BF16)
    y = jnp.dot(ya_ref[...], woa_ref[...], preferred_element_type=F32)
    y = y + lax.dot_general(ybn, wob_ref[...], _TN, preferred_element_type=F32)
    h1 = h_ref[...] + y
    h1_ref[...] = h1
    xn = _rms(h1, gmoe_ref[...])
    _store_planes(xp_ref, _pack_rows(xn))

    logits = lax.dot_general(wrt_ref[...], xn.astype(BF16), _NT, preferred_element_type=F32) + br_ref[...]
    ri = lax.broadcasted_iota(I32, (N_EXPERTS, tm), 0)
    vals, idxs = [], []
    l = logits
    for _ in range(TOP_K):
        m = jnp.max(l, axis=0, keepdims=True)
        idx = jnp.min(jnp.where(l == m, ri, N_EXPERTS), axis=0, keepdims=True)
        vals.append(m)
        idxs.append(idx)
        l = jnp.where(ri == idx, -jnp.inf, l)
    es = [jnp.exp(vk - vals[0]) for vk in vals]
    den = es[0] + es[1] + es[2] + es[3]
    zrow = jnp.zeros((8 - TOP_K, tm), F32)
    gates_ref[...] = jnp.concatenate([e / den for e in es] + [zrow], axis=0).T

    sel = jnp.zeros((N_EXPERTS, tm), F32)
    for idx in idxs:
        sel = sel + jnp.where(ri == idx, 1.0, 0.0)
    cnt = jnp.dot(sel.astype(BF16), tri_ref[...], preferred_element_type=F32) + carry_ref[:, 0:1]
    ranks = [jnp.sum(jnp.where(ri == idx, cnt, 0.0), axis=0, keepdims=True) for idx in idxs]
    zi = jnp.zeros((8 - TOP_K, tm), I32)
    rank_ref[...] = jnp.concatenate([r.astype(I32) for r in ranks] + [zi], axis=0)
    tope_ref[...] = jnp.concatenate(idxs + [zi], axis=0)
    total = carry_ref[...] + jnp.sum(sel, axis=1, keepdims=True)
    carry_ref[...] = total
    counts_ref[...] = total.astype(I32)


def _out_router(group, h, ya, ybt, B, S, prm, tri_strict):
    T, D = h.shape
    Tg = T // MOE_GROUPS
    tm = min(ROW_TILE, S)
    ns = S // tm
    E = N_EXPERTS
    planes = (D // 2) // SC_ROW_WORDS
    off = group * (Tg // tm)
    const = lambda shape: pl.BlockSpec(shape, lambda i: (0,) * len(shape))
    return pl.pallas_call(
        _out_router_kernel,
        grid=(Tg // tm,),
        in_specs=[
            pl.BlockSpec((tm, D), lambda i: (off + i, 0)),
            pl.BlockSpec((tm, GMLP_WIDTH), lambda i: (i, 0)),
            pl.BlockSpec((1, FOX_WIDTH, tm), lambda i: (i // ns, 0, i % ns)),
            const((GMLP_WIDTH, D)),
            const((FOX_WIDTH, D)),
            const((FOX_WIDTH, 1)),
            const((1, D)),
            const((E, D)),
            const((E, 1)),
            const((tm, tm)),
        ],
        out_specs=[
            pl.BlockSpec((tm, D), lambda i: (i, 0)),
            pl.BlockSpec((planes, tm, SC_ROW_WORDS), lambda i: (0, i, 0)),
            pl.BlockSpec((8, tm), lambda i: (0, i)),
            pl.BlockSpec((8, tm), lambda i: (0, i)),
            pl.BlockSpec((tm, 8), lambda i: (i, 0)),
            const((E, 128)),
        ],
        out_shape=[
            jax.ShapeDtypeStruct((Tg, D), F32),
            jax.ShapeDtypeStruct((planes, Tg, SC_ROW_WORDS), U32),
            jax.ShapeDtypeStruct((8, Tg), I32),
            jax.ShapeDtypeStruct((8, Tg), I32),
            jax.ShapeDtypeStruct((Tg, 8), F32),
            jax.ShapeDtypeStruct((E, 128), I32),
        ],
        scratch_shapes=[pltpu.VMEM((E, 128), F32)],
        compiler_params=pltpu.CompilerParams(
            dimension_semantics=("arbitrary",), vmem_limit_bytes=VMEM_LIMIT),
        name="out_router",
    )(h, ya, ybt, prm["woa"], prm["wob"], prm["g_b"], prm["g_moe"], prm["wrt"], prm["b_router"], tri_strict)


def _dest_kernel(ps_ref, tope_ref, rank_ref, dest_ref):
    e_sel = tope_ref[...]
    d = rank_ref[...]
    for e in range(N_EXPERTS):
        d = d + jnp.where(e_sel == e, ps_ref[e], 0)
    dest_ref[...] = d


def _dest_rows(pad_starts, tope, rank):
    T = tope.shape[1]
    tc = min(4096, T)
    return pl.pallas_call(
        _dest_kernel,
        grid_spec=pltpu.PrefetchScalarGridSpec(
            num_scalar_prefetch=1,
            grid=(T // tc,),
            in_specs=[pl.BlockSpec((8, tc), lambda i, ps: (0, i)),
                      pl.BlockSpec((8, tc), lambda i, ps: (0, i))],
            out_specs=pl.BlockSpec((8, tc), lambda i, ps: (0, i)),
        ),
        out_shape=jax.ShapeDtypeStruct((8, T), I32),
        name="dest_rows",
    )(pad_starts, tope, rank)


def _sc_mesh():
    return plsc.VectorSubcoreMesh(core_axis_name="c", subcore_axis_name="s")


def _sc_scatter_rows(x, idx, n_out):
    R, W = x.shape
    N = idx.shape[0]
    nsrc = R // SC_WINDOW

    @functools.partial(pl.kernel, out_type=jax.ShapeDtypeStruct((n_out, W), x.dtype), mesh=_sc_mesh())
    def k(x_hbm, i_hbm, o_hbm):
        def body(x_vmem, i_vmem):
            pltpu.sync_copy(x_vmem, o_hbm.at[i_vmem.at[0]])

        pltpu.emit_pipeline(
            body,
            grid=(N // SC_WINDOW,),
            in_specs=[pl.BlockSpec((SC_WINDOW, W), lambda i: (i % nsrc, 0)),
                      pl.BlockSpec((1, SC_WINDOW), lambda i: (0, i))],
            out_specs=[],
            core_axis_name=("c", "s"),
            dimension_semantics=(pltpu.PARALLEL,),
        )(x_hbm, i_hbm)

    return k(x, idx.reshape(1, N))


def _sc_gather_rows(table, idx):
    V, W = table.shape
    N = idx.shape[0]

    @functools.partial(pl.kernel, out_type=jax.ShapeDtypeStruct((N, W), table.dtype), mesh=_sc_mesh())
    def k(t_hbm, i_hbm, o_hbm):
        def body(i_vmem, o_vmem):
            pltpu.sync_copy(t_hbm.at[i_vmem.at[0]], o_vmem)

        pltpu.emit_pipeline(
            body,
            grid=(N // SC_WINDOW,),
            in_specs=[pl.BlockSpec((1, SC_WINDOW), lambda i: (0, i))],
            out_specs=[pl.BlockSpec((SC_WINDOW, W), lambda i: (i, 0))],
            core_axis_name=("c", "s"),
            dimension_semantics=(pltpu.PARALLEL,),
        )(i_hbm, o_hbm)

    return k(table, idx.reshape(1, N))


def _expert_kernel(be_ref, nv_ref, fs_ref, nxt_ref, xs_ref, wgu_hbm, bgu_ref, wdn_hbm, bdn_ref, ys_ref,
                   wgu_f32, wdn_f32, wgu_bf, wdn_bf, sem, *, layer):
    F = wdn_bf.shape[0]

    def weight_copies(e, slot):
        return (pltpu.make_async_copy(wgu_hbm.at[layer, e], wgu_f32.at[slot], sem.at[slot, 0]),
                pltpu.make_async_copy(wdn_hbm.at[layer, e], wdn_f32.at[slot], sem.at[slot, 1]))

    @pl.when(pl.program_id(0) == 0)
    def _():
        for cp in weight_copies(be_ref[0], 0):
            cp.start()

    for j in range(EXPERT_STEP_BLOCKS):
        b = pl.program_id(0) * EXPERT_STEP_BLOCKS + j
        rs = slice(j * EXPERT_ROWS, (j + 1) * EXPERT_ROWS)
        e = be_ref[b]
        nv = nv_ref[b]
        fs = fs_ref[b]

        @pl.when(fs >= 0)
        def _():
            for cp in weight_copies(e, fs):
                cp.wait()
            wgu_bf[...] = wgu_f32[fs].astype(BF16)
            wdn_bf[...] = wdn_f32[fs].astype(BF16)
            nxt = nxt_ref[b]

            @pl.when(nxt >= 0)
            def _():
                for cp in weight_copies(nxt, 1 - fs):
                    cp.start(priority=1)

        @pl.when(nv > 0)
        def _():
            x = _unpack_rows(jnp.concatenate([xs_ref[pn, rs, :] for pn in range(xs_ref.shape[0])], axis=1))
            rows = lax.broadcasted_iota(I32, x.shape, 0)
            x = jnp.where(rows < nv, x, 0.0).astype(BF16)
            hgu = jnp.dot(x, wgu_bf[...], preferred_element_type=F32) + bgu_ref[e]
            g = jnp.minimum(hgu[:, :F], SWIGLU_LIMIT)
            lin = jnp.clip(hgu[:, F:], -SWIGLU_LIMIT, SWIGLU_LIMIT)
            act = g * jax.nn.sigmoid(SWIGLU_ALPHA * g) * (lin + 1.0)
            y = _pack_rows(jnp.dot(act.astype(BF16), wdn_bf[...], preferred_element_type=F32) + bdn_ref[e])
            for pn in range(ys_ref.shape[0]):
                ys_ref[pn, rs, :] = y[:, pn * SC_ROW_WORDS:(pn + 1) * SC_ROW_WORDS]

        @pl.when(nv == 0)
        def _():
            ys_ref[:, rs, :] = jnp.zeros((ys_ref.shape[0], EXPERT_ROWS, SC_ROW_WORDS), U32)


def _experts(layer, plan, xs, w_gu, b_gu, w_dn, b_dn):
    planes, P, _ = xs.shape
    _, E, D, F2 = w_gu.shape
    F = F2 // 2
    step_rows = EXPERT_STEP_BLOCKS * EXPERT_ROWS
    by_step = lambda i, be, nv, fs, nxt: (0, i, 0)
    layer_all = lambda i, be, nv, fs, nxt: (layer, 0, 0, 0)
    return pl.pallas_call(
        functools.partial(_expert_kernel, layer=layer),
        grid_spec=pltpu.PrefetchScalarGridSpec(
            num_scalar_prefetch=4,
            grid=(P // step_rows,),
            in_specs=[
                pl.BlockSpec((planes, step_rows, SC_ROW_WORDS), by_step),
                pl.BlockSpec(memory_space=pl.ANY),
                pl.BlockSpec((None, E, 1, F2), layer_all),
                pl.BlockSpec(memory_space=pl.ANY),
                pl.BlockSpec((None, E, 1, D), layer_all),
            ],
            out_specs=pl.BlockSpec((planes, step_rows, SC_ROW_WORDS), by_step),
            scratch_shapes=[
                pltpu.VMEM((2, D, F2), F32), pltpu.VMEM((2, F, D), F32),
                pltpu.VMEM((D, F2), BF16), pltpu.VMEM((F, D), BF16),
                pltpu.SemaphoreType.DMA((2, 2)),
            ],
        ),
        out_shape=jax.ShapeDtypeStruct((planes, P, SC_ROW_WORDS), U32),
        compiler_params=pltpu.CompilerParams(
            dimension_semantics=("arbitrary",), vmem_limit_bytes=VMEM_LIMIT),
        name="experts",
    )(plan["block_e"], plan["block_nv"], plan["first_slot"], plan["next_e"],
      xs, w_gu, b_gu.reshape(-1, E, 1, F2), w_dn, b_dn.reshape(-1, E, 1, D))


def _combine_ple_kernel(h1_ref, yg_ref, gates_ref, gple_ref, wpg_ref, p_ref, wple_ref, gfin_ref, *rest, final):
    o_ref = rest[-1]
    h2 = h1_ref[...]
    gates = gates_ref[...]
    for k in range(TOP_K):
        h2 = h2 + gates[:, k:k + 1] * _unpack_rows(_load_planes(yg_ref.at[k]))
    xn = _rms(h2, gple_ref[...]).astype(BF16)
    gate = jax.nn.sigmoid(jnp.dot(xn, wpg_ref[...], preferred_element_type=F32))
    pe = jnp.dot(p_ref[...].astype(BF16), wple_ref[...], preferred_element_type=F32)
    h3 = h2 + gate * pe
    if final:
        h3 = _rms(h3, gfin_ref[...])
    o_ref[...] = h3


def _combine_ple(layer, group, h1, yg, gates, p, prm, g_final, final, h_prev):
    Tg, D = h1.shape
    T = Tg * MOE_GROUPS
    tm = min(ROW_TILE, Tg)
    PD = p.shape[2]
    planes = yg.shape[1]
    off = group * (Tg // tm)
    const = lambda shape: pl.BlockSpec(shape, lambda i: (0,) * len(shape))
    in_specs = [
        pl.BlockSpec((tm, D), lambda i: (i, 0)),
        pl.BlockSpec((TOP_K, planes, tm, SC_ROW_WORDS), lambda i: (0, 0, i, 0)),
        pl.BlockSpec((tm, 8), lambda i: (i, 0)),
        const((1, D)),
        const((D, D)),
        pl.BlockSpec((None, tm, PD), lambda i: (layer, off + i, 0)),
        const((PD, D)),
        const((1, D)),
    ]
    args = [h1, yg, gates, prm["g_ple"], prm["wpg"], p, prm["wple"], g_final]
    aliases = {}
    if h_prev is not None:
        in_specs.append(pl.BlockSpec(memory_space=pl.ANY))
        args.append(h_prev)
        aliases = {len(args) - 1: 0}
    return pl.pallas_call(
        functools.partial(_combine_ple_kernel, final=final),
        grid=(Tg // tm,),
        in_specs=in_specs,
        out_specs=pl.BlockSpec((tm, D), lambda i: (off + i, 0)),
        out_shape=jax.ShapeDtypeStruct((T, D), F32),
        input_output_aliases=aliases,
        compiler_params=pltpu.CompilerParams(
            dimension_semantics=("parallel",), vmem_limit_bytes=VMEM_LIMIT),
        name="combine_ple",
    )(*args)


def _layer_params(i, w):
    D = w["w_in"].shape[1]
    H = FOX_HEADS
    w_in = w["w_in"][i]
    c1 = 2 * GMLP_WIDTH
    c2, c3, c4 = c1 + FOX_WIDTH, c1 + 2 * FOX_WIDTH, c1 + 3 * FOX_WIDTH
    wk = jnp.zeros((D, H, HEAD_PAD), F32).at[:, :, :FOX_HEAD_DIM].set(
        w_in[:, c2:c3].reshape(D, H, FOX_HEAD_DIM)).reshape(D, H * HEAD_PAD)
    return {
        "g_mix": w["g_mix"][i].reshape(1, D),
        "wuv": w_in[:, :c1].astype(BF16),
        "wqt": (w_in[:, c1:c2] * (FOX_HEAD_DIM ** -0.5 * LOG2E)).T.astype(BF16),
        "wk": wk.astype(BF16),
        "wvt": w_in[:, c3:c4].T.astype(BF16),
        "wft": w_in[:, c4:].T.astype(BF16),
        "b_f": w["b_f"][i].reshape(H, 1),
        "ln_g": w["ln_g"][i].reshape(1, GMLP_WIDTH),
        "ln_b": w["ln_b"][i].reshape(1, GMLP_WIDTH),
        "w_s": w["w_s"][i],
        "b_s": w["b_s"][i].reshape(GMLP_GROUPS, GMLP_BLOCK, 1),
        "g_a": w["g_a"][i].reshape(1, GMLP_WIDTH),
        "woa": w["w_out"][i][:GMLP_WIDTH].astype(BF16),
        "wob": w["w_out"][i][GMLP_WIDTH:].astype(BF16),
        "g_b": w["g_b"][i].reshape(FOX_WIDTH, 1),
        "g_moe": w["g_moe"][i].reshape(1, D),
        "wrt": w["w_router"][i].T.astype(BF16),
        "b_router": w["b_router"][i].reshape(N_EXPERTS, 1),
        "g_ple": w["g_ple"][i].reshape(1, D),
        "wpg": w["w_ple_gate"][i].astype(BF16),
        "wple": w["w_ple"][i].astype(BF16),
    }


def _bias_constants(tm):
    H = FOX_HEADS
    j = jnp.arange(tm)
    tri_incl = (j[:, None] <= j[None, :]).astype(BF16)
    tri_strict = (j[:, None] < j[None, :]).astype(BF16)
    part = jnp.arange(BIAS_PARTS)[:, None]
    head = jnp.arange(H)[None, :]
    rows = (part * H + head).reshape(-1)
    cols = (head * HEAD_PAD + FOX_HEAD_DIM + BIAS_PARTS + part).reshape(-1)
    selk = jnp.zeros((32, H * HEAD_PAD), F32).at[rows, cols].set(-1.0).astype(BF16)
    one_cols = (jnp.arange(H)[:, None] * HEAD_PAD + FOX_HEAD_DIM + jnp.arange(BIAS_PARTS)[None, :]).reshape(-1)
    onesk = jnp.zeros((1, H * HEAD_PAD), F32).at[0, one_cols].set(1.0)
    return tri_incl, tri_strict, selk, onesk


def _moe_dispatch_plan(counts, n_blocks):
    padded = (counts + EXPERT_ROWS - 1) // EXPERT_ROWS * EXPERT_ROWS
    pad_ends = jnp.cumsum(padded)
    pad_starts = (pad_ends - padded).astype(I32)
    block_start = jnp.arange(n_blocks, dtype=I32) * EXPERT_ROWS
    n_before = jnp.sum((pad_ends[None, :] <= block_start[:, None]).astype(I32), axis=1)
    block_e = jnp.minimum(n_before, N_EXPERTS - 1).astype(I32)
    own = block_e[:, None] == jnp.arange(N_EXPERTS, dtype=I32)[None, :]
    seg_end = jnp.sum(jnp.where(own, (pad_starts + counts)[None, :], 0), axis=1)
    block_nv = jnp.clip(seg_end - block_start, 0, EXPERT_ROWS)
    block_nv = jnp.where(block_start < pad_ends[-1], block_nv, 0).astype(I32)
    prev_e = jnp.concatenate([jnp.full((1,), -1, I32), block_e[:-1]])
    first = (block_nv > 0) & (block_e != prev_e)
    run = jnp.cumsum(first.astype(I32)) - 1
    first_slot = jnp.where(first, run % 2, -1).astype(I32)
    ar = jnp.arange(N_EXPERTS, dtype=I32)
    later = (ar[None, :] > ar[:, None]) & (counts > 0)[None, :]
    next_of = jnp.min(jnp.where(later, ar[None, :], N_EXPERTS), axis=1)
    next_of = jnp.where(next_of < N_EXPERTS, next_of, -1)
    next_e = jnp.sum(jnp.where(own, next_of[None, :], 0), axis=1).astype(I32)
    return dict(pad_starts=pad_starts, block_e=block_e, block_nv=block_nv, first_slot=first_slot, next_e=next_e)


def kernel(x, p, g_mix, w_in, ln_g, ln_b, w_s, b_s, b_f, g_a, g_b, w_out, g_moe, w_router, b_router,
           w_gu, b_gu, w_dn, b_dn, g_ple, w_ple_gate, w_ple, g_final):
    B, S, D = x.shape
    T = B * S
    depth = w_in.shape[0]
    w = dict(g_mix=g_mix, w_in=w_in, ln_g=ln_g, ln_b=ln_b, w_s=w_s, b_s=b_s, b_f=b_f, g_a=g_a, g_b=g_b,
             w_out=w_out, g_moe=g_moe, w_router=w_router, b_router=b_router, w_gu=w_gu, b_gu=b_gu,
             w_dn=w_dn, b_dn=b_dn, g_ple=g_ple, w_ple_gate=w_ple_gate, w_ple=w_ple)
    tri_incl, tri_strict, selk, onesk = _bias_constants(min(ROW_TILE, S))
    G = MOE_GROUPS
    Tg = T // G
    n_blocks = -(-(Tg * TOP_K) // EXPERT_ROWS) + N_EXPERTS
    n_blocks = -(-n_blocks // EXPERT_STEP_BLOCKS) * EXPERT_STEP_BLOCKS
    P = n_blocks * EXPERT_ROWS
    planes = (D // 2) // SC_ROW_WORDS
    plane_base = (jnp.arange(planes, dtype=I32) * P)[None, :, None]
    p_rows = p.reshape(depth, T, p.shape[-1])

    h = x.reshape(T, D)
    for i in range(depth):
        prm = _layer_params(i, w)
        prm.update(tri_incl=tri_incl, selk=selk, onesk=onesk)
        routed = []
        idx = jnp.zeros((8, 128), I32)
        for g in range(G):
            ya, qt, kx, vt = _mix_in(g, h, B, S, prm, idx)
            ybt = _fox_attn(qt, kx, vt, B // G, S)
            h1, xp, tope, rank, gates, counts = _out_router(g, h, ya, ybt, B, S, prm, tri_strict)
            plan = _moe_dispatch_plan(counts[:, 0], n_blocks)
            dest = _dest_rows(plan["pad_starts"], tope, rank)[:TOP_K]
            idx = (dest[:, None, :] + plane_base).reshape(-1)
            xs = _sc_scatter_rows(xp.reshape(planes * Tg, SC_ROW_WORDS), idx, planes * P)
            routed.append((h1, gates, plan, idx, xs))
        h_out = None
        for g, (h1, gates, plan, idx, xs) in enumerate(routed):
            ys = _experts(i, plan, xs.reshape(planes, P, SC_ROW_WORDS), w_gu, b_gu, w_dn, b_dn)
            yg = _sc_gather_rows(ys.reshape(planes * P, SC_ROW_WORDS), idx)
            h_out = _combine_ple(i, g, h1, yg.reshape(TOP_K, planes, Tg, SC_ROW_WORDS), gates, p_rows, prm,
                                 g_final.reshape(1, D), i == depth - 1, h_out)
        h = h_out
    return h.reshape(B, S, D)
```

```python
import functools

import jax
import jax.numpy as jnp
from jax import lax
from jax.experimental import pallas as pl
from jax.experimental.pallas import tpu as pltpu
from jax.experimental.pallas import tpu_sc as plsc

F32 = jnp.float32
BF16 = jnp.bfloat16
I32 = jnp.int32
U32 = jnp.uint32

CHUNK = 64
GMLP_WIDTH = 512
GMLP_GROUPS = 4
GMLP_GROUP_CH = 128
GMLP_BLOCK = 128
FOX_WIDTH = 512
FOX_HEAD_DIM = 64
FOX_HEADS = 8
N_EXPERTS = 32
TOP_K = 4
EXPERT_ROWS = 256
SWIGLU_LIMIT = 7.0
SWIGLU_ALPHA = 1.702
RMS_EPS = 1e-5
LN_EPS = 1e-5

HEAD_PAD = 128
BIAS_PARTS = 3
ROW_TILE = 512
ATTN_TILE = 512
ATTN_HEADS = 1
ACC_ROWS = 80
LOG2E = 1.4426950408889634
SC_WINDOW = 128
SC_ROW_WORDS = 256
EXPERT_STEP_BLOCKS = 4
MOE_GROUPS = 2
VMEM_LIMIT = 56 * 1024 * 1024
NEG_BIG = -1e30

_NT = (((1,), (1,)), ((), ()))
_TN = (((0,), (0,)), ((), ()))


def _rms(x, g):
    return x * lax.rsqrt(jnp.mean(x * x, axis=-1, keepdims=True) + RMS_EPS) * g


def _split3(c):
    a1 = c.astype(BF16).astype(F32)
    r1 = c - a1
    a2 = r1.astype(BF16).astype(F32)
    a3 = (r1 - a2).astype(BF16).astype(F32)
    return a1, a2, a3


def _pack_rows(x):
    n = x.shape[1] // 2
    lo = lax.bitcast_convert_type(x[:, :n].astype(BF16).astype(F32), U32) >> 16
    hi = lax.bitcast_convert_type(x[:, n:].astype(BF16).astype(F32), U32) & jnp.uint32(0xFFFF0000)
    return lo | hi


def _unpack_rows(u):
    lo = lax.bitcast_convert_type(u << 16, F32)
    hi = lax.bitcast_convert_type(u & jnp.uint32(0xFFFF0000), F32)
    return jnp.concatenate([lo, hi], axis=1)


def _store_planes(ref, u):
    for j in range(ref.shape[0]):
        ref[j] = u[:, j * SC_ROW_WORDS:(j + 1) * SC_ROW_WORDS]


def _load_planes(ref):
    return jnp.concatenate([ref[j] for j in range(ref.shape[0])], axis=1)


def _mix_in_kernel(h_ref, gmix_ref, wuv_ref, wqt_ref, wk_ref, wvt_ref, wft_ref, bf_ref, lng_ref, lnb_ref,
                   ws_ref, bs_ref, ga_ref, tri_ref, selk_ref, onesk_ref, after_ref,
                   ya_ref, qt_ref, k_ref, vt_ref, carry_ref, ya_scr):
    del after_ref
    tm = h_ref.shape[0]

    @pl.when(pl.program_id(1) == 0)
    def _():
        carry_ref[...] = jnp.zeros_like(carry_ref)

    xn = _rms(h_ref[...], gmix_ref[...]).astype(BF16)

    z = jnp.dot(xn, wuv_ref[...], preferred_element_type=F32)
    qt = lax.dot_general(wqt_ref[...], xn, _NT, preferred_element_type=F32)
    vt_ref[0] = lax.dot_general(wvt_ref[...], xn, _NT, preferred_element_type=F32).astype(BF16)
    kp = jnp.dot(xn, wk_ref[...], preferred_element_type=F32)
    ft = lax.dot_general(wft_ref[...], xn, _NT, preferred_element_type=F32) + bf_ref[...]

    z = 0.5 * z * (1.0 + lax.erf(z * (0.5 ** 0.5)))
    u = z[:, :GMLP_WIDTH]
    v = z[:, GMLP_WIDTH:]
    mu = jnp.mean(v, axis=-1, keepdims=True)
    vc = v - mu
    var = jnp.mean(vc * vc, axis=-1, keepdims=True)
    vn = (vc * lax.rsqrt(var + LN_EPS) * lng_ref[...] + lnb_ref[...]).astype(BF16)
    pi = lax.broadcasted_iota(I32, (GMLP_BLOCK, GMLP_BLOCK), 0) // CHUNK
    pj = lax.broadcasted_iota(I32, (GMLP_BLOCK, GMLP_BLOCK), 1) // CHUNK
    for g in range(GMLP_GROUPS):
        wm = jnp.where(pj <= pi, ws_ref[g], 0.0).astype(BF16)
        cs = slice(g * GMLP_GROUP_CH, (g + 1) * GMLP_GROUP_CH)
        for n in range(tm // GMLP_BLOCK):
            rs = slice(n * GMLP_BLOCK, (n + 1) * GMLP_BLOCK)
            sv = jnp.dot(wm, vn[rs, cs], preferred_element_type=F32) + bs_ref[g]
            ya_scr[rs, cs] = u[rs, cs] * sv
    ya_ref[...] = _rms(ya_scr[...], ga_ref[...]).astype(BF16)

    lf = (jnp.minimum(ft, 0.0) - jnp.log1p(jnp.exp(-jnp.abs(ft)))) * LOG2E
    zpad = jnp.zeros((32 - BIAS_PARTS * FOX_HEADS, tm), F32)
    stack = jnp.concatenate(list(_split3(lf)) + [zpad], axis=0).astype(BF16)
    cum = jnp.dot(stack, tri_ref[...], preferred_element_type=F32)
    c = cum[0:8] + cum[8:16] + cum[16:24] + carry_ref[:, 0:1]
    carry_ref[...] = jnp.broadcast_to(c[:, tm - 1:tm], carry_ref.shape)
    a1, a2, a3 = _split3(c)

    si = lax.broadcasted_iota(I32, (8, tm), 0)
    zrows = jnp.zeros((HEAD_PAD - FOX_HEAD_DIM - 8, tm), F32)
    for hd in range(FOX_HEADS):
        b1 = jnp.broadcast_to(a1[hd:hd + 1], (8, tm))
        b2 = jnp.broadcast_to(a2[hd:hd + 1], (8, tm))
        b3 = jnp.broadcast_to(a3[hd:hd + 1], (8, tm))
        ext = jnp.where(si == 0, b1, jnp.where(si == 1, b2, jnp.where(si == 2, b3,
                        jnp.where(si < 2 * BIAS_PARTS, 1.0, 0.0))))
        blk = jnp.concatenate([qt[hd * FOX_HEAD_DIM:(hd + 1) * FOX_HEAD_DIM], ext, zrows], axis=0)
        qt_ref[0, hd * HEAD_PAD:(hd + 1) * HEAD_PAD, :] = blk.astype(BF16)

    cstack = jnp.concatenate([a1, a2, a3, zpad], axis=0).astype(BF16)
    kext = lax.dot_general(cstack, selk_ref[...], _TN, preferred_element_type=F32)
    k_ref[...] = (kp + kext + onesk_ref[...]).astype(BF16)


def _mix_in(group, h, B, S, prm, after):
    T, D = h.shape
    Bg = B // MOE_GROUPS
    Tg = Bg * S
    tm = min(ROW_TILE, S)
    ns = S // tm
    H = FOX_HEADS
    const = lambda shape: pl.BlockSpec(shape, lambda b, s: (0,) * len(shape))
    return pl.pallas_call(
        _mix_in_kernel,
        grid=(Bg, ns),
        in_specs=[
            pl.BlockSpec((tm, D), lambda b, s: ((group * Bg + b) * ns + s, 0)),
            const((1, D)),
            const((D, 2 * GMLP_WIDTH)),
            const((FOX_WIDTH, D)),
            const((D, H * HEAD_PAD)),
            const((FOX_WIDTH, D)),
            const((H, D)),
            const((H, 1)),
            const((1, GMLP_WIDTH)),
            const((1, GMLP_WIDTH)),
            const((GMLP_GROUPS, GMLP_BLOCK, GMLP_BLOCK)),
            const((GMLP_GROUPS, GMLP_BLOCK, 1)),
            const((1, GMLP_WIDTH)),
            const((tm, tm)),
            const((32, H * HEAD_PAD)),
            const((1, H * HEAD_PAD)),
            pl.BlockSpec(memory_space=pl.ANY),
        ],
        out_specs=[
            pl.BlockSpec((tm, GMLP_WIDTH), lambda b, s: (b * ns + s, 0)),
            pl.BlockSpec((1, H * HEAD_PAD, tm), lambda b, s: (b, 0, s)),
            pl.BlockSpec((tm, H * HEAD_PAD), lambda b, s: (b * ns + s, 0)),
            pl.BlockSpec((1, FOX_WIDTH, tm), lambda b, s: (b, 0, s)),
        ],
        out_shape=[
            jax.ShapeDtypeStruct((Tg, GMLP_WIDTH), BF16),
            jax.ShapeDtypeStruct((Bg, H * HEAD_PAD, S), BF16),
            jax.ShapeDtypeStruct((Tg, H * HEAD_PAD), BF16),
            jax.ShapeDtypeStruct((Bg, FOX_WIDTH, S), BF16),
        ],
        scratch_shapes=[pltpu.VMEM((H, 128), F32), pltpu.VMEM((tm, GMLP_WIDTH), F32)],
        compiler_params=pltpu.CompilerParams(
            dimension_semantics=("arbitrary", "arbitrary"), vmem_limit_bytes=VMEM_LIMIT),
        name="mix_in",
    )(h, prm["g_mix"], prm["wuv"], prm["wqt"], prm["wk"], prm["wvt"], prm["wft"], prm["b_f"],
      prm["ln_g"], prm["ln_b"], prm["w_s"], prm["b_s"], prm["g_a"], prm["tri_incl"], prm["selk"], prm["onesk"],
      after)


def _fox_attn_kernel(qt_ref, k_ref, vt_ref, ot_ref):
    S = k_ref.shape[0]
    t = min(ATTN_TILE, S)
    nq = S // t
    ones_rows = jnp.ones((ACC_ROWS - FOX_HEAD_DIM, t), BF16)
    causal = lax.broadcasted_iota(I32, (t, t), 0) <= lax.broadcasted_iota(I32, (t, t), 1)
    pairs = [(qi, kj) for qi in range(nq) for kj in range(qi + 1)]

    def scores(qi, kj):
        out = []
        for hh in range(ATTN_HEADS):
            hs = slice(hh * HEAD_PAD, (hh + 1) * HEAD_PAD)
            s = jnp.dot(k_ref[kj * t:(kj + 1) * t, hs], qt_ref[0, hs, qi * t:(qi + 1) * t],
                        preferred_element_type=F32)
            out.append(jnp.where(causal, s, NEG_BIG) if kj == qi else s)
        return out

    s_next = scores(*pairs[0])
    state = None
    for n, (qi, kj) in enumerate(pairs):
        s_cur = s_next
        if n + 1 < len(pairs):
            s_next = scores(*pairs[n + 1])
        new_state = []
        for hh in range(ATTN_HEADS):
            s = s_cur[hh]
            v = vt_ref[0, hh * FOX_HEAD_DIM:(hh + 1) * FOX_HEAD_DIM, kj * t:(kj + 1) * t]
            v_aug = jnp.concatenate([v, ones_rows], axis=0)
            if kj == 0:
                m_new = jnp.max(s, axis=0, keepdims=True)
                acc = jnp.dot(v_aug, jnp.exp2(s - m_new).astype(BF16), preferred_element_type=F32)
            else:
                m, acc = state[hh]
                m_new = jnp.maximum(m, jnp.max(s, axis=0, keepdims=True))
                p = jnp.exp2(s - m_new).astype(BF16)
                acc = jnp.exp2(m - m_new) * acc + jnp.dot(v_aug, p, preferred_element_type=F32)
            new_state.append((m_new, acc))
        state = new_state
        if kj == qi:
            for hh in range(ATTN_HEADS):
                acc = state[hh][1]
                o = acc[:FOX_HEAD_DIM] / acc[FOX_HEAD_DIM:FOX_HEAD_DIM + 1]
                ot_ref[0, hh * FOX_HEAD_DIM:(hh + 1) * FOX_HEAD_DIM, qi * t:(qi + 1) * t] = o.astype(BF16)


def _fox_attn(qt, kx, vt, B, S):
    G = ATTN_HEADS
    return pl.pallas_call(
        _fox_attn_kernel,
        grid=(B, FOX_HEADS // G),
        in_specs=[
            pl.BlockSpec((1, G * HEAD_PAD, S), lambda b, h: (b, h, 0)),
            pl.BlockSpec((S, G * HEAD_PAD), lambda b, h: (b, h)),
            pl.BlockSpec((1, G * FOX_HEAD_DIM, S), lambda b, h: (b, h, 0)),
        ],
        out_specs=pl.BlockSpec((1, G * FOX_HEAD_DIM, S), lambda b, h: (b, h, 0)),
        out_shape=jax.ShapeDtypeStruct((B, FOX_WIDTH, S), BF16),
        compiler_params=pltpu.CompilerParams(
            dimension_semantics=("parallel", "parallel"), vmem_limit_bytes=VMEM_LIMIT),
        name="fox_attn",
    )(qt, kx, vt)


def _out_router_kernel(h_ref, ya_ref, ybt_ref, woa_ref, wob_ref, gb_ref, gmoe_ref, wrt_ref, br_ref, tri_ref,
                       h1_ref, xp_ref, tope_ref, rank_ref, gates_ref, counts_ref, carry_ref):
    tm = h_ref.shape[0]

    @pl.when(pl.program_id(0) == 0)
    def _():
        carry_ref[...] = jnp.zeros_like(carry_ref)

    yb = ybt_ref[0].astype(F32)
    ybn = (yb * lax.rsqrt(jnp.mean(yb * yb, axis=0, keepdims=True) + RMS_EPS) * gb_ref[...]).astype(BF16)
    y = jnp.dot(ya_ref[...], woa_ref[...], preferred_element_type=F32)
    y = y + lax.dot_general(ybn, wob_ref[...], _TN, preferred_element_type=F32)
    h1 = h_ref[...] + y
    h1_ref[...] = h1
    xn = _rms(h1, gmoe_ref[...])
    _store_planes(xp_ref, _pack_rows(xn))

    logits = lax.dot_general(wrt_ref[...], xn.astype(BF16), _NT, preferred_element_type=F32) + br_ref[...]
    ri = lax.broadcasted_iota(I32, (N_EXPERTS, tm), 0)
    vals, idxs = [], []
    l = logits
    for _ in range(TOP_K):
        m = jnp.max(l, axis=0, keepdims=True)
        idx = jnp.min(jnp.where(l == m, ri, N_EXPERTS), axis=0, keepdims=True)
        vals.append(m)
        idxs.append(idx)
        l = jnp.where(ri == idx, -jnp.inf, l)
    es = [jnp.exp(vk - vals[0]) for vk in vals]
    den = es[0] + es[1] + es[2] + es[3]
    zrow = jnp.zeros((8 - TOP_K, tm), F32)
    gates_ref[...] = jnp.concatenate([e / den for e in es] + [zrow], axis=0).T

    sel = jnp.zeros((N_EXPERTS, tm), F32)
    for idx in idxs:
        sel = sel + jnp.where(ri == idx, 1.0, 0.0)
    cnt = jnp.dot(sel.astype(BF16), tri_ref[...], preferred_element_type=F32) + carry_ref[:, 0:1]
    ranks = [jnp.sum(jnp.where(ri == idx, cnt, 0.0), axis=0, keepdims=True) for idx in idxs]
    zi = jnp.zeros((8 - TOP_K, tm), I32)
    rank_ref[...] = jnp.concatenate([r.astype(I32) for r in ranks] + [zi], axis=0)
    tope_ref[...] = jnp.concatenate(idxs + [zi], axis=0)
    total = carry_ref[...] + jnp.sum(sel, axis=1, keepdims=True)
    carry_ref[...] = total
    counts_ref[...] = total.astype(I32)


def _out_router(group, h, ya, ybt, B, S, prm, tri_strict):
    T, D = h.shape
    Tg = T // MOE_GROUPS
    tm = min(ROW_TILE, S)
    ns = S // tm
    E = N_EXPERTS
    planes = (D // 2) // SC_ROW_WORDS
    off = group * (Tg // tm)
    const = lambda shape: pl.BlockSpec(shape, lambda i: (0,) * len(shape))
    return pl.pallas_call(
        _out_router_kernel,
        grid=(Tg // tm,),
        in_specs=[
            pl.BlockSpec((tm, D), lambda i: (off + i, 0)),
            pl.BlockSpec((tm, GMLP_WIDTH), lambda i: (i, 0)),
            pl.BlockSpec((1, FOX_WIDTH, tm), lambda i: (i // ns, 0, i % ns)),
            const((GMLP_WIDTH, D)),
            const((FOX_WIDTH, D)),
            const((FOX_WIDTH, 1)),
            const((1, D)),
            const((E, D)),
            const((E, 1)),
            const((tm, tm)),
        ],
        out_specs=[
            pl.BlockSpec((tm, D), lambda i: (i, 0)),
            pl.BlockSpec((planes, tm, SC_ROW_WORDS), lambda i: (0, i, 0)),
            pl.BlockSpec((8, tm), lambda i: (0, i)),
            pl.BlockSpec((8, tm), lambda i: (0, i)),
            pl.BlockSpec((tm, 8), lambda i: (i, 0)),
            const((E, 128)),
        ],
        out_shape=[
            jax.ShapeDtypeStruct((Tg, D), F32),
            jax.ShapeDtypeStruct((planes, Tg, SC_ROW_WORDS), U32),
            jax.ShapeDtypeStruct((8, Tg), I32),
            jax.ShapeDtypeStruct((8, Tg), I32),
            jax.ShapeDtypeStruct((Tg, 8), F32),
            jax.ShapeDtypeStruct((E, 128), I32),
        ],
        scratch_shapes=[pltpu.VMEM((E, 128), F32)],
        compiler_params=pltpu.CompilerParams(
            dimension_semantics=("arbitrary",), vmem_limit_bytes=VMEM_LIMIT),
        name="out_router",
    )(h, ya, ybt, prm["woa"], prm["wob"], prm["g_b"], prm["g_moe"], prm["wrt"], prm["b_router"], tri_strict)


def _dest_kernel(ps_ref, tope_ref, rank_ref, dest_ref):
    e_sel = tope_ref[...]
    d = rank_ref[...]
    for e in range(N_EXPERTS):
        d = d + jnp.where(e_sel == e, ps_ref[e], 0)
    dest_ref[...] = d


def _dest_rows(pad_starts, tope, rank):
    T = tope.shape[1]
    tc = min(4096, T)
    return pl.pallas_call(
        _dest_kernel,
        grid_spec=pltpu.PrefetchScalarGridSpec(
            num_scalar_prefetch=1,
            grid=(T // tc,),
            in_specs=[pl.BlockSpec((8, tc), lambda i, ps: (0, i)),
                      pl.BlockSpec((8, tc), lambda i, ps: (0, i))],
            out_specs=pl.BlockSpec((8, tc), lambda i, ps: (0, i)),
        ),
        out_shape=jax.ShapeDtypeStruct((8, T), I32),
        name="dest_rows",
    )(pad_starts, tope, rank)


def _sc_mesh():
    return plsc.VectorSubcoreMesh(core_axis_name="c", subcore_axis_name="s")


def _sc_scatter_rows(x, idx, n_out):
    R, W = x.shape
    N = idx.shape[0]
    nsrc = R // SC_WINDOW

    @functools.partial(pl.kernel, out_type=jax.ShapeDtypeStruct((n_out, W), x.dtype), mesh=_sc_mesh())
    def k(x_hbm, i_hbm, o_hbm):
        def body(x_vmem, i_vmem):
            pltpu.sync_copy(x_vmem, o_hbm.at[i_vmem.at[0]])

        pltpu.emit_pipeline(
            body,
            grid=(N // SC_WINDOW,),
            in_specs=[pl.BlockSpec((SC_WINDOW, W), lambda i: (i % nsrc, 0)),
                      pl.BlockSpec((1, SC_WINDOW), lambda i: (0, i))],
            out_specs=[],
            core_axis_name=("c", "s"),
            dimension_semantics=(pltpu.PARALLEL,),
        )(x_hbm, i_hbm)

    return k(x, idx.reshape(1, N))


def _sc_gather_rows(table, idx):
    V, W = table.shape
    N = idx.shape[0]

    @functools.partial(pl.kernel, out_type=jax.ShapeDtypeStruct((N, W), table.dtype), mesh=_sc_mesh())
    def k(t_hbm, i_hbm, o_hbm):
        def body(i_vmem, o_vmem):
            pltpu.sync_copy(t_hbm.at[i_vmem.at[0]], o_vmem)

        pltpu.emit_pipeline(
            body,
            grid=(N // SC_WINDOW,),
            in_specs=[pl.BlockSpec((1, SC_WINDOW), lambda i: (0, i))],
            out_specs=[pl.BlockSpec((SC_WINDOW, W), lambda i: (i, 0))],
            core_axis_name=("c", "s"),
            dimension_semantics=(pltpu.PARALLEL,),
        )(i_hbm, o_hbm)

    return k(table, idx.reshape(1, N))


def _expert_kernel(be_ref, nv_ref, fs_ref, nxt_ref, xs_ref, wgu_hbm, bgu_ref, wdn_hbm, bdn_ref, ys_ref,
                   wgu_f32, wdn_f32, wgu_bf, wdn_bf, sem, *, layer):
    F = wdn_bf.shape[0]

    def weight_copies(e, slot):
        return (pltpu.make_async_copy(wgu_hbm.at[layer, e], wgu_f32.at[slot], sem.at[slot, 0]),
                pltpu.make_async_copy(wdn_hbm.at[layer, e], wdn_f32.at[slot], sem.at[slot, 1]))

    @pl.when(pl.program_id(0) == 0)
    def _():
        for cp in weight_copies(be_ref[0], 0):
            cp.start()

    for j in range(EXPERT_STEP_BLOCKS):
        b = pl.program_id(0) * EXPERT_STEP_BLOCKS + j
        rs = slice(j * EXPERT_ROWS, (j + 1) * EXPERT_ROWS)
        e = be_ref[b]
        nv = nv_ref[b]
        fs = fs_ref[b]

        @pl.when(fs >= 0)
        def _():
            for cp in weight_copies(e, fs):
                cp.wait()
            wgu_bf[...] = wgu_f32[fs].astype(BF16)
            wdn_bf[...] = wdn_f32[fs].astype(BF16)
            nxt = nxt_ref[b]

            @pl.when(nxt >= 0)
            def _():
                for cp in weight_copies(nxt, 1 - fs):
                    cp.start(priority=1)

        @pl.when(nv > 0)
        def _():
            x = _unpack_rows(jnp.concatenate([xs_ref[pn, rs, :] for pn in range(xs_ref.shape[0])], axis=1))
            rows = lax.broadcasted_iota(I32, x.shape, 0)
            x = jnp.where(rows < nv, x, 0.0).astype(BF16)
            hgu = jnp.dot(x, wgu_bf[...], preferred_element_type=F32) + bgu_ref[e]
            g = jnp.minimum(hgu[:, :F], SWIGLU_LIMIT)
            lin = jnp.clip(hgu[:, F:], -SWIGLU_LIMIT, SWIGLU_LIMIT)
            act = g * jax.nn.sigmoid(SWIGLU_ALPHA * g) * (lin + 1.0)
            y = _pack_rows(jnp.dot(act.astype(BF16), wdn_bf[...], preferred_element_type=F32) + bdn_ref[e])
            for pn in range(ys_ref.shape[0]):
                ys_ref[pn, rs, :] = y[:, pn * SC_ROW_WORDS:(pn + 1) * SC_ROW_WORDS]

        @pl.when(nv == 0)
        def _():
            ys_ref[:, rs, :] = jnp.zeros((ys_ref.shape[0], EXPERT_ROWS, SC_ROW_WORDS), U32)


def _experts(layer, plan, xs, w_gu, b_gu, w_dn, b_dn):
    planes, P, _ = xs.shape
    _, E, D, F2 = w_gu.shape
    F = F2 // 2
    step_rows = EXPERT_STEP_BLOCKS * EXPERT_ROWS
    by_step = lambda i, be, nv, fs, nxt: (0, i, 0)
    layer_all = lambda i, be, nv, fs, nxt: (layer, 0, 0, 0)
    return pl.pallas_call(
        functools.partial(_expert_kernel, layer=layer),
        grid_spec=pltpu.PrefetchScalarGridSpec(
            num_scalar_prefetch=4,
            grid=(P // step_rows,),
            in_specs=[
                pl.BlockSpec((planes, step_rows, SC_ROW_WORDS), by_step),
                pl.BlockSpec(memory_space=pl.ANY),
                pl.BlockSpec((None, E, 1, F2), layer_all),
                pl.BlockSpec(memory_space=pl.ANY),
                pl.BlockSpec((None, E, 1, D), layer_all),
            ],
            out_specs=pl.BlockSpec((planes, step_rows, SC_ROW_WORDS), by_step),
            scratch_shapes=[
                pltpu.VMEM((2, D, F2), F32), pltpu.VMEM((2, F, D), F32),
                pltpu.VMEM((D, F2), BF16), pltpu.VMEM((F, D), BF16),
                pltpu.SemaphoreType.DMA((2, 2)),
            ],
        ),
        out_shape=jax.ShapeDtypeStruct((planes, P, SC_ROW_WORDS), U32),
        compiler_params=pltpu.CompilerParams(
            dimension_semantics=("arbitrary",), vmem_limit_bytes=VMEM_LIMIT),
        name="experts",
    )(plan["block_e"], plan["block_nv"], plan["first_slot"], plan["next_e"],
      xs, w_gu, b_gu.reshape(-1, E, 1, F2), w_dn, b_dn.reshape(-1, E, 1, D))


def _combine_ple_kernel(h1_ref, yg_ref, gates_ref, gple_ref, wpg_ref, p_ref, wple_ref, gfin_ref, *rest, final):
    o_ref = rest[-1]
    h2 = h1_ref[...]
    gates = gates_ref[...]
    for k in range(TOP_K):
        h2 = h2 + gates[:, k:k + 1] * _unpack_rows(_load_planes(yg_ref.at[k]))
    xn = _rms(h2, gple_ref[...]).astype(BF16)
    gate = jax.nn.sigmoid(jnp.dot(xn, wpg_ref[...], preferred_element_type=F32))
    pe = jnp.dot(p_ref[...].astype(BF16), wple_ref[...], preferred_element_type=F32)
    h3 = h2 + gate * pe
    if final:
        h3 = _rms(h3, gfin_ref[...])
    o_ref[...] = h3


def _combine_ple(layer, group, h1, yg, gates, p, prm, g_final, final, h_prev):
    Tg, D = h1.shape
    T = Tg * MOE_GROUPS
    tm = min(ROW_TILE, Tg)
    PD = p.shape[2]
    planes = yg.shape[1]
    off = group * (Tg // tm)
    const = lambda shape: pl.BlockSpec(shape, lambda i: (0,) * len(shape))
    in_specs = [
        pl.BlockSpec((tm, D), lambda i: (i, 0)),
        pl.BlockSpec((TOP_K, planes, tm, SC_ROW_WORDS), lambda i: (0, 0, i, 0)),
        pl.BlockSpec((tm, 8), lambda i: (i, 0)),
        const((1, D)),
        const((D, D)),
        pl.BlockSpec((None, tm, PD), lambda i: (layer, off + i, 0)),
        const((PD, D)),
        const((1, D)),
    ]
    args = [h1, yg, gates, prm["g_ple"], prm["wpg"], p, prm["wple"], g_final]
    aliases = {}
    if h_prev is not None:
        in_specs.append(pl.BlockSpec(memory_space=pl.ANY))
        args.append(h_prev)
        aliases = {len(args) - 1: 0}
    return pl.pallas_call(
        functools.partial(_combine_ple_kernel, final=final),
        grid=(Tg // tm,),
        in_specs=in_specs,
        out_specs=pl.BlockSpec((tm, D), lambda i: (off + i, 0)),
        out_shape=jax.ShapeDtypeStruct((T, D), F32),
        input_output_aliases=aliases,
        compiler_params=pltpu.CompilerParams(
            dimension_semantics=("parallel",), vmem_limit_bytes=VMEM_LIMIT),
        name="combine_ple",
    )(*args)


def _layer_params(i, w):
    D = w["w_in"].shape[1]
    H = FOX_HEADS
    w_in = w["w_in"][i]
    c1 = 2 * GMLP_WIDTH
    c2, c3, c4 = c1 + FOX_WIDTH, c1 + 2 * FOX_WIDTH, c1 + 3 * FOX_WIDTH
    wk = jnp.zeros((D, H, HEAD_PAD), F32).at[:, :, :FOX_HEAD_DIM].set(
        w_in[:, c2:c3].reshape(D, H, FOX_HEAD_DIM)).reshape(D, H * HEAD_PAD)
    return {
        "g_mix": w["g_mix"][i].reshape(1, D),
        "wuv": w_in[:, :c1].astype(BF16),
        "wqt": (w_in[:, c1:c2] * (FOX_HEAD_DIM ** -0.5 * LOG2E)).T.astype(BF16),
        "wk": wk.astype(BF16),
        "wvt": w_in[:, c3:c4].T.astype(BF16),
        "wft": w_in[:, c4:].T.astype(BF16),
        "b_f": w["b_f"][i].reshape(H, 1),
        "ln_g": w["ln_g"][i].reshape(1, GMLP_WIDTH),
        "ln_b": w["ln_b"][i].reshape(1, GMLP_WIDTH),
        "w_s": w["w_s"][i],
        "b_s": w["b_s"][i].reshape(GMLP_GROUPS, GMLP_BLOCK, 1),
        "g_a": w["g_a"][i].reshape(1, GMLP_WIDTH),
        "woa": w["w_out"][i][:GMLP_WIDTH].astype(BF16),
        "wob": w["w_out"][i][GMLP_WIDTH:].astype(BF16),
        "g_b": w["g_b"][i].reshape(FOX_WIDTH, 1),
        "g_moe": w["g_moe"][i].reshape(1, D),
        "wrt": w["w_router"][i].T.astype(BF16),
        "b_router": w["b_router"][i].reshape(N_EXPERTS, 1),
        "g_ple": w["g_ple"][i].reshape(1, D),
        "wpg": w["w_ple_gate"][i].astype(BF16),
        "wple": w["w_ple"][i].astype(BF16),
    }


def _bias_constants(tm):
    H = FOX_HEADS
    j = jnp.arange(tm)
    tri_incl = (j[:, None] <= j[None, :]).astype(BF16)
    tri_strict = (j[:, None] < j[None, :]).astype(BF16)
    part = jnp.arange(BIAS_PARTS)[:, None]
    head = jnp.arange(H)[None, :]
    rows = (part * H + head).reshape(-1)
    cols = (head * HEAD_PAD + FOX_HEAD_DIM + BIAS_PARTS + part).reshape(-1)
    selk = jnp.zeros((32, H * HEAD_PAD), F32).at[rows, cols].set(-1.0).astype(BF16)
    one_cols = (jnp.arange(H)[:, None] * HEAD_PAD + FOX_HEAD_DIM + jnp.arange(BIAS_PARTS)[None, :]).reshape(-1)
    onesk = jnp.zeros((1, H * HEAD_PAD), F32).at[0, one_cols].set(1.0)
    return tri_incl, tri_strict, selk, onesk


def _moe_dispatch_plan(counts, n_blocks):
    padded = (counts + EXPERT_ROWS - 1) // EXPERT_ROWS * EXPERT_ROWS
    pad_ends = jnp.cumsum(padded)
    pad_starts = (pad_ends - padded).astype(I32)
    block_start = jnp.arange(n_blocks, dtype=I32) * EXPERT_ROWS
    n_before = jnp.sum((pad_ends[None, :] <= block_start[:, None]).astype(I32), axis=1)
    block_e = jnp.minimum(n_before, N_EXPERTS - 1).astype(I32)
    own = block_e[:, None] == jnp.arange(N_EXPERTS, dtype=I32)[None, :]
    seg_end = jnp.sum(jnp.where(own, (pad_starts + counts)[None, :], 0), axis=1)
    block_nv = jnp.clip(seg_end - block_start, 0, EXPERT_ROWS)
    block_nv = jnp.where(block_start < pad_ends[-1], block_nv, 0).astype(I32)
    prev_e = jnp.concatenate([jnp.full((1,), -1, I32), block_e[:-1]])
    first = (block_nv > 0) & (block_e != prev_e)
    run = jnp.cumsum(first.astype(I32)) - 1
    first_slot = jnp.where(first, run % 2, -1).astype(I32)
    ar = jnp.arange(N_EXPERTS, dtype=I32)
    later = (ar[None, :] > ar[:, None]) & (counts > 0)[None, :]
    next_of = jnp.min(jnp.where(later, ar[None, :], N_EXPERTS), axis=1)
    next_of = jnp.where(next_of < N_EXPERTS, next_of, -1)
    next_e = jnp.sum(jnp.where(own, next_of[None, :], 0), axis=1).astype(I32)
    return dict(pad_starts=pad_starts, block_e=block_e, block_nv=block_nv, first_slot=first_slot, next_e=next_e)


def kernel(x, p, g_mix, w_in, ln_g, ln_b, w_s, b_s, b_f, g_a, g_b, w_out, g_moe, w_router, b_router,
           w_gu, b_gu, w_dn, b_dn, g_ple, w_ple_gate, w_ple, g_final):
    B, S, D = x.shape
    T = B * S
    depth = w_in.shape[0]
    w = dict(g_mix=g_mix, w_in=w_in, ln_g=ln_g, ln_b=ln_b, w_s=w_s, b_s=b_s, b_f=b_f, g_a=g_a, g_b=g_b,
             w_out=w_out, g_moe=g_moe, w_router=w_router, b_router=b_router, w_gu=w_gu, b_gu=b_gu,
             w_dn=w_dn, b_dn=b_dn, g_ple=g_ple, w_ple_gate=w_ple_gate, w_ple=w_ple)
    tri_incl, tri_strict, selk, onesk = _bias_constants(min(ROW_TILE, S))
    G = MOE_GROUPS
    Tg = T // G
    n_blocks = -(-(Tg * TOP_K) // EXPERT_ROWS) + N_EXPERTS
    n_blocks = -(-n_blocks // EXPERT_STEP_BLOCKS) * EXPERT_STEP_BLOCKS
    P = n_blocks * EXPERT_ROWS
    planes = (D // 2) // SC_ROW_WORDS
    plane_base = (jnp.arange(planes, dtype=I32) * P)[None, :, None]
    p_rows = p.reshape(depth, T, p.shape[-1])

    h = x.reshape(T, D)
    for i in range(depth):
        prm = _layer_params(i, w)
        prm.update(tri_incl=tri_incl, selk=selk, onesk=onesk)
        routed = []
        idx = jnp.zeros((8, 128), I32)
        for g in range(G):
            ya, qt, kx, vt = _mix_in(g, h, B, S, prm, idx)
            ybt = _fox_attn(qt, kx, vt, B // G, S)
            h1, xp, tope, rank, gates, counts = _out_router(g, h, ya, ybt, B, S, prm, tri_strict)
            plan = _moe_dispatch_plan(counts[:, 0], n_blocks)
            dest = _dest_rows(plan["pad_starts"], tope, rank)[:TOP_K]
            idx = (dest[:, None, :] + plane_base).reshape(-1)
            xs = _sc_scatter_rows(xp.reshape(planes * Tg, SC_ROW_WORDS), idx, planes * P)
            routed.append((h1, gates, plan, idx, xs))
        h_out = None
        for g, (h1, gates, plan, idx, xs) in enumerate(routed):
            ys = _experts(i, plan, xs.reshape(planes, P, SC_ROW_WORDS), w_gu, b_gu, w_dn, b_dn)
            yg = _sc_gather_rows(ys.reshape(planes * P, SC_ROW_WORDS), idx)
            h_out = _combine_ple(i, g, h1, yg.reshape(TOP_K, planes, Tg, SC_ROW_WORDS), gates, p_rows, prm,
                                 g_final.reshape(1, D), i == depth - 1, h_out)
        h = h_out
    return h.reshape(B, S, D)
```

```python
import functools

import jax
import jax.numpy as jnp
from jax import lax
from jax.experimental import pallas as pl
from jax.experimental.pallas import tpu as pltpu
from jax.experimental.pallas import tpu_sc as plsc

F32 = jnp.float32
BF16 = jnp.bfloat16
I32 = jnp.int32
U32 = jnp.uint32

CHUNK = 64
GMLP_WIDTH = 512
GMLP_GROUPS = 4
GMLP_GROUP_CH = 128
GMLP_BLOCK = 128
FOX_WIDTH = 512
FOX_HEAD_DIM = 64
FOX_HEADS = 8
N_EXPERTS = 32
TOP_K = 4
EXPERT_ROWS = 256
SWIGLU_LIMIT = 7.0
SWIGLU_ALPHA = 1.702
RMS_EPS = 1e-5
LN_EPS = 1e-5

HEAD_PAD = 128
BIAS_PARTS = 3
ROW_TILE = 512
ATTN_TILE = 512
ATTN_HEADS = 1
ACC_ROWS = 80
LOG2E = 1.4426950408889634
SC_WINDOW = 128
SC_ROW_WORDS = 256
EXPERT_STEP_BLOCKS = 4
MOE_GROUPS = 2
VMEM_LIMIT = 56 * 1024 * 1024
NEG_BIG = -1e30

_NT = (((1,), (1,)), ((), ()))
_TN = (((0,), (0,)), ((), ()))


def _rms(x, g):
    return x * lax.rsqrt(jnp.mean(x * x, axis=-1, keepdims=True) + RMS_EPS) * g


def _split3(c):
    a1 = c.astype(BF16).astype(F32)
    r1 = c - a1
    a2 = r1.astype(BF16).astype(F32)
    a3 = (r1 - a2).astype(BF16).astype(F32)
    return a1, a2, a3


def _pack_rows(x):
    n = x.shape[1] // 2
    lo = lax.bitcast_convert_type(x[:, :n].astype(BF16).astype(F32), U32) >> 16
    hi = lax.bitcast_convert_type(x[:, n:].astype(BF16).astype(F32), U32) & jnp.uint32(0xFFFF0000)
    return lo | hi


def _unpack_rows(u):
    lo = lax.bitcast_convert_type(u << 16, F32)
    hi = lax.bitcast_convert_type(u & jnp.uint32(0xFFFF0000), F32)
    return jnp.concatenate([lo, hi], axis=1)


def _store_planes(ref, u):
    for j in range(ref.shape[0]):
        ref[j] = u[:, j * SC_ROW_WORDS:(j + 1) * SC_ROW_WORDS]


def _load_planes(ref):
    return jnp.concatenate([ref[j] for j in range(ref.shape[0])], axis=1)


def _mix_in_kernel(h_ref, gmix_ref, wuv_ref, wqt_ref, wk_ref, wvt_ref, wft_ref, bf_ref, lng_ref, lnb_ref,
                   ws_ref, bs_ref, ga_ref, tri_ref, selk_ref, onesk_ref, after_ref,
                   ya_ref, qt_ref, k_ref, vt_ref, carry_ref, ya_scr):
    del after_ref
    tm = h_ref.shape[0]

    @pl.when(pl.program_id(1) == 0)
    def _():
        carry_ref[...] = jnp.zeros_like(carry_ref)

    xn = _rms(h_ref[...], gmix_ref[...]).astype(BF16)

    z = jnp.dot(xn, wuv_ref[...], preferred_element_type=F32)
    qt = lax.dot_general(wqt_ref[...], xn, _NT, preferred_element_type=F32)
    vt_ref[0] = lax.dot_general(wvt_ref[...], xn, _NT, preferred_element_type=F32).astype(BF16)
    kp = jnp.dot(xn, wk_ref[...], preferred_element_type=F32)
    ft = lax.dot_general(wft_ref[...], xn, _NT, preferred_element_type=F32) + bf_ref[...]

    z = 0.5 * z * (1.0 + lax.erf(z * (0.5 ** 0.5)))
    u = z[:, :GMLP_WIDTH]
    v = z[:, GMLP_WIDTH:]
    mu = jnp.mean(v, axis=-1, keepdims=True)
    vc = v - mu
    var = jnp.mean(vc * vc, axis=-1, keepdims=True)
    vn = (vc * lax.rsqrt(var + LN_EPS) * lng_ref[...] + lnb_ref[...]).astype(BF16)
    pi = lax.broadcasted_iota(I32, (GMLP_BLOCK, GMLP_BLOCK), 0) // CHUNK
    pj = lax.broadcasted_iota(I32, (GMLP_BLOCK, GMLP_BLOCK), 1) // CHUNK
    for g in range(GMLP_GROUPS):
        wm = jnp.where(pj <= pi, ws_ref[g], 0.0).astype(BF16)
        cs = slice(g * GMLP_GROUP_CH, (g + 1) * GMLP_GROUP_CH)
        for n in range(tm // GMLP_BLOCK):
            rs = slice(n * GMLP_BLOCK, (n + 1) * GMLP_BLOCK)
            sv = jnp.dot(wm, vn[rs, cs], preferred_element_type=F32) + bs_ref[g]
            ya_scr[rs, cs] = u[rs, cs] * sv
    ya_ref[...] = _rms(ya_scr[...], ga_ref[...]).astype(BF16)

    lf = (jnp.minimum(ft, 0.0) - jnp.log1p(jnp.exp(-jnp.abs(ft)))) * LOG2E
    zpad = jnp.zeros((32 - BIAS_PARTS * FOX_HEADS, tm), F32)
    stack = jnp.concatenate(list(_split3(lf)) + [zpad], axis=0).astype(BF16)
    cum = jnp.dot(stack, tri_ref[...], preferred_element_type=F32)
    c = cum[0:8] + cum[8:16] + cum[16:24] + carry_ref[:, 0:1]
    carry_ref[...] = jnp.broadcast_to(c[:, tm - 1:tm], carry_ref.shape)
    a1, a2, a3 = _split3(c)

    si = lax.broadcasted_iota(I32, (8, tm), 0)
    zrows = jnp.zeros((HEAD_PAD - FOX_HEAD_DIM - 8, tm), F32)
    for hd in range(FOX_HEADS):
        b1 = jnp.broadcast_to(a1[hd:hd + 1], (8, tm))
        b2 = jnp.broadcast_to(a2[hd:hd + 1], (8, tm))
        b3 = jnp.broadcast_to(a3[hd:hd + 1], (8, tm))
        ext = jnp.where(si == 0, b1, jnp.where(si == 1, b2, jnp.where(si == 2, b3,
                        jnp.where(si < 2 * BIAS_PARTS, 1.0, 0.0))))
        blk = jnp.concatenate([qt[hd * FOX_HEAD_DIM:(hd + 1) * FOX_HEAD_DIM], ext, zrows], axis=0)
        qt_ref[0, hd * HEAD_PAD:(hd + 1) * HEAD_PAD, :] = blk.astype(BF16)

    cstack = jnp.concatenate([a1, a2, a3, zpad], axis=0).astype(BF16)
    kext = lax.dot_general(cstack, selk_ref[...], _TN, preferred_element_type=F32)
    k_ref[...] = (kp + kext + onesk_ref[...]).astype(BF16)


def _mix_in(group, h, B, S, prm, after):
    T, D = h.shape
    Bg = B // MOE_GROUPS
    Tg = Bg * S
    tm = min(ROW_TILE, S)
    ns = S // tm
    H = FOX_HEADS
    const = lambda shape: pl.BlockSpec(shape, lambda b, s: (0,) * len(shape))
    return pl.pallas_call(
        _mix_in_kernel,
        grid=(Bg, ns),
        in_specs=[
            pl.BlockSpec((tm, D), lambda b, s: ((group * Bg + b) * ns + s, 0)),
            const((1, D)),
            const((D, 2 * GMLP_WIDTH)),
            const((FOX_WIDTH, D)),
            const((D, H * HEAD_PAD)),
            const((FOX_WIDTH, D)),
            const((H, D)),
            const((H, 1)),
            const((1, GMLP_WIDTH)),
            const((1, GMLP_WIDTH)),
            const((GMLP_GROUPS, GMLP_BLOCK, GMLP_BLOCK)),
            const((GMLP_GROUPS, GMLP_BLOCK, 1)),
            const((1, GMLP_WIDTH)),
            const((tm, tm)),
            const((32, H * HEAD_PAD)),
            const((1, H * HEAD_PAD)),
            pl.BlockSpec(memory_space=pl.ANY),
        ],
        out_specs=[
            pl.BlockSpec((tm, GMLP_WIDTH), lambda b, s: (b * ns + s, 0)),
            pl.BlockSpec((1, H * HEAD_PAD, tm), lambda b, s: (b, 0, s)),
            pl.BlockSpec((tm, H * HEAD_PAD), lambda b, s: (b * ns + s, 0)),
            pl.BlockSpec((1, FOX_WIDTH, tm), lambda b, s: (b, 0, s)),
        ],
        out_shape=[
            jax.ShapeDtypeStruct((Tg, GMLP_WIDTH), BF16),
            jax.ShapeDtypeStruct((Bg, H * HEAD_PAD, S), BF16),
            jax.ShapeDtypeStruct((Tg, H * HEAD_PAD), BF16),
            jax.ShapeDtypeStruct((Bg, FOX_WIDTH, S), BF16),
        ],
        scratch_shapes=[pltpu.VMEM((H, 128), F32), pltpu.VMEM((tm, GMLP_WIDTH), F32)],
        compiler_params=pltpu.CompilerParams(
            dimension_semantics=("arbitrary", "arbitrary"), vmem_limit_bytes=VMEM_LIMIT),
        name="mix_in",
    )(h, prm["g_mix"], prm["wuv"], prm["wqt"], prm["wk"], prm["wvt"], prm["wft"], prm["b_f"],
      prm["ln_g"], prm["ln_b"], prm["w_s"], prm["b_s"], prm["g_a"], prm["tri_incl"], prm["selk"], prm["onesk"],
      after)


def _fox_attn_kernel(qt_ref, k_ref, vt_ref, ot_ref):
    S = k_ref.shape[0]
    t = min(ATTN_TILE, S)
    nq = S // t
    hf = t // 2
    ones_rows = jnp.ones((ACC_ROWS - FOX_HEAD_DIM, t), BF16)
    pairs = [(qi, kj) for qi in range(nq) for kj in range(qi + 1)]

    def tri(rows, cols):
        return lax.broadcasted_iota(I32, (rows, cols), 0) <= lax.broadcasted_iota(I32, (rows, cols), 1)

    def scores(qi, kj):
        out = []
        for hh in range(ATTN_HEADS):
            hs = slice(hh * HEAD_PAD, (hh + 1) * HEAD_PAD)
            if kj < qi:
                out.append([jnp.dot(k_ref[kj * t:(kj + 1) * t, hs], qt_ref[0, hs, qi * t:(qi + 1) * t],
                                    preferred_element_type=F32)])
            else:
                top = jnp.dot(k_ref[kj * t:kj * t + hf, hs], qt_ref[0, hs, qi * t:(qi + 1) * t],
                              preferred_element_type=F32)
                bot = jnp.dot(k_ref[kj * t + hf:(kj + 1) * t, hs], qt_ref[0, hs, qi * t + hf:(qi + 1) * t],
                              preferred_element_type=F32)
                out.append([jnp.where(tri(hf, t), top, NEG_BIG), jnp.where(tri(hf, hf), bot, NEG_BIG)])
        return out

    s_next = scores(*pairs[0])
    state = None
    for n, (qi, kj) in enumerate(pairs):
        s_cur = s_next
        if n + 1 < len(pairs):
            s_next = scores(*pairs[n + 1])
        new_state = []
        for hh in range(ATTN_HEADS):
            parts = s_cur[hh]
            v = vt_ref[0, hh * FOX_HEAD_DIM:(hh + 1) * FOX_HEAD_DIM, kj * t:(kj + 1) * t]
            v_aug = jnp.concatenate([v, ones_rows], axis=0)
            mx = jnp.max(parts[0], axis=0, keepdims=True)
            if len(parts) == 2:
                mx = jnp.concatenate(
                    [mx[:, :hf], jnp.maximum(mx[:, hf:], jnp.max(parts[1], axis=0, keepdims=True))], axis=1)
            m_new = mx if kj == 0 else jnp.maximum(state[hh][0], mx)
            if len(parts) == 1:
                pv = jnp.dot(v_aug, jnp.exp2(parts[0] - m_new).astype(BF16), preferred_element_type=F32)
            else:
                pv = jnp.dot(v_aug[:, :hf], jnp.exp2(parts[0] - m_new).astype(BF16), preferred_element_type=F32)
                pv_r = jnp.dot(v_aug[:, hf:], jnp.exp2(parts[1] - m_new[:, hf:]).astype(BF16),
                               preferred_element_type=F32)
                pv = jnp.concatenate([pv[:, :hf], pv[:, hf:] + pv_r], axis=1)
            acc = pv if kj == 0 else jnp.exp2(state[hh][0] - m_new) * state[hh][1] + pv
            new_state.append((m_new, acc))
        state = new_state
        if kj == qi:
            for hh in range(ATTN_HEADS):
                acc = state[hh][1]
                o = acc[:FOX_HEAD_DIM] / acc[FOX_HEAD_DIM:FOX_HEAD_DIM + 1]
                ot_ref[0, hh * FOX_HEAD_DIM:(hh + 1) * FOX_HEAD_DIM, qi * t:(qi + 1) * t] = o.astype(BF16)


def _fox_attn(qt, kx, vt, B, S):
    G = ATTN_HEADS
    return pl.pallas_call(
        _fox_attn_kernel,
        grid=(B, FOX_HEADS // G),
        in_specs=[
            pl.BlockSpec((1, G * HEAD_PAD, S), lambda b, h: (b, h, 0)),
            pl.BlockSpec((S, G * HEAD_PAD), lambda b, h: (b, h)),
            pl.BlockSpec((1, G * FOX_HEAD_DIM, S), lambda b, h: (b, h, 0)),
        ],
        out_specs=pl.BlockSpec((1, G * FOX_HEAD_DIM, S), lambda b, h: (b, h, 0)),
        out_shape=jax.ShapeDtypeStruct((B, FOX_WIDTH, S), BF16),
        compiler_params=pltpu.CompilerParams(
            dimension_semantics=("parallel", "parallel"), vmem_limit_bytes=VMEM_LIMIT),
        name="fox_attn",
    )(qt, kx, vt)


def _out_router_kernel(h_ref, ya_ref, ybt_ref, woa_ref, wob_ref, gb_ref, gmoe_ref, wrt_ref, br_ref, tri_ref,
                       h1_ref, xp_ref, tope_ref, rank_ref, gates_ref, counts_ref, carry_ref):
    tm = h_ref.shape[0]

    @pl.when(pl.program_id(0) == 0)
    def _():
        carry_ref[...] = jnp.zeros_like(carry_ref)

    yb = ybt_ref[0].astype(F32)
    ybn = (yb * lax.rsqrt(jnp.mean(yb * yb, axis=0, keepdims=True) + RMS_EPS) * gb_ref[...]).astype(BF16)
    y = jnp.dot(ya_ref[...], woa_ref[...], preferred_element_type=F32)
    y = y + lax.dot_general(ybn, wob_ref[...], _TN, preferred_element_type=F32)
    h1 = h_ref[...] + y
    h1_ref[...] = h1
    xn = _rms(h1, gmoe_ref[...])
    _store_planes(xp_ref, _pack_rows(xn))

    logits = lax.dot_general(wrt_ref[...], xn.astype(BF16), _NT, preferred_element_type=F32) + br_ref[...]
    ri = lax.broadcasted_iota(I32, (N_EXPERTS, tm), 0)
    vals, idxs = [], []
    l = logits
    for _ in range(TOP_K):
        m = jnp.max(l, axis=0, keepdims=True)
        idx = jnp.min(jnp.where(l == m, ri, N_EXPERTS), axis=0, keepdims=True)
        vals.append(m)
        idxs.append(idx)
        l = jnp.where(ri == idx, -jnp.inf, l)
    es = [jnp.exp(vk - vals[0]) for vk in vals]
    den = es[0] + es[1] + es[2] + es[3]
    zrow = jnp.zeros((8 - TOP_K, tm), F32)
    gates_ref[...] = jnp.concatenate([e / den for e in es] + [zrow], axis=0).T

    sel = jnp.zeros((N_EXPERTS, tm), F32)
    for idx in idxs:
        sel = sel + jnp.where(ri == idx, 1.0, 0.0)
    cnt = jnp.dot(sel.astype(BF16), tri_ref[...], preferred_element_type=F32) + carry_ref[:, 0:1]
    ranks = [jnp.sum(jnp.where(ri == idx, cnt, 0.0), axis=0, keepdims=True) for idx in idxs]
    zi = jnp.zeros((8 - TOP_K, tm), I32)
    rank_ref[...] = jnp.concatenate([r.astype(I32) for r in ranks] + [zi], axis=0)
    tope_ref[...] = jnp.concatenate(idxs + [zi], axis=0)
    total = carry_ref[...] + jnp.sum(sel, axis=1, keepdims=True)
    carry_ref[...] = total
    counts_ref[...] = total.astype(I32)


def _out_router(group, h, ya, ybt, B, S, prm, tri_strict):
    T, D = h.shape
    Tg = T // MOE_GROUPS
    tm = min(ROW_TILE, S)
    ns = S // tm
    E = N_EXPERTS
    planes = (D // 2) // SC_ROW_WORDS
    off = group * (Tg // tm)
    const = lambda shape: pl.BlockSpec(shape, lambda i: (0,) * len(shape))
    return pl.pallas_call(
        _out_router_kernel,
        grid=(Tg // tm,),
        in_specs=[
            pl.BlockSpec((tm, D), lambda i: (off + i, 0)),
            pl.BlockSpec((tm, GMLP_WIDTH), lambda i: (i, 0)),
            pl.BlockSpec((1, FOX_WIDTH, tm), lambda i: (i // ns, 0, i % ns)),
            const((GMLP_WIDTH, D)),
            const((FOX_WIDTH, D)),
            const((FOX_WIDTH, 1)),
            const((1, D)),
            const((E, D)),
            const((E, 1)),
            const((tm, tm)),
        ],
        out_specs=[
            pl.BlockSpec((tm, D), lambda i: (i, 0)),
            pl.BlockSpec((planes, tm, SC_ROW_WORDS), lambda i: (0, i, 0)),
            pl.BlockSpec((8, tm), lambda i: (0, i)),
            pl.BlockSpec((8, tm), lambda i: (0, i)),
            pl.BlockSpec((tm, 8), lambda i: (i, 0)),
            const((E, 128)),
        ],
        out_shape=[
            jax.ShapeDtypeStruct((Tg, D), F32),
            jax.ShapeDtypeStruct((planes, Tg, SC_ROW_WORDS), U32),
            jax.ShapeDtypeStruct((8, Tg), I32),
            jax.ShapeDtypeStruct((8, Tg), I32),
            jax.ShapeDtypeStruct((Tg, 8), F32),
            jax.ShapeDtypeStruct((E, 128), I32),
        ],
        scratch_shapes=[pltpu.VMEM((E, 128), F32)],
        compiler_params=pltpu.CompilerParams(
            dimension_semantics=("arbitrary",), vmem_limit_bytes=VMEM_LIMIT),
        name="out_router",
    )(h, ya, ybt, prm["woa"], prm["wob"], prm["g_b"], prm["g_moe"], prm["wrt"], prm["b_router"], tri_strict)


def _dest_kernel(ps_ref, tope_ref, rank_ref, dest_ref):
    e_sel = tope_ref[...]
    d = rank_ref[...]
    for e in range(N_EXPERTS):
        d = d + jnp.where(e_sel == e, ps_ref[e], 0)
    dest_ref[...] = d


def _dest_rows(pad_starts, tope, rank):
    T = tope.shape[1]
    tc = min(4096, T)
    return pl.pallas_call(
        _dest_kernel,
        grid_spec=pltpu.PrefetchScalarGridSpec(
            num_scalar_prefetch=1,
            grid=(T // tc,),
            in_specs=[pl.BlockSpec((8, tc), lambda i, ps: (0, i)),
                      pl.BlockSpec((8, tc), lambda i, ps: (0, i))],
            out_specs=pl.BlockSpec((8, tc), lambda i, ps: (0, i)),
        ),
        out_shape=jax.ShapeDtypeStruct((8, T), I32),
        name="dest_rows",
    )(pad_starts, tope, rank)


def _sc_mesh():
    return plsc.VectorSubcoreMesh(core_axis_name="c", subcore_axis_name="s")


def _sc_scatter_rows(x, idx, n_out):
    R, W = x.shape
    N = idx.shape[0]
    nsrc = R // SC_WINDOW

    @functools.partial(pl.kernel, out_type=jax.ShapeDtypeStruct((n_out, W), x.dtype), mesh=_sc_mesh())
    def k(x_hbm, i_hbm, o_hbm):
        def body(x_vmem, i_vmem):
            pltpu.sync_copy(x_vmem, o_hbm.at[i_vmem.at[0]])

        pltpu.emit_pipeline(
            body,
            grid=(N // SC_WINDOW,),
            in_specs=[pl.BlockSpec((SC_WINDOW, W), lambda i: (i % nsrc, 0)),
                      pl.BlockSpec((1, SC_WINDOW), lambda i: (0, i))],
            out_specs=[],
            core_axis_name=("c", "s"),
            dimension_semantics=(pltpu.PARALLEL,),
        )(x_hbm, i_hbm)

    return k(x, idx.reshape(1, N))


def _sc_gather_rows(table, idx):
    V, W = table.shape
    N = idx.shape[0]

    @functools.partial(pl.kernel, out_type=jax.ShapeDtypeStruct((N, W), table.dtype), mesh=_sc_mesh())
    def k(t_hbm, i_hbm, o_hbm):
        def body(i_vmem, o_vmem):
            pltpu.sync_copy(t_hbm.at[i_vmem.at[0]], o_vmem)

        pltpu.emit_pipeline(
            body,
            grid=(N // SC_WINDOW,),
            in_specs=[pl.BlockSpec((1, SC_WINDOW), lambda i: (0, i))],
            out_specs=[pl.BlockSpec((SC_WINDOW, W), lambda i: (i, 0))],
            core_axis_name=("c", "s"),
            dimension_semantics=(pltpu.PARALLEL,),
        )(i_hbm, o_hbm)

    return k(table, idx.reshape(1, N))


def _expert_kernel(be_ref, nv_ref, fs_ref, nxt_ref, xs_ref, wgu_hbm, bgu_ref, wdn_hbm, bdn_ref, ys_ref,
                   wgu_f32, wdn_f32, wgu_bf, wdn_bf, sem, *, layer):
    F = wdn_bf.shape[0]

    def weight_copies(e, slot):
        return (pltpu.make_async_copy(wgu_hbm.at[layer, e], wgu_f32.at[slot], sem.at[slot, 0]),
                pltpu.make_async_copy(wdn_hbm.at[layer, e], wdn_f32.at[slot], sem.at[slot, 1]))

    @pl.when(pl.program_id(0) == 0)
    def _():
        for cp in weight_copies(be_ref[0], 0):
            cp.start()

    for j in range(EXPERT_STEP_BLOCKS):
        b = pl.program_id(0) * EXPERT_STEP_BLOCKS + j
        rs = slice(j * EXPERT_ROWS, (j + 1) * EXPERT_ROWS)
        e = be_ref[b]
        nv = nv_ref[b]
        fs = fs_ref[b]

        @pl.when(fs >= 0)
        def _():
            for cp in weight_copies(e, fs):
                cp.wait()
            wgu_bf[...] = wgu_f32[fs].astype(BF16)
            wdn_bf[...] = wdn_f32[fs].astype(BF16)
            nxt = nxt_ref[b]

            @pl.when(nxt >= 0)
            def _():
                for cp in weight_copies(nxt, 1 - fs):
                    cp.start(priority=1)

        @pl.when(nv > 0)
        def _():
            x = _unpack_rows(jnp.concatenate([xs_ref[pn, rs, :] for pn in range(xs_ref.shape[0])], axis=1))
            rows = lax.broadcasted_iota(I32, x.shape, 0)
            x = jnp.where(rows < nv, x, 0.0).astype(BF16)
            hgu = jnp.dot(x, wgu_bf[...], preferred_element_type=F32) + bgu_ref[e]
            g = jnp.minimum(hgu[:, :F], SWIGLU_LIMIT)
            lin = jnp.clip(hgu[:, F:], -SWIGLU_LIMIT, SWIGLU_LIMIT)
            act = g * jax.nn.sigmoid(SWIGLU_ALPHA * g) * (lin + 1.0)
            y = _pack_rows(jnp.dot(act.astype(BF16), wdn_bf[...], preferred_element_type=F32) + bdn_ref[e])
            for pn in range(ys_ref.shape[0]):
                ys_ref[pn, rs, :] = y[:, pn * SC_ROW_WORDS:(pn + 1) * SC_ROW_WORDS]

        @pl.when(nv == 0)
        def _():
            ys_ref[:, rs, :] = jnp.zeros((ys_ref.shape[0], EXPERT_ROWS, SC_ROW_WORDS), U32)


def _experts(layer, plan, xs, w_gu, b_gu, w_dn, b_dn):
    planes, P, _ = xs.shape
    _, E, D, F2 = w_gu.shape
    F = F2 // 2
    step_rows = EXPERT_STEP_BLOCKS * EXPERT_ROWS
    by_step = lambda i, be, nv, fs, nxt: (0, i, 0)
    layer_all = lambda i, be, nv, fs, nxt: (layer, 0, 0, 0)
    return pl.pallas_call(
        functools.partial(_expert_kernel, layer=layer),
        grid_spec=pltpu.PrefetchScalarGridSpec(
            num_scalar_prefetch=4,
            grid=(P // step_rows,),
            in_specs=[
                pl.BlockSpec((planes, step_rows, SC_ROW_WORDS), by_step),
                pl.BlockSpec(memory_space=pl.ANY),
                pl.BlockSpec((None, E, 1, F2), layer_all),
                pl.BlockSpec(memory_space=pl.ANY),
                pl.BlockSpec((None, E, 1, D), layer_all),
            ],
            out_specs=pl.BlockSpec((planes, step_rows, SC_ROW_WORDS), by_step),
            scratch_shapes=[
                pltpu.VMEM((2, D, F2), F32), pltpu.VMEM((2, F, D), F32),
                pltpu.VMEM((D, F2), BF16), pltpu.VMEM((F, D), BF16),
                pltpu.SemaphoreType.DMA((2, 2)),
            ],
        ),
        out_shape=jax.ShapeDtypeStruct((planes, P, SC_ROW_WORDS), U32),
        compiler_params=pltpu.CompilerParams(
            dimension_semantics=("arbitrary",), vmem_limit_bytes=VMEM_LIMIT),
        name="experts",
    )(plan["block_e"], plan["block_nv"], plan["first_slot"], plan["next_e"],
      xs, w_gu, b_gu.reshape(-1, E, 1, F2), w_dn, b_dn.reshape(-1, E, 1, D))


def _combine_ple_kernel(h1_ref, yg_ref, gates_ref, gple_ref, wpg_ref, p_ref, wple_ref, gfin_ref, *rest, final):
    o_ref = rest[-1]
    h2 = h1_ref[...]
    gates = gates_ref[...]
    for k in range(TOP_K):
        h2 = h2 + gates[:, k:k + 1] * _unpack_rows(_load_planes(yg_ref.at[k]))
    xn = _rms(h2, gple_ref[...]).astype(BF16)
    gate = jax.nn.sigmoid(jnp.dot(xn, wpg_ref[...], preferred_element_type=F32))
    pe = jnp.dot(p_ref[...].astype(BF16), wple_ref[...], preferred_element_type=F32)
    h3 = h2 + gate * pe
    if final:
        h3 = _rms(h3, gfin_ref[...])
    o_ref[...] = h3


def _combine_ple(layer, group, h1, yg, gates, p, prm, g_final, final, h_prev):
    Tg, D = h1.shape
    T = Tg * MOE_GROUPS
    tm = min(ROW_TILE, Tg)
    PD = p.shape[2]
    planes = yg.shape[1]
    off = group * (Tg // tm)
    const = lambda shape: pl.BlockSpec(shape, lambda i: (0,) * len(shape))
    in_specs = [
        pl.BlockSpec((tm, D), lambda i: (i, 0)),
        pl.BlockSpec((TOP_K, planes, tm, SC_ROW_WORDS), lambda i: (0, 0, i, 0)),
        pl.BlockSpec((tm, 8), lambda i: (i, 0)),
        const((1, D)),
        const((D, D)),
        pl.BlockSpec((None, tm, PD), lambda i: (layer, off + i, 0)),
        const((PD, D)),
        const((1, D)),
    ]
    args = [h1, yg, gates, prm["g_ple"], prm["wpg"], p, prm["wple"], g_final]
    aliases = {}
    if h_prev is not None:
        in_specs.append(pl.BlockSpec(memory_space=pl.ANY))
        args.append(h_prev)
        aliases = {len(args) - 1: 0}
    return pl.pallas_call(
        functools.partial(_combine_ple_kernel, final=final),
        grid=(Tg // tm,),
        in_specs=in_specs,
        out_specs=pl.BlockSpec((tm, D), lambda i: (off + i, 0)),
        out_shape=jax.ShapeDtypeStruct((T, D), F32),
        input_output_aliases=aliases,
        compiler_params=pltpu.CompilerParams(
            dimension_semantics=("parallel",), vmem_limit_bytes=VMEM_LIMIT),
        name="combine_ple",
    )(*args)


def _layer_params(i, w):
    D = w["w_in"].shape[1]
    H = FOX_HEADS
    w_in = w["w_in"][i]
    c1 = 2 * GMLP_WIDTH
    c2, c3, c4 = c1 + FOX_WIDTH, c1 + 2 * FOX_WIDTH, c1 + 3 * FOX_WIDTH
    wk = jnp.zeros((D, H, HEAD_PAD), F32).at[:, :, :FOX_HEAD_DIM].set(
        w_in[:, c2:c3].reshape(D, H, FOX_HEAD_DIM)).reshape(D, H * HEAD_PAD)
    return {
        "g_mix": w["g_mix"][i].reshape(1, D),
        "wuv": w_in[:, :c1].astype(BF16),
        "wqt": (w_in[:, c1:c2] * (FOX_HEAD_DIM ** -0.5 * LOG2E)).T.astype(BF16),
        "wk": wk.astype(BF16),
        "wvt": w_in[:, c3:c4].T.astype(BF16),
        "wft": w_in[:, c4:].T.astype(BF16),
        "b_f": w["b_f"][i].reshape(H, 1),
        "ln_g": w["ln_g"][i].reshape(1, GMLP_WIDTH),
        "ln_b": w["ln_b"][i].reshape(1, GMLP_WIDTH),
        "w_s": w["w_s"][i],
        "b_s": w["b_s"][i].reshape(GMLP_GROUPS, GMLP_BLOCK, 1),
        "g_a": w["g_a"][i].reshape(1, GMLP_WIDTH),
        "woa": w["w_out"][i][:GMLP_WIDTH].astype(BF16),
        "wob": w["w_out"][i][GMLP_WIDTH:].astype(BF16),
        "g_b": w["g_b"][i].reshape(FOX_WIDTH, 1),
        "g_moe": w["g_moe"][i].reshape(1, D),
        "wrt": w["w_router"][i].T.astype(BF16),
        "b_router": w["b_router"][i].reshape(N_EXPERTS, 1),
        "g_ple": w["g_ple"][i].reshape(1, D),
        "wpg": w["w_ple_gate"][i].astype(BF16),
        "wple": w["w_ple"][i].astype(BF16),
    }


def _bias_constants(tm):
    H = FOX_HEADS
    j = jnp.arange(tm)
    tri_incl = (j[:, None] <= j[None, :]).astype(BF16)
    tri_strict = (j[:, None] < j[None, :]).astype(BF16)
    part = jnp.arange(BIAS_PARTS)[:, None]
    head = jnp.arange(H)[None, :]
    rows = (part * H + head).reshape(-1)
    cols = (head * HEAD_PAD + FOX_HEAD_DIM + BIAS_PARTS + part).reshape(-1)
    selk = jnp.zeros((32, H * HEAD_PAD), F32).at[rows, cols].set(-1.0).astype(BF16)
    one_cols = (jnp.arange(H)[:, None] * HEAD_PAD + FOX_HEAD_DIM + jnp.arange(BIAS_PARTS)[None, :]).reshape(-1)
    onesk = jnp.zeros((1, H * HEAD_PAD), F32).at[0, one_cols].set(1.0)
    return tri_incl, tri_strict, selk, onesk


def _moe_dispatch_plan(counts, n_blocks):
    padded = (counts + EXPERT_ROWS - 1) // EXPERT_ROWS * EXPERT_ROWS
    pad_ends = jnp.cumsum(padded)
    pad_starts = (pad_ends - padded).astype(I32)
    block_start = jnp.arange(n_blocks, dtype=I32) * EXPERT_ROWS
    n_before = jnp.sum((pad_ends[None, :] <= block_start[:, None]).astype(I32), axis=1)
    block_e = jnp.minimum(n_before, N_EXPERTS - 1).astype(I32)
    own = block_e[:, None] == jnp.arange(N_EXPERTS, dtype=I32)[None, :]
    seg_end = jnp.sum(jnp.where(own, (pad_starts + counts)[None, :], 0), axis=1)
    block_nv = jnp.clip(seg_end - block_start, 0, EXPERT_ROWS)
    block_nv = jnp.where(block_start < pad_ends[-1], block_nv, 0).astype(I32)
    prev_e = jnp.concatenate([jnp.full((1,), -1, I32), block_e[:-1]])
    first = (block_nv > 0) & (block_e != prev_e)
    run = jnp.cumsum(first.astype(I32)) - 1
    first_slot = jnp.where(first, run % 2, -1).astype(I32)
    ar = jnp.arange(N_EXPERTS, dtype=I32)
    later = (ar[None, :] > ar[:, None]) & (counts > 0)[None, :]
    next_of = jnp.min(jnp.where(later, ar[None, :], N_EXPERTS), axis=1)
    next_of = jnp.where(next_of < N_EXPERTS, next_of, -1)
    next_e = jnp.sum(jnp.where(own, next_of[None, :], 0), axis=1).astype(I32)
    return dict(pad_starts=pad_starts, block_e=block_e, block_nv=block_nv, first_slot=first_slot, next_e=next_e)


def kernel(x, p, g_mix, w_in, ln_g, ln_b, w_s, b_s, b_f, g_a, g_b, w_out, g_moe, w_router, b_router,
           w_gu, b_gu, w_dn, b_dn, g_ple, w_ple_gate, w_ple, g_final):
    B, S, D = x.shape
    T = B * S
    depth = w_in.shape[0]
    w = dict(g_mix=g_mix, w_in=w_in, ln_g=ln_g, ln_b=ln_b, w_s=w_s, b_s=b_s, b_f=b_f, g_a=g_a, g_b=g_b,
             w_out=w_out, g_moe=g_moe, w_router=w_router, b_router=b_router, w_gu=w_gu, b_gu=b_gu,
             w_dn=w_dn, b_dn=b_dn, g_ple=g_ple, w_ple_gate=w_ple_gate, w_ple=w_ple)
    tri_incl, tri_strict, selk, onesk = _bias_constants(min(ROW_TILE, S))
    G = MOE_GROUPS
    Tg = T // G
    n_blocks = -(-(Tg * TOP_K) // EXPERT_ROWS) + N_EXPERTS
    n_blocks = -(-n_blocks // EXPERT_STEP_BLOCKS) * EXPERT_STEP_BLOCKS
    P = n_blocks * EXPERT_ROWS
    planes = (D // 2) // SC_ROW_WORDS
    plane_base = (jnp.arange(planes, dtype=I32) * P)[None, :, None]
    p_rows = p.reshape(depth, T, p.shape[-1])

    h = x.reshape(T, D)
    for i in range(depth):
        prm = _layer_params(i, w)
        prm.update(tri_incl=tri_incl, selk=selk, onesk=onesk)
        routed = []
        idx = jnp.zeros((8, 128), I32)
        for g in range(G):
            ya, qt, kx, vt = _mix_in(g, h, B, S, prm, idx)
            ybt = _fox_attn(qt, kx, vt, B // G, S)
            h1, xp, tope, rank, gates, counts = _out_router(g, h, ya, ybt, B, S, prm, tri_strict)
            plan = _moe_dispatch_plan(counts[:, 0], n_blocks)
            dest = _dest_rows(plan["pad_starts"], tope, rank)[:TOP_K]
            idx = (dest[:, None, :] + plane_base).reshape(-1)
            xs = _sc_scatter_rows(xp.reshape(planes * Tg, SC_ROW_WORDS), idx, planes * P)
            routed.append((h1, gates, plan, idx, xs))
        h_out = None
        for g, (h1, gates, plan, idx, xs) in enumerate(routed):
            ys = _experts(i, plan, xs.reshape(planes, P, SC_ROW_WORDS), w_gu, b_gu, w_dn, b_dn)
            yg = _sc_gather_rows(ys.reshape(planes * P, SC_ROW_WORDS), idx)
            h_out = _combine_ple(i, g, h1, yg.reshape(TOP_K, planes, Tg, SC_ROW_WORDS), gates, p_rows, prm,
                                 g_final.reshape(1, D), i == depth - 1, h_out)
        h = h_out
    return h.reshape(B, S, D)
```

```python
import functools

import jax
import jax.numpy as jnp
from jax import lax
from jax.experimental import pallas as pl
from jax.experimental.pallas import tpu as pltpu
from jax.experimental.pallas import tpu_sc as plsc

F32 = jnp.float32
BF16 = jnp.bfloat16
I32 = jnp.int32
U32 = jnp.uint32

CHUNK = 64
GMLP_WIDTH = 512
GMLP_GROUPS = 4
GMLP_GROUP_CH = 128
GMLP_BLOCK = 128
FOX_WIDTH = 512
FOX_HEAD_DIM = 64
FOX_HEADS = 8
N_EXPERTS = 32
TOP_K = 4
EXPERT_ROWS = 256
SWIGLU_LIMIT = 7.0
SWIGLU_ALPHA = 1.702
RMS_EPS = 1e-5
LN_EPS = 1e-5

HEAD_PAD = 128
BIAS_PARTS = 3
ROW_TILE = 512
ATTN_TILE = 512
ATTN_HEADS = 1
ACC_ROWS = 80
LOG2E = 1.4426950408889634
SC_WINDOW = 128
SC_ROW_WORDS = 256
EXPERT_STEP_BLOCKS = 8
MOE_GROUPS = 2
VMEM_LIMIT = 56 * 1024 * 1024
NEG_BIG = -1e30

_NT = (((1,), (1,)), ((), ()))
_TN = (((0,), (0,)), ((), ()))


def _rms(x, g):
    return x * lax.rsqrt(jnp.mean(x * x, axis=-1, keepdims=True) + RMS_EPS) * g


def _split3(c):
    a1 = c.astype(BF16).astype(F32)
    r1 = c - a1
    a2 = r1.astype(BF16).astype(F32)
    a3 = (r1 - a2).astype(BF16).astype(F32)
    return a1, a2, a3


def _pack_rows(x):
    n = x.shape[1] // 2
    lo = lax.bitcast_convert_type(x[:, :n].astype(BF16).astype(F32), U32) >> 16
    hi = lax.bitcast_convert_type(x[:, n:].astype(BF16).astype(F32), U32) & jnp.uint32(0xFFFF0000)
    return lo | hi


def _unpack_rows(u):
    lo = lax.bitcast_convert_type(u << 16, F32)
    hi = lax.bitcast_convert_type(u & jnp.uint32(0xFFFF0000), F32)
    return jnp.concatenate([lo, hi], axis=1)


def _store_planes(ref, u):
    for j in range(ref.shape[0]):
        ref[j] = u[:, j * SC_ROW_WORDS:(j + 1) * SC_ROW_WORDS]


def _load_planes(ref):
    return jnp.concatenate([ref[j] for j in range(ref.shape[0])], axis=1)


def _mix_in_kernel(h_ref, gmix_ref, wuv_ref, wqt_ref, wk_ref, wvt_ref, wft_ref, bf_ref, lng_ref, lnb_ref,
                   ws_ref, bs_ref, ga_ref, tri_ref, selk_ref, onesk_ref, after_ref,
                   ya_ref, qt_ref, k_ref, vt_ref, carry_ref, ya_scr):
    del after_ref
    tm = h_ref.shape[0]

    @pl.when(pl.program_id(1) == 0)
    def _():
        carry_ref[...] = jnp.zeros_like(carry_ref)

    xn = _rms(h_ref[...], gmix_ref[...]).astype(BF16)

    z = jnp.dot(xn, wuv_ref[...], preferred_element_type=F32)
    qt = lax.dot_general(wqt_ref[...], xn, _NT, preferred_element_type=F32)
    vt_ref[0] = lax.dot_general(wvt_ref[...], xn, _NT, preferred_element_type=F32).astype(BF16)
    kp = jnp.dot(xn, wk_ref[...], preferred_element_type=F32)
    ft = lax.dot_general(wft_ref[...], xn, _NT, preferred_element_type=F32) + bf_ref[...]

    z = 0.5 * z * (1.0 + lax.erf(z * (0.5 ** 0.5)))
    u = z[:, :GMLP_WIDTH]
    v = z[:, GMLP_WIDTH:]
    mu = jnp.mean(v, axis=-1, keepdims=True)
    vc = v - mu
    var = jnp.mean(vc * vc, axis=-1, keepdims=True)
    vn = (vc * lax.rsqrt(var + LN_EPS) * lng_ref[...] + lnb_ref[...]).astype(BF16)
    pi = lax.broadcasted_iota(I32, (GMLP_BLOCK, GMLP_BLOCK), 0) // CHUNK
    pj = lax.broadcasted_iota(I32, (GMLP_BLOCK, GMLP_BLOCK), 1) // CHUNK
    for g in range(GMLP_GROUPS):
        wm = jnp.where(pj <= pi, ws_ref[g], 0.0).astype(BF16)
        cs = slice(g * GMLP_GROUP_CH, (g + 1) * GMLP_GROUP_CH)
        for n in range(tm // GMLP_BLOCK):
            rs = slice(n * GMLP_BLOCK, (n + 1) * GMLP_BLOCK)
            sv = jnp.dot(wm, vn[rs, cs], preferred_element_type=F32) + bs_ref[g]
            ya_scr[rs, cs] = u[rs, cs] * sv
    ya_ref[...] = _rms(ya_scr[...], ga_ref[...]).astype(BF16)

    lf = (jnp.minimum(ft, 0.0) - jnp.log1p(jnp.exp(-jnp.abs(ft)))) * LOG2E
    zpad = jnp.zeros((32 - BIAS_PARTS * FOX_HEADS, tm), F32)
    stack = jnp.concatenate(list(_split3(lf)) + [zpad], axis=0).astype(BF16)
    cum = jnp.dot(stack, tri_ref[...], preferred_element_type=F32)
    c = cum[0:8] + cum[8:16] + cum[16:24] + carry_ref[:, 0:1]
    carry_ref[...] = jnp.broadcast_to(c[:, tm - 1:tm], carry_ref.shape)
    a1, a2, a3 = _split3(c)

    si = lax.broadcasted_iota(I32, (8, tm), 0)
    zrows = jnp.zeros((HEAD_PAD - FOX_HEAD_DIM - 8, tm), F32)
    for hd in range(FOX_HEADS):
        b1 = jnp.broadcast_to(a1[hd:hd + 1], (8, tm))
        b2 = jnp.broadcast_to(a2[hd:hd + 1], (8, tm))
        b3 = jnp.broadcast_to(a3[hd:hd + 1], (8, tm))
        ext = jnp.where(si == 0, b1, jnp.where(si == 1, b2, jnp.where(si == 2, b3,
                        jnp.where(si < 2 * BIAS_PARTS, 1.0, 0.0))))
        blk = jnp.concatenate([qt[hd * FOX_HEAD_DIM:(hd + 1) * FOX_HEAD_DIM], ext, zrows], axis=0)
        qt_ref[0, hd * HEAD_PAD:(hd + 1) * HEAD_PAD, :] = blk.astype(BF16)

    cstack = jnp.concatenate([a1, a2, a3, zpad], axis=0).astype(BF16)
    kext = lax.dot_general(cstack, selk_ref[...], _TN, preferred_element_type=F32)
    k_ref[...] = (kp + kext + onesk_ref[...]).astype(BF16)


def _mix_in(group, h, B, S, prm, after):
    T, D = h.shape
    Bg = B // MOE_GROUPS
    Tg = Bg * S
    tm = min(ROW_TILE, S)
    ns = S // tm
    H = FOX_HEADS
    const = lambda shape: pl.BlockSpec(shape, lambda b, s: (0,) * len(shape))
    return pl.pallas_call(
        _mix_in_kernel,
        grid=(Bg, ns),
        in_specs=[
            pl.BlockSpec((tm, D), lambda b, s: ((group * Bg + b) * ns + s, 0)),
            const((1, D)),
            const((D, 2 * GMLP_WIDTH)),
            const((FOX_WIDTH, D)),
            const((D, H * HEAD_PAD)),
            const((FOX_WIDTH, D)),
            const((H, D)),
            const((H, 1)),
            const((1, GMLP_WIDTH)),
            const((1, GMLP_WIDTH)),
            const((GMLP_GROUPS, GMLP_BLOCK, GMLP_BLOCK)),
            const((GMLP_GROUPS, GMLP_BLOCK, 1)),
            const((1, GMLP_WIDTH)),
            const((tm, tm)),
            const((32, H * HEAD_PAD)),
            const((1, H * HEAD_PAD)),
            pl.BlockSpec(memory_space=pl.ANY),
        ],
        out_specs=[
            pl.BlockSpec((tm, GMLP_WIDTH), lambda b, s: (b * ns + s, 0)),
            pl.BlockSpec((1, H * HEAD_PAD, tm), lambda b, s: (b, 0, s)),
            pl.BlockSpec((tm, H * HEAD_PAD), lambda b, s: (b * ns + s, 0)),
            pl.BlockSpec((1, FOX_WIDTH, tm), lambda b, s: (b, 0, s)),
        ],
        out_shape=[
            jax.ShapeDtypeStruct((Tg, GMLP_WIDTH), BF16),
            jax.ShapeDtypeStruct((Bg, H * HEAD_PAD, S), BF16),
            jax.ShapeDtypeStruct((Tg, H * HEAD_PAD), BF16),
            jax.ShapeDtypeStruct((Bg, FOX_WIDTH, S), BF16),
        ],
        scratch_shapes=[pltpu.VMEM((H, 128), F32), pltpu.VMEM((tm, GMLP_WIDTH), F32)],
        compiler_params=pltpu.CompilerParams(
            dimension_semantics=("arbitrary", "arbitrary"), vmem_limit_bytes=VMEM_LIMIT),
        name="mix_in",
    )(h, prm["g_mix"], prm["wuv"], prm["wqt"], prm["wk"], prm["wvt"], prm["wft"], prm["b_f"],
      prm["ln_g"], prm["ln_b"], prm["w_s"], prm["b_s"], prm["g_a"], prm["tri_incl"], prm["selk"], prm["onesk"],
      after)


def _fox_attn_kernel(qt_ref, k_ref, vt_ref, ot_ref):
    S = k_ref.shape[0]
    t = min(ATTN_TILE, S)
    nq = S // t
    hf = t // 2
    ones_rows = jnp.ones((ACC_ROWS - FOX_HEAD_DIM, t), BF16)
    pairs = [(qi, kj) for qi in range(nq) for kj in range(qi + 1)]

    def tri(rows, cols):
        return lax.broadcasted_iota(I32, (rows, cols), 0) <= lax.broadcasted_iota(I32, (rows, cols), 1)

    def scores(qi, kj):
        out = []
        for hh in range(ATTN_HEADS):
            hs = slice(hh * HEAD_PAD, (hh + 1) * HEAD_PAD)
            if kj < qi:
                out.append([jnp.dot(k_ref[kj * t:(kj + 1) * t, hs], qt_ref[0, hs, qi * t:(qi + 1) * t],
                                    preferred_element_type=F32)])
            else:
                top = jnp.dot(k_ref[kj * t:kj * t + hf, hs], qt_ref[0, hs, qi * t:(qi + 1) * t],
                              preferred_element_type=F32)
                bot = jnp.dot(k_ref[kj * t + hf:(kj + 1) * t, hs], qt_ref[0, hs, qi * t + hf:(qi + 1) * t],
                              preferred_element_type=F32)
                out.append([jnp.where(tri(hf, t), top, NEG_BIG), jnp.where(tri(hf, hf), bot, NEG_BIG)])
        return out

    s_next = scores(*pairs[0])
    state = None
    for n, (qi, kj) in enumerate(pairs):
        s_cur = s_next
        if n + 1 < len(pairs):
            s_next = scores(*pairs[n + 1])
        new_state = []
        for hh in range(ATTN_HEADS):
            parts = s_cur[hh]
            v = vt_ref[0, hh * FOX_HEAD_DIM:(hh + 1) * FOX_HEAD_DIM, kj * t:(kj + 1) * t]
            v_aug = jnp.concatenate([v, ones_rows], axis=0)
            mx = jnp.max(parts[0], axis=0, keepdims=True)
            if len(parts) == 2:
                mx = jnp.concatenate(
                    [mx[:, :hf], jnp.maximum(mx[:, hf:], jnp.max(parts[1], axis=0, keepdims=True))], axis=1)
            m_new = mx if kj == 0 else jnp.maximum(state[hh][0], mx)
            if len(parts) == 1:
                pv = jnp.dot(v_aug, jnp.exp2(parts[0] - m_new).astype(BF16), preferred_element_type=F32)
            else:
                pv = jnp.dot(v_aug[:, :hf], jnp.exp2(parts[0] - m_new).astype(BF16), preferred_element_type=F32)
                pv_r = jnp.dot(v_aug[:, hf:], jnp.exp2(parts[1] - m_new[:, hf:]).astype(BF16),
                               preferred_element_type=F32)
                pv = jnp.concatenate([pv[:, :hf], pv[:, hf:] + pv_r], axis=1)
            acc = pv if kj == 0 else jnp.exp2(state[hh][0] - m_new) * state[hh][1] + pv
            new_state.append((m_new, acc))
        state = new_state
        if kj == qi:
            for hh in range(ATTN_HEADS):
                acc = state[hh][1]
                o = acc[:FOX_HEAD_DIM] / acc[FOX_HEAD_DIM:FOX_HEAD_DIM + 1]
                ot_ref[0, hh * FOX_HEAD_DIM:(hh + 1) * FOX_HEAD_DIM, qi * t:(qi + 1) * t] = o.astype(BF16)


def _fox_attn(qt, kx, vt, B, S):
    G = ATTN_HEADS
    return pl.pallas_call(
        _fox_attn_kernel,
        grid=(B, FOX_HEADS // G),
        in_specs=[
            pl.BlockSpec((1, G * HEAD_PAD, S), lambda b, h: (b, h, 0)),
            pl.BlockSpec((S, G * HEAD_PAD), lambda b, h: (b, h)),
            pl.BlockSpec((1, G * FOX_HEAD_DIM, S), lambda b, h: (b, h, 0)),
        ],
        out_specs=pl.BlockSpec((1, G * FOX_HEAD_DIM, S), lambda b, h: (b, h, 0)),
        out_shape=jax.ShapeDtypeStruct((B, FOX_WIDTH, S), BF16),
        compiler_params=pltpu.CompilerParams(
            dimension_semantics=("parallel", "parallel"), vmem_limit_bytes=VMEM_LIMIT),
        name="fox_attn",
    )(qt, kx, vt)


def _out_router_kernel(h_ref, ya_ref, ybt_ref, woa_ref, wob_ref, gb_ref, gmoe_ref, wrt_ref, br_ref, tri_ref,
                       h1_ref, xp_ref, tope_ref, rank_ref, gates_ref, counts_ref, carry_ref):
    tm = h_ref.shape[0]

    @pl.when(pl.program_id(0) == 0)
    def _():
        carry_ref[...] = jnp.zeros_like(carry_ref)

    yb = ybt_ref[0].astype(F32)
    ybn = (yb * lax.rsqrt(jnp.mean(yb * yb, axis=0, keepdims=True) + RMS_EPS) * gb_ref[...]).astype(BF16)
    y = jnp.dot(ya_ref[...], woa_ref[...], preferred_element_type=F32)
    y = y + lax.dot_general(ybn, wob_ref[...], _TN, preferred_element_type=F32)
    h1 = h_ref[...] + y
    h1_ref[...] = h1
    xn = _rms(h1, gmoe_ref[...])
    _store_planes(xp_ref, _pack_rows(xn))

    logits = lax.dot_general(wrt_ref[...], xn.astype(BF16), _NT, preferred_element_type=F32) + br_ref[...]
    ri = lax.broadcasted_iota(I32, (N_EXPERTS, tm), 0)
    vals, idxs = [], []
    l = logits
    for _ in range(TOP_K):
        m = jnp.max(l, axis=0, keepdims=True)
        idx = jnp.min(jnp.where(l == m, ri, N_EXPERTS), axis=0, keepdims=True)
        vals.append(m)
        idxs.append(idx)
        l = jnp.where(ri == idx, -jnp.inf, l)
    es = [jnp.exp(vk - vals[0]) for vk in vals]
    den = es[0] + es[1] + es[2] + es[3]
    zrow = jnp.zeros((8 - TOP_K, tm), F32)
    gates_ref[...] = jnp.concatenate([e / den for e in es] + [zrow], axis=0).T

    sel = jnp.zeros((N_EXPERTS, tm), F32)
    for idx in idxs:
        sel = sel + jnp.where(ri == idx, 1.0, 0.0)
    cnt = jnp.dot(sel.astype(BF16), tri_ref[...], preferred_element_type=F32) + carry_ref[:, 0:1]
    ranks = [jnp.sum(jnp.where(ri == idx, cnt, 0.0), axis=0, keepdims=True) for idx in idxs]
    zi = jnp.zeros((8 - TOP_K, tm), I32)
    rank_ref[...] = jnp.concatenate([r.astype(I32) for r in ranks] + [zi], axis=0)
    tope_ref[...] = jnp.concatenate(idxs + [zi], axis=0)
    total = carry_ref[...] + jnp.sum(sel, axis=1, keepdims=True)
    carry_ref[...] = total
    counts_ref[...] = total.astype(I32)


def _out_router(group, h, ya, ybt, B, S, prm, tri_strict):
    T, D = h.shape
    Tg = T // MOE_GROUPS
    tm = min(ROW_TILE, S)
    ns = S // tm
    E = N_EXPERTS
    planes = (D // 2) // SC_ROW_WORDS
    off = group * (Tg // tm)
    const = lambda shape: pl.BlockSpec(shape, lambda i: (0,) * len(shape))
    return pl.pallas_call(
        _out_router_kernel,
        grid=(Tg // tm,),
        in_specs=[
            pl.BlockSpec((tm, D), lambda i: (off + i, 0)),
            pl.BlockSpec((tm, GMLP_WIDTH), lambda i: (i, 0)),
            pl.BlockSpec((1, FOX_WIDTH, tm), lambda i: (i // ns, 0, i % ns)),
            const((GMLP_WIDTH, D)),
            const((FOX_WIDTH, D)),
            const((FOX_WIDTH, 1)),
            const((1, D)),
            const((E, D)),
            const((E, 1)),
            const((tm, tm)),
        ],
        out_specs=[
            pl.BlockSpec((tm, D), lambda i: (i, 0)),
            pl.BlockSpec((planes, tm, SC_ROW_WORDS), lambda i: (0, i, 0)),
            pl.BlockSpec((8, tm), lambda i: (0, i)),
            pl.BlockSpec((8, tm), lambda i: (0, i)),
            pl.BlockSpec((tm, 8), lambda i: (i, 0)),
            const((E, 128)),
        ],
        out_shape=[
            jax.ShapeDtypeStruct((Tg, D), F32),
            jax.ShapeDtypeStruct((planes, Tg, SC_ROW_WORDS), U32),
            jax.ShapeDtypeStruct((8, Tg), I32),
            jax.ShapeDtypeStruct((8, Tg), I32),
            jax.ShapeDtypeStruct((Tg, 8), F32),
            jax.ShapeDtypeStruct((E, 128), I32),
        ],
        scratch_shapes=[pltpu.VMEM((E, 128), F32)],
        compiler_params=pltpu.CompilerParams(
            dimension_semantics=("arbitrary",), vmem_limit_bytes=VMEM_LIMIT),
        name="out_router",
    )(h, ya, ybt, prm["woa"], prm["wob"], prm["g_b"], prm["g_moe"], prm["wrt"], prm["b_router"], tri_strict)


def _dest_kernel(ps_ref, tope_ref, rank_ref, dest_ref):
    e_sel = tope_ref[...]
    d = rank_ref[...]
    for e in range(N_EXPERTS):
        d = d + jnp.where(e_sel == e, ps_ref[e], 0)
    dest_ref[...] = d


def _dest_rows(pad_starts, tope, rank):
    T = tope.shape[1]
    tc = min(4096, T)
    return pl.pallas_call(
        _dest_kernel,
        grid_spec=pltpu.PrefetchScalarGridSpec(
            num_scalar_prefetch=1,
            grid=(T // tc,),
            in_specs=[pl.BlockSpec((8, tc), lambda i, ps: (0, i)),
                      pl.BlockSpec((8, tc), lambda i, ps: (0, i))],
            out_specs=pl.BlockSpec((8, tc), lambda i, ps: (0, i)),
        ),
        out_shape=jax.ShapeDtypeStruct((8, T), I32),
        name="dest_rows",
    )(pad_starts, tope, rank)


def _sc_mesh():
    return plsc.VectorSubcoreMesh(core_axis_name="c", subcore_axis_name="s")


def _sc_scatter_rows(x, idx, n_out):
    R, W = x.shape
    N = idx.shape[0]
    nsrc = R // SC_WINDOW

    @functools.partial(pl.kernel, out_type=jax.ShapeDtypeStruct((n_out, W), x.dtype), mesh=_sc_mesh())
    def k(x_hbm, i_hbm, o_hbm):
        def body(x_vmem, i_vmem):
            pltpu.sync_copy(x_vmem, o_hbm.at[i_vmem.at[0]])

        pltpu.emit_pipeline(
            body,
            grid=(N // SC_WINDOW,),
            in_specs=[pl.BlockSpec((SC_WINDOW, W), lambda i: (i % nsrc, 0)),
                      pl.BlockSpec((1, SC_WINDOW), lambda i: (0, i))],
            out_specs=[],
            core_axis_name=("c", "s"),
            dimension_semantics=(pltpu.PARALLEL,),
        )(x_hbm, i_hbm)

    return k(x, idx.reshape(1, N))


def _sc_gather_rows(table, idx):
    V, W = table.shape
    N = idx.shape[0]

    @functools.partial(pl.kernel, out_type=jax.ShapeDtypeStruct((N, W), table.dtype), mesh=_sc_mesh())
    def k(t_hbm, i_hbm, o_hbm):
        def body(i_vmem, o_vmem):
            pltpu.sync_copy(t_hbm.at[i_vmem.at[0]], o_vmem)

        pltpu.emit_pipeline(
            body,
            grid=(N // SC_WINDOW,),
            in_specs=[pl.BlockSpec((1, SC_WINDOW), lambda i: (0, i))],
            out_specs=[pl.BlockSpec((SC_WINDOW, W), lambda i: (i, 0))],
            core_axis_name=("c", "s"),
            dimension_semantics=(pltpu.PARALLEL,),
        )(i_hbm, o_hbm)

    return k(table, idx.reshape(1, N))


def _expert_kernel(be_ref, nv_ref, fs_ref, nxt_ref, xs_ref, wgu_hbm, bgu_ref, wdn_hbm, bdn_ref, ys_ref,
                   wgu_f32, wdn_f32, cur_slot, sem, *, layer):
    F = wdn_f32.shape[1]

    def weight_copies(e, slot):
        return (pltpu.make_async_copy(wgu_hbm.at[layer, e], wgu_f32.at[slot], sem.at[slot, 0]),
                pltpu.make_async_copy(wdn_hbm.at[layer, e], wdn_f32.at[slot], sem.at[slot, 1]))

    @pl.when(pl.program_id(0) == 0)
    def _():
        for cp in weight_copies(be_ref[0], 0):
            cp.start()

    for j in range(EXPERT_STEP_BLOCKS):
        b = pl.program_id(0) * EXPERT_STEP_BLOCKS + j
        rs = slice(j * EXPERT_ROWS, (j + 1) * EXPERT_ROWS)
        e = be_ref[b]
        nv = nv_ref[b]
        fs = fs_ref[b]

        @pl.when(fs >= 0)
        def _():
            for cp in weight_copies(e, fs):
                cp.wait()
            cur_slot[0] = fs
            nxt = nxt_ref[b]

            @pl.when(nxt >= 0)
            def _():
                for cp in weight_copies(nxt, 1 - fs):
                    cp.start(priority=1)

        @pl.when(nv > 0)
        def _():
            x = _unpack_rows(jnp.concatenate([xs_ref[pn, rs, :] for pn in range(xs_ref.shape[0])], axis=1))
            rows = lax.broadcasted_iota(I32, x.shape, 0)
            x = jnp.where(rows < nv, x, 0.0)
            slot = cur_slot[0]
            hgu = jnp.dot(x, wgu_f32[slot], preferred_element_type=F32) + bgu_ref[e]
            g = jnp.minimum(hgu[:, :F], SWIGLU_LIMIT)
            lin = jnp.clip(hgu[:, F:], -SWIGLU_LIMIT, SWIGLU_LIMIT)
            act = g * jax.nn.sigmoid(SWIGLU_ALPHA * g) * (lin + 1.0)
            y = _pack_rows(jnp.dot(act, wdn_f32[slot], preferred_element_type=F32) + bdn_ref[e])
            for pn in range(ys_ref.shape[0]):
                ys_ref[pn, rs, :] = y[:, pn * SC_ROW_WORDS:(pn + 1) * SC_ROW_WORDS]

        @pl.when(nv == 0)
        def _():
            ys_ref[:, rs, :] = jnp.zeros((ys_ref.shape[0], EXPERT_ROWS, SC_ROW_WORDS), U32)


def _experts(layer, plan, xs, w_gu, b_gu, w_dn, b_dn):
    planes, P, _ = xs.shape
    _, E, D, F2 = w_gu.shape
    F = F2 // 2
    step_rows = EXPERT_STEP_BLOCKS * EXPERT_ROWS
    by_step = lambda i, be, nv, fs, nxt: (0, i, 0)
    layer_all = lambda i, be, nv, fs, nxt: (layer, 0, 0, 0)
    return pl.pallas_call(
        functools.partial(_expert_kernel, layer=layer),
        grid_spec=pltpu.PrefetchScalarGridSpec(
            num_scalar_prefetch=4,
            grid=(P // step_rows,),
            in_specs=[
                pl.BlockSpec((planes, step_rows, SC_ROW_WORDS), by_step),
                pl.BlockSpec(memory_space=pl.ANY),
                pl.BlockSpec((None, E, 1, F2), layer_all),
                pl.BlockSpec(memory_space=pl.ANY),
                pl.BlockSpec((None, E, 1, D), layer_all),
            ],
            out_specs=pl.BlockSpec((planes, step_rows, SC_ROW_WORDS), by_step),
            scratch_shapes=[
                pltpu.VMEM((2, D, F2), F32), pltpu.VMEM((2, F, D), F32),
                pltpu.SMEM((1,), I32),
                pltpu.SemaphoreType.DMA((2, 2)),
            ],
        ),
        out_shape=jax.ShapeDtypeStruct((planes, P, SC_ROW_WORDS), U32),
        compiler_params=pltpu.CompilerParams(
            dimension_semantics=("arbitrary",), vmem_limit_bytes=VMEM_LIMIT),
        name="experts",
    )(plan["block_e"], plan["block_nv"], plan["first_slot"], plan["next_e"],
      xs, w_gu, b_gu.reshape(-1, E, 1, F2), w_dn, b_dn.reshape(-1, E, 1, D))


def _combine_ple_kernel(h1_ref, yg_ref, gates_ref, gple_ref, wpg_ref, p_ref, wple_ref, gfin_ref, *rest, final):
    o_ref = rest[-1]
    h2 = h1_ref[...]
    gates = gates_ref[...]
    for k in range(TOP_K):
        h2 = h2 + gates[:, k:k + 1] * _unpack_rows(_load_planes(yg_ref.at[k]))
    xn = _rms(h2, gple_ref[...]).astype(BF16)
    gate = jax.nn.sigmoid(jnp.dot(xn, wpg_ref[...], preferred_element_type=F32))
    pe = jnp.dot(p_ref[...].astype(BF16), wple_ref[...], preferred_element_type=F32)
    h3 = h2 + gate * pe
    if final:
        h3 = _rms(h3, gfin_ref[...])
    o_ref[...] = h3


def _combine_ple(layer, group, h1, yg, gates, p, prm, g_final, final, h_prev):
    Tg, D = h1.shape
    T = Tg * MOE_GROUPS
    tm = min(ROW_TILE, Tg)
    PD = p.shape[2]
    planes = yg.shape[1]
    off = group * (Tg // tm)
    const = lambda shape: pl.BlockSpec(shape, lambda i: (0,) * len(shape))
    in_specs = [
        pl.BlockSpec((tm, D), lambda i: (i, 0)),
        pl.BlockSpec((TOP_K, planes, tm, SC_ROW_WORDS), lambda i: (0, 0, i, 0)),
        pl.BlockSpec((tm, 8), lambda i: (i, 0)),
        const((1, D)),
        const((D, D)),
        pl.BlockSpec((None, tm, PD), lambda i: (layer, off + i, 0)),
        const((PD, D)),
        const((1, D)),
    ]
    args = [h1, yg, gates, prm["g_ple"], prm["wpg"], p, prm["wple"], g_final]
    aliases = {}
    if h_prev is not None:
        in_specs.append(pl.BlockSpec(memory_space=pl.ANY))
        args.append(h_prev)
        aliases = {len(args) - 1: 0}
    return pl.pallas_call(
        functools.partial(_combine_ple_kernel, final=final),
        grid=(Tg // tm,),
        in_specs=in_specs,
        out_specs=pl.BlockSpec((tm, D), lambda i: (off + i, 0)),
        out_shape=jax.ShapeDtypeStruct((T, D), F32),
        input_output_aliases=aliases,
        compiler_params=pltpu.CompilerParams(
            dimension_semantics=("parallel",), vmem_limit_bytes=VMEM_LIMIT),
        name="combine_ple",
    )(*args)


def _layer_params(i, w):
    D = w["w_in"].shape[1]
    H = FOX_HEADS
    w_in = w["w_in"][i]
    c1 = 2 * GMLP_WIDTH
    c2, c3, c4 = c1 + FOX_WIDTH, c1 + 2 * FOX_WIDTH, c1 + 3 * FOX_WIDTH
    wk = jnp.zeros((D, H, HEAD_PAD), F32).at[:, :, :FOX_HEAD_DIM].set(
        w_in[:, c2:c3].reshape(D, H, FOX_HEAD_DIM)).reshape(D, H * HEAD_PAD)
    return {
        "g_mix": w["g_mix"][i].reshape(1, D),
        "wuv": w_in[:, :c1].astype(BF16),
        "wqt": (w_in[:, c1:c2] * (FOX_HEAD_DIM ** -0.5 * LOG2E)).T.astype(BF16),
        "wk": wk.astype(BF16),
        "wvt": w_in[:, c3:c4].T.astype(BF16),
        "wft": w_in[:, c4:].T.astype(BF16),
        "b_f": w["b_f"][i].reshape(H, 1),
        "ln_g": w["ln_g"][i].reshape(1, GMLP_WIDTH),
        "ln_b": w["ln_b"][i].reshape(1, GMLP_WIDTH),
        "w_s": w["w_s"][i],
        "b_s": w["b_s"][i].reshape(GMLP_GROUPS, GMLP_BLOCK, 1),
        "g_a": w["g_a"][i].reshape(1, GMLP_WIDTH),
        "woa": w["w_out"][i][:GMLP_WIDTH].astype(BF16),
        "wob": w["w_out"][i][GMLP_WIDTH:].astype(BF16),
        "g_b": w["g_b"][i].reshape(FOX_WIDTH, 1),
        "g_moe": w["g_moe"][i].reshape(1, D),
        "wrt": w["w_router"][i].T.astype(BF16),
        "b_router": w["b_router"][i].reshape(N_EXPERTS, 1),
        "g_ple": w["g_ple"][i].reshape(1, D),
        "wpg": w["w_ple_gate"][i].astype(BF16),
        "wple": w["w_ple"][i].astype(BF16),
    }


def _bias_constants(tm):
    H = FOX_HEADS
    j = jnp.arange(tm)
    tri_incl = (j[:, None] <= j[None, :]).astype(BF16)
    tri_strict = (j[:, None] < j[None, :]).astype(BF16)
    part = jnp.arange(BIAS_PARTS)[:, None]
    head = jnp.arange(H)[None, :]
    rows = (part * H + head).reshape(-1)
    cols = (head * HEAD_PAD + FOX_HEAD_DIM + BIAS_PARTS + part).reshape(-1)
    selk = jnp.zeros((32, H * HEAD_PAD), F32).at[rows, cols].set(-1.0).astype(BF16)
    one_cols = (jnp.arange(H)[:, None] * HEAD_PAD + FOX_HEAD_DIM + jnp.arange(BIAS_PARTS)[None, :]).reshape(-1)
    onesk = jnp.zeros((1, H * HEAD_PAD), F32).at[0, one_cols].set(1.0)
    return tri_incl, tri_strict, selk, onesk


def _moe_dispatch_plan(counts, n_blocks):
    padded = (counts + EXPERT_ROWS - 1) // EXPERT_ROWS * EXPERT_ROWS
    pad_ends = jnp.cumsum(padded)
    pad_starts = (pad_ends - padded).astype(I32)
    block_start = jnp.arange(n_blocks, dtype=I32) * EXPERT_ROWS
    n_before = jnp.sum((pad_ends[None, :] <= block_start[:, None]).astype(I32), axis=1)
    block_e = jnp.minimum(n_before, N_EXPERTS - 1).astype(I32)
    own = block_e[:, None] == jnp.arange(N_EXPERTS, dtype=I32)[None, :]
    seg_end = jnp.sum(jnp.where(own, (pad_starts + counts)[None, :], 0), axis=1)
    block_nv = jnp.clip(seg_end - block_start, 0, EXPERT_ROWS)
    block_nv = jnp.where(block_start < pad_ends[-1], block_nv, 0).astype(I32)
    prev_e = jnp.concatenate([jnp.full((1,), -1, I32), block_e[:-1]])
    first = (block_nv > 0) & (block_e != prev_e)
    run = jnp.cumsum(first.astype(I32)) - 1
    first_slot = jnp.where(first, run % 2, -1).astype(I32)
    ar = jnp.arange(N_EXPERTS, dtype=I32)
    later = (ar[None, :] > ar[:, None]) & (counts > 0)[None, :]
    next_of = jnp.min(jnp.where(later, ar[None, :], N_EXPERTS), axis=1)
    next_of = jnp.where(next_of < N_EXPERTS, next_of, -1)
    next_e = jnp.sum(jnp.where(own, next_of[None, :], 0), axis=1).astype(I32)
    return dict(pad_starts=pad_starts, block_e=block_e, block_nv=block_nv, first_slot=first_slot, next_e=next_e)


def kernel(x, p, g_mix, w_in, ln_g, ln_b, w_s, b_s, b_f, g_a, g_b, w_out, g_moe, w_router, b_router,
           w_gu, b_gu, w_dn, b_dn, g_ple, w_ple_gate, w_ple, g_final):
    B, S, D = x.shape
    T = B * S
    depth = w_in.shape[0]
    w = dict(g_mix=g_mix, w_in=w_in, ln_g=ln_g, ln_b=ln_b, w_s=w_s, b_s=b_s, b_f=b_f, g_a=g_a, g_b=g_b,
             w_out=w_out, g_moe=g_moe, w_router=w_router, b_router=b_router, w_gu=w_gu, b_gu=b_gu,
             w_dn=w_dn, b_dn=b_dn, g_ple=g_ple, w_ple_gate=w_ple_gate, w_ple=w_ple)
    tri_incl, tri_strict, selk, onesk = _bias_constants(min(ROW_TILE, S))
    G = MOE_GROUPS
    Tg = T // G
    n_blocks = -(-(Tg * TOP_K) // EXPERT_ROWS) + N_EXPERTS
    n_blocks = -(-n_blocks // EXPERT_STEP_BLOCKS) * EXPERT_STEP_BLOCKS
    P = n_blocks * EXPERT_ROWS
    planes = (D // 2) // SC_ROW_WORDS
    plane_base = (jnp.arange(planes, dtype=I32) * P)[None, :, None]
    p_rows = p.reshape(depth, T, p.shape[-1])

    h = x.reshape(T, D)
    for i in range(depth):
        prm = _layer_params(i, w)
        prm.update(tri_incl=tri_incl, selk=selk, onesk=onesk)
        routed = []
        idx = jnp.zeros((8, 128), I32)
        for g in range(G):
            ya, qt, kx, vt = _mix_in(g, h, B, S, prm, idx)
            ybt = _fox_attn(qt, kx, vt, B // G, S)
            h1, xp, tope, rank, gates, counts = _out_router(g, h, ya, ybt, B, S, prm, tri_strict)
            plan = _moe_dispatch_plan(counts[:, 0], n_blocks)
            dest = _dest_rows(plan["pad_starts"], tope, rank)[:TOP_K]
            idx = (dest[:, None, :] + plane_base).reshape(-1)
            xs = _sc_scatter_rows(xp.reshape(planes * Tg, SC_ROW_WORDS), idx, planes * P)
            routed.append((h1, gates, plan, idx, xs))
        h_out = None
        for g, (h1, gates, plan, idx, xs) in enumerate(routed):
            ys = _experts(i, plan, xs.reshape(planes, P, SC_ROW_WORDS), w_gu, b_gu, w_dn, b_dn)
            yg = _sc_gather_rows(ys.reshape(planes * P, SC_ROW_WORDS), idx)
            h_out = _combine_ple(i, g, h1, yg.reshape(TOP_K, planes, Tg, SC_ROW_WORDS), gates, p_rows, prm,
                                 g_final.reshape(1, D), i == depth - 1, h_out)
        h = h_out
    return h.reshape(B, S, D)
```

```python
import functools

import jax
import jax.numpy as jnp
from jax import lax
from jax.experimental import pallas as pl
from jax.experimental.pallas import tpu as pltpu
from jax.experimental.pallas import tpu_sc as plsc

F32 = jnp.float32
BF16 = jnp.bfloat16
I32 = jnp.int32
U32 = jnp.uint32

CHUNK = 64
GMLP_WIDTH = 512
GMLP_GROUPS = 4
GMLP_GROUP_CH = 128
GMLP_BLOCK = 128
FOX_WIDTH = 512
FOX_HEAD_DIM = 64
FOX_HEADS = 8
N_EXPERTS = 32
TOP_K = 4
EXPERT_ROWS = 256
SWIGLU_LIMIT = 7.0
SWIGLU_ALPHA = 1.702
RMS_EPS = 1e-5
LN_EPS = 1e-5

HEAD_PAD = 128
BIAS_PARTS = 3
ROW_TILE = 512
ATTN_TILE = 512
ATTN_HEADS = 1
ACC_ROWS = 80
LOG2E = 1.4426950408889634
SC_WINDOW = 128
SC_ROW_WORDS = 256
COMBINE_ROWS = 256
EXPERT_STEP_BLOCKS = 8
MOE_GROUPS = 2
VMEM_LIMIT = 56 * 1024 * 1024
NEG_BIG = -1e30

_NT = (((1,), (1,)), ((), ()))
_TN = (((0,), (0,)), ((), ()))


def _rms(x, g):
    return x * lax.rsqrt(jnp.mean(x * x, axis=-1, keepdims=True) + RMS_EPS) * g


def _split3(c):
    a1 = c.astype(BF16).astype(F32)
    r1 = c - a1
    a2 = r1.astype(BF16).astype(F32)
    a3 = (r1 - a2).astype(BF16).astype(F32)
    return a1, a2, a3


def _pack_rows(x):
    n = x.shape[1] // 2
    lo = lax.bitcast_convert_type(x[:, :n].astype(BF16).astype(F32), U32) >> 16
    hi = lax.bitcast_convert_type(x[:, n:].astype(BF16).astype(F32), U32) & jnp.uint32(0xFFFF0000)
    return lo | hi


def _unpack_rows(u):
    lo = lax.bitcast_convert_type(u << 16, F32)
    hi = lax.bitcast_convert_type(u & jnp.uint32(0xFFFF0000), F32)
    return jnp.concatenate([lo, hi], axis=1)


def _store_planes(ref, u):
    for j in range(ref.shape[0]):
        ref[j] = u[:, j * SC_ROW_WORDS:(j + 1) * SC_ROW_WORDS]


def _load_planes(ref):
    return jnp.concatenate([ref[j] for j in range(ref.shape[0])], axis=1)


def _mix_in_kernel(h_ref, gmix_ref, wuv_ref, wqt_ref, wk_ref, wvt_ref, wft_ref, bf_ref, lng_ref, lnb_ref,
                   ws_ref, bs_ref, ga_ref, tri_ref, selk_ref, onesk_ref, after_ref,
                   ya_ref, qt_ref, k_ref, vt_ref, carry_ref, ya_scr):
    del after_ref
    tm = h_ref.shape[0]

    @pl.when(pl.program_id(1) == 0)
    def _():
        carry_ref[...] = jnp.zeros_like(carry_ref)

    xn = _rms(h_ref[...], gmix_ref[...]).astype(BF16)

    z = jnp.dot(xn, wuv_ref[...], preferred_element_type=F32)
    qt = lax.dot_general(wqt_ref[...], xn, _NT, preferred_element_type=F32)
    vt_ref[0] = lax.dot_general(wvt_ref[...], xn, _NT, preferred_element_type=F32).astype(BF16)
    kp = jnp.dot(xn, wk_ref[...], preferred_element_type=F32)
    ft = lax.dot_general(wft_ref[...], xn, _NT, preferred_element_type=F32) + bf_ref[...]

    z = 0.5 * z * (1.0 + lax.erf(z * (0.5 ** 0.5)))
    u = z[:, :GMLP_WIDTH]
    v = z[:, GMLP_WIDTH:]
    mu = jnp.mean(v, axis=-1, keepdims=True)
    vc = v - mu
    var = jnp.mean(vc * vc, axis=-1, keepdims=True)
    vn = (vc * lax.rsqrt(var + LN_EPS) * lng_ref[...] + lnb_ref[...]).astype(BF16)
    pi = lax.broadcasted_iota(I32, (GMLP_BLOCK, GMLP_BLOCK), 0) // CHUNK
    pj = lax.broadcasted_iota(I32, (GMLP_BLOCK, GMLP_BLOCK), 1) // CHUNK
    for g in range(GMLP_GROUPS):
        wm = jnp.where(pj <= pi, ws_ref[g], 0.0).astype(BF16)
        cs = slice(g * GMLP_GROUP_CH, (g + 1) * GMLP_GROUP_CH)
        for n in range(tm // GMLP_BLOCK):
            rs = slice(n * GMLP_BLOCK, (n + 1) * GMLP_BLOCK)
            sv = jnp.dot(wm, vn[rs, cs], preferred_element_type=F32) + bs_ref[g]
            ya_scr[rs, cs] = u[rs, cs] * sv
    ya_ref[...] = _rms(ya_scr[...], ga_ref[...]).astype(BF16)

    lf = (jnp.minimum(ft, 0.0) - jnp.log1p(jnp.exp(-jnp.abs(ft)))) * LOG2E
    zpad = jnp.zeros((32 - BIAS_PARTS * FOX_HEADS, tm), F32)
    stack = jnp.concatenate(list(_split3(lf)) + [zpad], axis=0).astype(BF16)
    cum = jnp.dot(stack, tri_ref[...], preferred_element_type=F32)
    c = cum[0:8] + cum[8:16] + cum[16:24] + carry_ref[:, 0:1]
    carry_ref[...] = jnp.broadcast_to(c[:, tm - 1:tm], carry_ref.shape)
    a1, a2, a3 = _split3(c)

    si = lax.broadcasted_iota(I32, (8, tm), 0)
    zrows = jnp.zeros((HEAD_PAD - FOX_HEAD_DIM - 8, tm), F32)
    for hd in range(FOX_HEADS):
        b1 = jnp.broadcast_to(a1[hd:hd + 1], (8, tm))
        b2 = jnp.broadcast_to(a2[hd:hd + 1], (8, tm))
        b3 = jnp.broadcast_to(a3[hd:hd + 1], (8, tm))
        ext = jnp.where(si == 0, b1, jnp.where(si == 1, b2, jnp.where(si == 2, b3,
                        jnp.where(si < 2 * BIAS_PARTS, 1.0, 0.0))))
        blk = jnp.concatenate([qt[hd * FOX_HEAD_DIM:(hd + 1) * FOX_HEAD_DIM], ext, zrows], axis=0)
        qt_ref[0, hd * HEAD_PAD:(hd + 1) * HEAD_PAD, :] = blk.astype(BF16)

    cstack = jnp.concatenate([a1, a2, a3, zpad], axis=0).astype(BF16)
    kext = lax.dot_general(cstack, selk_ref[...], _TN, preferred_element_type=F32)
    k_ref[...] = (kp + kext + onesk_ref[...]).astype(BF16)


def _mix_in(group, h, B, S, prm, after):
    T, D = h.shape
    Bg = B // MOE_GROUPS
    Tg = Bg * S
    tm = min(ROW_TILE, S)
    ns = S // tm
    H = FOX_HEADS
    const = lambda shape: pl.BlockSpec(shape, lambda b, s: (0,) * len(shape))
    return pl.pallas_call(
        _mix_in_kernel,
        grid=(Bg, ns),
        in_specs=[
            pl.BlockSpec((tm, D), lambda b, s: ((group * Bg + b) * ns + s, 0)),
            const((1, D)),
            const((D, 2 * GMLP_WIDTH)),
            const((FOX_WIDTH, D)),
            const((D, H * HEAD_PAD)),
            const((FOX_WIDTH, D)),
            const((H, D)),
            const((H, 1)),
            const((1, GMLP_WIDTH)),
            const((1, GMLP_WIDTH)),
            const((GMLP_GROUPS, GMLP_BLOCK, GMLP_BLOCK)),
            const((GMLP_GROUPS, GMLP_BLOCK, 1)),
            const((1, GMLP_WIDTH)),
            const((tm, tm)),
            const((32, H * HEAD_PAD)),
            const((1, H * HEAD_PAD)),
            pl.BlockSpec(memory_space=pl.ANY),
        ],
        out_specs=[
            pl.BlockSpec((tm, GMLP_WIDTH), lambda b, s: (b * ns + s, 0)),
            pl.BlockSpec((1, H * HEAD_PAD, tm), lambda b, s: (b, 0, s)),
            pl.BlockSpec((tm, H * HEAD_PAD), lambda b, s: (b * ns + s, 0)),
            pl.BlockSpec((1, FOX_WIDTH, tm), lambda b, s: (b, 0, s)),
        ],
        out_shape=[
            jax.ShapeDtypeStruct((Tg, GMLP_WIDTH), BF16),
            jax.ShapeDtypeStruct((Bg, H * HEAD_PAD, S), BF16),
            jax.ShapeDtypeStruct((Tg, H * HEAD_PAD), BF16),
            jax.ShapeDtypeStruct((Bg, FOX_WIDTH, S), BF16),
        ],
        scratch_shapes=[pltpu.VMEM((H, 128), F32), pltpu.VMEM((tm, GMLP_WIDTH), F32)],
        compiler_params=pltpu.CompilerParams(
            dimension_semantics=("arbitrary", "arbitrary"), vmem_limit_bytes=VMEM_LIMIT),
        name="mix_in",
    )(h, prm["g_mix"], prm["wuv"], prm["wqt"], prm["wk"], prm["wvt"], prm["wft"], prm["b_f"],
      prm["ln_g"], prm["ln_b"], prm["w_s"], prm["b_s"], prm["g_a"], prm["tri_incl"], prm["selk"], prm["onesk"],
      after)


def _fox_attn_kernel(qt_ref, k_ref, vt_ref, ot_ref):
    S = k_ref.shape[0]
    t = min(ATTN_TILE, S)
    nq = S // t
    hf = t // 2
    ones_rows = jnp.ones((ACC_ROWS - FOX_HEAD_DIM, t), BF16)
    pairs = [(qi, kj) for qi in range(nq) for kj in range(qi + 1)]

    def tri(rows, cols):
        return lax.broadcasted_iota(I32, (rows, cols), 0) <= lax.broadcasted_iota(I32, (rows, cols), 1)

    def scores(qi, kj):
        out = []
        for hh in range(ATTN_HEADS):
            hs = slice(hh * HEAD_PAD, (hh + 1) * HEAD_PAD)
            if kj < qi:
                out.append([jnp.dot(k_ref[kj * t:(kj + 1) * t, hs], qt_ref[0, hs, qi * t:(qi + 1) * t],
                                    preferred_element_type=F32)])
            else:
                top = jnp.dot(k_ref[kj * t:kj * t + hf, hs], qt_ref[0, hs, qi * t:(qi + 1) * t],
                              preferred_element_type=F32)
                bot = jnp.dot(k_ref[kj * t + hf:(kj + 1) * t, hs], qt_ref[0, hs, qi * t + hf:(qi + 1) * t],
                              preferred_element_type=F32)
                out.append([jnp.where(tri(hf, t), top, NEG_BIG), jnp.where(tri(hf, hf), bot, NEG_BIG)])
        return out

    s_next = scores(*pairs[0])
    state = None
    for n, (qi, kj) in enumerate(pairs):
        s_cur = s_next
        if n + 1 < len(pairs):
            s_next = scores(*pairs[n + 1])
        new_state = []
        for hh in range(ATTN_HEADS):
            parts = s_cur[hh]
            v = vt_ref[0, hh * FOX_HEAD_DIM:(hh + 1) * FOX_HEAD_DIM, kj * t:(kj + 1) * t]
            v_aug = jnp.concatenate([v, ones_rows], axis=0)
            mx = jnp.max(parts[0], axis=0, keepdims=True)
            if len(parts) == 2:
                mx = jnp.concatenate(
                    [mx[:, :hf], jnp.maximum(mx[:, hf:], jnp.max(parts[1], axis=0, keepdims=True))], axis=1)
            m_new = mx if kj == 0 else jnp.maximum(state[hh][0], mx)
            if len(parts) == 1:
                pv = jnp.dot(v_aug, jnp.exp2(parts[0] - m_new).astype(BF16), preferred_element_type=F32)
            else:
                pv = jnp.dot(v_aug[:, :hf], jnp.exp2(parts[0] - m_new).astype(BF16), preferred_element_type=F32)
                pv_r = jnp.dot(v_aug[:, hf:], jnp.exp2(parts[1] - m_new[:, hf:]).astype(BF16),
                               preferred_element_type=F32)
                pv = jnp.concatenate([pv[:, :hf], pv[:, hf:] + pv_r], axis=1)
            acc = pv if kj == 0 else jnp.exp2(state[hh][0] - m_new) * state[hh][1] + pv
            new_state.append((m_new, acc))
        state = new_state
        if kj == qi:
            for hh in range(ATTN_HEADS):
                acc = state[hh][1]
                o = acc[:FOX_HEAD_DIM] / acc[FOX_HEAD_DIM:FOX_HEAD_DIM + 1]
                ot_ref[0, hh * FOX_HEAD_DIM:(hh + 1) * FOX_HEAD_DIM, qi * t:(qi + 1) * t] = o.astype(BF16)


def _fox_attn(qt, kx, vt, B, S):
    G = ATTN_HEADS
    return pl.pallas_call(
        _fox_attn_kernel,
        grid=(B, FOX_HEADS // G),
        in_specs=[
            pl.BlockSpec((1, G * HEAD_PAD, S), lambda b, h: (b, h, 0)),
            pl.BlockSpec((S, G * HEAD_PAD), lambda b, h: (b, h)),
            pl.BlockSpec((1, G * FOX_HEAD_DIM, S), lambda b, h: (b, h, 0)),
        ],
        out_specs=pl.BlockSpec((1, G * FOX_HEAD_DIM, S), lambda b, h: (b, h, 0)),
        out_shape=jax.ShapeDtypeStruct((B, FOX_WIDTH, S), BF16),
        compiler_params=pltpu.CompilerParams(
            dimension_semantics=("parallel", "parallel"), vmem_limit_bytes=VMEM_LIMIT),
        name="fox_attn",
    )(qt, kx, vt)


def _out_router_kernel(h_ref, ya_ref, ybt_ref, woa_ref, wob_ref, gb_ref, gmoe_ref, wrt_ref, br_ref, tri_ref,
                       h1_ref, xp_ref, tope_ref, rank_ref, gates_ref, counts_ref, carry_ref):
    tm = h_ref.shape[0]

    @pl.when(pl.program_id(0) == 0)
    def _():
        carry_ref[...] = jnp.zeros_like(carry_ref)

    yb = ybt_ref[0].astype(F32)
    ybn = (yb * lax.rsqrt(jnp.mean(yb * yb, axis=0, keepdims=True) + RMS_EPS) * gb_ref[...]).astype(BF16)
    y = jnp.dot(ya_ref[...], woa_ref[...], preferred_element_type=F32)
    y = y + lax.dot_general(ybn, wob_ref[...], _TN, preferred_element_type=F32)
    h1 = h_ref[...] + y
    h1_ref[...] = h1
    xn = _rms(h1, gmoe_ref[...])
    _store_planes(xp_ref, _pack_rows(xn))

    logits = lax.dot_general(wrt_ref[...], xn.astype(BF16), _NT, preferred_element_type=F32) + br_ref[...]
    ri = lax.broadcasted_iota(I32, (N_EXPERTS, tm), 0)
    vals, idxs = [], []
    l = logits
    for _ in range(TOP_K):
        m = jnp.max(l, axis=0, keepdims=True)
        idx = jnp.min(jnp.where(l == m, ri, N_EXPERTS), axis=0, keepdims=True)
        vals.append(m)
        idxs.append(idx)
        l = jnp.where(ri == idx, -jnp.inf, l)
    es = [jnp.exp(vk - vals[0]) for vk in vals]
    den = es[0] + es[1] + es[2] + es[3]
    zrow = jnp.zeros((8 - TOP_K, tm), F32)
    gates_ref[...] = jnp.concatenate([e / den for e in es] + [zrow], axis=0).T

    sel = jnp.zeros((N_EXPERTS, tm), F32)
    for idx in idxs:
        sel = sel + jnp.where(ri == idx, 1.0, 0.0)
    cnt = jnp.dot(sel.astype(BF16), tri_ref[...], preferred_element_type=F32) + carry_ref[:, 0:1]
    ranks = [jnp.sum(jnp.where(ri == idx, cnt, 0.0), axis=0, keepdims=True) for idx in idxs]
    zi = jnp.zeros((8 - TOP_K, tm), I32)
    rank_ref[...] = jnp.concatenate([r.astype(I32) for r in ranks] + [zi], axis=0)
    tope_ref[...] = jnp.concatenate(idxs + [zi], axis=0)
    total = carry_ref[...] + jnp.sum(sel, axis=1, keepdims=True)
    carry_ref[...] = total
    counts_ref[...] = total.astype(I32)


def _out_router(group, h, ya, ybt, B, S, prm, tri_strict):
    T, D = h.shape
    Tg = T // MOE_GROUPS
    tm = min(ROW_TILE, S)
    ns = S // tm
    E = N_EXPERTS
    planes = (D // 2) // SC_ROW_WORDS
    off = group * (Tg // tm)
    const = lambda shape: pl.BlockSpec(shape, lambda i: (0,) * len(shape))
    return pl.pallas_call(
        _out_router_kernel,
        grid=(Tg // tm,),
        in_specs=[
            pl.BlockSpec((tm, D), lambda i: (off + i, 0)),
            pl.BlockSpec((tm, GMLP_WIDTH), lambda i: (i, 0)),
            pl.BlockSpec((1, FOX_WIDTH, tm), lambda i: (i // ns, 0, i % ns)),
            const((GMLP_WIDTH, D)),
            const((FOX_WIDTH, D)),
            const((FOX_WIDTH, 1)),
            const((1, D)),
            const((E, D)),
            const((E, 1)),
            const((tm, tm)),
        ],
        out_specs=[
            pl.BlockSpec((tm, D), lambda i: (i, 0)),
            pl.BlockSpec((planes, tm, SC_ROW_WORDS), lambda i: (0, i, 0)),
            pl.BlockSpec((8, tm), lambda i: (0, i)),
            pl.BlockSpec((8, tm), lambda i: (0, i)),
            pl.BlockSpec((tm, 8), lambda i: (i, 0)),
            const((E, 128)),
        ],
        out_shape=[
            jax.ShapeDtypeStruct((Tg, D), F32),
            jax.ShapeDtypeStruct((planes, Tg, SC_ROW_WORDS), U32),
            jax.ShapeDtypeStruct((8, Tg), I32),
            jax.ShapeDtypeStruct((8, Tg), I32),
            jax.ShapeDtypeStruct((Tg, 8), F32),
            jax.ShapeDtypeStruct((E, 128), I32),
        ],
        scratch_shapes=[pltpu.VMEM((E, 128), F32)],
        compiler_params=pltpu.CompilerParams(
            dimension_semantics=("arbitrary",), vmem_limit_bytes=VMEM_LIMIT),
        name="out_router",
    )(h, ya, ybt, prm["woa"], prm["wob"], prm["g_b"], prm["g_moe"], prm["wrt"], prm["b_router"], tri_strict)


def _dest_kernel(ps_ref, tope_ref, rank_ref, dest_ref):
    e_sel = tope_ref[...]
    d = rank_ref[...]
    for e in range(N_EXPERTS):
        d = d + jnp.where(e_sel == e, ps_ref[e], 0)
    dest_ref[...] = d


def _dest_rows(pad_starts, tope, rank):
    T = tope.shape[1]
    tc = min(4096, T)
    return pl.pallas_call(
        _dest_kernel,
        grid_spec=pltpu.PrefetchScalarGridSpec(
            num_scalar_prefetch=1,
            grid=(T // tc,),
            in_specs=[pl.BlockSpec((8, tc), lambda i, ps: (0, i)),
                      pl.BlockSpec((8, tc), lambda i, ps: (0, i))],
            out_specs=pl.BlockSpec((8, tc), lambda i, ps: (0, i)),
        ),
        out_shape=jax.ShapeDtypeStruct((8, T), I32),
        name="dest_rows",
    )(pad_starts, tope, rank)


def _sc_mesh():
    return plsc.VectorSubcoreMesh(core_axis_name="c", subcore_axis_name="s")


def _sc_scatter_rows(x, idx, n_out):
    R, W = x.shape
    N = idx.shape[0]
    nsrc = R // SC_WINDOW

    @functools.partial(pl.kernel, out_type=jax.ShapeDtypeStruct((n_out, W), x.dtype), mesh=_sc_mesh())
    def k(x_hbm, i_hbm, o_hbm):
        def body(x_vmem, i_vmem):
            pltpu.sync_copy(x_vmem, o_hbm.at[i_vmem.at[0]])

        pltpu.emit_pipeline(
            body,
            grid=(N // SC_WINDOW,),
            in_specs=[pl.BlockSpec((SC_WINDOW, W), lambda i: (i % nsrc, 0)),
                      pl.BlockSpec((1, SC_WINDOW), lambda i: (0, i))],
            out_specs=[],
            core_axis_name=("c", "s"),
            dimension_semantics=(pltpu.PARALLEL,),
        )(x_hbm, i_hbm)

    return k(x, idx.reshape(1, N))


def _sc_gather_rows(table, idx):
    V, W = table.shape
    N = idx.shape[0]

    @functools.partial(pl.kernel, out_type=jax.ShapeDtypeStruct((N, W), table.dtype), mesh=_sc_mesh())
    def k(t_hbm, i_hbm, o_hbm):
        def body(i_vmem, o_vmem):
            pltpu.sync_copy(t_hbm.at[i_vmem.at[0]], o_vmem)

        pltpu.emit_pipeline(
            body,
            grid=(N // SC_WINDOW,),
            in_specs=[pl.BlockSpec((1, SC_WINDOW), lambda i: (0, i))],
            out_specs=[pl.BlockSpec((SC_WINDOW, W), lambda i: (i, 0))],
            core_axis_name=("c", "s"),
            dimension_semantics=(pltpu.PARALLEL,),
        )(i_hbm, o_hbm)

    return k(table, idx.reshape(1, N))


def _expert_kernel(be_ref, nv_ref, fs_ref, nxt_ref, xs_ref, wgu_hbm, bgu_ref, wdn_hbm, bdn_ref, ys_ref,
                   wgu_f32, wdn_f32, cur_slot, sem, *, layer):
    F = wdn_f32.shape[1]

    def weight_copies(e, slot):
        return (pltpu.make_async_copy(wgu_hbm.at[layer, e], wgu_f32.at[slot], sem.at[slot, 0]),
                pltpu.make_async_copy(wdn_hbm.at[layer, e], wdn_f32.at[slot], sem.at[slot, 1]))

    @pl.when(pl.program_id(0) == 0)
    def _():
        for cp in weight_copies(be_ref[0], 0):
            cp.start()

    for j in range(EXPERT_STEP_BLOCKS):
        b = pl.program_id(0) * EXPERT_STEP_BLOCKS + j
        rs = slice(j * EXPERT_ROWS, (j + 1) * EXPERT_ROWS)
        e = be_ref[b]
        nv = nv_ref[b]
        fs = fs_ref[b]

        @pl.when(fs >= 0)
        def _():
            for cp in weight_copies(e, fs):
                cp.wait()
            cur_slot[0] = fs
            nxt = nxt_ref[b]

            @pl.when(nxt >= 0)
            def _():
                for cp in weight_copies(nxt, 1 - fs):
                    cp.start(priority=1)

        @pl.when(nv > 0)
        def _():
            x = _unpack_rows(jnp.concatenate([xs_ref[pn, rs, :] for pn in range(xs_ref.shape[0])], axis=1))
            rows = lax.broadcasted_iota(I32, x.shape, 0)
            x = jnp.where(rows < nv, x, 0.0)
            slot = cur_slot[0]
            hgu = jnp.dot(x, wgu_f32[slot], preferred_element_type=F32) + bgu_ref[e]
            g = jnp.minimum(hgu[:, :F], SWIGLU_LIMIT)
            lin = jnp.clip(hgu[:, F:], -SWIGLU_LIMIT, SWIGLU_LIMIT)
            act = g * jax.nn.sigmoid(SWIGLU_ALPHA * g) * (lin + 1.0)
            y = _pack_rows(jnp.dot(act, wdn_f32[slot], preferred_element_type=F32) + bdn_ref[e])
            for pn in range(ys_ref.shape[0]):
                ys_ref[pn, rs, :] = y[:, pn * SC_ROW_WORDS:(pn + 1) * SC_ROW_WORDS]

        @pl.when(nv == 0)
        def _():
            ys_ref[:, rs, :] = jnp.zeros((ys_ref.shape[0], EXPERT_ROWS, SC_ROW_WORDS), U32)


def _experts(layer, plan, xs, w_gu, b_gu, w_dn, b_dn):
    planes, P, _ = xs.shape
    _, E, D, F2 = w_gu.shape
    F = F2 // 2
    step_rows = EXPERT_STEP_BLOCKS * EXPERT_ROWS
    by_step = lambda i, be, nv, fs, nxt: (0, i, 0)
    layer_all = lambda i, be, nv, fs, nxt: (layer, 0, 0, 0)
    return pl.pallas_call(
        functools.partial(_expert_kernel, layer=layer),
        grid_spec=pltpu.PrefetchScalarGridSpec(
            num_scalar_prefetch=4,
            grid=(P // step_rows,),
            in_specs=[
                pl.BlockSpec((planes, step_rows, SC_ROW_WORDS), by_step),
                pl.BlockSpec(memory_space=pl.ANY),
                pl.BlockSpec((None, E, 1, F2), layer_all),
                pl.BlockSpec(memory_space=pl.ANY),
                pl.BlockSpec((None, E, 1, D), layer_all),
            ],
            out_specs=pl.BlockSpec((planes, step_rows, SC_ROW_WORDS), by_step),
            scratch_shapes=[
                pltpu.VMEM((2, D, F2), F32), pltpu.VMEM((2, F, D), F32),
                pltpu.SMEM((1,), I32),
                pltpu.SemaphoreType.DMA((2, 2)),
            ],
        ),
        out_shape=jax.ShapeDtypeStruct((planes, P, SC_ROW_WORDS), U32),
        compiler_params=pltpu.CompilerParams(
            dimension_semantics=("arbitrary",), vmem_limit_bytes=VMEM_LIMIT),
        name="experts",
    )(plan["block_e"], plan["block_nv"], plan["first_slot"], plan["next_e"],
      xs, w_gu, b_gu.reshape(-1, E, 1, F2), w_dn, b_dn.reshape(-1, E, 1, D))


def _combine_ple_kernel(h1_ref, yg_ref, gates_ref, gple_ref, wpg_ref, p_ref, wple_ref, gfin_ref, *rest, final):
    o_ref = rest[-1]
    for r in range(h1_ref.shape[0] // COMBINE_ROWS):
        rs = slice(r * COMBINE_ROWS, (r + 1) * COMBINE_ROWS)
        h2 = h1_ref[rs, :]
        gates = gates_ref[rs, :]
        for k in range(TOP_K):
            yk = jnp.concatenate([yg_ref[k, pn, rs, :] for pn in range(yg_ref.shape[1])], axis=1)
            h2 = h2 + gates[:, k:k + 1] * _unpack_rows(yk)
        xn = _rms(h2, gple_ref[...]).astype(BF16)
        gate = jax.nn.sigmoid(jnp.dot(xn, wpg_ref[...], preferred_element_type=F32))
        pe = jnp.dot(p_ref[rs, :].astype(BF16), wple_ref[...], preferred_element_type=F32)
        h3 = h2 + gate * pe
        if final:
            h3 = _rms(h3, gfin_ref[...])
        o_ref[rs, :] = h3


def _combine_ple(layer, group, h1, yg, gates, p, prm, g_final, final, h_prev):
    Tg, D = h1.shape
    T = Tg * MOE_GROUPS
    tm = min(ROW_TILE, Tg)
    PD = p.shape[2]
    planes = yg.shape[1]
    off = group * (Tg // tm)
    const = lambda shape: pl.BlockSpec(shape, lambda i: (0,) * len(shape))
    in_specs = [
        pl.BlockSpec((tm, D), lambda i: (i, 0)),
        pl.BlockSpec((TOP_K, planes, tm, SC_ROW_WORDS), lambda i: (0, 0, i, 0)),
        pl.BlockSpec((tm, 8), lambda i: (i, 0)),
        const((1, D)),
        const((D, D)),
        pl.BlockSpec((None, tm, PD), lambda i: (layer, off + i, 0)),
        const((PD, D)),
        const((1, D)),
    ]
    args = [h1, yg, gates, prm["g_ple"], prm["wpg"], p, prm["wple"], g_final]
    aliases = {}
    if h_prev is not None:
        in_specs.append(pl.BlockSpec(memory_space=pl.ANY))
        args.append(h_prev)
        aliases = {len(args) - 1: 0}
    return pl.pallas_call(
        functools.partial(_combine_ple_kernel, final=final),
        grid=(Tg // tm,),
        in_specs=in_specs,
        out_specs=pl.BlockSpec((tm, D), lambda i: (off + i, 0)),
        out_shape=jax.ShapeDtypeStruct((T, D), F32),
        input_output_aliases=aliases,
        compiler_params=pltpu.CompilerParams(
            dimension_semantics=("parallel",), vmem_limit_bytes=VMEM_LIMIT),
        name="combine_ple",
    )(*args)


def _layer_params(i, w):
    D = w["w_in"].shape[1]
    H = FOX_HEADS
    w_in = w["w_in"][i]
    c1 = 2 * GMLP_WIDTH
    c2, c3, c4 = c1 + FOX_WIDTH, c1 + 2 * FOX_WIDTH, c1 + 3 * FOX_WIDTH
    wk = jnp.zeros((D, H, HEAD_PAD), F32).at[:, :, :FOX_HEAD_DIM].set(
        w_in[:, c2:c3].reshape(D, H, FOX_HEAD_DIM)).reshape(D, H * HEAD_PAD)
    return {
        "g_mix": w["g_mix"][i].reshape(1, D),
        "wuv": w_in[:, :c1].astype(BF16),
        "wqt": (w_in[:, c1:c2] * (FOX_HEAD_DIM ** -0.5 * LOG2E)).T.astype(BF16),
        "wk": wk.astype(BF16),
        "wvt": w_in[:, c3:c4].T.astype(BF16),
        "wft": w_in[:, c4:].T.astype(BF16),
        "b_f": w["b_f"][i].reshape(H, 1),
        "ln_g": w["ln_g"][i].reshape(1, GMLP_WIDTH),
        "ln_b": w["ln_b"][i].reshape(1, GMLP_WIDTH),
        "w_s": w["w_s"][i],
        "b_s": w["b_s"][i].reshape(GMLP_GROUPS, GMLP_BLOCK, 1),
        "g_a": w["g_a"][i].reshape(1, GMLP_WIDTH),
        "woa": w["w_out"][i][:GMLP_WIDTH].astype(BF16),
        "wob": w["w_out"][i][GMLP_WIDTH:].astype(BF16),
        "g_b": w["g_b"][i].reshape(FOX_WIDTH, 1),
        "g_moe": w["g_moe"][i].reshape(1, D),
        "wrt": w["w_router"][i].T.astype(BF16),
        "b_router": w["b_router"][i].reshape(N_EXPERTS, 1),
        "g_ple": w["g_ple"][i].reshape(1, D),
        "wpg": w["w_ple_gate"][i].astype(BF16),
        "wple": w["w_ple"][i].astype(BF16),
    }


def _bias_constants(tm):
    H = FOX_HEADS
    j = jnp.arange(tm)
    tri_incl = (j[:, None] <= j[None, :]).astype(BF16)
    tri_strict = (j[:, None] < j[None, :]).astype(BF16)
    part = jnp.arange(BIAS_PARTS)[:, None]
    head = jnp.arange(H)[None, :]
    rows = (part * H + head).reshape(-1)
    cols = (head * HEAD_PAD + FOX_HEAD_DIM + BIAS_PARTS + part).reshape(-1)
    selk = jnp.zeros((32, H * HEAD_PAD), F32).at[rows, cols].set(-1.0).astype(BF16)
    one_cols = (jnp.arange(H)[:, None] * HEAD_PAD + FOX_HEAD_DIM + jnp.arange(BIAS_PARTS)[None, :]).reshape(-1)
    onesk = jnp.zeros((1, H * HEAD_PAD), F32).at[0, one_cols].set(1.0)
    return tri_incl, tri_strict, selk, onesk


def _moe_dispatch_plan(counts, n_blocks):
    padded = (counts + EXPERT_ROWS - 1) // EXPERT_ROWS * EXPERT_ROWS
    pad_ends = jnp.cumsum(padded)
    pad_starts = (pad_ends - padded).astype(I32)
    block_start = jnp.arange(n_blocks, dtype=I32) * EXPERT_ROWS
    n_before = jnp.sum((pad_ends[None, :] <= block_start[:, None]).astype(I32), axis=1)
    block_e = jnp.minimum(n_before, N_EXPERTS - 1).astype(I32)
    own = block_e[:, None] == jnp.arange(N_EXPERTS, dtype=I32)[None, :]
    seg_end = jnp.sum(jnp.where(own, (pad_starts + counts)[None, :], 0), axis=1)
    block_nv = jnp.clip(seg_end - block_start, 0, EXPERT_ROWS)
    block_nv = jnp.where(block_start < pad_ends[-1], block_nv, 0).astype(I32)
    prev_e = jnp.concatenate([jnp.full((1,), -1, I32), block_e[:-1]])
    first = (block_nv > 0) & (block_e != prev_e)
    run = jnp.cumsum(first.astype(I32)) - 1
    first_slot = jnp.where(first, run % 2, -1).astype(I32)
    ar = jnp.arange(N_EXPERTS, dtype=I32)
    later = (ar[None, :] > ar[:, None]) & (counts > 0)[None, :]
    next_of = jnp.min(jnp.where(later, ar[None, :], N_EXPERTS), axis=1)
    next_of = jnp.where(next_of < N_EXPERTS, next_of, -1)
    next_e = jnp.sum(jnp.where(own, next_of[None, :], 0), axis=1).astype(I32)
    return dict(pad_starts=pad_starts, block_e=block_e, block_nv=block_nv, first_slot=first_slot, next_e=next_e)


def kernel(x, p, g_mix, w_in, ln_g, ln_b, w_s, b_s, b_f, g_a, g_b, w_out, g_moe, w_router, b_router,
           w_gu, b_gu, w_dn, b_dn, g_ple, w_ple_gate, w_ple, g_final):
    B, S, D = x.shape
    T = B * S
    depth = w_in.shape[0]
    w = dict(g_mix=g_mix, w_in=w_in, ln_g=ln_g, ln_b=ln_b, w_s=w_s, b_s=b_s, b_f=b_f, g_a=g_a, g_b=g_b,
             w_out=w_out, g_moe=g_moe, w_router=w_router, b_router=b_router, w_gu=w_gu, b_gu=b_gu,
             w_dn=w_dn, b_dn=b_dn, g_ple=g_ple, w_ple_gate=w_ple_gate, w_ple=w_ple)
    tri_incl, tri_strict, selk, onesk = _bias_constants(min(ROW_TILE, S))
    G = MOE_GROUPS
    Tg = T // G
    n_blocks = -(-(Tg * TOP_K) // EXPERT_ROWS) + N_EXPERTS
    n_blocks = -(-n_blocks // EXPERT_STEP_BLOCKS) * EXPERT_STEP_BLOCKS
    P = n_blocks * EXPERT_ROWS
    planes = (D // 2) // SC_ROW_WORDS
    plane_base = (jnp.arange(planes, dtype=I32) * P)[None, :, None]
    p_rows = p.reshape(depth, T, p.shape[-1])

    h = x.reshape(T, D)
    for i in range(depth):
        prm = _layer_params(i, w)
        prm.update(tri_incl=tri_incl, selk=selk, onesk=onesk)
        routed = []
        idx = jnp.zeros((8, 128), I32)
        for g in range(G):
            ya, qt, kx, vt = _mix_in(g, h, B, S, prm, idx)
            ybt = _fox_attn(qt, kx, vt, B // G, S)
            h1, xp, tope, rank, gates, counts = _out_router(g, h, ya, ybt, B, S, prm, tri_strict)
            plan = _moe_dispatch_plan(counts[:, 0], n_blocks)
            dest = _dest_rows(plan["pad_starts"], tope, rank)[:TOP_K]
            idx = (dest[:, None, :] + plane_base).reshape(-1)
            xs = _sc_scatter_rows(xp.reshape(planes * Tg, SC_ROW_WORDS), idx, planes * P)
            routed.append((h1, gates, plan, idx, xs))
        h_out = None
        for g, (h1, gates, plan, idx, xs) in enumerate(routed):
            ys = _experts(i, plan, xs.reshape(planes, P, SC_ROW_WORDS), w_gu, b_gu, w_dn, b_dn)
            yg = _sc_gather_rows(ys.reshape(planes * P, SC_ROW_WORDS), idx)
            h_out = _combine_ple(i, g, h1, yg.reshape(TOP_K, planes, Tg, SC_ROW_WORDS), gates, p_rows, prm,
                                 g_final.reshape(1, D), i == depth - 1, h_out)
        h = h_out
    return h.reshape(B, S, D)
```

```python
import functools

import jax
import jax.numpy as jnp
from jax import lax
from jax.experimental import pallas as pl
from jax.experimental.pallas import tpu as pltpu
from jax.experimental.pallas import tpu_sc as plsc

F32 = jnp.float32
BF16 = jnp.bfloat16
I32 = jnp.int32
U32 = jnp.uint32

CHUNK = 64
GMLP_WIDTH = 512
GMLP_GROUPS = 4
GMLP_GROUP_CH = 128
GMLP_BLOCK = 128
FOX_WIDTH = 512
FOX_HEAD_DIM = 64
FOX_HEADS = 8
N_EXPERTS = 32
TOP_K = 4
EXPERT_ROWS = 256
SWIGLU_LIMIT = 7.0
SWIGLU_ALPHA = 1.702
RMS_EPS = 1e-5
LN_EPS = 1e-5

HEAD_PAD = 128
BIAS_PARTS = 3
ROW_TILE = 512
ATTN_TILE = 512
ATTN_HEADS = 1
ACC_ROWS = 80
LOG2E = 1.4426950408889634
SC_WINDOW = 128
SC_ROW_WORDS = 256
COMBINE_ROWS = 256
EXPERT_STEP_BLOCKS = 4
MOE_GROUPS = 2
VMEM_LIMIT = 56 * 1024 * 1024
NEG_BIG = -1e30

_NT = (((1,), (1,)), ((), ()))
_TN = (((0,), (0,)), ((), ()))


def _rms(x, g):
    return x * lax.rsqrt(jnp.mean(x * x, axis=-1, keepdims=True) + RMS_EPS) * g


def _split3(c):
    a1 = c.astype(BF16).astype(F32)
    r1 = c - a1
    a2 = r1.astype(BF16).astype(F32)
    a3 = (r1 - a2).astype(BF16).astype(F32)
    return a1, a2, a3


def _pack_rows(x):
    n = x.shape[1] // 2
    lo = lax.bitcast_convert_type(x[:, :n].astype(BF16).astype(F32), U32) >> 16
    hi = lax.bitcast_convert_type(x[:, n:].astype(BF16).astype(F32), U32) & jnp.uint32(0xFFFF0000)
    return lo | hi


def _unpack_rows(u):
    lo = lax.bitcast_convert_type(u << 16, F32)
    hi = lax.bitcast_convert_type(u & jnp.uint32(0xFFFF0000), F32)
    return jnp.concatenate([lo, hi], axis=1)


def _store_planes(ref, u):
    for j in range(ref.shape[0]):
        ref[j] = u[:, j * SC_ROW_WORDS:(j + 1) * SC_ROW_WORDS]


def _load_planes(ref):
    return jnp.concatenate([ref[j] for j in range(ref.shape[0])], axis=1)


def _mix_in_kernel(h_ref, gmix_ref, wuv_ref, wqt_ref, wk_ref, wvt_ref, wft_ref, bf_ref, lng_ref, lnb_ref,
                   ws_ref, bs_ref, ga_ref, tri_ref, selk_ref, onesk_ref, after_ref,
                   ya_ref, qt_ref, k_ref, vt_ref, carry_ref, ya_scr):
    del after_ref
    tm = h_ref.shape[0]

    @pl.when(pl.program_id(1) == 0)
    def _():
        carry_ref[...] = jnp.zeros_like(carry_ref)

    xn = _rms(h_ref[...], gmix_ref[...]).astype(BF16)

    z = jnp.dot(xn, wuv_ref[...], preferred_element_type=F32)
    qt = lax.dot_general(wqt_ref[...], xn, _NT, preferred_element_type=F32)
    vt_ref[0] = lax.dot_general(wvt_ref[...], xn, _NT, preferred_element_type=F32).astype(BF16)
    kp = jnp.dot(xn, wk_ref[...], preferred_element_type=F32)
    ft = lax.dot_general(wft_ref[...], xn, _NT, preferred_element_type=F32) + bf_ref[...]

    z = 0.5 * z * (1.0 + lax.erf(z * (0.5 ** 0.5)))
    u = z[:, :GMLP_WIDTH]
    v = z[:, GMLP_WIDTH:]
    mu = jnp.mean(v, axis=-1, keepdims=True)
    vc = v - mu
    var = jnp.mean(vc * vc, axis=-1, keepdims=True)
    vn = (vc * lax.rsqrt(var + LN_EPS) * lng_ref[...] + lnb_ref[...]).astype(BF16)
    pi = lax.broadcasted_iota(I32, (GMLP_BLOCK, GMLP_BLOCK), 0) // CHUNK
    pj = lax.broadcasted_iota(I32, (GMLP_BLOCK, GMLP_BLOCK), 1) // CHUNK
    for g in range(GMLP_GROUPS):
        wm = jnp.where(pj <= pi, ws_ref[g], 0.0).astype(BF16)
        cs = slice(g * GMLP_GROUP_CH, (g + 1) * GMLP_GROUP_CH)
        for n in range(tm // GMLP_BLOCK):
            rs = slice(n * GMLP_BLOCK, (n + 1) * GMLP_BLOCK)
            sv = jnp.dot(wm, vn[rs, cs], preferred_element_type=F32) + bs_ref[g]
            ya_scr[rs, cs] = u[rs, cs] * sv
    ya_ref[...] = _rms(ya_scr[...], ga_ref[...]).astype(BF16)

    lf = (jnp.minimum(ft, 0.0) - jnp.log1p(jnp.exp(-jnp.abs(ft)))) * LOG2E
    zpad = jnp.zeros((32 - BIAS_PARTS * FOX_HEADS, tm), F32)
    stack = jnp.concatenate(list(_split3(lf)) + [zpad], axis=0).astype(BF16)
    cum = jnp.dot(stack, tri_ref[...], preferred_element_type=F32)
    c = cum[0:8] + cum[8:16] + cum[16:24] + carry_ref[:, 0:1]
    carry_ref[...] = jnp.broadcast_to(c[:, tm - 1:tm], carry_ref.shape)
    a1, a2, a3 = _split3(c)

    si = lax.broadcasted_iota(I32, (8, tm), 0)
    zrows = jnp.zeros((HEAD_PAD - FOX_HEAD_DIM - 8, tm), F32)
    for hd in range(FOX_HEADS):
        b1 = jnp.broadcast_to(a1[hd:hd + 1], (8, tm))
        b2 = jnp.broadcast_to(a2[hd:hd + 1], (8, tm))
        b3 = jnp.broadcast_to(a3[hd:hd + 1], (8, tm))
        ext = jnp.where(si == 0, b1, jnp.where(si == 1, b2, jnp.where(si == 2, b3,
                        jnp.where(si < 2 * BIAS_PARTS, 1.0, 0.0))))
        blk = jnp.concatenate([qt[hd * FOX_HEAD_DIM:(hd + 1) * FOX_HEAD_DIM], ext, zrows], axis=0)
        qt_ref[0, hd * HEAD_PAD:(hd + 1) * HEAD_PAD, :] = blk.astype(BF16)

    cstack = jnp.concatenate([a1, a2, a3, zpad], axis=0).astype(BF16)
    kext = lax.dot_general(cstack, selk_ref[...], _TN, preferred_element_type=F32)
    k_ref[...] = (kp + kext + onesk_ref[...]).astype(BF16)


def _mix_in(group, h, B, S, prm, after):
    T, D = h.shape
    Bg = B // MOE_GROUPS
    Tg = Bg * S
    tm = min(ROW_TILE, S)
    ns = S // tm
    H = FOX_HEADS
    const = lambda shape: pl.BlockSpec(shape, lambda b, s: (0,) * len(shape))
    return pl.pallas_call(
        _mix_in_kernel,
        grid=(Bg, ns),
        in_specs=[
            pl.BlockSpec((tm, D), lambda b, s: ((group * Bg + b) * ns + s, 0)),
            const((1, D)),
            const((D, 2 * GMLP_WIDTH)),
            const((FOX_WIDTH, D)),
            const((D, H * HEAD_PAD)),
            const((FOX_WIDTH, D)),
            const((H, D)),
            const((H, 1)),
            const((1, GMLP_WIDTH)),
            const((1, GMLP_WIDTH)),
            const((GMLP_GROUPS, GMLP_BLOCK, GMLP_BLOCK)),
            const((GMLP_GROUPS, GMLP_BLOCK, 1)),
            const((1, GMLP_WIDTH)),
            const((tm, tm)),
            const((32, H * HEAD_PAD)),
            const((1, H * HEAD_PAD)),
            pl.BlockSpec(memory_space=pl.ANY),
        ],
        out_specs=[
            pl.BlockSpec((tm, GMLP_WIDTH), lambda b, s: (b * ns + s, 0)),
            pl.BlockSpec((1, H * HEAD_PAD, tm), lambda b, s: (b, 0, s)),
            pl.BlockSpec((tm, H * HEAD_PAD), lambda b, s: (b * ns + s, 0)),
            pl.BlockSpec((1, FOX_WIDTH, tm), lambda b, s: (b, 0, s)),
        ],
        out_shape=[
            jax.ShapeDtypeStruct((Tg, GMLP_WIDTH), BF16),
            jax.ShapeDtypeStruct((Bg, H * HEAD_PAD, S), BF16),
            jax.ShapeDtypeStruct((Tg, H * HEAD_PAD), BF16),
            jax.ShapeDtypeStruct((Bg, FOX_WIDTH, S), BF16),
        ],
        scratch_shapes=[pltpu.VMEM((H, 128), F32), pltpu.VMEM((tm, GMLP_WIDTH), F32)],
        compiler_params=pltpu.CompilerParams(
            dimension_semantics=("arbitrary", "arbitrary"), vmem_limit_bytes=VMEM_LIMIT),
        name="mix_in",
    )(h, prm["g_mix"], prm["wuv"], prm["wqt"], prm["wk"], prm["wvt"], prm["wft"], prm["b_f"],
      prm["ln_g"], prm["ln_b"], prm["w_s"], prm["b_s"], prm["g_a"], prm["tri_incl"], prm["selk"], prm["onesk"],
      after)


def _fox_attn_kernel(qt_ref, k_ref, vt_ref, ot_ref):
    S = k_ref.shape[0]
    t = min(ATTN_TILE, S)
    nq = S // t
    hf = t // 2
    ones_rows = jnp.ones((ACC_ROWS - FOX_HEAD_DIM, t), BF16)
    pairs = [(qi, kj) for qi in range(nq) for kj in range(qi + 1)]

    def tri(rows, cols):
        return lax.broadcasted_iota(I32, (rows, cols), 0) <= lax.broadcasted_iota(I32, (rows, cols), 1)

    def scores(qi, kj):
        out = []
        for hh in range(ATTN_HEADS):
            hs = slice(hh * HEAD_PAD, (hh + 1) * HEAD_PAD)
            if kj < qi:
                out.append([jnp.dot(k_ref[kj * t:(kj + 1) * t, hs], qt_ref[0, hs, qi * t:(qi + 1) * t],
                                    preferred_element_type=F32)])
            else:
                top = jnp.dot(k_ref[kj * t:kj * t + hf, hs], qt_ref[0, hs, qi * t:(qi + 1) * t],
                              preferred_element_type=F32)
                bot = jnp.dot(k_ref[kj * t + hf:(kj + 1) * t, hs], qt_ref[0, hs, qi * t + hf:(qi + 1) * t],
                              preferred_element_type=F32)
                out.append([jnp.where(tri(hf, t), top, NEG_BIG), jnp.where(tri(hf, hf), bot, NEG_BIG)])
        return out

    s_next = scores(*pairs[0])
    state = None
    for n, (qi, kj) in enumerate(pairs):
        s_cur = s_next
        if n + 1 < len(pairs):
            s_next = scores(*pairs[n + 1])
        new_state = []
        for hh in range(ATTN_HEADS):
            parts = s_cur[hh]
            v = vt_ref[0, hh * FOX_HEAD_DIM:(hh + 1) * FOX_HEAD_DIM, kj * t:(kj + 1) * t]
            v_aug = jnp.concatenate([v, ones_rows], axis=0)
            mx = jnp.max(parts[0], axis=0, keepdims=True)
            if len(parts) == 2:
                mx = jnp.concatenate(
                    [mx[:, :hf], jnp.maximum(mx[:, hf:], jnp.max(parts[1], axis=0, keepdims=True))], axis=1)
            m_new = mx if kj == 0 else jnp.maximum(state[hh][0], mx)
            if len(parts) == 1:
                pv = jnp.dot(v_aug, jnp.exp2(parts[0] - m_new).astype(BF16), preferred_element_type=F32)
            else:
                pv = jnp.dot(v_aug[:, :hf], jnp.exp2(parts[0] - m_new).astype(BF16), preferred_element_type=F32)
                pv_r = jnp.dot(v_aug[:, hf:], jnp.exp2(parts[1] - m_new[:, hf:]).astype(BF16),
                               preferred_element_type=F32)
                pv = jnp.concatenate([pv[:, :hf], pv[:, hf:] + pv_r], axis=1)
            acc = pv if kj == 0 else jnp.exp2(state[hh][0] - m_new) * state[hh][1] + pv
            new_state.append((m_new, acc))
        state = new_state
        if kj == qi:
            for hh in range(ATTN_HEADS):
                acc = state[hh][1]
                o = acc[:FOX_HEAD_DIM] / acc[FOX_HEAD_DIM:FOX_HEAD_DIM + 1]
                ot_ref[0, hh * FOX_HEAD_DIM:(hh + 1) * FOX_HEAD_DIM, qi * t:(qi + 1) * t] = o.astype(BF16)


def _fox_attn(qt, kx, vt, B, S):
    G = ATTN_HEADS
    return pl.pallas_call(
        _fox_attn_kernel,
        grid=(B, FOX_HEADS // G),
        in_specs=[
            pl.BlockSpec((1, G * HEAD_PAD, S), lambda b, h: (b, h, 0)),
            pl.BlockSpec((S, G * HEAD_PAD), lambda b, h: (b, h)),
            pl.BlockSpec((1, G * FOX_HEAD_DIM, S), lambda b, h: (b, h, 0)),
        ],
        out_specs=pl.BlockSpec((1, G * FOX_HEAD_DIM, S), lambda b, h: (b, h, 0)),
        out_shape=jax.ShapeDtypeStruct((B, FOX_WIDTH, S), BF16),
        compiler_params=pltpu.CompilerParams(
            dimension_semantics=("parallel", "parallel"), vmem_limit_bytes=VMEM_LIMIT),
        name="fox_attn",
    )(qt, kx, vt)


def _out_router_kernel(h_ref, ya_ref, ybt_ref, woa_ref, wob_ref, gb_ref, gmoe_ref, wrt_ref, br_ref, tri_ref,
                       h1_ref, xp_ref, tope_ref, rank_ref, gates_ref, counts_ref, carry_ref):
    tm = h_ref.shape[0]

    @pl.when(pl.program_id(0) == 0)
    def _():
        carry_ref[...] = jnp.zeros_like(carry_ref)

    yb = ybt_ref[0].astype(F32)
    ybn = (yb * lax.rsqrt(jnp.mean(yb * yb, axis=0, keepdims=True) + RMS_EPS) * gb_ref[...]).astype(BF16)
    y = jnp.dot(ya_ref[...], woa_ref[...], preferred_element_type=F32)
    y = y + lax.dot_general(ybn, wob_ref[...], _TN, preferred_element_type=F32)
    h1 = h_ref[...] + y
    h1_ref[...] = h1
    xn = _rms(h1, gmoe_ref[...])
    _store_planes(xp_ref, _pack_rows(xn))

    logits = lax.dot_general(wrt_ref[...], xn.astype(BF16), _NT, preferred_element_type=F32) + br_ref[...]
    ri = lax.broadcasted_iota(I32, (N_EXPERTS, tm), 0)
    vals, idxs = [], []
    l = logits
    for _ in range(TOP_K):
        m = jnp.max(l, axis=0, keepdims=True)
        idx = jnp.min(jnp.where(l == m, ri, N_EXPERTS), axis=0, keepdims=True)
        vals.append(m)
        idxs.append(idx)
        l = jnp.where(ri == idx, -jnp.inf, l)
    es = [jnp.exp(vk - vals[0]) for vk in vals]
    den = es[0] + es[1] + es[2] + es[3]
    zrow = jnp.zeros((8 - TOP_K, tm), F32)
    gates_ref[...] = jnp.concatenate([e / den for e in es] + [zrow], axis=0).T

    sel = jnp.zeros((N_EXPERTS, tm), F32)
    for idx in idxs:
        sel = sel + jnp.where(ri == idx, 1.0, 0.0)
    cnt = jnp.dot(sel.astype(BF16), tri_ref[...], preferred_element_type=F32) + carry_ref[:, 0:1]
    ranks = [jnp.sum(jnp.where(ri == idx, cnt, 0.0), axis=0, keepdims=True) for idx in idxs]
    zi = jnp.zeros((8 - TOP_K, tm), I32)
    rank_ref[...] = jnp.concatenate([r.astype(I32) for r in ranks] + [zi], axis=0)
    tope_ref[...] = jnp.concatenate(idxs + [zi], axis=0)
    total = carry_ref[...] + jnp.sum(sel, axis=1, keepdims=True)
    carry_ref[...] = total
    counts_ref[...] = total.astype(I32)


def _out_router(group, h, ya, ybt, B, S, prm, tri_strict):
    T, D = h.shape
    Tg = T // MOE_GROUPS
    tm = min(ROW_TILE, S)
    ns = S // tm
    E = N_EXPERTS
    planes = (D // 2) // SC_ROW_WORDS
    off = group * (Tg // tm)
    const = lambda shape: pl.BlockSpec(shape, lambda i: (0,) * len(shape))
    return pl.pallas_call(
        _out_router_kernel,
        grid=(Tg // tm,),
        in_specs=[
            pl.BlockSpec((tm, D), lambda i: (off + i, 0)),
            pl.BlockSpec((tm, GMLP_WIDTH), lambda i: (i, 0)),
            pl.BlockSpec((1, FOX_WIDTH, tm), lambda i: (i // ns, 0, i % ns)),
            const((GMLP_WIDTH, D)),
            const((FOX_WIDTH, D)),
            const((FOX_WIDTH, 1)),
            const((1, D)),
            const((E, D)),
            const((E, 1)),
            const((tm, tm)),
        ],
        out_specs=[
            pl.BlockSpec((tm, D), lambda i: (i, 0)),
            pl.BlockSpec((planes, tm, SC_ROW_WORDS), lambda i: (0, i, 0)),
            pl.BlockSpec((8, tm), lambda i: (0, i)),
            pl.BlockSpec((8, tm), lambda i: (0, i)),
            pl.BlockSpec((tm, 8), lambda i: (i, 0)),
            const((E, 128)),
        ],
        out_shape=[
            jax.ShapeDtypeStruct((Tg, D), F32),
            jax.ShapeDtypeStruct((planes, Tg, SC_ROW_WORDS), U32),
            jax.ShapeDtypeStruct((8, Tg), I32),
            jax.ShapeDtypeStruct((8, Tg), I32),
            jax.ShapeDtypeStruct((Tg, 8), F32),
            jax.ShapeDtypeStruct((E, 128), I32),
        ],
        scratch_shapes=[pltpu.VMEM((E, 128), F32)],
        compiler_params=pltpu.CompilerParams(
            dimension_semantics=("arbitrary",), vmem_limit_bytes=VMEM_LIMIT),
        name="out_router",
    )(h, ya, ybt, prm["woa"], prm["wob"], prm["g_b"], prm["g_moe"], prm["wrt"], prm["b_router"], tri_strict)


def _dest_kernel(ps_ref, tope_ref, rank_ref, dest_ref):
    e_sel = tope_ref[...]
    d = rank_ref[...]
    for e in range(N_EXPERTS):
        d = d + jnp.where(e_sel == e, ps_ref[e], 0)
    dest_ref[...] = d


def _dest_rows(pad_starts, tope, rank):
    T = tope.shape[1]
    tc = min(4096, T)
    return pl.pallas_call(
        _dest_kernel,
        grid_spec=pltpu.PrefetchScalarGridSpec(
            num_scalar_prefetch=1,
            grid=(T // tc,),
            in_specs=[pl.BlockSpec((8, tc), lambda i, ps: (0, i)),
                      pl.BlockSpec((8, tc), lambda i, ps: (0, i))],
            out_specs=pl.BlockSpec((8, tc), lambda i, ps: (0, i)),
        ),
        out_shape=jax.ShapeDtypeStruct((8, T), I32),
        name="dest_rows",
    )(pad_starts, tope, rank)


def _sc_mesh():
    return plsc.VectorSubcoreMesh(core_axis_name="c", subcore_axis_name="s")


def _sc_scatter_rows(x, idx, n_out):
    R, W = x.shape
    N = idx.shape[0]
    nsrc = R // SC_WINDOW

    @functools.partial(pl.kernel, out_type=jax.ShapeDtypeStruct((n_out, W), x.dtype), mesh=_sc_mesh())
    def k(x_hbm, i_hbm, o_hbm):
        def body(x_vmem, i_vmem):
            pltpu.sync_copy(x_vmem, o_hbm.at[i_vmem.at[0]])

        pltpu.emit_pipeline(
            body,
            grid=(N // SC_WINDOW,),
            in_specs=[pl.BlockSpec((SC_WINDOW, W), lambda i: (i % nsrc, 0)),
                      pl.BlockSpec((1, SC_WINDOW), lambda i: (0, i))],
            out_specs=[],
            core_axis_name=("c", "s"),
            dimension_semantics=(pltpu.PARALLEL,),
        )(x_hbm, i_hbm)

    return k(x, idx.reshape(1, N))


def _sc_gather_rows(table, idx):
    V, W = table.shape
    N = idx.shape[0]

    @functools.partial(pl.kernel, out_type=jax.ShapeDtypeStruct((N, W), table.dtype), mesh=_sc_mesh())
    def k(t_hbm, i_hbm, o_hbm):
        def body(i_vmem, o_vmem):
            pltpu.sync_copy(t_hbm.at[i_vmem.at[0]], o_vmem)

        pltpu.emit_pipeline(
            body,
            grid=(N // SC_WINDOW,),
            in_specs=[pl.BlockSpec((1, SC_WINDOW), lambda i: (0, i))],
            out_specs=[pl.BlockSpec((SC_WINDOW, W), lambda i: (i, 0))],
            core_axis_name=("c", "s"),
            dimension_semantics=(pltpu.PARALLEL,),
        )(i_hbm, o_hbm)

    return k(table, idx.reshape(1, N))


def _expert_kernel(be_ref, nv_ref, fs_ref, nxt_ref, uni_ref, xs_ref, wgu_hbm, bgu_ref, wdn_hbm, bdn_ref, ys_ref,
                   wgu_f32, wdn_f32, cur_slot, sem, *, layer):
    F = wdn_f32.shape[1]

    def weight_copies(e, slot):
        return (pltpu.make_async_copy(wgu_hbm.at[layer, e], wgu_f32.at[slot], sem.at[slot, 0]),
                pltpu.make_async_copy(wdn_hbm.at[layer, e], wdn_f32.at[slot], sem.at[slot, 1]))

    @pl.when(pl.program_id(0) == 0)
    def _():
        for cp in weight_copies(be_ref[0], 0):
            cp.start()

    def ffn(x, e):
        slot = cur_slot[0]
        hgu = jnp.dot(x, wgu_f32[slot], preferred_element_type=F32) + bgu_ref[e]
        g = jnp.minimum(hgu[:, :F], SWIGLU_LIMIT)
        lin = jnp.clip(hgu[:, F:], -SWIGLU_LIMIT, SWIGLU_LIMIT)
        act = g * jax.nn.sigmoid(SWIGLU_ALPHA * g) * (lin + 1.0)
        return _pack_rows(jnp.dot(act, wdn_f32[slot], preferred_element_type=F32) + bdn_ref[e])

    def load_rows(rs):
        return _unpack_rows(jnp.concatenate([xs_ref[pn, rs, :] for pn in range(xs_ref.shape[0])], axis=1))

    def store_rows(rs, y):
        for pn in range(ys_ref.shape[0]):
            ys_ref[pn, rs, :] = y[:, pn * SC_ROW_WORDS:(pn + 1) * SC_ROW_WORDS]

    step = pl.program_id(0)
    uniform = uni_ref[step] > 0

    @pl.when(uniform)
    def _():
        rs = slice(0, EXPERT_STEP_BLOCKS * EXPERT_ROWS)
        store_rows(rs, ffn(load_rows(rs), be_ref[step * EXPERT_STEP_BLOCKS]))

    for j in range(EXPERT_STEP_BLOCKS):
        b = step * EXPERT_STEP_BLOCKS + j
        rs = slice(j * EXPERT_ROWS, (j + 1) * EXPERT_ROWS)
        e = be_ref[b]
        nv = nv_ref[b]
        fs = fs_ref[b]

        @pl.when(fs >= 0)
        def _():
            for cp in weight_copies(e, fs):
                cp.wait()
            cur_slot[0] = fs
            nxt = nxt_ref[b]

            @pl.when(nxt >= 0)
            def _():
                for cp in weight_copies(nxt, 1 - fs):
                    cp.start(priority=1)

        @pl.when(jnp.logical_not(uniform) & (nv > 0))
        def _():
            x = load_rows(rs)
            rows = lax.broadcasted_iota(I32, x.shape, 0)
            store_rows(rs, ffn(jnp.where(rows < nv, x, 0.0), e))

        @pl.when(nv == 0)
        def _():
            ys_ref[:, rs, :] = jnp.zeros((ys_ref.shape[0], EXPERT_ROWS, SC_ROW_WORDS), U32)


def _experts(layer, plan, xs, w_gu, b_gu, w_dn, b_dn):
    planes, P, _ = xs.shape
    _, E, D, F2 = w_gu.shape
    F = F2 // 2
    step_rows = EXPERT_STEP_BLOCKS * EXPERT_ROWS
    by_step = lambda i, be, nv, fs, nxt, uni: (0, i, 0)
    layer_all = lambda i, be, nv, fs, nxt, uni: (layer, 0, 0, 0)
    return pl.pallas_call(
        functools.partial(_expert_kernel, layer=layer),
        grid_spec=pltpu.PrefetchScalarGridSpec(
            num_scalar_prefetch=5,
            grid=(P // step_rows,),
            in_specs=[
                pl.BlockSpec((planes, step_rows, SC_ROW_WORDS), by_step),
                pl.BlockSpec(memory_space=pl.ANY),
                pl.BlockSpec((None, E, 1, F2), layer_all),
                pl.BlockSpec(memory_space=pl.ANY),
                pl.BlockSpec((None, E, 1, D), layer_all),
            ],
            out_specs=pl.BlockSpec((planes, step_rows, SC_ROW_WORDS), by_step),
            scratch_shapes=[
                pltpu.VMEM((2, D, F2), F32), pltpu.VMEM((2, F, D), F32),
                pltpu.SMEM((1,), I32),
                pltpu.SemaphoreType.DMA((2, 2)),
            ],
        ),
        out_shape=jax.ShapeDtypeStruct((planes, P, SC_ROW_WORDS), U32),
        compiler_params=pltpu.CompilerParams(
            dimension_semantics=("arbitrary",), vmem_limit_bytes=VMEM_LIMIT),
        name="experts",
    )(plan["block_e"], plan["block_nv"], plan["first_slot"], plan["next_e"], plan["step_uniform"],
      xs, w_gu, b_gu.reshape(-1, E, 1, F2), w_dn, b_dn.reshape(-1, E, 1, D))


def _combine_ple_kernel(h1_ref, yg_ref, gates_ref, gple_ref, wpg_ref, p_ref, wple_ref, gfin_ref, *rest, final):
    o_ref = rest[-1]
    for r in range(h1_ref.shape[0] // COMBINE_ROWS):
        rs = slice(r * COMBINE_ROWS, (r + 1) * COMBINE_ROWS)
        h2 = h1_ref[rs, :]
        gates = gates_ref[rs, :]
        for k in range(TOP_K):
            yk = jnp.concatenate([yg_ref[k, pn, rs, :] for pn in range(yg_ref.shape[1])], axis=1)
            h2 = h2 + gates[:, k:k + 1] * _unpack_rows(yk)
        xn = _rms(h2, gple_ref[...]).astype(BF16)
        gate = jax.nn.sigmoid(jnp.dot(xn, wpg_ref[...], preferred_element_type=F32))
        pe = jnp.dot(p_ref[rs, :].astype(BF16), wple_ref[...], preferred_element_type=F32)
        h3 = h2 + gate * pe
        if final:
            h3 = _rms(h3, gfin_ref[...])
        o_ref[rs, :] = h3


def _combine_ple(layer, group, h1, yg, gates, p, prm, g_final, final, h_prev):
    Tg, D = h1.shape
    T = Tg * MOE_GROUPS
    tm = min(ROW_TILE, Tg)
    PD = p.shape[2]
    planes = yg.shape[1]
    off = group * (Tg // tm)
    const = lambda shape: pl.BlockSpec(shape, lambda i: (0,) * len(shape))
    in_specs = [
        pl.BlockSpec((tm, D), lambda i: (i, 0)),
        pl.BlockSpec((TOP_K, planes, tm, SC_ROW_WORDS), lambda i: (0, 0, i, 0)),
        pl.BlockSpec((tm, 8), lambda i: (i, 0)),
        const((1, D)),
        const((D, D)),
        pl.BlockSpec((None, tm, PD), lambda i: (layer, off + i, 0)),
        const((PD, D)),
        const((1, D)),
    ]
    args = [h1, yg, gates, prm["g_ple"], prm["wpg"], p, prm["wple"], g_final]
    aliases = {}
    if h_prev is not None:
        in_specs.append(pl.BlockSpec(memory_space=pl.ANY))
        args.append(h_prev)
        aliases = {len(args) - 1: 0}
    return pl.pallas_call(
        functools.partial(_combine_ple_kernel, final=final),
        grid=(Tg // tm,),
        in_specs=in_specs,
        out_specs=pl.BlockSpec((tm, D), lambda i: (off + i, 0)),
        out_shape=jax.ShapeDtypeStruct((T, D), F32),
        input_output_aliases=aliases,
        compiler_params=pltpu.CompilerParams(
            dimension_semantics=("parallel",), vmem_limit_bytes=VMEM_LIMIT),
        name="combine_ple",
    )(*args)


def _layer_params(i, w):
    D = w["w_in"].shape[1]
    H = FOX_HEADS
    w_in = w["w_in"][i]
    c1 = 2 * GMLP_WIDTH
    c2, c3, c4 = c1 + FOX_WIDTH, c1 + 2 * FOX_WIDTH, c1 + 3 * FOX_WIDTH
    wk = jnp.zeros((D, H, HEAD_PAD), F32).at[:, :, :FOX_HEAD_DIM].set(
        w_in[:, c2:c3].reshape(D, H, FOX_HEAD_DIM)).reshape(D, H * HEAD_PAD)
    return {
        "g_mix": w["g_mix"][i].reshape(1, D),
        "wuv": w_in[:, :c1].astype(BF16),
        "wqt": (w_in[:, c1:c2] * (FOX_HEAD_DIM ** -0.5 * LOG2E)).T.astype(BF16),
        "wk": wk.astype(BF16),
        "wvt": w_in[:, c3:c4].T.astype(BF16),
        "wft": w_in[:, c4:].T.astype(BF16),
        "b_f": w["b_f"][i].reshape(H, 1),
        "ln_g": w["ln_g"][i].reshape(1, GMLP_WIDTH),
        "ln_b": w["ln_b"][i].reshape(1, GMLP_WIDTH),
        "w_s": w["w_s"][i],
        "b_s": w["b_s"][i].reshape(GMLP_GROUPS, GMLP_BLOCK, 1),
        "g_a": w["g_a"][i].reshape(1, GMLP_WIDTH),
        "woa": w["w_out"][i][:GMLP_WIDTH].astype(BF16),
        "wob": w["w_out"][i][GMLP_WIDTH:].astype(BF16),
        "g_b": w["g_b"][i].reshape(FOX_WIDTH, 1),
        "g_moe": w["g_moe"][i].reshape(1, D),
        "wrt": w["w_router"][i].T.astype(BF16),
        "b_router": w["b_router"][i].reshape(N_EXPERTS, 1),
        "g_ple": w["g_ple"][i].reshape(1, D),
        "wpg": w["w_ple_gate"][i].astype(BF16),
        "wple": w["w_ple"][i].astype(BF16),
    }


def _bias_constants(tm):
    H = FOX_HEADS
    j = jnp.arange(tm)
    tri_incl = (j[:, None] <= j[None, :]).astype(BF16)
    tri_strict = (j[:, None] < j[None, :]).astype(BF16)
    part = jnp.arange(BIAS_PARTS)[:, None]
    head = jnp.arange(H)[None, :]
    rows = (part * H + head).reshape(-1)
    cols = (head * HEAD_PAD + FOX_HEAD_DIM + BIAS_PARTS + part).reshape(-1)
    selk = jnp.zeros((32, H * HEAD_PAD), F32).at[rows, cols].set(-1.0).astype(BF16)
    one_cols = (jnp.arange(H)[:, None] * HEAD_PAD + FOX_HEAD_DIM + jnp.arange(BIAS_PARTS)[None, :]).reshape(-1)
    onesk = jnp.zeros((1, H * HEAD_PAD), F32).at[0, one_cols].set(1.0)
    return tri_incl, tri_strict, selk, onesk


def _moe_dispatch_plan(counts, n_blocks):
    padded = (counts + EXPERT_ROWS - 1) // EXPERT_ROWS * EXPERT_ROWS
    pad_ends = jnp.cumsum(padded)
    pad_starts = (pad_ends - padded).astype(I32)
    block_start = jnp.arange(n_blocks, dtype=I32) * EXPERT_ROWS
    n_before = jnp.sum((pad_ends[None, :] <= block_start[:, None]).astype(I32), axis=1)
    block_e = jnp.minimum(n_before, N_EXPERTS - 1).astype(I32)
    own = block_e[:, None] == jnp.arange(N_EXPERTS, dtype=I32)[None, :]
    seg_end = jnp.sum(jnp.where(own, (pad_starts + counts)[None, :], 0), axis=1)
    block_nv = jnp.clip(seg_end - block_start, 0, EXPERT_ROWS)
    block_nv = jnp.where(block_start < pad_ends[-1], block_nv, 0).astype(I32)
    prev_e = jnp.concatenate([jnp.full((1,), -1, I32), block_e[:-1]])
    first = (block_nv > 0) & (block_e != prev_e)
    run = jnp.cumsum(first.astype(I32)) - 1
    first_slot = jnp.where(first, run % 2, -1).astype(I32)
    ar = jnp.arange(N_EXPERTS, dtype=I32)
    later = (ar[None, :] > ar[:, None]) & (counts > 0)[None, :]
    next_of = jnp.min(jnp.where(later, ar[None, :], N_EXPERTS), axis=1)
    next_of = jnp.where(next_of < N_EXPERTS, next_of, -1)
    next_e = jnp.sum(jnp.where(own, next_of[None, :], 0), axis=1).astype(I32)
    plain = ((block_nv == EXPERT_ROWS) & jnp.logical_not(first)).reshape(-1, EXPERT_STEP_BLOCKS)
    step_uniform = jnp.all(plain, axis=1).astype(I32)
    return dict(pad_starts=pad_starts, block_e=block_e, block_nv=block_nv, first_slot=first_slot, next_e=next_e,
                step_uniform=step_uniform)


def kernel(x, p, g_mix, w_in, ln_g, ln_b, w_s, b_s, b_f, g_a, g_b, w_out, g_moe, w_router, b_router,
           w_gu, b_gu, w_dn, b_dn, g_ple, w_ple_gate, w_ple, g_final):
    B, S, D = x.shape
    T = B * S
    depth = w_in.shape[0]
    w = dict(g_mix=g_mix, w_in=w_in, ln_g=ln_g, ln_b=ln_b, w_s=w_s, b_s=b_s, b_f=b_f, g_a=g_a, g_b=g_b,
             w_out=w_out, g_moe=g_moe, w_router=w_router, b_router=b_router, w_gu=w_gu, b_gu=b_gu,
             w_dn=w_dn, b_dn=b_dn, g_ple=g_ple, w_ple_gate=w_ple_gate, w_ple=w_ple)
    tri_incl, tri_strict, selk, onesk = _bias_constants(min(ROW_TILE, S))
    G = MOE_GROUPS
    Tg = T // G
    n_blocks = -(-(Tg * TOP_K) // EXPERT_ROWS) + N_EXPERTS
    n_blocks = -(-n_blocks // EXPERT_STEP_BLOCKS) * EXPERT_STEP_BLOCKS
    P = n_blocks * EXPERT_ROWS
    planes = (D // 2) // SC_ROW_WORDS
    plane_base = (jnp.arange(planes, dtype=I32) * P)[None, :, None]
    p_rows = p.reshape(depth, T, p.shape[-1])

    h = x.reshape(T, D)
    for i in range(depth):
        prm = _layer_params(i, w)
        prm.update(tri_incl=tri_incl, selk=selk, onesk=onesk)
        routed = []
        idx = jnp.zeros((8, 128), I32)
        for g in range(G):
            ya, qt, kx, vt = _mix_in(g, h, B, S, prm, idx)
            ybt = _fox_attn(qt, kx, vt, B // G, S)
            h1, xp, tope, rank, gates, counts = _out_router(g, h, ya, ybt, B, S, prm, tri_strict)
            plan = _moe_dispatch_plan(counts[:, 0], n_blocks)
            dest = _dest_rows(plan["pad_starts"], tope, rank)[:TOP_K]
            idx = (dest[:, None, :] + plane_base).reshape(-1)
            xs = _sc_scatter_rows(xp.reshape(planes * Tg, SC_ROW_WORDS), idx, planes * P)
            routed.append((h1, gates, plan, idx, xs))
        h_out = None
        for g, (h1, gates, plan, idx, xs) in enumerate(routed):
            ys = _experts(i, plan, xs.reshape(planes, P, SC_ROW_WORDS), w_gu, b_gu, w_dn, b_dn)
            yg = _sc_gather_rows(ys.reshape(planes * P, SC_ROW_WORDS), idx)
            h_out = _combine_ple(i, g, h1, yg.reshape(TOP_K, planes, Tg, SC_ROW_WORDS), gates, p_rows, prm,
                                 g_final.reshape(1, D), i == depth - 1, h_out)
        h = h_out
    return h.reshape(B, S, D)
```

```python
import functools

import jax
import jax.numpy as jnp
from jax import lax
from jax.experimental import pallas as pl
from jax.experimental.pallas import tpu as pltpu
from jax.experimental.pallas import tpu_sc as plsc

F32 = jnp.float32
BF16 = jnp.bfloat16
I32 = jnp.int32
U32 = jnp.uint32

CHUNK = 64
GMLP_WIDTH = 512
GMLP_GROUPS = 4
GMLP_GROUP_CH = 128
GMLP_BLOCK = 128
FOX_WIDTH = 512
FOX_HEAD_DIM = 64
FOX_HEADS = 8
N_EXPERTS = 32
TOP_K = 4
EXPERT_ROWS = 256
SWIGLU_LIMIT = 7.0
SWIGLU_ALPHA = 1.702
RMS_EPS = 1e-5
LN_EPS = 1e-5

HEAD_PAD = 128
BIAS_PARTS = 3
ROW_TILE = 512
ATTN_TILE = 512
ATTN_HEADS = 1
ACC_ROWS = 80
LOG2E = 1.4426950408889634
SC_WINDOW = 128
SC_ROW_WORDS = 256
COMBINE_ROWS = 256
EXPERT_STEP_BLOCKS = 4
MOE_GROUPS = 2
VMEM_LIMIT = 56 * 1024 * 1024
NEG_BIG = -1e30

_NT = (((1,), (1,)), ((), ()))
_TN = (((0,), (0,)), ((), ()))


def _rms(x, g):
    return x * lax.rsqrt(jnp.mean(x * x, axis=-1, keepdims=True) + RMS_EPS) * g


def _split3(c):
    a1 = c.astype(BF16).astype(F32)
    r1 = c - a1
    a2 = r1.astype(BF16).astype(F32)
    a3 = (r1 - a2).astype(BF16).astype(F32)
    return a1, a2, a3


def _pack_rows(x):
    n = x.shape[1] // 2
    lo = lax.bitcast_convert_type(x[:, :n].astype(BF16).astype(F32), U32) >> 16
    hi = lax.bitcast_convert_type(x[:, n:].astype(BF16).astype(F32), U32) & jnp.uint32(0xFFFF0000)
    return lo | hi


def _unpack_rows(u):
    lo = lax.bitcast_convert_type(u << 16, F32)
    hi = lax.bitcast_convert_type(u & jnp.uint32(0xFFFF0000), F32)
    return jnp.concatenate([lo, hi], axis=1)


def _store_planes(ref, u):
    for j in range(ref.shape[0]):
        ref[j] = u[:, j * SC_ROW_WORDS:(j + 1) * SC_ROW_WORDS]


def _load_planes(ref):
    return jnp.concatenate([ref[j] for j in range(ref.shape[0])], axis=1)


def _mix_in_kernel(h_ref, gmix_ref, wuv_ref, wqt_ref, wk_ref, wvt_ref, wft_ref, bf_ref, lng_ref, lnb_ref,
                   ws_ref, bs_ref, ga_ref, tri_ref, selk_ref, onesk_ref, after_ref,
                   ya_ref, qt_ref, k_ref, vt_ref, carry_ref, ya_scr):
    del after_ref
    tm = h_ref.shape[0]

    @pl.when(pl.program_id(1) == 0)
    def _():
        carry_ref[...] = jnp.zeros_like(carry_ref)

    xn = _rms(h_ref[...], gmix_ref[...]).astype(BF16)

    z = jnp.dot(xn, wuv_ref[...], preferred_element_type=F32)
    qt = lax.dot_general(wqt_ref[...], xn, _NT, preferred_element_type=F32)
    vt_ref[0] = lax.dot_general(wvt_ref[...], xn, _NT, preferred_element_type=F32).astype(BF16)
    kp = jnp.dot(xn, wk_ref[...], preferred_element_type=F32)
    ft = lax.dot_general(wft_ref[...], xn, _NT, preferred_element_type=F32) + bf_ref[...]

    z = 0.5 * z * (1.0 + lax.erf(z * (0.5 ** 0.5)))
    u = z[:, :GMLP_WIDTH]
    v = z[:, GMLP_WIDTH:]
    mu = jnp.mean(v, axis=-1, keepdims=True)
    vc = v - mu
    var = jnp.mean(vc * vc, axis=-1, keepdims=True)
    vn = (vc * lax.rsqrt(var + LN_EPS) * lng_ref[...] + lnb_ref[...]).astype(BF16)
    pi = lax.broadcasted_iota(I32, (GMLP_BLOCK, GMLP_BLOCK), 0) // CHUNK
    pj = lax.broadcasted_iota(I32, (GMLP_BLOCK, GMLP_BLOCK), 1) // CHUNK
    for g in range(GMLP_GROUPS):
        wm = jnp.where(pj <= pi, ws_ref[g], 0.0).astype(BF16)
        cs = slice(g * GMLP_GROUP_CH, (g + 1) * GMLP_GROUP_CH)
        for n in range(tm // GMLP_BLOCK):
            rs = slice(n * GMLP_BLOCK, (n + 1) * GMLP_BLOCK)
            sv = jnp.dot(wm, vn[rs, cs], preferred_element_type=F32) + bs_ref[g]
            ya_scr[rs, cs] = u[rs, cs] * sv
    ya_ref[...] = _rms(ya_scr[...], ga_ref[...]).astype(BF16)

    lf = (jnp.minimum(ft, 0.0) - jnp.log1p(jnp.exp(-jnp.abs(ft)))) * LOG2E
    zpad = jnp.zeros((32 - BIAS_PARTS * FOX_HEADS, tm), F32)
    stack = jnp.concatenate(list(_split3(lf)) + [zpad], axis=0).astype(BF16)
    cum = jnp.dot(stack, tri_ref[...], preferred_element_type=F32)
    c = cum[0:8] + cum[8:16] + cum[16:24] + carry_ref[:, 0:1]
    carry_ref[...] = jnp.broadcast_to(c[:, tm - 1:tm], carry_ref.shape)
    a1, a2, a3 = _split3(c)

    si = lax.broadcasted_iota(I32, (8, tm), 0)
    zrows = jnp.zeros((HEAD_PAD - FOX_HEAD_DIM - 8, tm), F32)
    for hd in range(FOX_HEADS):
        b1 = jnp.broadcast_to(a1[hd:hd + 1], (8, tm))
        b2 = jnp.broadcast_to(a2[hd:hd + 1], (8, tm))
        b3 = jnp.broadcast_to(a3[hd:hd + 1], (8, tm))
        ext = jnp.where(si == 0, b1, jnp.where(si == 1, b2, jnp.where(si == 2, b3,
                        jnp.where(si < 2 * BIAS_PARTS, 1.0, 0.0))))
        blk = jnp.concatenate([qt[hd * FOX_HEAD_DIM:(hd + 1) * FOX_HEAD_DIM], ext, zrows], axis=0)
        qt_ref[0, hd * HEAD_PAD:(hd + 1) * HEAD_PAD, :] = blk.astype(BF16)

    cstack = jnp.concatenate([a1, a2, a3, zpad], axis=0).astype(BF16)
    kext = lax.dot_general(cstack, selk_ref[...], _TN, preferred_element_type=F32)
    k_ref[...] = (kp + kext + onesk_ref[...]).astype(BF16)


def _mix_in(group, h, B, S, prm, after):
    T, D = h.shape
    Bg = B // MOE_GROUPS
    Tg = Bg * S
    tm = min(ROW_TILE, S)
    ns = S // tm
    H = FOX_HEADS
    const = lambda shape: pl.BlockSpec(shape, lambda b, s: (0,) * len(shape))
    return pl.pallas_call(
        _mix_in_kernel,
        grid=(Bg, ns),
        in_specs=[
            pl.BlockSpec((tm, D), lambda b, s: ((group * Bg + b) * ns + s, 0)),
            const((1, D)),
            const((D, 2 * GMLP_WIDTH)),
            const((FOX_WIDTH, D)),
            const((D, H * HEAD_PAD)),
            const((FOX_WIDTH, D)),
            const((H, D)),
            const((H, 1)),
            const((1, GMLP_WIDTH)),
            const((1, GMLP_WIDTH)),
            const((GMLP_GROUPS, GMLP_BLOCK, GMLP_BLOCK)),
            const((GMLP_GROUPS, GMLP_BLOCK, 1)),
            const((1, GMLP_WIDTH)),
            const((tm, tm)),
            const((32, H * HEAD_PAD)),
            const((1, H * HEAD_PAD)),
            pl.BlockSpec(memory_space=pl.ANY),
        ],
        out_specs=[
            pl.BlockSpec((tm, GMLP_WIDTH), lambda b, s: (b * ns + s, 0)),
            pl.BlockSpec((1, H * HEAD_PAD, tm), lambda b, s: (b, 0, s)),
            pl.BlockSpec((tm, H * HEAD_PAD), lambda b, s: (b * ns + s, 0)),
            pl.BlockSpec((1, FOX_WIDTH, tm), lambda b, s: (b, 0, s)),
        ],
        out_shape=[
            jax.ShapeDtypeStruct((Tg, GMLP_WIDTH), BF16),
            jax.ShapeDtypeStruct((Bg, H * HEAD_PAD, S), BF16),
            jax.ShapeDtypeStruct((Tg, H * HEAD_PAD), BF16),
            jax.ShapeDtypeStruct((Bg, FOX_WIDTH, S), BF16),
        ],
        scratch_shapes=[pltpu.VMEM((H, 128), F32), pltpu.VMEM((tm, GMLP_WIDTH), F32)],
        compiler_params=pltpu.CompilerParams(
            dimension_semantics=("arbitrary", "arbitrary"), vmem_limit_bytes=VMEM_LIMIT),
        name="mix_in",
    )(h, prm["g_mix"], prm["wuv"], prm["wqt"], prm["wk"], prm["wvt"], prm["wft"], prm["b_f"],
      prm["ln_g"], prm["ln_b"], prm["w_s"], prm["b_s"], prm["g_a"], prm["tri_incl"], prm["selk"], prm["onesk"],
      after)


def _fox_attn_kernel(qt_ref, k_ref, vt_ref, ot_ref):
    S = k_ref.shape[0]
    t = min(ATTN_TILE, S)
    nq = S // t
    hf = t // 2
    ones_rows = jnp.ones((ACC_ROWS - FOX_HEAD_DIM, t), BF16)
    pairs = [(qi, kj) for qi in range(nq) for kj in range(qi + 1)]

    def tri(rows, cols):
        return lax.broadcasted_iota(I32, (rows, cols), 0) <= lax.broadcasted_iota(I32, (rows, cols), 1)

    def scores(qi, kj):
        out = []
        for hh in range(ATTN_HEADS):
            hs = slice(hh * HEAD_PAD, (hh + 1) * HEAD_PAD)
            if kj < qi:
                out.append([jnp.dot(k_ref[kj * t:(kj + 1) * t, hs], qt_ref[0, hs, qi * t:(qi + 1) * t],
                                    preferred_element_type=F32)])
            else:
                top = jnp.dot(k_ref[kj * t:kj * t + hf, hs], qt_ref[0, hs, qi * t:(qi + 1) * t],
                              preferred_element_type=F32)
                bot = jnp.dot(k_ref[kj * t + hf:(kj + 1) * t, hs], qt_ref[0, hs, qi * t + hf:(qi + 1) * t],
                              preferred_element_type=F32)
                out.append([jnp.where(tri(hf, t), top, NEG_BIG), jnp.where(tri(hf, hf), bot, NEG_BIG)])
        return out

    s_next = scores(*pairs[0])
    state = None
    for n, (qi, kj) in enumerate(pairs):
        s_cur = s_next
        if n + 1 < len(pairs):
            s_next = scores(*pairs[n + 1])
        new_state = []
        for hh in range(ATTN_HEADS):
            parts = s_cur[hh]
            v = vt_ref[0, hh * FOX_HEAD_DIM:(hh + 1) * FOX_HEAD_DIM, kj * t:(kj + 1) * t]
            v_aug = jnp.concatenate([v, ones_rows], axis=0)
            mx = jnp.max(parts[0], axis=0, keepdims=True)
            if len(parts) == 2:
                mx = jnp.concatenate(
                    [mx[:, :hf], jnp.maximum(mx[:, hf:], jnp.max(parts[1], axis=0, keepdims=True))], axis=1)
            m_new = mx if kj == 0 else jnp.maximum(state[hh][0], mx)
            if len(parts) == 1:
                pv = jnp.dot(v_aug, jnp.exp2(parts[0] - m_new).astype(BF16), preferred_element_type=F32)
            else:
                pv = jnp.dot(v_aug[:, :hf], jnp.exp2(parts[0] - m_new).astype(BF16), preferred_element_type=F32)
                pv_r = jnp.dot(v_aug[:, hf:], jnp.exp2(parts[1] - m_new[:, hf:]).astype(BF16),
                               preferred_element_type=F32)
                pv = jnp.concatenate([pv[:, :hf], pv[:, hf:] + pv_r], axis=1)
            acc = pv if kj == 0 else jnp.exp2(state[hh][0] - m_new) * state[hh][1] + pv
            new_state.append((m_new, acc))
        state = new_state
        if kj == qi:
            for hh in range(ATTN_HEADS):
                acc = state[hh][1]
                o = acc[:FOX_HEAD_DIM] / acc[FOX_HEAD_DIM:FOX_HEAD_DIM + 1]
                ot_ref[0, hh * FOX_HEAD_DIM:(hh + 1) * FOX_HEAD_DIM, qi * t:(qi + 1) * t] = o.astype(BF16)


def _fox_attn(qt, kx, vt, B, S):
    G = ATTN_HEADS
    return pl.pallas_call(
        _fox_attn_kernel,
        grid=(B, FOX_HEADS // G),
        in_specs=[
            pl.BlockSpec((1, G * HEAD_PAD, S), lambda b, h: (b, h, 0)),
            pl.BlockSpec((S, G * HEAD_PAD), lambda b, h: (b, h)),
            pl.BlockSpec((1, G * FOX_HEAD_DIM, S), lambda b, h: (b, h, 0)),
        ],
        out_specs=pl.BlockSpec((1, G * FOX_HEAD_DIM, S), lambda b, h: (b, h, 0)),
        out_shape=jax.ShapeDtypeStruct((B, FOX_WIDTH, S), BF16),
        compiler_params=pltpu.CompilerParams(
            dimension_semantics=("parallel", "parallel"), vmem_limit_bytes=VMEM_LIMIT),
        name="fox_attn",
    )(qt, kx, vt)


def _out_router_kernel(h_ref, ya_ref, ybt_ref, woa_ref, wob_ref, gb_ref, gmoe_ref, wrt_ref, br_ref, tri_ref,
                       h1_ref, xp_ref, tope_ref, rank_ref, gates_ref, counts_ref, carry_ref):
    tm = h_ref.shape[0]

    @pl.when(pl.program_id(0) == 0)
    def _():
        carry_ref[...] = jnp.zeros_like(carry_ref)

    yb = ybt_ref[0].astype(F32)
    ybn = (yb * lax.rsqrt(jnp.mean(yb * yb, axis=0, keepdims=True) + RMS_EPS) * gb_ref[...]).astype(BF16)
    y = jnp.dot(ya_ref[...], woa_ref[...], preferred_element_type=F32)
    y = y + lax.dot_general(ybn, wob_ref[...], _TN, preferred_element_type=F32)
    h1 = h_ref[...] + y
    h1_ref[...] = h1
    xn = _rms(h1, gmoe_ref[...])
    _store_planes(xp_ref, _pack_rows(xn))

    logits = lax.dot_general(wrt_ref[...], xn.astype(BF16), _NT, preferred_element_type=F32) + br_ref[...]
    ri = lax.broadcasted_iota(I32, (N_EXPERTS, tm), 0)
    vals, idxs = [], []
    l = logits
    for _ in range(TOP_K):
        m = jnp.max(l, axis=0, keepdims=True)
        idx = jnp.min(jnp.where(l == m, ri, N_EXPERTS), axis=0, keepdims=True)
        vals.append(m)
        idxs.append(idx)
        l = jnp.where(ri == idx, -jnp.inf, l)
    es = [jnp.exp(vk - vals[0]) for vk in vals]
    den = es[0] + es[1] + es[2] + es[3]
    zrow = jnp.zeros((8 - TOP_K, tm), F32)
    gates_ref[...] = jnp.concatenate([e / den for e in es] + [zrow], axis=0).T

    sel = jnp.zeros((N_EXPERTS, tm), F32)
    for idx in idxs:
        sel = sel + jnp.where(ri == idx, 1.0, 0.0)
    cnt = jnp.dot(sel.astype(BF16), tri_ref[...], preferred_element_type=F32) + carry_ref[:, 0:1]
    ranks = [jnp.sum(jnp.where(ri == idx, cnt, 0.0), axis=0, keepdims=True) for idx in idxs]
    zi = jnp.zeros((8 - TOP_K, tm), I32)
    rank_ref[...] = jnp.concatenate([r.astype(I32) for r in ranks] + [zi], axis=0)
    tope_ref[...] = jnp.concatenate(idxs + [zi], axis=0)
    total = carry_ref[...] + jnp.sum(sel, axis=1, keepdims=True)
    carry_ref[...] = total
    counts_ref[...] = total.astype(I32)


def _out_router(group, h, ya, ybt, B, S, prm, tri_strict):
    T, D = h.shape
    Tg = T // MOE_GROUPS
    tm = min(ROW_TILE, S)
    ns = S // tm
    E = N_EXPERTS
    planes = (D // 2) // SC_ROW_WORDS
    off = group * (Tg // tm)
    const = lambda shape: pl.BlockSpec(shape, lambda i: (0,) * len(shape))
    return pl.pallas_call(
        _out_router_kernel,
        grid=(Tg // tm,),
        in_specs=[
            pl.BlockSpec((tm, D), lambda i: (off + i, 0)),
            pl.BlockSpec((tm, GMLP_WIDTH), lambda i: (i, 0)),
            pl.BlockSpec((1, FOX_WIDTH, tm), lambda i: (i // ns, 0, i % ns)),
            const((GMLP_WIDTH, D)),
            const((FOX_WIDTH, D)),
            const((FOX_WIDTH, 1)),
            const((1, D)),
            const((E, D)),
            const((E, 1)),
            const((tm, tm)),
        ],
        out_specs=[
            pl.BlockSpec((tm, D), lambda i: (i, 0)),
            pl.BlockSpec((planes, tm, SC_ROW_WORDS), lambda i: (0, i, 0)),
            pl.BlockSpec((8, tm), lambda i: (0, i)),
            pl.BlockSpec((8, tm), lambda i: (0, i)),
            pl.BlockSpec((tm, 8), lambda i: (i, 0)),
            const((E, 128)),
        ],
        out_shape=[
            jax.ShapeDtypeStruct((Tg, D), F32),
            jax.ShapeDtypeStruct((planes, Tg, SC_ROW_WORDS), U32),
            jax.ShapeDtypeStruct((8, Tg), I32),
            jax.ShapeDtypeStruct((8, Tg), I32),
            jax.ShapeDtypeStruct((Tg, 8), F32),
            jax.ShapeDtypeStruct((E, 128), I32),
        ],
        scratch_shapes=[pltpu.VMEM((E, 128), F32)],
        compiler_params=pltpu.CompilerParams(
            dimension_semantics=("arbitrary",), vmem_limit_bytes=VMEM_LIMIT),
        name="out_router",
    )(h, ya, ybt, prm["woa"], prm["wob"], prm["g_b"], prm["g_moe"], prm["wrt"], prm["b_router"], tri_strict)


def _dest_kernel(ps_ref, tope_ref, rank_ref, dest_ref):
    e_sel = tope_ref[...]
    d = rank_ref[...]
    for e in range(N_EXPERTS):
        d = d + jnp.where(e_sel == e, ps_ref[e], 0)
    dest_ref[...] = d


def _dest_rows(pad_starts, tope, rank):
    T = tope.shape[1]
    tc = min(4096, T)
    return pl.pallas_call(
        _dest_kernel,
        grid_spec=pltpu.PrefetchScalarGridSpec(
            num_scalar_prefetch=1,
            grid=(T // tc,),
            in_specs=[pl.BlockSpec((8, tc), lambda i, ps: (0, i)),
                      pl.BlockSpec((8, tc), lambda i, ps: (0, i))],
            out_specs=pl.BlockSpec((8, tc), lambda i, ps: (0, i)),
        ),
        out_shape=jax.ShapeDtypeStruct((8, T), I32),
        name="dest_rows",
    )(pad_starts, tope, rank)


def _sc_mesh():
    return plsc.VectorSubcoreMesh(core_axis_name="c", subcore_axis_name="s")


def _sc_scatter_rows(x, idx, n_out):
    R, W = x.shape
    N = idx.shape[0]
    nsrc = R // SC_WINDOW
    copies = N // R

    @functools.partial(pl.kernel, out_type=jax.ShapeDtypeStruct((n_out, W), x.dtype), mesh=_sc_mesh())
    def k(x_hbm, i_hbm, o_hbm):
        def body(x_vmem, *i_vmems):
            for i_vmem in i_vmems:
                pltpu.sync_copy(x_vmem, o_hbm.at[i_vmem.at[0]])

        idx_spec = lambda c: pl.BlockSpec((1, SC_WINDOW), lambda i: (0, c * nsrc + i))
        pltpu.emit_pipeline(
            body,
            grid=(nsrc,),
            in_specs=[pl.BlockSpec((SC_WINDOW, W), lambda i: (i, 0))] + [idx_spec(c) for c in range(copies)],
            out_specs=[],
            core_axis_name=("c", "s"),
            dimension_semantics=(pltpu.PARALLEL,),
        )(x_hbm, *([i_hbm] * copies))

    return k(x, idx.reshape(1, N))


def _sc_gather_rows(table, idx):
    V, W = table.shape
    N = idx.shape[0]

    @functools.partial(pl.kernel, out_type=jax.ShapeDtypeStruct((N, W), table.dtype), mesh=_sc_mesh())
    def k(t_hbm, i_hbm, o_hbm):
        def body(i_vmem, o_vmem):
            pltpu.sync_copy(t_hbm.at[i_vmem.at[0]], o_vmem)

        pltpu.emit_pipeline(
            body,
            grid=(N // SC_WINDOW,),
            in_specs=[pl.BlockSpec((1, SC_WINDOW), lambda i: (0, i))],
            out_specs=[pl.BlockSpec((SC_WINDOW, W), lambda i: (i, 0))],
            core_axis_name=("c", "s"),
            dimension_semantics=(pltpu.PARALLEL,),
        )(i_hbm, o_hbm)

    return k(table, idx.reshape(1, N))


def _expert_kernel(be_ref, nv_ref, fs_ref, nxt_ref, uni_ref, xs_ref, wgu_hbm, bgu_ref, wdn_hbm, bdn_ref, ys_ref,
                   wgu_f32, wdn_f32, cur_slot, sem, *, layer):
    F = wdn_f32.shape[1]

    def weight_copies(e, slot):
        return (pltpu.make_async_copy(wgu_hbm.at[layer, e], wgu_f32.at[slot], sem.at[slot, 0]),
                pltpu.make_async_copy(wdn_hbm.at[layer, e], wdn_f32.at[slot], sem.at[slot, 1]))

    @pl.when(pl.program_id(0) == 0)
    def _():
        for cp in weight_copies(be_ref[0], 0):
            cp.start()

    def ffn(x, e):
        slot = cur_slot[0]
        hgu = jnp.dot(x, wgu_f32[slot], preferred_element_type=F32) + bgu_ref[e]
        g = jnp.minimum(hgu[:, :F], SWIGLU_LIMIT)
        lin = jnp.clip(hgu[:, F:], -SWIGLU_LIMIT, SWIGLU_LIMIT)
        act = g * jax.nn.sigmoid(SWIGLU_ALPHA * g) * (lin + 1.0)
        return _pack_rows(jnp.dot(act, wdn_f32[slot], preferred_element_type=F32) + bdn_ref[e])

    def load_rows(rs):
        return _unpack_rows(jnp.concatenate([xs_ref[pn, rs, :] for pn in range(xs_ref.shape[0])], axis=1))

    def store_rows(rs, y):
        for pn in range(ys_ref.shape[0]):
            ys_ref[pn, rs, :] = y[:, pn * SC_ROW_WORDS:(pn + 1) * SC_ROW_WORDS]

    step = pl.program_id(0)
    uniform = uni_ref[step] > 0

    @pl.when(uniform)
    def _():
        rs = slice(0, EXPERT_STEP_BLOCKS * EXPERT_ROWS)
        store_rows(rs, ffn(load_rows(rs), be_ref[step * EXPERT_STEP_BLOCKS]))

    for j in range(EXPERT_STEP_BLOCKS):
        b = step * EXPERT_STEP_BLOCKS + j
        rs = slice(j * EXPERT_ROWS, (j + 1) * EXPERT_ROWS)
        e = be_ref[b]
        nv = nv_ref[b]
        fs = fs_ref[b]

        @pl.when(fs >= 0)
        def _():
            for cp in weight_copies(e, fs):
                cp.wait()
            cur_slot[0] = fs
            nxt = nxt_ref[b]

            @pl.when(nxt >= 0)
            def _():
                for cp in weight_copies(nxt, 1 - fs):
                    cp.start(priority=1)

        @pl.when(jnp.logical_not(uniform) & (nv > 0))
        def _():
            x = load_rows(rs)
            rows = lax.broadcasted_iota(I32, x.shape, 0)
            store_rows(rs, ffn(jnp.where(rows < nv, x, 0.0), e))

        @pl.when(nv == 0)
        def _():
            ys_ref[:, rs, :] = jnp.zeros((ys_ref.shape[0], EXPERT_ROWS, SC_ROW_WORDS), U32)


def _experts(layer, plan, xs, w_gu, b_gu, w_dn, b_dn):
    planes, P, _ = xs.shape
    _, E, D, F2 = w_gu.shape
    F = F2 // 2
    step_rows = EXPERT_STEP_BLOCKS * EXPERT_ROWS
    by_step = lambda i, be, nv, fs, nxt, uni: (0, i, 0)
    layer_all = lambda i, be, nv, fs, nxt, uni: (layer, 0, 0, 0)
    return pl.pallas_call(
        functools.partial(_expert_kernel, layer=layer),
        grid_spec=pltpu.PrefetchScalarGridSpec(
            num_scalar_prefetch=5,
            grid=(P // step_rows,),
            in_specs=[
                pl.BlockSpec((planes, step_rows, SC_ROW_WORDS), by_step),
                pl.BlockSpec(memory_space=pl.ANY),
                pl.BlockSpec((None, E, 1, F2), layer_all),
                pl.BlockSpec(memory_space=pl.ANY),
                pl.BlockSpec((None, E, 1, D), layer_all),
            ],
            out_specs=pl.BlockSpec((planes, step_rows, SC_ROW_WORDS), by_step),
            scratch_shapes=[
                pltpu.VMEM((2, D, F2), F32), pltpu.VMEM((2, F, D), F32),
                pltpu.SMEM((1,), I32),
                pltpu.SemaphoreType.DMA((2, 2)),
            ],
        ),
        out_shape=jax.ShapeDtypeStruct((planes, P, SC_ROW_WORDS), U32),
        compiler_params=pltpu.CompilerParams(
            dimension_semantics=("arbitrary",), vmem_limit_bytes=VMEM_LIMIT),
        name="experts",
    )(plan["block_e"], plan["block_nv"], plan["first_slot"], plan["next_e"], plan["step_uniform"],
      xs, w_gu, b_gu.reshape(-1, E, 1, F2), w_dn, b_dn.reshape(-1, E, 1, D))


def _combine_ple_kernel(h1_ref, yg_ref, gates_ref, gple_ref, wpg_ref, p_ref, wple_ref, gfin_ref, *rest, final):
    o_ref = rest[-1]
    for r in range(h1_ref.shape[0] // COMBINE_ROWS):
        rs = slice(r * COMBINE_ROWS, (r + 1) * COMBINE_ROWS)
        h2 = h1_ref[rs, :]
        gates = gates_ref[rs, :]
        for k in range(TOP_K):
            yk = jnp.concatenate([yg_ref[k, pn, rs, :] for pn in range(yg_ref.shape[1])], axis=1)
            h2 = h2 + gates[:, k:k + 1] * _unpack_rows(yk)
        xn = _rms(h2, gple_ref[...]).astype(BF16)
        gate = jax.nn.sigmoid(jnp.dot(xn, wpg_ref[...], preferred_element_type=F32))
        pe = jnp.dot(p_ref[rs, :].astype(BF16), wple_ref[...], preferred_element_type=F32)
        h3 = h2 + gate * pe
        if final:
            h3 = _rms(h3, gfin_ref[...])
        o_ref[rs, :] = h3


def _combine_ple(layer, group, h1, yg, gates, p, prm, g_final, final, h_prev):
    Tg, D = h1.shape
    T = Tg * MOE_GROUPS
    tm = min(ROW_TILE, Tg)
    PD = p.shape[2]
    planes = yg.shape[1]
    off = group * (Tg // tm)
    const = lambda shape: pl.BlockSpec(shape, lambda i: (0,) * len(shape))
    in_specs = [
        pl.BlockSpec((tm, D), lambda i: (i, 0)),
        pl.BlockSpec((TOP_K, planes, tm, SC_ROW_WORDS), lambda i: (0, 0, i, 0)),
        pl.BlockSpec((tm, 8), lambda i: (i, 0)),
        const((1, D)),
        const((D, D)),
        pl.BlockSpec((None, tm, PD), lambda i: (layer, off + i, 0)),
        const((PD, D)),
        const((1, D)),
    ]
    args = [h1, yg, gates, prm["g_ple"], prm["wpg"], p, prm["wple"], g_final]
    aliases = {}
    if h_prev is not None:
        in_specs.append(pl.BlockSpec(memory_space=pl.ANY))
        args.append(h_prev)
        aliases = {len(args) - 1: 0}
    return pl.pallas_call(
        functools.partial(_combine_ple_kernel, final=final),
        grid=(Tg // tm,),
        in_specs=in_specs,
        out_specs=pl.BlockSpec((tm, D), lambda i: (off + i, 0)),
        out_shape=jax.ShapeDtypeStruct((T, D), F32),
        input_output_aliases=aliases,
        compiler_params=pltpu.CompilerParams(
            dimension_semantics=("parallel",), vmem_limit_bytes=VMEM_LIMIT),
        name="combine_ple",
    )(*args)


def _layer_params(i, w):
    D = w["w_in"].shape[1]
    H = FOX_HEADS
    w_in = w["w_in"][i]
    c1 = 2 * GMLP_WIDTH
    c2, c3, c4 = c1 + FOX_WIDTH, c1 + 2 * FOX_WIDTH, c1 + 3 * FOX_WIDTH
    wk = jnp.zeros((D, H, HEAD_PAD), F32).at[:, :, :FOX_HEAD_DIM].set(
        w_in[:, c2:c3].reshape(D, H, FOX_HEAD_DIM)).reshape(D, H * HEAD_PAD)
    return {
        "g_mix": w["g_mix"][i].reshape(1, D),
        "wuv": w_in[:, :c1].astype(BF16),
        "wqt": (w_in[:, c1:c2] * (FOX_HEAD_DIM ** -0.5 * LOG2E)).T.astype(BF16),
        "wk": wk.astype(BF16),
        "wvt": w_in[:, c3:c4].T.astype(BF16),
        "wft": w_in[:, c4:].T.astype(BF16),
        "b_f": w["b_f"][i].reshape(H, 1),
        "ln_g": w["ln_g"][i].reshape(1, GMLP_WIDTH),
        "ln_b": w["ln_b"][i].reshape(1, GMLP_WIDTH),
        "w_s": w["w_s"][i],
        "b_s": w["b_s"][i].reshape(GMLP_GROUPS, GMLP_BLOCK, 1),
        "g_a": w["g_a"][i].reshape(1, GMLP_WIDTH),
        "woa": w["w_out"][i][:GMLP_WIDTH].astype(BF16),
        "wob": w["w_out"][i][GMLP_WIDTH:].astype(BF16),
        "g_b": w["g_b"][i].reshape(FOX_WIDTH, 1),
        "g_moe": w["g_moe"][i].reshape(1, D),
        "wrt": w["w_router"][i].T.astype(BF16),
        "b_router": w["b_router"][i].reshape(N_EXPERTS, 1),
        "g_ple": w["g_ple"][i].reshape(1, D),
        "wpg": w["w_ple_gate"][i].astype(BF16),
        "wple": w["w_ple"][i].astype(BF16),
    }


def _bias_constants(tm):
    H = FOX_HEADS
    j = jnp.arange(tm)
    tri_incl = (j[:, None] <= j[None, :]).astype(BF16)
    tri_strict = (j[:, None] < j[None, :]).astype(BF16)
    part = jnp.arange(BIAS_PARTS)[:, None]
    head = jnp.arange(H)[None, :]
    rows = (part * H + head).reshape(-1)
    cols = (head * HEAD_PAD + FOX_HEAD_DIM + BIAS_PARTS + part).reshape(-1)
    selk = jnp.zeros((32, H * HEAD_PAD), F32).at[rows, cols].set(-1.0).astype(BF16)
    one_cols = (jnp.arange(H)[:, None] * HEAD_PAD + FOX_HEAD_DIM + jnp.arange(BIAS_PARTS)[None, :]).reshape(-1)
    onesk = jnp.zeros((1, H * HEAD_PAD), F32).at[0, one_cols].set(1.0)
    return tri_incl, tri_strict, selk, onesk


def _moe_dispatch_plan(counts, n_blocks):
    padded = (counts + EXPERT_ROWS - 1) // EXPERT_ROWS * EXPERT_ROWS
    pad_ends = jnp.cumsum(padded)
    pad_starts = (pad_ends - padded).astype(I32)
    block_start = jnp.arange(n_blocks, dtype=I32) * EXPERT_ROWS
    n_before = jnp.sum((pad_ends[None, :] <= block_start[:, None]).astype(I32), axis=1)
    block_e = jnp.minimum(n_before, N_EXPERTS - 1).astype(I32)
    own = block_e[:, None] == jnp.arange(N_EXPERTS, dtype=I32)[None, :]
    seg_end = jnp.sum(jnp.where(own, (pad_starts + counts)[None, :], 0), axis=1)
    block_nv = jnp.clip(seg_end - block_start, 0, EXPERT_ROWS)
    block_nv = jnp.where(block_start < pad_ends[-1], block_nv, 0).astype(I32)
    prev_e = jnp.concatenate([jnp.full((1,), -1, I32), block_e[:-1]])
    first = (block_nv > 0) & (block_e != prev_e)
    run = jnp.cumsum(first.astype(I32)) - 1
    first_slot = jnp.where(first, run % 2, -1).astype(I32)
    ar = jnp.arange(N_EXPERTS, dtype=I32)
    later = (ar[None, :] > ar[:, None]) & (counts > 0)[None, :]
    next_of = jnp.min(jnp.where(later, ar[None, :], N_EXPERTS), axis=1)
    next_of = jnp.where(next_of < N_EXPERTS, next_of, -1)
    next_e = jnp.sum(jnp.where(own, next_of[None, :], 0), axis=1).astype(I32)
    plain = ((block_nv == EXPERT_ROWS) & jnp.logical_not(first)).reshape(-1, EXPERT_STEP_BLOCKS)
    step_uniform = jnp.all(plain, axis=1).astype(I32)
    return dict(pad_starts=pad_starts, block_e=block_e, block_nv=block_nv, first_slot=first_slot, next_e=next_e,
                step_uniform=step_uniform)


def kernel(x, p, g_mix, w_in, ln_g, ln_b, w_s, b_s, b_f, g_a, g_b, w_out, g_moe, w_router, b_router,
           w_gu, b_gu, w_dn, b_dn, g_ple, w_ple_gate, w_ple, g_final):
    B, S, D = x.shape
    T = B * S
    depth = w_in.shape[0]
    w = dict(g_mix=g_mix, w_in=w_in, ln_g=ln_g, ln_b=ln_b, w_s=w_s, b_s=b_s, b_f=b_f, g_a=g_a, g_b=g_b,
             w_out=w_out, g_moe=g_moe, w_router=w_router, b_router=b_router, w_gu=w_gu, b_gu=b_gu,
             w_dn=w_dn, b_dn=b_dn, g_ple=g_ple, w_ple_gate=w_ple_gate, w_ple=w_ple)
    tri_incl, tri_strict, selk, onesk = _bias_constants(min(ROW_TILE, S))
    G = MOE_GROUPS
    Tg = T // G
    n_blocks = -(-(Tg * TOP_K) // EXPERT_ROWS) + N_EXPERTS
    n_blocks = -(-n_blocks // EXPERT_STEP_BLOCKS) * EXPERT_STEP_BLOCKS
    P = n_blocks * EXPERT_ROWS
    planes = (D // 2) // SC_ROW_WORDS
    plane_base = (jnp.arange(planes, dtype=I32) * P)[None, :, None]
    p_rows = p.reshape(depth, T, p.shape[-1])

    h = x.reshape(T, D)
    for i in range(depth):
        prm = _layer_params(i, w)
        prm.update(tri_incl=tri_incl, selk=selk, onesk=onesk)
        routed = []
        idx = jnp.zeros((8, 128), I32)
        for g in range(G):
            ya, qt, kx, vt = _mix_in(g, h, B, S, prm, idx)
            ybt = _fox_attn(qt, kx, vt, B // G, S)
            h1, xp, tope, rank, gates, counts = _out_router(g, h, ya, ybt, B, S, prm, tri_strict)
            plan = _moe_dispatch_plan(counts[:, 0], n_blocks)
            dest = _dest_rows(plan["pad_starts"], tope, rank)[:TOP_K]
            idx = (dest[:, None, :] + plane_base).reshape(-1)
            xs = _sc_scatter_rows(xp.reshape(planes * Tg, SC_ROW_WORDS), idx, planes * P)
            routed.append((h1, gates, plan, idx, xs))
        h_out = None
        for g, (h1, gates, plan, idx, xs) in enumerate(routed):
            ys = _experts(i, plan, xs.reshape(planes, P, SC_ROW_WORDS), w_gu, b_gu, w_dn, b_dn)
            yg = _sc_gather_rows(ys.reshape(planes * P, SC_ROW_WORDS), idx)
            h_out = _combine_ple(i, g, h1, yg.reshape(TOP_K, planes, Tg, SC_ROW_WORDS), gates, p_rows, prm,
                                 g_final.reshape(1, D), i == depth - 1, h_out)
        h = h_out
    return h.reshape(B, S, D)
```

```python
import functools

import jax
import jax.numpy as jnp
from jax import lax
from jax.experimental import pallas as pl
from jax.experimental.pallas import tpu as pltpu
from jax.experimental.pallas import tpu_sc as plsc

F32 = jnp.float32
BF16 = jnp.bfloat16
I32 = jnp.int32
U32 = jnp.uint32

CHUNK = 64
GMLP_WIDTH = 512
GMLP_GROUPS = 4
GMLP_GROUP_CH = 128
GMLP_BLOCK = 128
FOX_WIDTH = 512
FOX_HEAD_DIM = 64
FOX_HEADS = 8
N_EXPERTS = 32
TOP_K = 4
EXPERT_ROWS = 256
SWIGLU_LIMIT = 7.0
SWIGLU_ALPHA = 1.702
RMS_EPS = 1e-5
LN_EPS = 1e-5

HEAD_PAD = 128
BIAS_PARTS = 3
ROW_TILE = 512
ATTN_TILE = 512
ATTN_HEADS = 1
ACC_ROWS = 80
LOG2E = 1.4426950408889634
SC_WINDOW = 128
SC_ROW_WORDS = 256
COMBINE_ROWS = 256
EXPERT_STEP_BLOCKS = 8
EXPERT_CHAIN_BLOCKS = 4
MOE_GROUPS = 2
VMEM_LIMIT = 56 * 1024 * 1024
NEG_BIG = -1e30

_NT = (((1,), (1,)), ((), ()))
_TN = (((0,), (0,)), ((), ()))


def _rms(x, g):
    return x * lax.rsqrt(jnp.mean(x * x, axis=-1, keepdims=True) + RMS_EPS) * g


def _split3(c):
    a1 = c.astype(BF16).astype(F32)
    r1 = c - a1
    a2 = r1.astype(BF16).astype(F32)
    a3 = (r1 - a2).astype(BF16).astype(F32)
    return a1, a2, a3


def _pack_rows(x):
    n = x.shape[1] // 2
    lo = lax.bitcast_convert_type(x[:, :n].astype(BF16).astype(F32), U32) >> 16
    hi = lax.bitcast_convert_type(x[:, n:].astype(BF16).astype(F32), U32) & jnp.uint32(0xFFFF0000)
    return lo | hi


def _unpack_rows(u):
    lo = lax.bitcast_convert_type(u << 16, F32)
    hi = lax.bitcast_convert_type(u & jnp.uint32(0xFFFF0000), F32)
    return jnp.concatenate([lo, hi], axis=1)


def _store_planes(ref, u):
    for j in range(ref.shape[0]):
        ref[j] = u[:, j * SC_ROW_WORDS:(j + 1) * SC_ROW_WORDS]


def _load_planes(ref):
    return jnp.concatenate([ref[j] for j in range(ref.shape[0])], axis=1)


def _mix_in_kernel(h_ref, gmix_ref, wuv_ref, wqt_ref, wk_ref, wvt_ref, wft_ref, bf_ref, lng_ref, lnb_ref,
                   ws_ref, bs_ref, ga_ref, tri_ref, selk_ref, onesk_ref, after_ref,
                   ya_ref, qt_ref, k_ref, vt_ref, carry_ref, ya_scr):
    del after_ref
    tm = h_ref.shape[0]

    @pl.when(pl.program_id(1) == 0)
    def _():
        carry_ref[...] = jnp.zeros_like(carry_ref)

    xn = _rms(h_ref[...], gmix_ref[...]).astype(BF16)

    z = jnp.dot(xn, wuv_ref[...], preferred_element_type=F32)
    qt = lax.dot_general(wqt_ref[...], xn, _NT, preferred_element_type=F32)
    vt_ref[0] = lax.dot_general(wvt_ref[...], xn, _NT, preferred_element_type=F32).astype(BF16)
    kp = jnp.dot(xn, wk_ref[...], preferred_element_type=F32)
    ft = lax.dot_general(wft_ref[...], xn, _NT, preferred_element_type=F32) + bf_ref[...]

    z = 0.5 * z * (1.0 + lax.erf(z * (0.5 ** 0.5)))
    u = z[:, :GMLP_WIDTH]
    v = z[:, GMLP_WIDTH:]
    mu = jnp.mean(v, axis=-1, keepdims=True)
    vc = v - mu
    var = jnp.mean(vc * vc, axis=-1, keepdims=True)
    vn = (vc * lax.rsqrt(var + LN_EPS) * lng_ref[...] + lnb_ref[...]).astype(BF16)
    pi = lax.broadcasted_iota(I32, (GMLP_BLOCK, GMLP_BLOCK), 0) // CHUNK
    pj = lax.broadcasted_iota(I32, (GMLP_BLOCK, GMLP_BLOCK), 1) // CHUNK
    for g in range(GMLP_GROUPS):
        wm = jnp.where(pj <= pi, ws_ref[g], 0.0).astype(BF16)
        cs = slice(g * GMLP_GROUP_CH, (g + 1) * GMLP_GROUP_CH)
        for n in range(tm // GMLP_BLOCK):
            rs = slice(n * GMLP_BLOCK, (n + 1) * GMLP_BLOCK)
            sv = jnp.dot(wm, vn[rs, cs], preferred_element_type=F32) + bs_ref[g]
            ya_scr[rs, cs] = u[rs, cs] * sv
    ya_ref[...] = _rms(ya_scr[...], ga_ref[...]).astype(BF16)

    lf = (jnp.minimum(ft, 0.0) - jnp.log1p(jnp.exp(-jnp.abs(ft)))) * LOG2E
    zpad = jnp.zeros((32 - BIAS_PARTS * FOX_HEADS, tm), F32)
    stack = jnp.concatenate(list(_split3(lf)) + [zpad], axis=0).astype(BF16)
    cum = jnp.dot(stack, tri_ref[...], preferred_element_type=F32)
    c = cum[0:8] + cum[8:16] + cum[16:24] + carry_ref[:, 0:1]
    carry_ref[...] = jnp.broadcast_to(c[:, tm - 1:tm], carry_ref.shape)
    a1, a2, a3 = _split3(c)

    si = lax.broadcasted_iota(I32, (8, tm), 0)
    zrows = jnp.zeros((HEAD_PAD - FOX_HEAD_DIM - 8, tm), F32)
    for hd in range(FOX_HEADS):
        b1 = jnp.broadcast_to(a1[hd:hd + 1], (8, tm))
        b2 = jnp.broadcast_to(a2[hd:hd + 1], (8, tm))
        b3 = jnp.broadcast_to(a3[hd:hd + 1], (8, tm))
        ext = jnp.where(si == 0, b1, jnp.where(si == 1, b2, jnp.where(si == 2, b3,
                        jnp.where(si < 2 * BIAS_PARTS, 1.0, 0.0))))
        blk = jnp.concatenate([qt[hd * FOX_HEAD_DIM:(hd + 1) * FOX_HEAD_DIM], ext, zrows], axis=0)
        qt_ref[0, hd * HEAD_PAD:(hd + 1) * HEAD_PAD, :] = blk.astype(BF16)

    cstack = jnp.concatenate([a1, a2, a3, zpad], axis=0).astype(BF16)
    kext = lax.dot_general(cstack, selk_ref[...], _TN, preferred_element_type=F32)
    k_ref[...] = (kp + kext + onesk_ref[...]).astype(BF16)


def _mix_in(group, h, B, S, prm, after):
    T, D = h.shape
    Bg = B // MOE_GROUPS
    Tg = Bg * S
    tm = min(ROW_TILE, S)
    ns = S // tm
    H = FOX_HEADS
    const = lambda shape: pl.BlockSpec(shape, lambda b, s: (0,) * len(shape))
    return pl.pallas_call(
        _mix_in_kernel,
        grid=(Bg, ns),
        in_specs=[
            pl.BlockSpec((tm, D), lambda b, s: ((group * Bg + b) * ns + s, 0)),
            const((1, D)),
            const((D, 2 * GMLP_WIDTH)),
            const((FOX_WIDTH, D)),
            const((D, H * HEAD_PAD)),
            const((FOX_WIDTH, D)),
            const((H, D)),
            const((H, 1)),
            const((1, GMLP_WIDTH)),
            const((1, GMLP_WIDTH)),
            const((GMLP_GROUPS, GMLP_BLOCK, GMLP_BLOCK)),
            const((GMLP_GROUPS, GMLP_BLOCK, 1)),
            const((1, GMLP_WIDTH)),
            const((tm, tm)),
            const((32, H * HEAD_PAD)),
            const((1, H * HEAD_PAD)),
            pl.BlockSpec(memory_space=pl.ANY),
        ],
        out_specs=[
            pl.BlockSpec((tm, GMLP_WIDTH), lambda b, s: (b * ns + s, 0)),
            pl.BlockSpec((1, H * HEAD_PAD, tm), lambda b, s: (b, 0, s)),
            pl.BlockSpec((tm, H * HEAD_PAD), lambda b, s: (b * ns + s, 0)),
            pl.BlockSpec((1, FOX_WIDTH, tm), lambda b, s: (b, 0, s)),
        ],
        out_shape=[
            jax.ShapeDtypeStruct((Tg, GMLP_WIDTH), BF16),
            jax.ShapeDtypeStruct((Bg, H * HEAD_PAD, S), BF16),
            jax.ShapeDtypeStruct((Tg, H * HEAD_PAD), BF16),
            jax.ShapeDtypeStruct((Bg, FOX_WIDTH, S), BF16),
        ],
        scratch_shapes=[pltpu.VMEM((H, 128), F32), pltpu.VMEM((tm, GMLP_WIDTH), F32)],
        compiler_params=pltpu.CompilerParams(
            dimension_semantics=("arbitrary", "arbitrary"), vmem_limit_bytes=VMEM_LIMIT),
        name="mix_in",
    )(h, prm["g_mix"], prm["wuv"], prm["wqt"], prm["wk"], prm["wvt"], prm["wft"], prm["b_f"],
      prm["ln_g"], prm["ln_b"], prm["w_s"], prm["b_s"], prm["g_a"], prm["tri_incl"], prm["selk"], prm["onesk"],
      after)


def _fox_attn_kernel(qt_ref, k_ref, vt_ref, ot_ref):
    S = k_ref.shape[0]
    t = min(ATTN_TILE, S)
    nq = S // t
    hf = t // 2
    ones_rows = jnp.ones((ACC_ROWS - FOX_HEAD_DIM, t), BF16)
    pairs = [(qi, kj) for qi in range(nq) for kj in range(qi + 1)]

    def tri(rows, cols):
        return lax.broadcasted_iota(I32, (rows, cols), 0) <= lax.broadcasted_iota(I32, (rows, cols), 1)

    def scores(qi, kj):
        out = []
        for hh in range(ATTN_HEADS):
            hs = slice(hh * HEAD_PAD, (hh + 1) * HEAD_PAD)
            if kj < qi:
                out.append([jnp.dot(k_ref[kj * t:(kj + 1) * t, hs], qt_ref[0, hs, qi * t:(qi + 1) * t],
                                    preferred_element_type=F32)])
            else:
                top = jnp.dot(k_ref[kj * t:kj * t + hf, hs], qt_ref[0, hs, qi * t:(qi + 1) * t],
                              preferred_element_type=F32)
                bot = jnp.dot(k_ref[kj * t + hf:(kj + 1) * t, hs], qt_ref[0, hs, qi * t + hf:(qi + 1) * t],
                              preferred_element_type=F32)
                out.append([jnp.where(tri(hf, t), top, NEG_BIG), jnp.where(tri(hf, hf), bot, NEG_BIG)])
        return out

    s_next = scores(*pairs[0])
    state = None
    for n, (qi, kj) in enumerate(pairs):
        s_cur = s_next
        if n + 1 < len(pairs):
            s_next = scores(*pairs[n + 1])
        new_state = []
        for hh in range(ATTN_HEADS):
            parts = s_cur[hh]
            v = vt_ref[0, hh * FOX_HEAD_DIM:(hh + 1) * FOX_HEAD_DIM, kj * t:(kj + 1) * t]
            v_aug = jnp.concatenate([v, ones_rows], axis=0)
            mx = jnp.max(parts[0], axis=0, keepdims=True)
            if len(parts) == 2:
                mx = jnp.concatenate(
                    [mx[:, :hf], jnp.maximum(mx[:, hf:], jnp.max(parts[1], axis=0, keepdims=True))], axis=1)
            m_new = mx if kj == 0 else jnp.maximum(state[hh][0], mx)
            if len(parts) == 1:
                pv = jnp.dot(v_aug, jnp.exp2(parts[0] - m_new).astype(BF16), preferred_element_type=F32)
            else:
                pv = jnp.dot(v_aug[:, :hf], jnp.exp2(parts[0] - m_new).astype(BF16), preferred_element_type=F32)
                pv_r = jnp.dot(v_aug[:, hf:], jnp.exp2(parts[1] - m_new[:, hf:]).astype(BF16),
                               preferred_element_type=F32)
                pv = jnp.concatenate([pv[:, :hf], pv[:, hf:] + pv_r], axis=1)
            acc = pv if kj == 0 else jnp.exp2(state[hh][0] - m_new) * state[hh][1] + pv
            new_state.append((m_new, acc))
        state = new_state
        if kj == qi:
            for hh in range(ATTN_HEADS):
                acc = state[hh][1]
                o = acc[:FOX_HEAD_DIM] / acc[FOX_HEAD_DIM:FOX_HEAD_DIM + 1]
                ot_ref[0, hh * FOX_HEAD_DIM:(hh + 1) * FOX_HEAD_DIM, qi * t:(qi + 1) * t] = o.astype(BF16)


def _fox_attn(qt, kx, vt, B, S):
    G = ATTN_HEADS
    return pl.pallas_call(
        _fox_attn_kernel,
        grid=(B, FOX_HEADS // G),
        in_specs=[
            pl.BlockSpec((1, G * HEAD_PAD, S), lambda b, h: (b, h, 0)),
            pl.BlockSpec((S, G * HEAD_PAD), lambda b, h: (b, h)),
            pl.BlockSpec((1, G * FOX_HEAD_DIM, S), lambda b, h: (b, h, 0)),
        ],
        out_specs=pl.BlockSpec((1, G * FOX_HEAD_DIM, S), lambda b, h: (b, h, 0)),
        out_shape=jax.ShapeDtypeStruct((B, FOX_WIDTH, S), BF16),
        compiler_params=pltpu.CompilerParams(
            dimension_semantics=("parallel", "parallel"), vmem_limit_bytes=VMEM_LIMIT),
        name="fox_attn",
    )(qt, kx, vt)


def _out_router_kernel(h_ref, ya_ref, ybt_ref, woa_ref, wob_ref, gb_ref, gmoe_ref, wrt_ref, br_ref, tri_ref,
                       h1_ref, xp_ref, tope_ref, rank_ref, gates_ref, counts_ref, carry_ref):
    tm = h_ref.shape[0]

    @pl.when(pl.program_id(0) == 0)
    def _():
        carry_ref[...] = jnp.zeros_like(carry_ref)

    yb = ybt_ref[0].astype(F32)
    ybn = (yb * lax.rsqrt(jnp.mean(yb * yb, axis=0, keepdims=True) + RMS_EPS) * gb_ref[...]).astype(BF16)
    y = jnp.dot(ya_ref[...], woa_ref[...], preferred_element_type=F32)
    y = y + lax.dot_general(ybn, wob_ref[...], _TN, preferred_element_type=F32)
    h1 = h_ref[...] + y
    h1_ref[...] = h1
    xn = _rms(h1, gmoe_ref[...])
    _store_planes(xp_ref, _pack_rows(xn))

    logits = lax.dot_general(wrt_ref[...], xn.astype(BF16), _NT, preferred_element_type=F32) + br_ref[...]
    ri = lax.broadcasted_iota(I32, (N_EXPERTS, tm), 0)
    vals, idxs = [], []
    l = logits
    for _ in range(TOP_K):
        m = jnp.max(l, axis=0, keepdims=True)
        idx = jnp.min(jnp.where(l == m, ri, N_EXPERTS), axis=0, keepdims=True)
        vals.append(m)
        idxs.append(idx)
        l = jnp.where(ri == idx, -jnp.inf, l)
    es = [jnp.exp(vk - vals[0]) for vk in vals]
    den = es[0] + es[1] + es[2] + es[3]
    zrow = jnp.zeros((8 - TOP_K, tm), F32)
    gates_ref[...] = jnp.concatenate([e / den for e in es] + [zrow], axis=0).T

    sel = jnp.zeros((N_EXPERTS, tm), F32)
    for idx in idxs:
        sel = sel + jnp.where(ri == idx, 1.0, 0.0)
    cnt = jnp.dot(sel.astype(BF16), tri_ref[...], preferred_element_type=F32) + carry_ref[:, 0:1]
    ranks = [jnp.sum(jnp.where(ri == idx, cnt, 0.0), axis=0, keepdims=True) for idx in idxs]
    zi = jnp.zeros((8 - TOP_K, tm), I32)
    rank_ref[...] = jnp.concatenate([r.astype(I32) for r in ranks] + [zi], axis=0)
    tope_ref[...] = jnp.concatenate(idxs + [zi], axis=0)
    total = carry_ref[...] + jnp.sum(sel, axis=1, keepdims=True)
    carry_ref[...] = total
    counts_ref[...] = total.astype(I32)


def _out_router(group, h, ya, ybt, B, S, prm, tri_strict):
    T, D = h.shape
    Tg = T // MOE_GROUPS
    tm = min(ROW_TILE, S)
    ns = S // tm
    E = N_EXPERTS
    planes = (D // 2) // SC_ROW_WORDS
    off = group * (Tg // tm)
    const = lambda shape: pl.BlockSpec(shape, lambda i: (0,) * len(shape))
    return pl.pallas_call(
        _out_router_kernel,
        grid=(Tg // tm,),
        in_specs=[
            pl.BlockSpec((tm, D), lambda i: (off + i, 0)),
            pl.BlockSpec((tm, GMLP_WIDTH), lambda i: (i, 0)),
            pl.BlockSpec((1, FOX_WIDTH, tm), lambda i: (i // ns, 0, i % ns)),
            const((GMLP_WIDTH, D)),
            const((FOX_WIDTH, D)),
            const((FOX_WIDTH, 1)),
            const((1, D)),
            const((E, D)),
            const((E, 1)),
            const((tm, tm)),
        ],
        out_specs=[
            pl.BlockSpec((tm, D), lambda i: (i, 0)),
            pl.BlockSpec((planes, tm, SC_ROW_WORDS), lambda i: (0, i, 0)),
            pl.BlockSpec((8, tm), lambda i: (0, i)),
            pl.BlockSpec((8, tm), lambda i: (0, i)),
            pl.BlockSpec((tm, 8), lambda i: (i, 0)),
            const((E, 128)),
        ],
        out_shape=[
            jax.ShapeDtypeStruct((Tg, D), F32),
            jax.ShapeDtypeStruct((planes, Tg, SC_ROW_WORDS), U32),
            jax.ShapeDtypeStruct((8, Tg), I32),
            jax.ShapeDtypeStruct((8, Tg), I32),
            jax.ShapeDtypeStruct((Tg, 8), F32),
            jax.ShapeDtypeStruct((E, 128), I32),
        ],
        scratch_shapes=[pltpu.VMEM((E, 128), F32)],
        compiler_params=pltpu.CompilerParams(
            dimension_semantics=("arbitrary",), vmem_limit_bytes=VMEM_LIMIT),
        name="out_router",
    )(h, ya, ybt, prm["woa"], prm["wob"], prm["g_b"], prm["g_moe"], prm["wrt"], prm["b_router"], tri_strict)


def _dest_kernel(ps_ref, tope_ref, rank_ref, dest_ref):
    e_sel = tope_ref[...]
    d = rank_ref[...]
    for e in range(N_EXPERTS):
        d = d + jnp.where(e_sel == e, ps_ref[e], 0)
    dest_ref[...] = d


def _dest_rows(pad_starts, tope, rank):
    T = tope.shape[1]
    tc = min(4096, T)
    return pl.pallas_call(
        _dest_kernel,
        grid_spec=pltpu.PrefetchScalarGridSpec(
            num_scalar_prefetch=1,
            grid=(T // tc,),
            in_specs=[pl.BlockSpec((8, tc), lambda i, ps: (0, i)),
                      pl.BlockSpec((8, tc), lambda i, ps: (0, i))],
            out_specs=pl.BlockSpec((8, tc), lambda i, ps: (0, i)),
        ),
        out_shape=jax.ShapeDtypeStruct((8, T), I32),
        name="dest_rows",
    )(pad_starts, tope, rank)


def _sc_mesh():
    return plsc.VectorSubcoreMesh(core_axis_name="c", subcore_axis_name="s")


def _sc_scatter_rows(x, idx, n_out):
    R, W = x.shape
    N = idx.shape[0]
    nsrc = R // SC_WINDOW
    copies = N // R

    @functools.partial(pl.kernel, out_type=jax.ShapeDtypeStruct((n_out, W), x.dtype), mesh=_sc_mesh())
    def k(x_hbm, i_hbm, o_hbm):
        def body(x_vmem, *i_vmems):
            for i_vmem in i_vmems:
                pltpu.sync_copy(x_vmem, o_hbm.at[i_vmem.at[0]])

        idx_spec = lambda c: pl.BlockSpec((1, SC_WINDOW), lambda i: (0, c * nsrc + i))
        pltpu.emit_pipeline(
            body,
            grid=(nsrc,),
            in_specs=[pl.BlockSpec((SC_WINDOW, W), lambda i: (i, 0))] + [idx_spec(c) for c in range(copies)],
            out_specs=[],
            core_axis_name=("c", "s"),
            dimension_semantics=(pltpu.PARALLEL,),
        )(x_hbm, *([i_hbm] * copies))

    return k(x, idx.reshape(1, N))


def _sc_gather_rows(table, idx):
    V, W = table.shape
    N = idx.shape[0]

    @functools.partial(pl.kernel, out_type=jax.ShapeDtypeStruct((N, W), table.dtype), mesh=_sc_mesh())
    def k(t_hbm, i_hbm, o_hbm):
        def body(i_vmem, o_vmem):
            pltpu.sync_copy(t_hbm.at[i_vmem.at[0]], o_vmem)

        pltpu.emit_pipeline(
            body,
            grid=(N // SC_WINDOW,),
            in_specs=[pl.BlockSpec((1, SC_WINDOW), lambda i: (0, i))],
            out_specs=[pl.BlockSpec((SC_WINDOW, W), lambda i: (i, 0))],
            core_axis_name=("c", "s"),
            dimension_semantics=(pltpu.PARALLEL,),
        )(i_hbm, o_hbm)

    return k(table, idx.reshape(1, N))


def _expert_kernel(be_ref, nv_ref, fs_ref, nxt_ref, uni_ref, xs_ref, wgu_hbm, bgu_ref, wdn_hbm, bdn_ref, ys_ref,
                   wgu_f32, wdn_f32, cur_slot, sem, *, layer):
    F = wdn_f32.shape[1]

    def weight_copies(e, slot):
        return (pltpu.make_async_copy(wgu_hbm.at[layer, e], wgu_f32.at[slot], sem.at[slot, 0]),
                pltpu.make_async_copy(wdn_hbm.at[layer, e], wdn_f32.at[slot], sem.at[slot, 1]))

    @pl.when(pl.program_id(0) == 0)
    def _():
        for cp in weight_copies(be_ref[0], 0):
            cp.start()

    def ffn(x, e):
        slot = cur_slot[0]
        hgu = jnp.dot(x, wgu_f32[slot], preferred_element_type=F32) + bgu_ref[e]
        g = jnp.minimum(hgu[:, :F], SWIGLU_LIMIT)
        lin = jnp.clip(hgu[:, F:], -SWIGLU_LIMIT, SWIGLU_LIMIT)
        act = g * jax.nn.sigmoid(SWIGLU_ALPHA * g) * (lin + 1.0)
        return _pack_rows(jnp.dot(act, wdn_f32[slot], preferred_element_type=F32) + bdn_ref[e])

    def load_rows(rs):
        return _unpack_rows(jnp.concatenate([xs_ref[pn, rs, :] for pn in range(xs_ref.shape[0])], axis=1))

    def store_rows(rs, y):
        for pn in range(ys_ref.shape[0]):
            ys_ref[pn, rs, :] = y[:, pn * SC_ROW_WORDS:(pn + 1) * SC_ROW_WORDS]

    for c in range(EXPERT_STEP_BLOCKS // EXPERT_CHAIN_BLOCKS):
        chain = pl.program_id(0) * (EXPERT_STEP_BLOCKS // EXPERT_CHAIN_BLOCKS) + c
        uniform = uni_ref[chain] > 0

        @pl.when(uniform)
        def _():
            rs = slice(c * EXPERT_CHAIN_BLOCKS * EXPERT_ROWS, (c + 1) * EXPERT_CHAIN_BLOCKS * EXPERT_ROWS)
            store_rows(rs, ffn(load_rows(rs), be_ref[chain * EXPERT_CHAIN_BLOCKS]))

        for j in range(EXPERT_CHAIN_BLOCKS):
            b = chain * EXPERT_CHAIN_BLOCKS + j
            r0 = (c * EXPERT_CHAIN_BLOCKS + j) * EXPERT_ROWS
            rs = slice(r0, r0 + EXPERT_ROWS)
            e = be_ref[b]
            nv = nv_ref[b]
            fs = fs_ref[b]

            @pl.when(fs >= 0)
            def _():
                for cp in weight_copies(e, fs):
                    cp.wait()
                cur_slot[0] = fs
                nxt = nxt_ref[b]

                @pl.when(nxt >= 0)
                def _():
                    for cp in weight_copies(nxt, 1 - fs):
                        cp.start(priority=1)

            @pl.when(jnp.logical_not(uniform) & (nv > 0))
            def _():
                x = load_rows(rs)
                rows = lax.broadcasted_iota(I32, x.shape, 0)
                store_rows(rs, ffn(jnp.where(rows < nv, x, 0.0), e))

            @pl.when(nv == 0)
            def _():
                ys_ref[:, rs, :] = jnp.zeros((ys_ref.shape[0], EXPERT_ROWS, SC_ROW_WORDS), U32)


def _experts(layer, plan, xs, w_gu, b_gu, w_dn, b_dn):
    planes, P, _ = xs.shape
    _, E, D, F2 = w_gu.shape
    F = F2 // 2
    step_rows = EXPERT_STEP_BLOCKS * EXPERT_ROWS
    by_step = lambda i, be, nv, fs, nxt, uni: (0, i, 0)
    layer_all = lambda i, be, nv, fs, nxt, uni: (layer, 0, 0, 0)
    return pl.pallas_call(
        functools.partial(_expert_kernel, layer=layer),
        grid_spec=pltpu.PrefetchScalarGridSpec(
            num_scalar_prefetch=5,
            grid=(P // step_rows,),
            in_specs=[
                pl.BlockSpec((planes, step_rows, SC_ROW_WORDS), by_step),
                pl.BlockSpec(memory_space=pl.ANY),
                pl.BlockSpec((None, E, 1, F2), layer_all),
                pl.BlockSpec(memory_space=pl.ANY),
                pl.BlockSpec((None, E, 1, D), layer_all),
            ],
            out_specs=pl.BlockSpec((planes, step_rows, SC_ROW_WORDS), by_step),
            scratch_shapes=[
                pltpu.VMEM((2, D, F2), F32), pltpu.VMEM((2, F, D), F32),
                pltpu.SMEM((1,), I32),
                pltpu.SemaphoreType.DMA((2, 2)),
            ],
        ),
        out_shape=jax.ShapeDtypeStruct((planes, P, SC_ROW_WORDS), U32),
        compiler_params=pltpu.CompilerParams(
            dimension_semantics=("arbitrary",), vmem_limit_bytes=VMEM_LIMIT),
        name="experts",
    )(plan["block_e"], plan["block_nv"], plan["first_slot"], plan["next_e"], plan["step_uniform"],
      xs, w_gu, b_gu.reshape(-1, E, 1, F2), w_dn, b_dn.reshape(-1, E, 1, D))


def _combine_ple_kernel(h1_ref, yg_ref, gates_ref, gple_ref, wpg_ref, p_ref, wple_ref, gfin_ref, *rest, final):
    o_ref = rest[-1]
    for r in range(h1_ref.shape[0] // COMBINE_ROWS):
        rs = slice(r * COMBINE_ROWS, (r + 1) * COMBINE_ROWS)
        h2 = h1_ref[rs, :]
        gates = gates_ref[rs, :]
        for k in range(TOP_K):
            yk = jnp.concatenate([yg_ref[k, pn, rs, :] for pn in range(yg_ref.shape[1])], axis=1)
            h2 = h2 + gates[:, k:k + 1] * _unpack_rows(yk)
        xn = _rms(h2, gple_ref[...]).astype(BF16)
        gate = jax.nn.sigmoid(jnp.dot(xn, wpg_ref[...], preferred_element_type=F32))
        pe = jnp.dot(p_ref[rs, :].astype(BF16), wple_ref[...], preferred_element_type=F32)
        h3 = h2 + gate * pe
        if final:
            h3 = _rms(h3, gfin_ref[...])
        o_ref[rs, :] = h3


def _combine_ple(layer, group, h1, yg, gates, p, prm, g_final, final, h_prev):
    Tg, D = h1.shape
    T = Tg * MOE_GROUPS
    tm = min(ROW_TILE, Tg)
    PD = p.shape[2]
    planes = yg.shape[1]
    off = group * (Tg // tm)
    const = lambda shape: pl.BlockSpec(shape, lambda i: (0,) * len(shape))
    in_specs = [
        pl.BlockSpec((tm, D), lambda i: (i, 0)),
        pl.BlockSpec((TOP_K, planes, tm, SC_ROW_WORDS), lambda i: (0, 0, i, 0)),
        pl.BlockSpec((tm, 8), lambda i: (i, 0)),
        const((1, D)),
        const((D, D)),
        pl.BlockSpec((None, tm, PD), lambda i: (layer, off + i, 0)),
        const((PD, D)),
        const((1, D)),
    ]
    args = [h1, yg, gates, prm["g_ple"], prm["wpg"], p, prm["wple"], g_final]
    aliases = {}
    if h_prev is not None:
        in_specs.append(pl.BlockSpec(memory_space=pl.ANY))
        args.append(h_prev)
        aliases = {len(args) - 1: 0}
    return pl.pallas_call(
        functools.partial(_combine_ple_kernel, final=final),
        grid=(Tg // tm,),
        in_specs=in_specs,
        out_specs=pl.BlockSpec((tm, D), lambda i: (off + i, 0)),
        out_shape=jax.ShapeDtypeStruct((T, D), F32),
        input_output_aliases=aliases,
        compiler_params=pltpu.CompilerParams(
            dimension_semantics=("parallel",), vmem_limit_bytes=VMEM_LIMIT),
        name="combine_ple",
    )(*args)


def _layer_params(i, w):
    D = w["w_in"].shape[1]
    H = FOX_HEADS
    w_in = w["w_in"][i]
    c1 = 2 * GMLP_WIDTH
    c2, c3, c4 = c1 + FOX_WIDTH, c1 + 2 * FOX_WIDTH, c1 + 3 * FOX_WIDTH
    wk = jnp.zeros((D, H, HEAD_PAD), F32).at[:, :, :FOX_HEAD_DIM].set(
        w_in[:, c2:c3].reshape(D, H, FOX_HEAD_DIM)).reshape(D, H * HEAD_PAD)
    return {
        "g_mix": w["g_mix"][i].reshape(1, D),
        "wuv": w_in[:, :c1].astype(BF16),
        "wqt": (w_in[:, c1:c2] * (FOX_HEAD_DIM ** -0.5 * LOG2E)).T.astype(BF16),
        "wk": wk.astype(BF16),
        "wvt": w_in[:, c3:c4].T.astype(BF16),
        "wft": w_in[:, c4:].T.astype(BF16),
        "b_f": w["b_f"][i].reshape(H, 1),
        "ln_g": w["ln_g"][i].reshape(1, GMLP_WIDTH),
        "ln_b": w["ln_b"][i].reshape(1, GMLP_WIDTH),
        "w_s": w["w_s"][i],
        "b_s": w["b_s"][i].reshape(GMLP_GROUPS, GMLP_BLOCK, 1),
        "g_a": w["g_a"][i].reshape(1, GMLP_WIDTH),
        "woa": w["w_out"][i][:GMLP_WIDTH].astype(BF16),
        "wob": w["w_out"][i][GMLP_WIDTH:].astype(BF16),
        "g_b": w["g_b"][i].reshape(FOX_WIDTH, 1),
        "g_moe": w["g_moe"][i].reshape(1, D),
        "wrt": w["w_router"][i].T.astype(BF16),
        "b_router": w["b_router"][i].reshape(N_EXPERTS, 1),
        "g_ple": w["g_ple"][i].reshape(1, D),
        "wpg": w["w_ple_gate"][i].astype(BF16),
        "wple": w["w_ple"][i].astype(BF16),
    }


def _bias_constants(tm):
    H = FOX_HEADS
    j = jnp.arange(tm)
    tri_incl = (j[:, None] <= j[None, :]).astype(BF16)
    tri_strict = (j[:, None] < j[None, :]).astype(BF16)
    part = jnp.arange(BIAS_PARTS)[:, None]
    head = jnp.arange(H)[None, :]
    rows = (part * H + head).reshape(-1)
    cols = (head * HEAD_PAD + FOX_HEAD_DIM + BIAS_PARTS + part).reshape(-1)
    selk = jnp.zeros((32, H * HEAD_PAD), F32).at[rows, cols].set(-1.0).astype(BF16)
    one_cols = (jnp.arange(H)[:, None] * HEAD_PAD + FOX_HEAD_DIM + jnp.arange(BIAS_PARTS)[None, :]).reshape(-1)
    onesk = jnp.zeros((1, H * HEAD_PAD), F32).at[0, one_cols].set(1.0)
    return tri_incl, tri_strict, selk, onesk


def _moe_dispatch_plan(counts, n_blocks):
    padded = (counts + EXPERT_ROWS - 1) // EXPERT_ROWS * EXPERT_ROWS
    pad_ends = jnp.cumsum(padded)
    pad_starts = (pad_ends - padded).astype(I32)
    block_start = jnp.arange(n_blocks, dtype=I32) * EXPERT_ROWS
    n_before = jnp.sum((pad_ends[None, :] <= block_start[:, None]).astype(I32), axis=1)
    block_e = jnp.minimum(n_before, N_EXPERTS - 1).astype(I32)
    own = block_e[:, None] == jnp.arange(N_EXPERTS, dtype=I32)[None, :]
    seg_end = jnp.sum(jnp.where(own, (pad_starts + counts)[None, :], 0), axis=1)
    block_nv = jnp.clip(seg_end - block_start, 0, EXPERT_ROWS)
    block_nv = jnp.where(block_start < pad_ends[-1], block_nv, 0).astype(I32)
    prev_e = jnp.concatenate([jnp.full((1,), -1, I32), block_e[:-1]])
    first = (block_nv > 0) & (block_e != prev_e)
    run = jnp.cumsum(first.astype(I32)) - 1
    first_slot = jnp.where(first, run % 2, -1).astype(I32)
    ar = jnp.arange(N_EXPERTS, dtype=I32)
    later = (ar[None, :] > ar[:, None]) & (counts > 0)[None, :]
    next_of = jnp.min(jnp.where(later, ar[None, :], N_EXPERTS), axis=1)
    next_of = jnp.where(next_of < N_EXPERTS, next_of, -1)
    next_e = jnp.sum(jnp.where(own, next_of[None, :], 0), axis=1).astype(I32)
    plain = ((block_nv == EXPERT_ROWS) & jnp.logical_not(first)).reshape(-1, EXPERT_CHAIN_BLOCKS)
    step_uniform = jnp.all(plain, axis=1).astype(I32)
    return dict(pad_starts=pad_starts, block_e=block_e, block_nv=block_nv, first_slot=first_slot, next_e=next_e,
                step_uniform=step_uniform)


def kernel(x, p, g_mix, w_in, ln_g, ln_b, w_s, b_s, b_f, g_a, g_b, w_out, g_moe, w_router, b_router,
           w_gu, b_gu, w_dn, b_dn, g_ple, w_ple_gate, w_ple, g_final):
    B, S, D = x.shape
    T = B * S
    depth = w_in.shape[0]
    w = dict(g_mix=g_mix, w_in=w_in, ln_g=ln_g, ln_b=ln_b, w_s=w_s, b_s=b_s, b_f=b_f, g_a=g_a, g_b=g_b,
             w_out=w_out, g_moe=g_moe, w_router=w_router, b_router=b_router, w_gu=w_gu, b_gu=b_gu,
             w_dn=w_dn, b_dn=b_dn, g_ple=g_ple, w_ple_gate=w_ple_gate, w_ple=w_ple)
    tri_incl, tri_strict, selk, onesk = _bias_constants(min(ROW_TILE, S))
    G = MOE_GROUPS
    Tg = T // G
    n_blocks = -(-(Tg * TOP_K) // EXPERT_ROWS) + N_EXPERTS
    n_blocks = -(-n_blocks // EXPERT_STEP_BLOCKS) * EXPERT_STEP_BLOCKS
    P = n_blocks * EXPERT_ROWS
    planes = (D // 2) // SC_ROW_WORDS
    plane_base = (jnp.arange(planes, dtype=I32) * P)[None, :, None]
    p_rows = p.reshape(depth, T, p.shape[-1])

    h = x.reshape(T, D)
    for i in range(depth):
        prm = _layer_params(i, w)
        prm.update(tri_incl=tri_incl, selk=selk, onesk=onesk)
        routed = []
        idx = jnp.zeros((8, 128), I32)
        for g in range(G):
            ya, qt, kx, vt = _mix_in(g, h, B, S, prm, idx)
            ybt = _fox_attn(qt, kx, vt, B // G, S)
            h1, xp, tope, rank, gates, counts = _out_router(g, h, ya, ybt, B, S, prm, tri_strict)
            plan = _moe_dispatch_plan(counts[:, 0], n_blocks)
            dest = _dest_rows(plan["pad_starts"], tope, rank)[:TOP_K]
            idx = (dest[:, None, :] + plane_base).reshape(-1)
            xs = _sc_scatter_rows(xp.reshape(planes * Tg, SC_ROW_WORDS), idx, planes * P)
            routed.append((h1, gates, plan, idx, xs))
        h_out = None
        for g, (h1, gates, plan, idx, xs) in enumerate(routed):
            ys = _experts(i, plan, xs.reshape(planes, P, SC_ROW_WORDS), w_gu, b_gu, w_dn, b_dn)
            yg = _sc_gather_rows(ys.reshape(planes * P, SC_ROW_WORDS), idx)
            h_out = _combine_ple(i, g, h1, yg.reshape(TOP_K, planes, Tg, SC_ROW_WORDS), gates, p_rows, prm,
                                 g_final.reshape(1, D), i == depth - 1, h_out)
        h = h_out
    return h.reshape(B, S, D)
```

```python
import functools

import jax
import jax.numpy as jnp
from jax import lax
from jax.experimental import pallas as pl
from jax.experimental.pallas import tpu as pltpu
from jax.experimental.pallas import tpu_sc as plsc

F32 = jnp.float32
BF16 = jnp.bfloat16
I32 = jnp.int32
U32 = jnp.uint32

CHUNK = 64
GMLP_WIDTH = 512
GMLP_GROUPS = 4
GMLP_GROUP_CH = 128
GMLP_BLOCK = 128
FOX_WIDTH = 512
FOX_HEAD_DIM = 64
FOX_HEADS = 8
N_EXPERTS = 32
TOP_K = 4
EXPERT_ROWS = 256
SWIGLU_LIMIT = 7.0
SWIGLU_ALPHA = 1.702
RMS_EPS = 1e-5
LN_EPS = 1e-5

HEAD_PAD = 128
BIAS_PARTS = 3
ROW_TILE = 512
ATTN_TILE = 512
ATTN_HEADS = 1
ACC_ROWS = 80
LOG2E = 1.4426950408889634
SC_WINDOW = 128
SC_ROW_WORDS = 256
COMBINE_ROWS = 256
EXPERT_STEP_BLOCKS = 4
EXPERT_CHAIN_BLOCKS = 4
MOE_GROUPS = 2
VMEM_LIMIT = 56 * 1024 * 1024
NEG_BIG = -1e30

_NT = (((1,), (1,)), ((), ()))
_TN = (((0,), (0,)), ((), ()))


def _rms(x, g):
    return x * lax.rsqrt(jnp.mean(x * x, axis=-1, keepdims=True) + RMS_EPS) * g


def _split3(c):
    a1 = c.astype(BF16).astype(F32)
    r1 = c - a1
    a2 = r1.astype(BF16).astype(F32)
    a3 = (r1 - a2).astype(BF16).astype(F32)
    return a1, a2, a3


def _pack_rows(x):
    n = x.shape[1] // 2
    lo = lax.bitcast_convert_type(x[:, :n].astype(BF16).astype(F32), U32) >> 16
    hi = lax.bitcast_convert_type(x[:, n:].astype(BF16).astype(F32), U32) & jnp.uint32(0xFFFF0000)
    return lo | hi


def _unpack_rows(u):
    lo = lax.bitcast_convert_type(u << 16, F32)
    hi = lax.bitcast_convert_type(u & jnp.uint32(0xFFFF0000), F32)
    return jnp.concatenate([lo, hi], axis=1)


def _store_planes(ref, u):
    for j in range(ref.shape[0]):
        ref[j] = u[:, j * SC_ROW_WORDS:(j + 1) * SC_ROW_WORDS]


def _load_planes(ref):
    return jnp.concatenate([ref[j] for j in range(ref.shape[0])], axis=1)


def _mix_in_kernel(h_ref, gmix_ref, wuv_ref, wqt_ref, wk_ref, wvt_ref, wft_ref, bf_ref, lng_ref, lnb_ref,
                   ws_ref, bs_ref, ga_ref, tri_ref, selk_ref, onesk_ref, after_ref,
                   ya_ref, qt_ref, k_ref, vt_ref, carry_ref, ya_scr):
    del after_ref
    tm = h_ref.shape[0]

    @pl.when(pl.program_id(1) == 0)
    def _():
        carry_ref[...] = jnp.zeros_like(carry_ref)

    xn = _rms(h_ref[...], gmix_ref[...]).astype(BF16)

    z = jnp.dot(xn, wuv_ref[...], preferred_element_type=F32)
    qt = lax.dot_general(wqt_ref[...], xn, _NT, preferred_element_type=F32)
    vt_ref[0] = lax.dot_general(wvt_ref[...], xn, _NT, preferred_element_type=F32).astype(BF16)
    kp = jnp.dot(xn, wk_ref[...], preferred_element_type=F32)
    ft = lax.dot_general(wft_ref[...], xn, _NT, preferred_element_type=F32) + bf_ref[...]

    z = 0.5 * z * (1.0 + lax.erf(z * (0.5 ** 0.5)))
    u = z[:, :GMLP_WIDTH]
    v = z[:, GMLP_WIDTH:]
    mu = jnp.mean(v, axis=-1, keepdims=True)
    vc = v - mu
    var = jnp.mean(vc * vc, axis=-1, keepdims=True)
    vn = (vc * lax.rsqrt(var + LN_EPS) * lng_ref[...] + lnb_ref[...]).astype(BF16)
    pi = lax.broadcasted_iota(I32, (GMLP_BLOCK, GMLP_BLOCK), 0) // CHUNK
    pj = lax.broadcasted_iota(I32, (GMLP_BLOCK, GMLP_BLOCK), 1) // CHUNK
    for g in range(GMLP_GROUPS):
        wm = jnp.where(pj <= pi, ws_ref[g], 0.0).astype(BF16)
        cs = slice(g * GMLP_GROUP_CH, (g + 1) * GMLP_GROUP_CH)
        for n in range(tm // GMLP_BLOCK):
            rs = slice(n * GMLP_BLOCK, (n + 1) * GMLP_BLOCK)
            sv = jnp.dot(wm, vn[rs, cs], preferred_element_type=F32) + bs_ref[g]
            ya_scr[rs, cs] = u[rs, cs] * sv
    ya_ref[...] = _rms(ya_scr[...], ga_ref[...]).astype(BF16)

    lf = (jnp.minimum(ft, 0.0) - jnp.log1p(jnp.exp(-jnp.abs(ft)))) * LOG2E
    zpad = jnp.zeros((32 - BIAS_PARTS * FOX_HEADS, tm), F32)
    stack = jnp.concatenate(list(_split3(lf)) + [zpad], axis=0).astype(BF16)
    cum = jnp.dot(stack, tri_ref[...], preferred_element_type=F32)
    c = cum[0:8] + cum[8:16] + cum[16:24] + carry_ref[:, 0:1]
    carry_ref[...] = jnp.broadcast_to(c[:, tm - 1:tm], carry_ref.shape)
    a1, a2, a3 = _split3(c)

    si = lax.broadcasted_iota(I32, (8, tm), 0)
    zrows = jnp.zeros((HEAD_PAD - FOX_HEAD_DIM - 8, tm), F32)
    for hd in range(FOX_HEADS):
        b1 = jnp.broadcast_to(a1[hd:hd + 1], (8, tm))
        b2 = jnp.broadcast_to(a2[hd:hd + 1], (8, tm))
        b3 = jnp.broadcast_to(a3[hd:hd + 1], (8, tm))
        ext = jnp.where(si == 0, b1, jnp.where(si == 1, b2, jnp.where(si == 2, b3,
                        jnp.where(si < 2 * BIAS_PARTS, 1.0, 0.0))))
        blk = jnp.concatenate([qt[hd * FOX_HEAD_DIM:(hd + 1) * FOX_HEAD_DIM], ext, zrows], axis=0)
        qt_ref[0, hd * HEAD_PAD:(hd + 1) * HEAD_PAD, :] = blk.astype(BF16)

    cstack = jnp.concatenate([a1, a2, a3, zpad], axis=0).astype(BF16)
    kext = lax.dot_general(cstack, selk_ref[...], _TN, preferred_element_type=F32)
    k_ref[...] = (kp + kext + onesk_ref[...]).astype(BF16)


def _mix_in(group, h, B, S, prm, after):
    T, D = h.shape
    Bg = B // MOE_GROUPS
    Tg = Bg * S
    tm = min(ROW_TILE, S)
    ns = S // tm
    H = FOX_HEADS
    const = lambda shape: pl.BlockSpec(shape, lambda b, s: (0,) * len(shape))
    return pl.pallas_call(
        _mix_in_kernel,
        grid=(Bg, ns),
        in_specs=[
            pl.BlockSpec((tm, D), lambda b, s: ((group * Bg + b) * ns + s, 0)),
            const((1, D)),
            const((D, 2 * GMLP_WIDTH)),
            const((FOX_WIDTH, D)),
            const((D, H * HEAD_PAD)),
            const((FOX_WIDTH, D)),
            const((H, D)),
            const((H, 1)),
            const((1, GMLP_WIDTH)),
            const((1, GMLP_WIDTH)),
            const((GMLP_GROUPS, GMLP_BLOCK, GMLP_BLOCK)),
            const((GMLP_GROUPS, GMLP_BLOCK, 1)),
            const((1, GMLP_WIDTH)),
            const((tm, tm)),
            const((32, H * HEAD_PAD)),
            const((1, H * HEAD_PAD)),
            pl.BlockSpec(memory_space=pl.ANY),
        ],
        out_specs=[
            pl.BlockSpec((tm, GMLP_WIDTH), lambda b, s: (b * ns + s, 0)),
            pl.BlockSpec((1, H * HEAD_PAD, tm), lambda b, s: (b, 0, s)),
            pl.BlockSpec((tm, H * HEAD_PAD), lambda b, s: (b * ns + s, 0)),
            pl.BlockSpec((1, FOX_WIDTH, tm), lambda b, s: (b, 0, s)),
        ],
        out_shape=[
            jax.ShapeDtypeStruct((Tg, GMLP_WIDTH), BF16),
            jax.ShapeDtypeStruct((Bg, H * HEAD_PAD, S), BF16),
            jax.ShapeDtypeStruct((Tg, H * HEAD_PAD), BF16),
            jax.ShapeDtypeStruct((Bg, FOX_WIDTH, S), BF16),
        ],
        scratch_shapes=[pltpu.VMEM((H, 128), F32), pltpu.VMEM((tm, GMLP_WIDTH), F32)],
        compiler_params=pltpu.CompilerParams(
            dimension_semantics=("arbitrary", "arbitrary"), vmem_limit_bytes=VMEM_LIMIT),
        name="mix_in",
    )(h, prm["g_mix"], prm["wuv"], prm["wqt"], prm["wk"], prm["wvt"], prm["wft"], prm["b_f"],
      prm["ln_g"], prm["ln_b"], prm["w_s"], prm["b_s"], prm["g_a"], prm["tri_incl"], prm["selk"], prm["onesk"],
      after)


def _fox_attn_kernel(qt_ref, k_ref, vt_ref, ot_ref):
    S = k_ref.shape[0]
    t = min(ATTN_TILE, S)
    nq = S // t
    hf = t // 2
    ones_rows = jnp.ones((ACC_ROWS - FOX_HEAD_DIM, t), BF16)
    pairs = [(qi, kj) for qi in range(nq) for kj in range(qi + 1)]

    def tri(rows, cols):
        return lax.broadcasted_iota(I32, (rows, cols), 0) <= lax.broadcasted_iota(I32, (rows, cols), 1)

    def scores(qi, kj):
        out = []
        for hh in range(ATTN_HEADS):
            hs = slice(hh * HEAD_PAD, (hh + 1) * HEAD_PAD)
            if kj < qi:
                out.append([jnp.dot(k_ref[kj * t:(kj + 1) * t, hs], qt_ref[0, hs, qi * t:(qi + 1) * t],
                                    preferred_element_type=F32)])
            else:
                top = jnp.dot(k_ref[kj * t:kj * t + hf, hs], qt_ref[0, hs, qi * t:(qi + 1) * t],
                              preferred_element_type=F32)
                bot = jnp.dot(k_ref[kj * t + hf:(kj + 1) * t, hs], qt_ref[0, hs, qi * t + hf:(qi + 1) * t],
                              preferred_element_type=F32)
                out.append([jnp.where(tri(hf, t), top, NEG_BIG), jnp.where(tri(hf, hf), bot, NEG_BIG)])
        return out

    s_next = scores(*pairs[0])
    state = None
    for n, (qi, kj) in enumerate(pairs):
        s_cur = s_next
        if n + 1 < len(pairs):
            s_next = scores(*pairs[n + 1])
        new_state = []
        for hh in range(ATTN_HEADS):
            parts = s_cur[hh]
            v = vt_ref[0, hh * FOX_HEAD_DIM:(hh + 1) * FOX_HEAD_DIM, kj * t:(kj + 1) * t]
            v_aug = jnp.concatenate([v, ones_rows], axis=0)
            mx = jnp.max(parts[0], axis=0, keepdims=True)
            if len(parts) == 2:
                mx = jnp.concatenate(
                    [mx[:, :hf], jnp.maximum(mx[:, hf:], jnp.max(parts[1], axis=0, keepdims=True))], axis=1)
            m_new = mx if kj == 0 else jnp.maximum(state[hh][0], mx)
            if len(parts) == 1:
                pv = jnp.dot(v_aug, jnp.exp2(parts[0] - m_new).astype(BF16), preferred_element_type=F32)
            else:
                pv = jnp.dot(v_aug[:, :hf], jnp.exp2(parts[0] - m_new).astype(BF16), preferred_element_type=F32)
                pv_r = jnp.dot(v_aug[:, hf:], jnp.exp2(parts[1] - m_new[:, hf:]).astype(BF16),
                               preferred_element_type=F32)
                pv = jnp.concatenate([pv[:, :hf], pv[:, hf:] + pv_r], axis=1)
            acc = pv if kj == 0 else jnp.exp2(state[hh][0] - m_new) * state[hh][1] + pv
            new_state.append((m_new, acc))
        state = new_state
        if kj == qi:
            for hh in range(ATTN_HEADS):
                acc = state[hh][1]
                o = acc[:FOX_HEAD_DIM] / acc[FOX_HEAD_DIM:FOX_HEAD_DIM + 1]
                ot_ref[0, hh * FOX_HEAD_DIM:(hh + 1) * FOX_HEAD_DIM, qi * t:(qi + 1) * t] = o.astype(BF16)


def _fox_attn(qt, kx, vt, B, S):
    G = ATTN_HEADS
    return pl.pallas_call(
        _fox_attn_kernel,
        grid=(B, FOX_HEADS // G),
        in_specs=[
            pl.BlockSpec((1, G * HEAD_PAD, S), lambda b, h: (b, h, 0)),
            pl.BlockSpec((S, G * HEAD_PAD), lambda b, h: (b, h)),
            pl.BlockSpec((1, G * FOX_HEAD_DIM, S), lambda b, h: (b, h, 0)),
        ],
        out_specs=pl.BlockSpec((1, G * FOX_HEAD_DIM, S), lambda b, h: (b, h, 0)),
        out_shape=jax.ShapeDtypeStruct((B, FOX_WIDTH, S), BF16),
        compiler_params=pltpu.CompilerParams(
            dimension_semantics=("parallel", "parallel"), vmem_limit_bytes=VMEM_LIMIT),
        name="fox_attn",
    )(qt, kx, vt)


def _out_router_kernel(h_ref, ya_ref, ybt_ref, woa_ref, wob_ref, gb_ref, gmoe_ref, wrt_ref, br_ref, tri_ref,
                       h1_ref, xp_ref, tope_ref, rank_ref, gates_ref, counts_ref, carry_ref):
    tm = h_ref.shape[0]

    @pl.when(pl.program_id(0) == 0)
    def _():
        carry_ref[...] = jnp.zeros_like(carry_ref)

    yb = ybt_ref[0].astype(F32)
    ybn = (yb * lax.rsqrt(jnp.mean(yb * yb, axis=0, keepdims=True) + RMS_EPS) * gb_ref[...]).astype(BF16)
    y = jnp.dot(ya_ref[...], woa_ref[...], preferred_element_type=F32)
    y = y + lax.dot_general(ybn, wob_ref[...], _TN, preferred_element_type=F32)
    h1 = h_ref[...] + y
    h1_ref[...] = h1
    xn = _rms(h1, gmoe_ref[...])
    _store_planes(xp_ref, _pack_rows(xn))

    logits = lax.dot_general(wrt_ref[...], xn.astype(BF16), _NT, preferred_element_type=F32) + br_ref[...]
    ri = lax.broadcasted_iota(I32, (N_EXPERTS, tm), 0)
    vals, idxs = [], []
    l = logits
    for _ in range(TOP_K):
        m = jnp.max(l, axis=0, keepdims=True)
        idx = jnp.min(jnp.where(l == m, ri, N_EXPERTS), axis=0, keepdims=True)
        vals.append(m)
        idxs.append(idx)
        l = jnp.where(ri == idx, -jnp.inf, l)
    es = [jnp.exp(vk - vals[0]) for vk in vals]
    den = es[0] + es[1] + es[2] + es[3]
    zrow = jnp.zeros((8 - TOP_K, tm), F32)
    gates_ref[...] = jnp.concatenate([e / den for e in es] + [zrow], axis=0).T

    sel = jnp.zeros((N_EXPERTS, tm), F32)
    for idx in idxs:
        sel = sel + jnp.where(ri == idx, 1.0, 0.0)
    cnt = jnp.dot(sel.astype(BF16), tri_ref[...], preferred_element_type=F32) + carry_ref[:, 0:1]
    ranks = [jnp.sum(jnp.where(ri == idx, cnt, 0.0), axis=0, keepdims=True) for idx in idxs]
    zi = jnp.zeros((8 - TOP_K, tm), I32)
    rank_ref[...] = jnp.concatenate([r.astype(I32) for r in ranks] + [zi], axis=0)
    tope_ref[...] = jnp.concatenate(idxs + [zi], axis=0)
    total = carry_ref[...] + jnp.sum(sel, axis=1, keepdims=True)
    carry_ref[...] = total
    counts_ref[...] = total.astype(I32)


def _out_router(group, h, ya, ybt, B, S, prm, tri_strict):
    T, D = h.shape
    Tg = T // MOE_GROUPS
    tm = min(ROW_TILE, S)
    ns = S // tm
    E = N_EXPERTS
    planes = (D // 2) // SC_ROW_WORDS
    off = group * (Tg // tm)
    const = lambda shape: pl.BlockSpec(shape, lambda i: (0,) * len(shape))
    return pl.pallas_call(
        _out_router_kernel,
        grid=(Tg // tm,),
        in_specs=[
            pl.BlockSpec((tm, D), lambda i: (off + i, 0)),
            pl.BlockSpec((tm, GMLP_WIDTH), lambda i: (i, 0)),
            pl.BlockSpec((1, FOX_WIDTH, tm), lambda i: (i // ns, 0, i % ns)),
            const((GMLP_WIDTH, D)),
            const((FOX_WIDTH, D)),
            const((FOX_WIDTH, 1)),
            const((1, D)),
            const((E, D)),
            const((E, 1)),
            const((tm, tm)),
        ],
        out_specs=[
            pl.BlockSpec((tm, D), lambda i: (i, 0)),
            pl.BlockSpec((planes, tm, SC_ROW_WORDS), lambda i: (0, i, 0)),
            pl.BlockSpec((8, tm), lambda i: (0, i)),
            pl.BlockSpec((8, tm), lambda i: (0, i)),
            pl.BlockSpec((tm, 8), lambda i: (i, 0)),
            const((E, 128)),
        ],
        out_shape=[
            jax.ShapeDtypeStruct((Tg, D), F32),
            jax.ShapeDtypeStruct((planes, Tg, SC_ROW_WORDS), U32),
            jax.ShapeDtypeStruct((8, Tg), I32),
            jax.ShapeDtypeStruct((8, Tg), I32),
            jax.ShapeDtypeStruct((Tg, 8), F32),
            jax.ShapeDtypeStruct((E, 128), I32),
        ],
        scratch_shapes=[pltpu.VMEM((E, 128), F32)],
        compiler_params=pltpu.CompilerParams(
            dimension_semantics=("arbitrary",), vmem_limit_bytes=VMEM_LIMIT),
        name="out_router",
    )(h, ya, ybt, prm["woa"], prm["wob"], prm["g_b"], prm["g_moe"], prm["wrt"], prm["b_router"], tri_strict)


def _dest_kernel(ps_ref, tope_ref, rank_ref, dest_ref):
    e_sel = tope_ref[...]
    d = rank_ref[...]
    for e in range(N_EXPERTS):
        d = d + jnp.where(e_sel == e, ps_ref[e], 0)
    dest_ref[...] = d


def _dest_rows(pad_starts, tope, rank):
    T = tope.shape[1]
    tc = min(4096, T)
    return pl.pallas_call(
        _dest_kernel,
        grid_spec=pltpu.PrefetchScalarGridSpec(
            num_scalar_prefetch=1,
            grid=(T // tc,),
            in_specs=[pl.BlockSpec((8, tc), lambda i, ps: (0, i)),
                      pl.BlockSpec((8, tc), lambda i, ps: (0, i))],
            out_specs=pl.BlockSpec((8, tc), lambda i, ps: (0, i)),
        ),
        out_shape=jax.ShapeDtypeStruct((8, T), I32),
        name="dest_rows",
    )(pad_starts, tope, rank)


def _sc_mesh():
    return plsc.VectorSubcoreMesh(core_axis_name="c", subcore_axis_name="s")


def _sc_scatter_rows(x, idx, n_out):
    R, W = x.shape
    N = idx.shape[0]
    nsrc = R // SC_WINDOW
    copies = N // R

    @functools.partial(pl.kernel, out_type=jax.ShapeDtypeStruct((n_out, W), x.dtype), mesh=_sc_mesh())
    def k(x_hbm, i_hbm, o_hbm):
        def body(x_vmem, *i_vmems):
            for i_vmem in i_vmems:
                pltpu.sync_copy(x_vmem, o_hbm.at[i_vmem.at[0]])

        idx_spec = lambda c: pl.BlockSpec((1, SC_WINDOW), lambda i: (0, c * nsrc + i))
        pltpu.emit_pipeline(
            body,
            grid=(nsrc,),
            in_specs=[pl.BlockSpec((SC_WINDOW, W), lambda i: (i, 0))] + [idx_spec(c) for c in range(copies)],
            out_specs=[],
            core_axis_name=("c", "s"),
            dimension_semantics=(pltpu.PARALLEL,),
        )(x_hbm, *([i_hbm] * copies))

    return k(x, idx.reshape(1, N))


def _sc_gather_rows(table, idx):
    V, W = table.shape
    N = idx.shape[0]

    @functools.partial(pl.kernel, out_type=jax.ShapeDtypeStruct((N, W), table.dtype), mesh=_sc_mesh())
    def k(t_hbm, i_hbm, o_hbm):
        def body(i_vmem, o_vmem):
            pltpu.sync_copy(t_hbm.at[i_vmem.at[0]], o_vmem)

        pltpu.emit_pipeline(
            body,
            grid=(N // SC_WINDOW,),
            in_specs=[pl.BlockSpec((1, SC_WINDOW), lambda i: (0, i))],
            out_specs=[pl.BlockSpec((SC_WINDOW, W), lambda i: (i, 0))],
            core_axis_name=("c", "s"),
            dimension_semantics=(pltpu.PARALLEL,),
        )(i_hbm, o_hbm)

    return k(table, idx.reshape(1, N))


def _expert_kernel(be_ref, nv_ref, fs_ref, nxt_ref, uni_ref, xs_ref, wgu_hbm, bgu_ref, wdn_hbm, bdn_ref, ys_ref,
                   wgu_f32, wdn_f32, cur_slot, sem, *, layer):
    F = wdn_f32.shape[1]

    def weight_copies(e, slot):
        return (pltpu.make_async_copy(wgu_hbm.at[layer, e], wgu_f32.at[slot], sem.at[slot, 0]),
                pltpu.make_async_copy(wdn_hbm.at[layer, e], wdn_f32.at[slot], sem.at[slot, 1]))

    @pl.when(pl.program_id(0) == 0)
    def _():
        for cp in weight_copies(be_ref[0], 0):
            cp.start()

    def ffn(x, e):
        slot = cur_slot[0]
        hgu = jnp.dot(x, wgu_f32[slot], preferred_element_type=F32) + bgu_ref[e]
        g = jnp.minimum(hgu[:, :F], SWIGLU_LIMIT)
        lin = jnp.clip(hgu[:, F:], -SWIGLU_LIMIT, SWIGLU_LIMIT)
        act = g * jax.nn.sigmoid(SWIGLU_ALPHA * g) * (lin + 1.0)
        return _pack_rows(jnp.dot(act, wdn_f32[slot], preferred_element_type=F32) + bdn_ref[e])

    def load_rows(rs):
        return _unpack_rows(jnp.concatenate([xs_ref[pn, rs, :] for pn in range(xs_ref.shape[0])], axis=1))

    def store_rows(rs, y):
        for pn in range(ys_ref.shape[0]):
            ys_ref[pn, rs, :] = y[:, pn * SC_ROW_WORDS:(pn + 1) * SC_ROW_WORDS]

    for c in range(EXPERT_STEP_BLOCKS // EXPERT_CHAIN_BLOCKS):
        chain = pl.program_id(0) * (EXPERT_STEP_BLOCKS // EXPERT_CHAIN_BLOCKS) + c
        uniform = uni_ref[chain] > 0

        @pl.when(uniform)
        def _():
            rs = slice(c * EXPERT_CHAIN_BLOCKS * EXPERT_ROWS, (c + 1) * EXPERT_CHAIN_BLOCKS * EXPERT_ROWS)
            store_rows(rs, ffn(load_rows(rs), be_ref[chain * EXPERT_CHAIN_BLOCKS]))

        for j in range(EXPERT_CHAIN_BLOCKS):
            b = chain * EXPERT_CHAIN_BLOCKS + j
            r0 = (c * EXPERT_CHAIN_BLOCKS + j) * EXPERT_ROWS
            rs = slice(r0, r0 + EXPERT_ROWS)
            e = be_ref[b]
            nv = nv_ref[b]
            fs = fs_ref[b]

            @pl.when(fs >= 0)
            def _():
                for cp in weight_copies(e, fs):
                    cp.wait()
                cur_slot[0] = fs
                nxt = nxt_ref[b]

                @pl.when(nxt >= 0)
                def _():
                    for cp in weight_copies(nxt, 1 - fs):
                        cp.start()

            @pl.when(jnp.logical_not(uniform) & (nv > 0))
            def _():
                x = load_rows(rs)
                rows = lax.broadcasted_iota(I32, x.shape, 0)
                store_rows(rs, ffn(jnp.where(rows < nv, x, 0.0), e))

            @pl.when(nv == 0)
            def _():
                ys_ref[:, rs, :] = jnp.zeros((ys_ref.shape[0], EXPERT_ROWS, SC_ROW_WORDS), U32)


def _experts(layer, plan, xs, w_gu, b_gu, w_dn, b_dn):
    planes, P, _ = xs.shape
    _, E, D, F2 = w_gu.shape
    F = F2 // 2
    step_rows = EXPERT_STEP_BLOCKS * EXPERT_ROWS
    by_step = lambda i, be, nv, fs, nxt, uni: (0, i, 0)
    layer_all = lambda i, be, nv, fs, nxt, uni: (layer, 0, 0, 0)
    return pl.pallas_call(
        functools.partial(_expert_kernel, layer=layer),
        grid_spec=pltpu.PrefetchScalarGridSpec(
            num_scalar_prefetch=5,
            grid=(P // step_rows,),
            in_specs=[
                pl.BlockSpec((planes, step_rows, SC_ROW_WORDS), by_step),
                pl.BlockSpec(memory_space=pl.ANY),
                pl.BlockSpec((None, E, 1, F2), layer_all),
                pl.BlockSpec(memory_space=pl.ANY),
                pl.BlockSpec((None, E, 1, D), layer_all),
            ],
            out_specs=pl.BlockSpec((planes, step_rows, SC_ROW_WORDS), by_step),
            scratch_shapes=[
                pltpu.VMEM((2, D, F2), F32), pltpu.VMEM((2, F, D), F32),
                pltpu.SMEM((1,), I32),
                pltpu.SemaphoreType.DMA((2, 2)),
            ],
        ),
        out_shape=jax.ShapeDtypeStruct((planes, P, SC_ROW_WORDS), U32),
        compiler_params=pltpu.CompilerParams(
            dimension_semantics=("arbitrary",), vmem_limit_bytes=VMEM_LIMIT),
        name="experts",
    )(plan["block_e"], plan["block_nv"], plan["first_slot"], plan["next_e"], plan["step_uniform"],
      xs, w_gu, b_gu.reshape(-1, E, 1, F2), w_dn, b_dn.reshape(-1, E, 1, D))


def _combine_ple_kernel(h1_ref, yg_ref, gates_ref, gple_ref, wpg_ref, p_ref, wple_ref, gfin_ref, *rest, final):
    o_ref = rest[-1]
    for r in range(h1_ref.shape[0] // COMBINE_ROWS):
        rs = slice(r * COMBINE_ROWS, (r + 1) * COMBINE_ROWS)
        h2 = h1_ref[rs, :]
        gates = gates_ref[rs, :]
        for k in range(TOP_K):
            yk = jnp.concatenate([yg_ref[k, pn, rs, :] for pn in range(yg_ref.shape[1])], axis=1)
            h2 = h2 + gates[:, k:k + 1] * _unpack_rows(yk)
        xn = _rms(h2, gple_ref[...]).astype(BF16)
        gate = jax.nn.sigmoid(jnp.dot(xn, wpg_ref[...], preferred_element_type=F32))
        pe = jnp.dot(p_ref[rs, :].astype(BF16), wple_ref[...], preferred_element_type=F32)
        h3 = h2 + gate * pe
        if final:
            h3 = _rms(h3, gfin_ref[...])
        o_ref[rs, :] = h3


def _combine_ple(layer, group, h1, yg, gates, p, prm, g_final, final, h_prev):
    Tg, D = h1.shape
    T = Tg * MOE_GROUPS
    tm = min(ROW_TILE, Tg)
    PD = p.shape[2]
    planes = yg.shape[1]
    off = group * (Tg // tm)
    const = lambda shape: pl.BlockSpec(shape, lambda i: (0,) * len(shape))
    in_specs = [
        pl.BlockSpec((tm, D), lambda i: (i, 0)),
        pl.BlockSpec((TOP_K, planes, tm, SC_ROW_WORDS), lambda i: (0, 0, i, 0)),
        pl.BlockSpec((tm, 8), lambda i: (i, 0)),
        const((1, D)),
        const((D, D)),
        pl.BlockSpec((None, tm, PD), lambda i: (layer, off + i, 0)),
        const((PD, D)),
        const((1, D)),
    ]
    args = [h1, yg, gates, prm["g_ple"], prm["wpg"], p, prm["wple"], g_final]
    aliases = {}
    if h_prev is not None:
        in_specs.append(pl.BlockSpec(memory_space=pl.ANY))
        args.append(h_prev)
        aliases = {len(args) - 1: 0}
    return pl.pallas_call(
        functools.partial(_combine_ple_kernel, final=final),
        grid=(Tg // tm,),
        in_specs=in_specs,
        out_specs=pl.BlockSpec((tm, D), lambda i: (off + i, 0)),
        out_shape=jax.ShapeDtypeStruct((T, D), F32),
        input_output_aliases=aliases,
        compiler_params=pltpu.CompilerParams(
            dimension_semantics=("parallel",), vmem_limit_bytes=VMEM_LIMIT),
        name="combine_ple",
    )(*args)


def _layer_params(i, w):
    D = w["w_in"].shape[1]
    H = FOX_HEADS
    w_in = w["w_in"][i]
    c1 = 2 * GMLP_WIDTH
    c2, c3, c4 = c1 + FOX_WIDTH, c1 + 2 * FOX_WIDTH, c1 + 3 * FOX_WIDTH
    wk = jnp.zeros((D, H, HEAD_PAD), F32).at[:, :, :FOX_HEAD_DIM].set(
        w_in[:, c2:c3].reshape(D, H, FOX_HEAD_DIM)).reshape(D, H * HEAD_PAD)
    return {
        "g_mix": w["g_mix"][i].reshape(1, D),
        "wuv": w_in[:, :c1].astype(BF16),
        "wqt": (w_in[:, c1:c2] * (FOX_HEAD_DIM ** -0.5 * LOG2E)).T.astype(BF16),
        "wk": wk.astype(BF16),
        "wvt": w_in[:, c3:c4].T.astype(BF16),
        "wft": w_in[:, c4:].T.astype(BF16),
        "b_f": w["b_f"][i].reshape(H, 1),
        "ln_g": w["ln_g"][i].reshape(1, GMLP_WIDTH),
        "ln_b": w["ln_b"][i].reshape(1, GMLP_WIDTH),
        "w_s": w["w_s"][i],
        "b_s": w["b_s"][i].reshape(GMLP_GROUPS, GMLP_BLOCK, 1),
        "g_a": w["g_a"][i].reshape(1, GMLP_WIDTH),
        "woa": w["w_out"][i][:GMLP_WIDTH].astype(BF16),
        "wob": w["w_out"][i][GMLP_WIDTH:].astype(BF16),
        "g_b": w["g_b"][i].reshape(FOX_WIDTH, 1),
        "g_moe": w["g_moe"][i].reshape(1, D),
        "wrt": w["w_router"][i].T.astype(BF16),
        "b_router": w["b_router"][i].reshape(N_EXPERTS, 1),
        "g_ple": w["g_ple"][i].reshape(1, D),
        "wpg": w["w_ple_gate"][i].astype(BF16),
        "wple": w["w_ple"][i].astype(BF16),
    }


def _bias_constants(tm):
    H = FOX_HEADS
    j = jnp.arange(tm)
    tri_incl = (j[:, None] <= j[None, :]).astype(BF16)
    tri_strict = (j[:, None] < j[None, :]).astype(BF16)
    part = jnp.arange(BIAS_PARTS)[:, None]
    head = jnp.arange(H)[None, :]
    rows = (part * H + head).reshape(-1)
    cols = (head * HEAD_PAD + FOX_HEAD_DIM + BIAS_PARTS + part).reshape(-1)
    selk = jnp.zeros((32, H * HEAD_PAD), F32).at[rows, cols].set(-1.0).astype(BF16)
    one_cols = (jnp.arange(H)[:, None] * HEAD_PAD + FOX_HEAD_DIM + jnp.arange(BIAS_PARTS)[None, :]).reshape(-1)
    onesk = jnp.zeros((1, H * HEAD_PAD), F32).at[0, one_cols].set(1.0)
    return tri_incl, tri_strict, selk, onesk


def _moe_dispatch_plan(counts, n_blocks):
    padded = (counts + EXPERT_ROWS - 1) // EXPERT_ROWS * EXPERT_ROWS
    pad_ends = jnp.cumsum(padded)
    pad_starts = (pad_ends - padded).astype(I32)
    block_start = jnp.arange(n_blocks, dtype=I32) * EXPERT_ROWS
    n_before = jnp.sum((pad_ends[None, :] <= block_start[:, None]).astype(I32), axis=1)
    block_e = jnp.minimum(n_before, N_EXPERTS - 1).astype(I32)
    own = block_e[:, None] == jnp.arange(N_EXPERTS, dtype=I32)[None, :]
    seg_end = jnp.sum(jnp.where(own, (pad_starts + counts)[None, :], 0), axis=1)
    block_nv = jnp.clip(seg_end - block_start, 0, EXPERT_ROWS)
    block_nv = jnp.where(block_start < pad_ends[-1], block_nv, 0).astype(I32)
    prev_e = jnp.concatenate([jnp.full((1,), -1, I32), block_e[:-1]])
    first = (block_nv > 0) & (block_e != prev_e)
    run = jnp.cumsum(first.astype(I32)) - 1
    first_slot = jnp.where(first, run % 2, -1).astype(I32)
    ar = jnp.arange(N_EXPERTS, dtype=I32)
    later = (ar[None, :] > ar[:, None]) & (counts > 0)[None, :]
    next_of = jnp.min(jnp.where(later, ar[None, :], N_EXPERTS), axis=1)
    next_of = jnp.where(next_of < N_EXPERTS, next_of, -1)
    next_e = jnp.sum(jnp.where(own, next_of[None, :], 0), axis=1).astype(I32)
    plain = ((block_nv == EXPERT_ROWS) & jnp.logical_not(first)).reshape(-1, EXPERT_CHAIN_BLOCKS)
    step_uniform = jnp.all(plain, axis=1).astype(I32)
    return dict(pad_starts=pad_starts, block_e=block_e, block_nv=block_nv, first_slot=first_slot, next_e=next_e,
                step_uniform=step_uniform)


def kernel(x, p, g_mix, w_in, ln_g, ln_b, w_s, b_s, b_f, g_a, g_b, w_out, g_moe, w_router, b_router,
           w_gu, b_gu, w_dn, b_dn, g_ple, w_ple_gate, w_ple, g_final):
    B, S, D = x.shape
    T = B * S
    depth = w_in.shape[0]
    w = dict(g_mix=g_mix, w_in=w_in, ln_g=ln_g, ln_b=ln_b, w_s=w_s, b_s=b_s, b_f=b_f, g_a=g_a, g_b=g_b,
             w_out=w_out, g_moe=g_moe, w_router=w_router, b_router=b_router, w_gu=w_gu, b_gu=b_gu,
             w_dn=w_dn, b_dn=b_dn, g_ple=g_ple, w_ple_gate=w_ple_gate, w_ple=w_ple)
    tri_incl, tri_strict, selk, onesk = _bias_constants(min(ROW_TILE, S))
    G = MOE_GROUPS
    Tg = T // G
    n_blocks = -(-(Tg * TOP_K) // EXPERT_ROWS) + N_EXPERTS
    n_blocks = -(-n_blocks // EXPERT_STEP_BLOCKS) * EXPERT_STEP_BLOCKS
    P = n_blocks * EXPERT_ROWS
    planes = (D // 2) // SC_ROW_WORDS
    plane_base = (jnp.arange(planes, dtype=I32) * P)[None, :, None]
    p_rows = p.reshape(depth, T, p.shape[-1])

    h = x.reshape(T, D)
    for i in range(depth):
        prm = _layer_params(i, w)
        prm.update(tri_incl=tri_incl, selk=selk, onesk=onesk)
        routed = []
        idx = jnp.zeros((8, 128), I32)
        for g in range(G):
            ya, qt, kx, vt = _mix_in(g, h, B, S, prm, idx)
            ybt = _fox_attn(qt, kx, vt, B // G, S)
            h1, xp, tope, rank, gates, counts = _out_router(g, h, ya, ybt, B, S, prm, tri_strict)
            plan = _moe_dispatch_plan(counts[:, 0], n_blocks)
            dest = _dest_rows(plan["pad_starts"], tope, rank)[:TOP_K]
            idx = (dest[:, None, :] + plane_base).reshape(-1)
            xs = _sc_scatter_rows(xp.reshape(planes * Tg, SC_ROW_WORDS), idx, planes * P)
            routed.append((h1, gates, plan, idx, xs))
        h_out = None
        for g, (h1, gates, plan, idx, xs) in enumerate(routed):
            ys = _experts(i, plan, xs.reshape(planes, P, SC_ROW_WORDS), w_gu, b_gu, w_dn, b_dn)
            yg = _sc_gather_rows(ys.reshape(planes * P, SC_ROW_WORDS), idx)
            h_out = _combine_ple(i, g, h1, yg.reshape(TOP_K, planes, Tg, SC_ROW_WORDS), gates, p_rows, prm,
                                 g_final.reshape(1, D), i == depth - 1, h_out)
        h = h_out
    return h.reshape(B, S, D)
```

```python
import functools

import jax
import jax.numpy as jnp
from jax import lax
from jax.experimental import pallas as pl
from jax.experimental.pallas import tpu as pltpu
from jax.experimental.pallas import tpu_sc as plsc

F32 = jnp.float32
BF16 = jnp.bfloat16
I32 = jnp.int32
U32 = jnp.uint32

CHUNK = 64
GMLP_WIDTH = 512
GMLP_GROUPS = 4
GMLP_GROUP_CH = 128
GMLP_BLOCK = 128
FOX_WIDTH = 512
FOX_HEAD_DIM = 64
FOX_HEADS = 8
N_EXPERTS = 32
TOP_K = 4
EXPERT_ROWS = 256
SWIGLU_LIMIT = 7.0
SWIGLU_ALPHA = 1.702
RMS_EPS = 1e-5
LN_EPS = 1e-5

HEAD_PAD = 128
BIAS_PARTS = 3
ROW_TILE = 512
ATTN_TILE = 512
ATTN_HEADS = 1
ACC_ROWS = 80
LOG2E = 1.4426950408889634
SC_WINDOW = 128
SC_ROW_WORDS = 256
COMBINE_ROWS = 256
EXPERT_STEP_BLOCKS = 4
EXPERT_CHAIN_BLOCKS = 4
MOE_GROUPS = 2
VMEM_LIMIT = 56 * 1024 * 1024
NEG_BIG = -1e30

_NT = (((1,), (1,)), ((), ()))
_TN = (((0,), (0,)), ((), ()))


def _rms(x, g):
    return x * lax.rsqrt(jnp.mean(x * x, axis=-1, keepdims=True) + RMS_EPS) * g


def _split3(c):
    a1 = c.astype(BF16).astype(F32)
    r1 = c - a1
    a2 = r1.astype(BF16).astype(F32)
    a3 = (r1 - a2).astype(BF16).astype(F32)
    return a1, a2, a3


def _pack_rows(x):
    n = x.shape[1] // 2
    lo = lax.bitcast_convert_type(x[:, :n].astype(BF16).astype(F32), U32) >> 16
    hi = lax.bitcast_convert_type(x[:, n:].astype(BF16).astype(F32), U32) & jnp.uint32(0xFFFF0000)
    return lo | hi


def _unpack_rows(u):
    lo = lax.bitcast_convert_type(u << 16, F32)
    hi = lax.bitcast_convert_type(u & jnp.uint32(0xFFFF0000), F32)
    return jnp.concatenate([lo, hi], axis=1)


def _store_planes(ref, u):
    for j in range(ref.shape[0]):
        ref[j] = u[:, j * SC_ROW_WORDS:(j + 1) * SC_ROW_WORDS]


def _load_planes(ref):
    return jnp.concatenate([ref[j] for j in range(ref.shape[0])], axis=1)


def _mix_in_kernel(h_ref, gmix_ref, wuv_ref, wqt_ref, wk_ref, wvt_ref, wft_ref, bf_ref, lng_ref, lnb_ref,
                   ws_ref, bs_ref, ga_ref, tri_ref, selk_ref, onesk_ref, after_ref,
                   ya_ref, qt_ref, k_ref, vt_ref, carry_ref, ya_scr):
    del after_ref
    tm = h_ref.shape[0]

    @pl.when(pl.program_id(1) == 0)
    def _():
        carry_ref[...] = jnp.zeros_like(carry_ref)

    xn = _rms(h_ref[...], gmix_ref[...]).astype(BF16)

    z = jnp.dot(xn, wuv_ref[...], preferred_element_type=F32)
    qt = lax.dot_general(wqt_ref[...], xn, _NT, preferred_element_type=F32)
    vt_ref[0] = lax.dot_general(wvt_ref[...], xn, _NT, preferred_element_type=F32).astype(BF16)
    kp = jnp.dot(xn, wk_ref[...], preferred_element_type=F32)
    ft = lax.dot_general(wft_ref[...], xn, _NT, preferred_element_type=F32) + bf_ref[...]

    z = 0.5 * z * (1.0 + lax.erf(z * (0.5 ** 0.5)))
    u = z[:, :GMLP_WIDTH]
    v = z[:, GMLP_WIDTH:]
    mu = jnp.mean(v, axis=-1, keepdims=True)
    vc = v - mu
    var = jnp.mean(vc * vc, axis=-1, keepdims=True)
    vn = (vc * lax.rsqrt(var + LN_EPS) * lng_ref[...] + lnb_ref[...]).astype(BF16)
    pi = lax.broadcasted_iota(I32, (GMLP_BLOCK, GMLP_BLOCK), 0) // CHUNK
    pj = lax.broadcasted_iota(I32, (GMLP_BLOCK, GMLP_BLOCK), 1) // CHUNK
    for g in range(GMLP_GROUPS):
        wm = jnp.where(pj <= pi, ws_ref[g], 0.0).astype(BF16)
        cs = slice(g * GMLP_GROUP_CH, (g + 1) * GMLP_GROUP_CH)
        for n in range(tm // GMLP_BLOCK):
            rs = slice(n * GMLP_BLOCK, (n + 1) * GMLP_BLOCK)
            sv = jnp.dot(wm, vn[rs, cs], preferred_element_type=F32) + bs_ref[g]
            ya_scr[rs, cs] = u[rs, cs] * sv
    ya_ref[...] = _rms(ya_scr[...], ga_ref[...]).astype(BF16)

    lf = (jnp.minimum(ft, 0.0) - jnp.log1p(jnp.exp(-jnp.abs(ft)))) * LOG2E
    zpad = jnp.zeros((32 - BIAS_PARTS * FOX_HEADS, tm), F32)
    stack = jnp.concatenate(list(_split3(lf)) + [zpad], axis=0).astype(BF16)
    cum = jnp.dot(stack, tri_ref[...], preferred_element_type=F32)
    c = cum[0:8] + cum[8:16] + cum[16:24] + carry_ref[:, 0:1]
    carry_ref[...] = jnp.broadcast_to(c[:, tm - 1:tm], carry_ref.shape)
    a1, a2, a3 = _split3(c)

    si = lax.broadcasted_iota(I32, (8, tm), 0)
    zrows = jnp.zeros((HEAD_PAD - FOX_HEAD_DIM - 8, tm), F32)
    for hd in range(FOX_HEADS):
        b1 = jnp.broadcast_to(a1[hd:hd + 1], (8, tm))
        b2 = jnp.broadcast_to(a2[hd:hd + 1], (8, tm))
        b3 = jnp.broadcast_to(a3[hd:hd + 1], (8, tm))
        ext = jnp.where(si == 0, b1, jnp.where(si == 1, b2, jnp.where(si == 2, b3,
                        jnp.where(si < 2 * BIAS_PARTS, 1.0, 0.0))))
        blk = jnp.concatenate([qt[hd * FOX_HEAD_DIM:(hd + 1) * FOX_HEAD_DIM], ext, zrows], axis=0)
        qt_ref[0, hd * HEAD_PAD:(hd + 1) * HEAD_PAD, :] = blk.astype(BF16)

    cstack = jnp.concatenate([a1, a2, a3, zpad], axis=0).astype(BF16)
    kext = lax.dot_general(cstack, selk_ref[...], _TN, preferred_element_type=F32)
    k_ref[...] = (kp + kext + onesk_ref[...]).astype(BF16)


def _mix_in(group, h, B, S, prm, after):
    T, D = h.shape
    Bg = B // MOE_GROUPS
    Tg = Bg * S
    tm = min(ROW_TILE, S)
    ns = S // tm
    H = FOX_HEADS
    const = lambda shape: pl.BlockSpec(shape, lambda b, s: (0,) * len(shape))
    return pl.pallas_call(
        _mix_in_kernel,
        grid=(Bg, ns),
        in_specs=[
            pl.BlockSpec((tm, D), lambda b, s: ((group * Bg + b) * ns + s, 0)),
            const((1, D)),
            const((D, 2 * GMLP_WIDTH)),
            const((FOX_WIDTH, D)),
            const((D, H * HEAD_PAD)),
            const((FOX_WIDTH, D)),
            const((H, D)),
            const((H, 1)),
            const((1, GMLP_WIDTH)),
            const((1, GMLP_WIDTH)),
            const((GMLP_GROUPS, GMLP_BLOCK, GMLP_BLOCK)),
            const((GMLP_GROUPS, GMLP_BLOCK, 1)),
            const((1, GMLP_WIDTH)),
            const((tm, tm)),
            const((32, H * HEAD_PAD)),
            const((1, H * HEAD_PAD)),
            pl.BlockSpec(memory_space=pl.ANY),
        ],
        out_specs=[
            pl.BlockSpec((tm, GMLP_WIDTH), lambda b, s: (b * ns + s, 0)),
            pl.BlockSpec((1, H * HEAD_PAD, tm), lambda b, s: (b, 0, s)),
            pl.BlockSpec((tm, H * HEAD_PAD), lambda b, s: (b * ns + s, 0)),
            pl.BlockSpec((1, FOX_WIDTH, tm), lambda b, s: (b, 0, s)),
        ],
        out_shape=[
            jax.ShapeDtypeStruct((Tg, GMLP_WIDTH), BF16),
            jax.ShapeDtypeStruct((Bg, H * HEAD_PAD, S), BF16),
            jax.ShapeDtypeStruct((Tg, H * HEAD_PAD), BF16),
            jax.ShapeDtypeStruct((Bg, FOX_WIDTH, S), BF16),
        ],
        scratch_shapes=[pltpu.VMEM((H, 128), F32), pltpu.VMEM((tm, GMLP_WIDTH), F32)],
        compiler_params=pltpu.CompilerParams(
            dimension_semantics=("arbitrary", "arbitrary"), vmem_limit_bytes=VMEM_LIMIT),
        name="mix_in",
    )(h, prm["g_mix"], prm["wuv"], prm["wqt"], prm["wk"], prm["wvt"], prm["wft"], prm["b_f"],
      prm["ln_g"], prm["ln_b"], prm["w_s"], prm["b_s"], prm["g_a"], prm["tri_incl"], prm["selk"], prm["onesk"],
      after)


def _fox_attn_kernel(qt_ref, k_ref, vt_ref, ot_ref):
    S = k_ref.shape[0]
    t = min(ATTN_TILE, S)
    nq = S // t
    hf = t // 2
    ones_rows = jnp.ones((ACC_ROWS - FOX_HEAD_DIM, t), BF16)
    pairs = [(qi, kj) for qi in range(nq) for kj in range(qi + 1)]

    def tri(rows, cols):
        return lax.broadcasted_iota(I32, (rows, cols), 0) <= lax.broadcasted_iota(I32, (rows, cols), 1)

    def scores(qi, kj):
        out = []
        for hh in range(ATTN_HEADS):
            hs = slice(hh * HEAD_PAD, (hh + 1) * HEAD_PAD)
            if kj < qi:
                out.append([jnp.dot(k_ref[kj * t:(kj + 1) * t, hs], qt_ref[0, hs, qi * t:(qi + 1) * t],
                                    preferred_element_type=F32)])
            else:
                top = jnp.dot(k_ref[kj * t:kj * t + hf, hs], qt_ref[0, hs, qi * t:(qi + 1) * t],
                              preferred_element_type=F32)
                bot = jnp.dot(k_ref[kj * t + hf:(kj + 1) * t, hs], qt_ref[0, hs, qi * t + hf:(qi + 1) * t],
                              preferred_element_type=F32)
                out.append([jnp.where(tri(hf, t), top, NEG_BIG), jnp.where(tri(hf, hf), bot, NEG_BIG)])
        return out

    s_next = scores(*pairs[0])
    state = None
    for n, (qi, kj) in enumerate(pairs):
        s_cur = s_next
        if n + 1 < len(pairs):
            s_next = scores(*pairs[n + 1])
        new_state = []
        for hh in range(ATTN_HEADS):
            parts = s_cur[hh]
            v = vt_ref[0, hh * FOX_HEAD_DIM:(hh + 1) * FOX_HEAD_DIM, kj * t:(kj + 1) * t]
            v_aug = jnp.concatenate([v, ones_rows], axis=0)
            mx = jnp.max(parts[0], axis=0, keepdims=True)
            if len(parts) == 2:
                mx = jnp.concatenate(
                    [mx[:, :hf], jnp.maximum(mx[:, hf:], jnp.max(parts[1], axis=0, keepdims=True))], axis=1)
            m_new = mx if kj == 0 else jnp.maximum(state[hh][0], mx)
            if len(parts) == 1:
                pv = jnp.dot(v_aug, jnp.exp2(parts[0] - m_new).astype(BF16), preferred_element_type=F32)
            else:
                pv = jnp.dot(v_aug[:, :hf], jnp.exp2(parts[0] - m_new).astype(BF16), preferred_element_type=F32)
                pv_r = jnp.dot(v_aug[:, hf:], jnp.exp2(parts[1] - m_new[:, hf:]).astype(BF16),
                               preferred_element_type=F32)
                pv = jnp.concatenate([pv[:, :hf], pv[:, hf:] + pv_r], axis=1)
            acc = pv if kj == 0 else jnp.exp2(state[hh][0] - m_new) * state[hh][1] + pv
            new_state.append((m_new, acc))
        state = new_state
        if kj == qi:
            for hh in range(ATTN_HEADS):
                acc = state[hh][1]
                o = acc[:FOX_HEAD_DIM] / acc[FOX_HEAD_DIM:FOX_HEAD_DIM + 1]
                ot_ref[0, hh * FOX_HEAD_DIM:(hh + 1) * FOX_HEAD_DIM, qi * t:(qi + 1) * t] = o.astype(BF16)


def _fox_attn(qt, kx, vt, B, S):
    G = ATTN_HEADS
    return pl.pallas_call(
        _fox_attn_kernel,
        grid=(B, FOX_HEADS // G),
        in_specs=[
            pl.BlockSpec((1, G * HEAD_PAD, S), lambda b, h: (b, h, 0)),
            pl.BlockSpec((S, G * HEAD_PAD), lambda b, h: (b, h)),
            pl.BlockSpec((1, G * FOX_HEAD_DIM, S), lambda b, h: (b, h, 0)),
        ],
        out_specs=pl.BlockSpec((1, G * FOX_HEAD_DIM, S), lambda b, h: (b, h, 0)),
        out_shape=jax.ShapeDtypeStruct((B, FOX_WIDTH, S), BF16),
        compiler_params=pltpu.CompilerParams(
            dimension_semantics=("parallel", "parallel"), vmem_limit_bytes=VMEM_LIMIT),
        name="fox_attn",
    )(qt, kx, vt)


def _out_router_kernel(h_ref, ya_ref, ybt_ref, woa_ref, wob_ref, gb_ref, gmoe_ref, wrt_ref, br_ref, tri_ref,
                       h1_ref, xp_ref, tope_ref, rank_ref, gates_ref, counts_ref, carry_ref):
    tm = h_ref.shape[0]

    @pl.when(pl.program_id(0) == 0)
    def _():
        carry_ref[...] = jnp.zeros_like(carry_ref)

    yb = ybt_ref[0].astype(F32)
    ybn = (yb * lax.rsqrt(jnp.mean(yb * yb, axis=0, keepdims=True) + RMS_EPS) * gb_ref[...]).astype(BF16)
    y = jnp.dot(ya_ref[...], woa_ref[...], preferred_element_type=F32)
    y = y + lax.dot_general(ybn, wob_ref[...], _TN, preferred_element_type=F32)
    h1 = h_ref[...] + y
    h1_ref[...] = h1
    xn = _rms(h1, gmoe_ref[...])
    _store_planes(xp_ref, _pack_rows(xn))

    logits = lax.dot_general(wrt_ref[...], xn.astype(BF16), _NT, preferred_element_type=F32) + br_ref[...]
    ri = lax.broadcasted_iota(I32, (N_EXPERTS, tm), 0)
    vals, idxs = [], []
    l = logits
    for _ in range(TOP_K):
        m = jnp.max(l, axis=0, keepdims=True)
        idx = jnp.min(jnp.where(l == m, ri, N_EXPERTS), axis=0, keepdims=True)
        vals.append(m)
        idxs.append(idx)
        l = jnp.where(ri == idx, -jnp.inf, l)
    es = [jnp.exp(vk - vals[0]) for vk in vals]
    den = es[0] + es[1] + es[2] + es[3]
    zrow = jnp.zeros((8 - TOP_K, tm), F32)
    gates_ref[...] = jnp.concatenate([e / den for e in es] + [zrow], axis=0).T

    sel = jnp.zeros((N_EXPERTS, tm), F32)
    for idx in idxs:
        sel = sel + jnp.where(ri == idx, 1.0, 0.0)
    cnt = jnp.dot(sel.astype(BF16), tri_ref[...], preferred_element_type=F32) + carry_ref[:, 0:1]
    ranks = [jnp.sum(jnp.where(ri == idx, cnt, 0.0), axis=0, keepdims=True) for idx in idxs]
    zi = jnp.zeros((8 - TOP_K, tm), I32)
    rank_ref[...] = jnp.concatenate([r.astype(I32) for r in ranks] + [zi], axis=0)
    tope_ref[...] = jnp.concatenate(idxs + [zi], axis=0)
    total = carry_ref[...] + jnp.sum(sel, axis=1, keepdims=True)
    carry_ref[...] = total
    counts_ref[...] = total.astype(I32)


def _out_router(group, h, ya, ybt, B, S, prm, tri_strict):
    T, D = h.shape
    Tg = T // MOE_GROUPS
    tm = min(ROW_TILE, S)
    ns = S // tm
    E = N_EXPERTS
    planes = (D // 2) // SC_ROW_WORDS
    off = group * (Tg // tm)
    const = lambda shape: pl.BlockSpec(shape, lambda i: (0,) * len(shape))
    return pl.pallas_call(
        _out_router_kernel,
        grid=(Tg // tm,),
        in_specs=[
            pl.BlockSpec((tm, D), lambda i: (off + i, 0)),
            pl.BlockSpec((tm, GMLP_WIDTH), lambda i: (i, 0)),
            pl.BlockSpec((1, FOX_WIDTH, tm), lambda i: (i // ns, 0, i % ns)),
            const((GMLP_WIDTH, D)),
            const((FOX_WIDTH, D)),
            const((FOX_WIDTH, 1)),
            const((1, D)),
            const((E, D)),
            const((E, 1)),
            const((tm, tm)),
        ],
        out_specs=[
            pl.BlockSpec((tm, D), lambda i: (i, 0)),
            pl.BlockSpec((planes, tm, SC_ROW_WORDS), lambda i: (0, i, 0)),
            pl.BlockSpec((8, tm), lambda i: (0, i)),
            pl.BlockSpec((8, tm), lambda i: (0, i)),
            pl.BlockSpec((tm, 8), lambda i: (i, 0)),
            const((E, 128)),
        ],
        out_shape=[
            jax.ShapeDtypeStruct((Tg, D), F32),
            jax.ShapeDtypeStruct((planes, Tg, SC_ROW_WORDS), U32),
            jax.ShapeDtypeStruct((8, Tg), I32),
            jax.ShapeDtypeStruct((8, Tg), I32),
            jax.ShapeDtypeStruct((Tg, 8), F32),
            jax.ShapeDtypeStruct((E, 128), I32),
        ],
        scratch_shapes=[pltpu.VMEM((E, 128), F32)],
        compiler_params=pltpu.CompilerParams(
            dimension_semantics=("arbitrary",), vmem_limit_bytes=VMEM_LIMIT),
        name="out_router",
    )(h, ya, ybt, prm["woa"], prm["wob"], prm["g_b"], prm["g_moe"], prm["wrt"], prm["b_router"], tri_strict)


def _dest_kernel(ps_ref, tope_ref, rank_ref, dest_ref):
    e_sel = tope_ref[...]
    d = rank_ref[...]
    for e in range(N_EXPERTS):
        d = d + jnp.where(e_sel == e, ps_ref[e], 0)
    dest_ref[...] = d


def _dest_rows(pad_starts, tope, rank):
    T = tope.shape[1]
    tc = min(4096, T)
    return pl.pallas_call(
        _dest_kernel,
        grid_spec=pltpu.PrefetchScalarGridSpec(
            num_scalar_prefetch=1,
            grid=(T // tc,),
            in_specs=[pl.BlockSpec((8, tc), lambda i, ps: (0, i)),
                      pl.BlockSpec((8, tc), lambda i, ps: (0, i))],
            out_specs=pl.BlockSpec((8, tc), lambda i, ps: (0, i)),
        ),
        out_shape=jax.ShapeDtypeStruct((8, T), I32),
        name="dest_rows",
    )(pad_starts, tope, rank)


def _sc_mesh():
    return plsc.VectorSubcoreMesh(core_axis_name="c", subcore_axis_name="s")


def _sc_scatter_rows(x, idx, n_out):
    R, W = x.shape
    N = idx.shape[0]
    nsrc = R // SC_WINDOW
    copies = N // R

    @functools.partial(pl.kernel, out_type=jax.ShapeDtypeStruct((n_out, W), x.dtype), mesh=_sc_mesh())
    def k(x_hbm, i_hbm, o_hbm):
        def body(x_vmem, *i_vmems):
            for i_vmem in i_vmems:
                pltpu.sync_copy(x_vmem, o_hbm.at[i_vmem.at[0]])

        idx_spec = lambda c: pl.BlockSpec((1, SC_WINDOW), lambda i: (0, c * nsrc + i))
        pltpu.emit_pipeline(
            body,
            grid=(nsrc,),
            in_specs=[pl.BlockSpec((SC_WINDOW, W), lambda i: (i, 0))] + [idx_spec(c) for c in range(copies)],
            out_specs=[],
            core_axis_name=("c", "s"),
            dimension_semantics=(pltpu.PARALLEL,),
        )(x_hbm, *([i_hbm] * copies))

    return k(x, idx.reshape(1, N))


def _sc_gather_rows(table, idx):
    V, W = table.shape
    N = idx.shape[0]

    @functools.partial(pl.kernel, out_type=jax.ShapeDtypeStruct((N, W), table.dtype), mesh=_sc_mesh())
    def k(t_hbm, i_hbm, o_hbm):
        def body(i_vmem, o_vmem):
            pltpu.sync_copy(t_hbm.at[i_vmem.at[0]], o_vmem)

        pltpu.emit_pipeline(
            body,
            grid=(N // SC_WINDOW,),
            in_specs=[pl.BlockSpec((1, SC_WINDOW), lambda i: (0, i))],
            out_specs=[pl.BlockSpec((SC_WINDOW, W), lambda i: (i, 0))],
            core_axis_name=("c", "s"),
            dimension_semantics=(pltpu.PARALLEL,),
        )(i_hbm, o_hbm)

    return k(table, idx.reshape(1, N))


def _expert_kernel(be_ref, nv_ref, fs_ref, nxt_ref, uni_ref, xs_ref, wgu_hbm, bgu_ref, wdn_hbm, bdn_ref, ys_ref,
                   wgu_f32, wdn_f32, cur_slot, sem, *, layer):
    F = wdn_f32.shape[1]

    def weight_copies(e, slot):
        return (pltpu.make_async_copy(wgu_hbm.at[layer, e], wgu_f32.at[slot], sem.at[slot, 0]),
                pltpu.make_async_copy(wdn_hbm.at[layer, e], wdn_f32.at[slot], sem.at[slot, 1]))

    @pl.when(pl.program_id(0) == 0)
    def _():
        for cp in weight_copies(be_ref[0], 0):
            cp.start()

    def ffn(x, e):
        slot = cur_slot[0]
        hgu = jnp.dot(x, wgu_f32[slot], preferred_element_type=F32) + bgu_ref[e]
        g = jnp.minimum(hgu[:, :F], SWIGLU_LIMIT)
        lin = jnp.clip(hgu[:, F:], -SWIGLU_LIMIT, SWIGLU_LIMIT)
        act = g * jax.nn.sigmoid(SWIGLU_ALPHA * g) * (lin + 1.0)
        return _pack_rows(jnp.dot(act, wdn_f32[slot], preferred_element_type=F32) + bdn_ref[e])

    def load_rows(rs):
        return _unpack_rows(jnp.concatenate([xs_ref[pn, rs, :] for pn in range(xs_ref.shape[0])], axis=1))

    def store_rows(rs, y):
        for pn in range(ys_ref.shape[0]):
            ys_ref[pn, rs, :] = y[:, pn * SC_ROW_WORDS:(pn + 1) * SC_ROW_WORDS]

    for c in range(EXPERT_STEP_BLOCKS // EXPERT_CHAIN_BLOCKS):
        chain = pl.program_id(0) * (EXPERT_STEP_BLOCKS // EXPERT_CHAIN_BLOCKS) + c
        uniform = uni_ref[chain] > 0

        @pl.when(uniform)
        def _():
            rs = slice(c * EXPERT_CHAIN_BLOCKS * EXPERT_ROWS, (c + 1) * EXPERT_CHAIN_BLOCKS * EXPERT_ROWS)
            store_rows(rs, ffn(load_rows(rs), be_ref[chain * EXPERT_CHAIN_BLOCKS]))

        for j in range(EXPERT_CHAIN_BLOCKS):
            b = chain * EXPERT_CHAIN_BLOCKS + j
            r0 = (c * EXPERT_CHAIN_BLOCKS + j) * EXPERT_ROWS
            rs = slice(r0, r0 + EXPERT_ROWS)
            e = be_ref[b]
            nv = nv_ref[b]
            fs = fs_ref[b]

            @pl.when(fs >= 0)
            def _():
                for cp in weight_copies(e, fs):
                    cp.wait()
                cur_slot[0] = fs
                nxt = nxt_ref[b]

                @pl.when(nxt >= 0)
                def _():
                    for cp in weight_copies(nxt, 1 - fs):
                        cp.start()

            half = EXPERT_ROWS // 2
            top, rest = slice(r0, r0 + half), slice(r0 + half, r0 + EXPERT_ROWS)

            def masked_ffn(rows_slice):
                x = load_rows(rows_slice)
                rows = lax.broadcasted_iota(I32, x.shape, 0)
                store_rows(rows_slice, ffn(jnp.where(rows < nv, x, 0.0), e))

            @pl.when(jnp.logical_not(uniform) & (nv > half))
            def _():
                masked_ffn(rs)

            @pl.when(jnp.logical_not(uniform) & (nv > 0) & (nv <= half))
            def _():
                masked_ffn(top)
                ys_ref[:, rest, :] = jnp.zeros((ys_ref.shape[0], half, SC_ROW_WORDS), U32)

            @pl.when(nv == 0)
            def _():
                ys_ref[:, rs, :] = jnp.zeros((ys_ref.shape[0], EXPERT_ROWS, SC_ROW_WORDS), U32)


def _experts(layer, plan, xs, w_gu, b_gu, w_dn, b_dn):
    planes, P, _ = xs.shape
    _, E, D, F2 = w_gu.shape
    F = F2 // 2
    step_rows = EXPERT_STEP_BLOCKS * EXPERT_ROWS
    by_step = lambda i, be, nv, fs, nxt, uni: (0, i, 0)
    layer_all = lambda i, be, nv, fs, nxt, uni: (layer, 0, 0, 0)
    return pl.pallas_call(
        functools.partial(_expert_kernel, layer=layer),
        grid_spec=pltpu.PrefetchScalarGridSpec(
            num_scalar_prefetch=5,
            grid=(P // step_rows,),
            in_specs=[
                pl.BlockSpec((planes, step_rows, SC_ROW_WORDS), by_step),
                pl.BlockSpec(memory_space=pl.ANY),
                pl.BlockSpec((None, E, 1, F2), layer_all),
                pl.BlockSpec(memory_space=pl.ANY),
                pl.BlockSpec((None, E, 1, D), layer_all),
            ],
            out_specs=pl.BlockSpec((planes, step_rows, SC_ROW_WORDS), by_step),
            scratch_shapes=[
                pltpu.VMEM((2, D, F2), F32), pltpu.VMEM((2, F, D), F32),
                pltpu.SMEM((1,), I32),
                pltpu.SemaphoreType.DMA((2, 2)),
            ],
        ),
        out_shape=jax.ShapeDtypeStruct((planes, P, SC_ROW_WORDS), U32),
        compiler_params=pltpu.CompilerParams(
            dimension_semantics=("arbitrary",), vmem_limit_bytes=VMEM_LIMIT),
        name="experts",
    )(plan["block_e"], plan["block_nv"], plan["first_slot"], plan["next_e"], plan["step_uniform"],
      xs, w_gu, b_gu.reshape(-1, E, 1, F2), w_dn, b_dn.reshape(-1, E, 1, D))


def _combine_ple_kernel(h1_ref, yg_ref, gates_ref, gple_ref, wpg_ref, p_ref, wple_ref, gfin_ref, *rest, final):
    o_ref = rest[-1]
    for r in range(h1_ref.shape[0] // COMBINE_ROWS):
        rs = slice(r * COMBINE_ROWS, (r + 1) * COMBINE_ROWS)
        h2 = h1_ref[rs, :]
        gates = gates_ref[rs, :]
        for k in range(TOP_K):
            yk = jnp.concatenate([yg_ref[k, pn, rs, :] for pn in range(yg_ref.shape[1])], axis=1)
            h2 = h2 + gates[:, k:k + 1] * _unpack_rows(yk)
        xn = _rms(h2, gple_ref[...]).astype(BF16)
        gate = jax.nn.sigmoid(jnp.dot(xn, wpg_ref[...], preferred_element_type=F32))
        pe = jnp.dot(p_ref[rs, :].astype(BF16), wple_ref[...], preferred_element_type=F32)
        h3 = h2 + gate * pe
        if final:
            h3 = _rms(h3, gfin_ref[...])
        o_ref[rs, :] = h3


def _combine_ple(layer, group, h1, yg, gates, p, prm, g_final, final, h_prev):
    Tg, D = h1.shape
    T = Tg * MOE_GROUPS
    tm = min(ROW_TILE, Tg)
    PD = p.shape[2]
    planes = yg.shape[1]
    off = group * (Tg // tm)
    const = lambda shape: pl.BlockSpec(shape, lambda i: (0,) * len(shape))
    in_specs = [
        pl.BlockSpec((tm, D), lambda i: (i, 0)),
        pl.BlockSpec((TOP_K, planes, tm, SC_ROW_WORDS), lambda i: (0, 0, i, 0)),
        pl.BlockSpec((tm, 8), lambda i: (i, 0)),
        const((1, D)),
        const((D, D)),
        pl.BlockSpec((None, tm, PD), lambda i: (layer, off + i, 0)),
        const((PD, D)),
        const((1, D)),
    ]
    args = [h1, yg, gates, prm["g_ple"], prm["wpg"], p, prm["wple"], g_final]
    aliases = {}
    if h_prev is not None:
        in_specs.append(pl.BlockSpec(memory_space=pl.ANY))
        args.append(h_prev)
        aliases = {len(args) - 1: 0}
    return pl.pallas_call(
        functools.partial(_combine_ple_kernel, final=final),
        grid=(Tg // tm,),
        in_specs=in_specs,
        out_specs=pl.BlockSpec((tm, D), lambda i: (off + i, 0)),
        out_shape=jax.ShapeDtypeStruct((T, D), F32),
        input_output_aliases=aliases,
        compiler_params=pltpu.CompilerParams(
            dimension_semantics=("parallel",), vmem_limit_bytes=VMEM_LIMIT),
        name="combine_ple",
    )(*args)


def _layer_params(i, w):
    D = w["w_in"].shape[1]
    H = FOX_HEADS
    w_in = w["w_in"][i]
    c1 = 2 * GMLP_WIDTH
    c2, c3, c4 = c1 + FOX_WIDTH, c1 + 2 * FOX_WIDTH, c1 + 3 * FOX_WIDTH
    wk = jnp.zeros((D, H, HEAD_PAD), F32).at[:, :, :FOX_HEAD_DIM].set(
        w_in[:, c2:c3].reshape(D, H, FOX_HEAD_DIM)).reshape(D, H * HEAD_PAD)
    return {
        "g_mix": w["g_mix"][i].reshape(1, D),
        "wuv": w_in[:, :c1].astype(BF16),
        "wqt": (w_in[:, c1:c2] * (FOX_HEAD_DIM ** -0.5 * LOG2E)).T.astype(BF16),
        "wk": wk.astype(BF16),
        "wvt": w_in[:, c3:c4].T.astype(BF16),
        "wft": w_in[:, c4:].T.astype(BF16),
        "b_f": w["b_f"][i].reshape(H, 1),
        "ln_g": w["ln_g"][i].reshape(1, GMLP_WIDTH),
        "ln_b": w["ln_b"][i].reshape(1, GMLP_WIDTH),
        "w_s": w["w_s"][i],
        "b_s": w["b_s"][i].reshape(GMLP_GROUPS, GMLP_BLOCK, 1),
        "g_a": w["g_a"][i].reshape(1, GMLP_WIDTH),
        "woa": w["w_out"][i][:GMLP_WIDTH].astype(BF16),
        "wob": w["w_out"][i][GMLP_WIDTH:].astype(BF16),
        "g_b": w["g_b"][i].reshape(FOX_WIDTH, 1),
        "g_moe": w["g_moe"][i].reshape(1, D),
        "wrt": w["w_router"][i].T.astype(BF16),
        "b_router": w["b_router"][i].reshape(N_EXPERTS, 1),
        "g_ple": w["g_ple"][i].reshape(1, D),
        "wpg": w["w_ple_gate"][i].astype(BF16),
        "wple": w["w_ple"][i].astype(BF16),
    }


def _bias_constants(tm):
    H = FOX_HEADS
    j = jnp.arange(tm)
    tri_incl = (j[:, None] <= j[None, :]).astype(BF16)
    tri_strict = (j[:, None] < j[None, :]).astype(BF16)
    part = jnp.arange(BIAS_PARTS)[:, None]
    head = jnp.arange(H)[None, :]
    rows = (part * H + head).reshape(-1)
    cols = (head * HEAD_PAD + FOX_HEAD_DIM + BIAS_PARTS + part).reshape(-1)
    selk = jnp.zeros((32, H * HEAD_PAD), F32).at[rows, cols].set(-1.0).astype(BF16)
    one_cols = (jnp.arange(H)[:, None] * HEAD_PAD + FOX_HEAD_DIM + jnp.arange(BIAS_PARTS)[None, :]).reshape(-1)
    onesk = jnp.zeros((1, H * HEAD_PAD), F32).at[0, one_cols].set(1.0)
    return tri_incl, tri_strict, selk, onesk


def _moe_dispatch_plan(counts, n_blocks):
    padded = (counts + EXPERT_ROWS - 1) // EXPERT_ROWS * EXPERT_ROWS
    pad_ends = jnp.cumsum(padded)
    pad_starts = (pad_ends - padded).astype(I32)
    block_start = jnp.arange(n_blocks, dtype=I32) * EXPERT_ROWS
    n_before = jnp.sum((pad_ends[None, :] <= block_start[:, None]).astype(I32), axis=1)
    block_e = jnp.minimum(n_before, N_EXPERTS - 1).astype(I32)
    own = block_e[:, None] == jnp.arange(N_EXPERTS, dtype=I32)[None, :]
    seg_end = jnp.sum(jnp.where(own, (pad_starts + counts)[None, :], 0), axis=1)
    block_nv = jnp.clip(seg_end - block_start, 0, EXPERT_ROWS)
    block_nv = jnp.where(block_start < pad_ends[-1], block_nv, 0).astype(I32)
    prev_e = jnp.concatenate([jnp.full((1,), -1, I32), block_e[:-1]])
    first = (block_nv > 0) & (block_e != prev_e)
    run = jnp.cumsum(first.astype(I32)) - 1
    first_slot = jnp.where(first, run % 2, -1).astype(I32)
    ar = jnp.arange(N_EXPERTS, dtype=I32)
    later = (ar[None, :] > ar[:, None]) & (counts > 0)[None, :]
    next_of = jnp.min(jnp.where(later, ar[None, :], N_EXPERTS), axis=1)
    next_of = jnp.where(next_of < N_EXPERTS, next_of, -1)
    next_e = jnp.sum(jnp.where(own, next_of[None, :], 0), axis=1).astype(I32)
    plain = ((block_nv == EXPERT_ROWS) & jnp.logical_not(first)).reshape(-1, EXPERT_CHAIN_BLOCKS)
    step_uniform = jnp.all(plain, axis=1).astype(I32)
    return dict(pad_starts=pad_starts, block_e=block_e, block_nv=block_nv, first_slot=first_slot, next_e=next_e,
                step_uniform=step_uniform)


def kernel(x, p, g_mix, w_in, ln_g, ln_b, w_s, b_s, b_f, g_a, g_b, w_out, g_moe, w_router, b_router,
           w_gu, b_gu, w_dn, b_dn, g_ple, w_ple_gate, w_ple, g_final):
    B, S, D = x.shape
    T = B * S
    depth = w_in.shape[0]
    w = dict(g_mix=g_mix, w_in=w_in, ln_g=ln_g, ln_b=ln_b, w_s=w_s, b_s=b_s, b_f=b_f, g_a=g_a, g_b=g_b,
             w_out=w_out, g_moe=g_moe, w_router=w_router, b_router=b_router, w_gu=w_gu, b_gu=b_gu,
             w_dn=w_dn, b_dn=b_dn, g_ple=g_ple, w_ple_gate=w_ple_gate, w_ple=w_ple)
    tri_incl, tri_strict, selk, onesk = _bias_constants(min(ROW_TILE, S))
    G = MOE_GROUPS
    Tg = T // G
    n_blocks = -(-(Tg * TOP_K) // EXPERT_ROWS) + N_EXPERTS
    n_blocks = -(-n_blocks // EXPERT_STEP_BLOCKS) * EXPERT_STEP_BLOCKS
    P = n_blocks * EXPERT_ROWS
    planes = (D // 2) // SC_ROW_WORDS
    plane_base = (jnp.arange(planes, dtype=I32) * P)[None, :, None]
    p_rows = p.reshape(depth, T, p.shape[-1])

    h = x.reshape(T, D)
    for i in range(depth):
        prm = _layer_params(i, w)
        prm.update(tri_incl=tri_incl, selk=selk, onesk=onesk)
        routed = []
        idx = jnp.zeros((8, 128), I32)
        for g in range(G):
            ya, qt, kx, vt = _mix_in(g, h, B, S, prm, idx)
            ybt = _fox_attn(qt, kx, vt, B // G, S)
            h1, xp, tope, rank, gates, counts = _out_router(g, h, ya, ybt, B, S, prm, tri_strict)
            plan = _moe_dispatch_plan(counts[:, 0], n_blocks)
            dest = _dest_rows(plan["pad_starts"], tope, rank)[:TOP_K]
            idx = (dest[:, None, :] + plane_base).reshape(-1)
            xs = _sc_scatter_rows(xp.reshape(planes * Tg, SC_ROW_WORDS), idx, planes * P)
            routed.append((h1, gates, plan, idx, xs))
        h_out = None
        for g, (h1, gates, plan, idx, xs) in enumerate(routed):
            ys = _experts(i, plan, xs.reshape(planes, P, SC_ROW_WORDS), w_gu, b_gu, w_dn, b_dn)
            yg = _sc_gather_rows(ys.reshape(planes * P, SC_ROW_WORDS), idx)
            h_out = _combine_ple(i, g, h1, yg.reshape(TOP_K, planes, Tg, SC_ROW_WORDS), gates, p_rows, prm,
                                 g_final.reshape(1, D), i == depth - 1, h_out)
        h = h_out
    return h.reshape(B, S, D)
```

```python
import functools

import jax
import jax.numpy as jnp
from jax import lax
from jax.experimental import pallas as pl
from jax.experimental.pallas import tpu as pltpu
from jax.experimental.pallas import tpu_sc as plsc

F32 = jnp.float32
BF16 = jnp.bfloat16
I32 = jnp.int32
U32 = jnp.uint32

CHUNK = 64
GMLP_WIDTH = 512
GMLP_GROUPS = 4
GMLP_GROUP_CH = 128
GMLP_BLOCK = 128
FOX_WIDTH = 512
FOX_HEAD_DIM = 64
FOX_HEADS = 8
N_EXPERTS = 32
TOP_K = 4
EXPERT_ROWS = 256
SWIGLU_LIMIT = 7.0
SWIGLU_ALPHA = 1.702
RMS_EPS = 1e-5
LN_EPS = 1e-5

HEAD_PAD = 128
BIAS_PARTS = 3
ROW_TILE = 512
ATTN_TILE = 512
ATTN_HEADS = 1
ACC_ROWS = 80
LOG2E = 1.4426950408889634
SC_WINDOW = 128
SC_ROW_WORDS = 256
COMBINE_ROWS = 256
EXPERT_STEP_BLOCKS = 4
EXPERT_CHAIN_BLOCKS = 4
MOE_GROUPS = 2
VMEM_LIMIT = 56 * 1024 * 1024
NEG_BIG = -1e30

_NT = (((1,), (1,)), ((), ()))
_TN = (((0,), (0,)), ((), ()))


def _rms(x, g):
    return x * lax.rsqrt(jnp.mean(x * x, axis=-1, keepdims=True) + RMS_EPS) * g


def _split3(c):
    a1 = c.astype(BF16).astype(F32)
    r1 = c - a1
    a2 = r1.astype(BF16).astype(F32)
    a3 = (r1 - a2).astype(BF16).astype(F32)
    return a1, a2, a3


def _pack_rows(x):
    n = x.shape[1] // 2
    lo = lax.bitcast_convert_type(x[:, :n].astype(BF16).astype(F32), U32) >> 16
    hi = lax.bitcast_convert_type(x[:, n:].astype(BF16).astype(F32), U32) & jnp.uint32(0xFFFF0000)
    return lo | hi


def _unpack_rows(u):
    lo = lax.bitcast_convert_type(u << 16, F32)
    hi = lax.bitcast_convert_type(u & jnp.uint32(0xFFFF0000), F32)
    return jnp.concatenate([lo, hi], axis=1)


def _store_planes(ref, u):
    for j in range(ref.shape[0]):
        ref[j] = u[:, j * SC_ROW_WORDS:(j + 1) * SC_ROW_WORDS]


def _mix_in_kernel(h_ref, gmix_ref, wuv_ref, wqt_ref, wk_ref, wvt_ref, wft_ref, bf_ref, lng_ref, lnb_ref,
                   ws_ref, bs_ref, ga_ref, tri_ref, selk_ref, onesk_ref, after_ref,
                   ya_ref, qt_ref, k_ref, vt_ref, carry_ref, ya_scr):
    del after_ref
    tm = h_ref.shape[0]

    @pl.when(pl.program_id(1) == 0)
    def _():
        carry_ref[...] = jnp.zeros_like(carry_ref)

    xn = _rms(h_ref[...], gmix_ref[...]).astype(BF16)

    z = jnp.dot(xn, wuv_ref[...], preferred_element_type=F32)
    qt = lax.dot_general(wqt_ref[...], xn, _NT, preferred_element_type=F32)
    vt_ref[0] = lax.dot_general(wvt_ref[...], xn, _NT, preferred_element_type=F32).astype(BF16)
    kp = jnp.dot(xn, wk_ref[...], preferred_element_type=F32)
    ft = lax.dot_general(wft_ref[...], xn, _NT, preferred_element_type=F32) + bf_ref[...]

    z = 0.5 * z * (1.0 + lax.erf(z * (0.5 ** 0.5)))
    u = z[:, :GMLP_WIDTH]
    v = z[:, GMLP_WIDTH:]
    mu = jnp.mean(v, axis=-1, keepdims=True)
    vc = v - mu
    var = jnp.mean(vc * vc, axis=-1, keepdims=True)
    vn = (vc * lax.rsqrt(var + LN_EPS) * lng_ref[...] + lnb_ref[...]).astype(BF16)
    pi = lax.broadcasted_iota(I32, (GMLP_BLOCK, GMLP_BLOCK), 0) // CHUNK
    pj = lax.broadcasted_iota(I32, (GMLP_BLOCK, GMLP_BLOCK), 1) // CHUNK
    for g in range(GMLP_GROUPS):
        wm = jnp.where(pj <= pi, ws_ref[g], 0.0).astype(BF16)
        cs = slice(g * GMLP_GROUP_CH, (g + 1) * GMLP_GROUP_CH)
        for n in range(tm // GMLP_BLOCK):
            rs = slice(n * GMLP_BLOCK, (n + 1) * GMLP_BLOCK)
            sv = jnp.dot(wm, vn[rs, cs], preferred_element_type=F32) + bs_ref[g]
            ya_scr[rs, cs] = u[rs, cs] * sv
    ya_ref[...] = _rms(ya_scr[...], ga_ref[...]).astype(BF16)

    lf = (jnp.minimum(ft, 0.0) - jnp.log1p(jnp.exp(-jnp.abs(ft)))) * LOG2E
    zpad = jnp.zeros((32 - BIAS_PARTS * FOX_HEADS, tm), F32)
    stack = jnp.concatenate(list(_split3(lf)) + [zpad], axis=0).astype(BF16)
    cum = jnp.dot(stack, tri_ref[...], preferred_element_type=F32)
    c = cum[0:8] + cum[8:16] + cum[16:24] + carry_ref[:, 0:1]
    carry_ref[...] = jnp.broadcast_to(c[:, tm - 1:tm], carry_ref.shape)
    a1, a2, a3 = _split3(c)

    si = lax.broadcasted_iota(I32, (8, tm), 0)
    zrows = jnp.zeros((HEAD_PAD - FOX_HEAD_DIM - 8, tm), F32)
    for hd in range(FOX_HEADS):
        b1 = jnp.broadcast_to(a1[hd:hd + 1], (8, tm))
        b2 = jnp.broadcast_to(a2[hd:hd + 1], (8, tm))
        b3 = jnp.broadcast_to(a3[hd:hd + 1], (8, tm))
        ext = jnp.where(si == 0, b1, jnp.where(si == 1, b2, jnp.where(si == 2, b3,
                        jnp.where(si < 2 * BIAS_PARTS, 1.0, 0.0))))
        blk = jnp.concatenate([qt[hd * FOX_HEAD_DIM:(hd + 1) * FOX_HEAD_DIM], ext, zrows], axis=0)
        qt_ref[0, hd * HEAD_PAD:(hd + 1) * HEAD_PAD, :] = blk.astype(BF16)

    cstack = jnp.concatenate([a1, a2, a3, zpad], axis=0).astype(BF16)
    kext = lax.dot_general(cstack, selk_ref[...], _TN, preferred_element_type=F32)
    k_ref[...] = (kp + kext + onesk_ref[...]).astype(BF16)


def _mix_in(group, h, B, S, prm, after):
    D = h.shape[1]
    Bg = B // MOE_GROUPS
    Tg = Bg * S
    tm = min(ROW_TILE, S)
    ns = S // tm
    H = FOX_HEADS
    const = lambda shape: pl.BlockSpec(shape, lambda b, s: (0,) * len(shape))
    return pl.pallas_call(
        _mix_in_kernel,
        grid=(Bg, ns),
        in_specs=[
            pl.BlockSpec((tm, D), lambda b, s: ((group * Bg + b) * ns + s, 0)),
            const((1, D)),
            const((D, 2 * GMLP_WIDTH)),
            const((FOX_WIDTH, D)),
            const((D, H * HEAD_PAD)),
            const((FOX_WIDTH, D)),
            const((H, D)),
            const((H, 1)),
            const((1, GMLP_WIDTH)),
            const((1, GMLP_WIDTH)),
            const((GMLP_GROUPS, GMLP_BLOCK, GMLP_BLOCK)),
            const((GMLP_GROUPS, GMLP_BLOCK, 1)),
            const((1, GMLP_WIDTH)),
            const((tm, tm)),
            const((32, H * HEAD_PAD)),
            const((1, H * HEAD_PAD)),
            pl.BlockSpec(memory_space=pl.ANY),
        ],
        out_specs=[
            pl.BlockSpec((tm, GMLP_WIDTH), lambda b, s: (b * ns + s, 0)),
            pl.BlockSpec((1, H * HEAD_PAD, tm), lambda b, s: (b, 0, s)),
            pl.BlockSpec((tm, H * HEAD_PAD), lambda b, s: (b * ns + s, 0)),
            pl.BlockSpec((1, FOX_WIDTH, tm), lambda b, s: (b, 0, s)),
        ],
        out_shape=[
            jax.ShapeDtypeStruct((Tg, GMLP_WIDTH), BF16),
            jax.ShapeDtypeStruct((Bg, H * HEAD_PAD, S), BF16),
            jax.ShapeDtypeStruct((Tg, H * HEAD_PAD), BF16),
            jax.ShapeDtypeStruct((Bg, FOX_WIDTH, S), BF16),
        ],
        scratch_shapes=[pltpu.VMEM((H, 128), F32), pltpu.VMEM((tm, GMLP_WIDTH), F32)],
        compiler_params=pltpu.CompilerParams(
            dimension_semantics=("arbitrary", "arbitrary"), vmem_limit_bytes=VMEM_LIMIT),
        name="mix_in",
    )(h, prm["g_mix"], prm["wuv"], prm["wqt"], prm["wk"], prm["wvt"], prm["wft"], prm["b_f"],
      prm["ln_g"], prm["ln_b"], prm["w_s"], prm["b_s"], prm["g_a"], prm["tri_incl"], prm["selk"], prm["onesk"],
      after)


def _fox_attn_kernel(qt_ref, k_ref, vt_ref, ot_ref):
    S = k_ref.shape[0]
    t = min(ATTN_TILE, S)
    nq = S // t
    hf = t // 2
    ones_rows = jnp.ones((ACC_ROWS - FOX_HEAD_DIM, t), BF16)
    pairs = [(qi, kj) for qi in range(nq) for kj in range(qi + 1)]

    def tri(rows, cols):
        return lax.broadcasted_iota(I32, (rows, cols), 0) <= lax.broadcasted_iota(I32, (rows, cols), 1)

    def scores(qi, kj):
        out = []
        for hh in range(ATTN_HEADS):
            hs = slice(hh * HEAD_PAD, (hh + 1) * HEAD_PAD)
            if kj < qi:
                out.append([jnp.dot(k_ref[kj * t:(kj + 1) * t, hs], qt_ref[0, hs, qi * t:(qi + 1) * t],
                                    preferred_element_type=F32)])
            else:
                top = jnp.dot(k_ref[kj * t:kj * t + hf, hs], qt_ref[0, hs, qi * t:(qi + 1) * t],
                              preferred_element_type=F32)
                bot = jnp.dot(k_ref[kj * t + hf:(kj + 1) * t, hs], qt_ref[0, hs, qi * t + hf:(qi + 1) * t],
                              preferred_element_type=F32)
                out.append([jnp.where(tri(hf, t), top, NEG_BIG), jnp.where(tri(hf, hf), bot, NEG_BIG)])
        return out

    s_next = scores(*pairs[0])
    state = None
    for n, (qi, kj) in enumerate(pairs):
        s_cur = s_next
        if n + 1 < len(pairs):
            s_next = scores(*pairs[n + 1])
        new_state = []
        for hh in range(ATTN_HEADS):
            parts = s_cur[hh]
            v = vt_ref[0, hh * FOX_HEAD_DIM:(hh + 1) * FOX_HEAD_DIM, kj * t:(kj + 1) * t]
            v_aug = jnp.concatenate([v, ones_rows], axis=0)
            mx = jnp.max(parts[0], axis=0, keepdims=True)
            if len(parts) == 2:
                mx = jnp.concatenate(
                    [mx[:, :hf], jnp.maximum(mx[:, hf:], jnp.max(parts[1], axis=0, keepdims=True))], axis=1)
            m_new = mx if kj == 0 else jnp.maximum(state[hh][0], mx)
            if len(parts) == 1:
                pv = jnp.dot(v_aug, jnp.exp2(parts[0] - m_new).astype(BF16), preferred_element_type=F32)
            else:
                pv = jnp.dot(v_aug[:, :hf], jnp.exp2(parts[0] - m_new).astype(BF16), preferred_element_type=F32)
                pv_r = jnp.dot(v_aug[:, hf:], jnp.exp2(parts[1] - m_new[:, hf:]).astype(BF16),
                               preferred_element_type=F32)
                pv = jnp.concatenate([pv[:, :hf], pv[:, hf:] + pv_r], axis=1)
            acc = pv if kj == 0 else jnp.exp2(state[hh][0] - m_new) * state[hh][1] + pv
            new_state.append((m_new, acc))
        state = new_state
        if kj == qi:
            for hh in range(ATTN_HEADS):
                acc = state[hh][1]
                o = acc[:FOX_HEAD_DIM] / acc[FOX_HEAD_DIM:FOX_HEAD_DIM + 1]
                ot_ref[0, hh * FOX_HEAD_DIM:(hh + 1) * FOX_HEAD_DIM, qi * t:(qi + 1) * t] = o.astype(BF16)


def _fox_attn(qt, kx, vt, B, S):
    G = ATTN_HEADS
    return pl.pallas_call(
        _fox_attn_kernel,
        grid=(B, FOX_HEADS // G),
        in_specs=[
            pl.BlockSpec((1, G * HEAD_PAD, S), lambda b, h: (b, h, 0)),
            pl.BlockSpec((S, G * HEAD_PAD), lambda b, h: (b, h)),
            pl.BlockSpec((1, G * FOX_HEAD_DIM, S), lambda b, h: (b, h, 0)),
        ],
        out_specs=pl.BlockSpec((1, G * FOX_HEAD_DIM, S), lambda b, h: (b, h, 0)),
        out_shape=jax.ShapeDtypeStruct((B, FOX_WIDTH, S), BF16),
        compiler_params=pltpu.CompilerParams(
            dimension_semantics=("parallel", "parallel"), vmem_limit_bytes=VMEM_LIMIT),
        name="fox_attn",
    )(qt, kx, vt)


def _out_router_kernel(h_ref, ya_ref, ybt_ref, woa_ref, wob_ref, gb_ref, gmoe_ref, wrt_ref, br_ref, tri_ref,
                       h1_ref, xp_ref, tope_ref, rank_ref, gates_ref, counts_ref, carry_ref):
    tm = h_ref.shape[0]

    @pl.when(pl.program_id(0) == 0)
    def _():
        carry_ref[...] = jnp.zeros_like(carry_ref)

    yb = ybt_ref[0].astype(F32)
    ybn = (yb * lax.rsqrt(jnp.mean(yb * yb, axis=0, keepdims=True) + RMS_EPS) * gb_ref[...]).astype(BF16)
    y = jnp.dot(ya_ref[...], woa_ref[...], preferred_element_type=F32)
    y = y + lax.dot_general(ybn, wob_ref[...], _TN, preferred_element_type=F32)
    h1 = h_ref[...] + y
    h1_ref[...] = h1
    xn = _rms(h1, gmoe_ref[...])
    _store_planes(xp_ref, _pack_rows(xn))

    logits = lax.dot_general(wrt_ref[...], xn.astype(BF16), _NT, preferred_element_type=F32) + br_ref[...]
    ri = lax.broadcasted_iota(I32, (N_EXPERTS, tm), 0)
    vals, idxs = [], []
    l = logits
    for _ in range(TOP_K):
        m = jnp.max(l, axis=0, keepdims=True)
        idx = jnp.min(jnp.where(l == m, ri, N_EXPERTS), axis=0, keepdims=True)
        vals.append(m)
        idxs.append(idx)
        l = jnp.where(ri == idx, -jnp.inf, l)
    es = [jnp.exp(vk - vals[0]) for vk in vals]
    den = es[0] + es[1] + es[2] + es[3]
    zrow = jnp.zeros((8 - TOP_K, tm), F32)
    gates_ref[...] = jnp.concatenate([e / den for e in es] + [zrow], axis=0).T

    sel = jnp.zeros((N_EXPERTS, tm), F32)
    for idx in idxs:
        sel = sel + jnp.where(ri == idx, 1.0, 0.0)
    cnt = jnp.dot(sel.astype(BF16), tri_ref[...], preferred_element_type=F32) + carry_ref[:, 0:1]
    ranks = [jnp.sum(jnp.where(ri == idx, cnt, 0.0), axis=0, keepdims=True) for idx in idxs]
    zi = jnp.zeros((8 - TOP_K, tm), I32)
    rank_ref[...] = jnp.concatenate([r.astype(I32) for r in ranks] + [zi], axis=0)
    tope_ref[...] = jnp.concatenate(idxs + [zi], axis=0)
    total = carry_ref[...] + jnp.sum(sel, axis=1, keepdims=True)
    carry_ref[...] = total
    counts_ref[...] = total.astype(I32)


def _out_router(group, h, ya, ybt, B, S, prm, tri_strict):
    T, D = h.shape
    Tg = T // MOE_GROUPS
    tm = min(ROW_TILE, S)
    ns = S // tm
    E = N_EXPERTS
    planes = (D // 2) // SC_ROW_WORDS
    off = group * (Tg // tm)
    const = lambda shape: pl.BlockSpec(shape, lambda i: (0,) * len(shape))
    return pl.pallas_call(
        _out_router_kernel,
        grid=(Tg // tm,),
        in_specs=[
            pl.BlockSpec((tm, D), lambda i: (off + i, 0)),
            pl.BlockSpec((tm, GMLP_WIDTH), lambda i: (i, 0)),
            pl.BlockSpec((1, FOX_WIDTH, tm), lambda i: (i // ns, 0, i % ns)),
            const((GMLP_WIDTH, D)),
            const((FOX_WIDTH, D)),
            const((FOX_WIDTH, 1)),
            const((1, D)),
            const((E, D)),
            const((E, 1)),
            const((tm, tm)),
        ],
        out_specs=[
            pl.BlockSpec((tm, D), lambda i: (i, 0)),
            pl.BlockSpec((planes, tm, SC_ROW_WORDS), lambda i: (0, i, 0)),
            pl.BlockSpec((8, tm), lambda i: (0, i)),
            pl.BlockSpec((8, tm), lambda i: (0, i)),
            pl.BlockSpec((tm, 8), lambda i: (i, 0)),
            const((E, 128)),
        ],
        out_shape=[
            jax.ShapeDtypeStruct((Tg, D), F32),
            jax.ShapeDtypeStruct((planes, Tg, SC_ROW_WORDS), U32),
            jax.ShapeDtypeStruct((8, Tg), I32),
            jax.ShapeDtypeStruct((8, Tg), I32),
            jax.ShapeDtypeStruct((Tg, 8), F32),
            jax.ShapeDtypeStruct((E, 128), I32),
        ],
        scratch_shapes=[pltpu.VMEM((E, 128), F32)],
        compiler_params=pltpu.CompilerParams(
            dimension_semantics=("arbitrary",), vmem_limit_bytes=VMEM_LIMIT),
        name="out_router",
    )(h, ya, ybt, prm["woa"], prm["wob"], prm["g_b"], prm["g_moe"], prm["wrt"], prm["b_router"], tri_strict)


def _dest_kernel(ps_ref, tope_ref, rank_ref, dest_ref):
    e_sel = tope_ref[...]
    d = rank_ref[...]
    for e in range(N_EXPERTS):
        d = d + jnp.where(e_sel == e, ps_ref[e], 0)
    dest_ref[...] = d


def _dest_rows(pad_starts, tope, rank):
    T = tope.shape[1]
    tc = min(4096, T)
    return pl.pallas_call(
        _dest_kernel,
        grid_spec=pltpu.PrefetchScalarGridSpec(
            num_scalar_prefetch=1,
            grid=(T // tc,),
            in_specs=[pl.BlockSpec((8, tc), lambda i, ps: (0, i)),
                      pl.BlockSpec((8, tc), lambda i, ps: (0, i))],
            out_specs=pl.BlockSpec((8, tc), lambda i, ps: (0, i)),
        ),
        out_shape=jax.ShapeDtypeStruct((8, T), I32),
        name="dest_rows",
    )(pad_starts, tope, rank)


def _sc_mesh():
    return plsc.VectorSubcoreMesh(core_axis_name="c", subcore_axis_name="s")


def _sc_scatter_rows(x, idx, n_out):
    R, W = x.shape
    N = idx.shape[0]
    nsrc = R // SC_WINDOW
    copies = N // R

    @functools.partial(pl.kernel, out_type=jax.ShapeDtypeStruct((n_out, W), x.dtype), mesh=_sc_mesh())
    def k(x_hbm, i_hbm, o_hbm):
        def body(x_vmem, *i_vmems):
            for i_vmem in i_vmems:
                pltpu.sync_copy(x_vmem, o_hbm.at[i_vmem.at[0]])

        idx_spec = lambda c: pl.BlockSpec((1, SC_WINDOW), lambda i: (0, c * nsrc + i))
        pltpu.emit_pipeline(
            body,
            grid=(nsrc,),
            in_specs=[pl.BlockSpec((SC_WINDOW, W), lambda i: (i, 0))] + [idx_spec(c) for c in range(copies)],
            out_specs=[],
            core_axis_name=("c", "s"),
            dimension_semantics=(pltpu.PARALLEL,),
        )(x_hbm, *([i_hbm] * copies))

    return k(x, idx.reshape(1, N))


def _sc_gather_rows(table, idx):
    V, W = table.shape
    N = idx.shape[0]

    @functools.partial(pl.kernel, out_type=jax.ShapeDtypeStruct((N, W), table.dtype), mesh=_sc_mesh())
    def k(t_hbm, i_hbm, o_hbm):
        def body(i_vmem, o_vmem):
            pltpu.sync_copy(t_hbm.at[i_vmem.at[0]], o_vmem)

        pltpu.emit_pipeline(
            body,
            grid=(N // SC_WINDOW,),
            in_specs=[pl.BlockSpec((1, SC_WINDOW), lambda i: (0, i))],
            out_specs=[pl.BlockSpec((SC_WINDOW, W), lambda i: (i, 0))],
            core_axis_name=("c", "s"),
            dimension_semantics=(pltpu.PARALLEL,),
        )(i_hbm, o_hbm)

    return k(table, idx.reshape(1, N))


def _expert_kernel(be_ref, nv_ref, fs_ref, nxt_ref, uni_ref, xs_ref, wgu_hbm, bgu_ref, wdn_hbm, bdn_ref, ys_ref,
                   wgu_f32, wdn_f32, cur_slot, sem, *, layer):
    F = wdn_f32.shape[1]

    def weight_copies(e, slot):
        return (pltpu.make_async_copy(wgu_hbm.at[layer, e], wgu_f32.at[slot], sem.at[slot, 0]),
                pltpu.make_async_copy(wdn_hbm.at[layer, e], wdn_f32.at[slot], sem.at[slot, 1]))

    @pl.when(pl.program_id(0) == 0)
    def _():
        for cp in weight_copies(be_ref[0], 0):
            cp.start()

    def ffn(x, e):
        slot = cur_slot[0]
        hgu = jnp.dot(x, wgu_f32[slot], preferred_element_type=F32) + bgu_ref[e]
        g = jnp.minimum(hgu[:, :F], SWIGLU_LIMIT)
        lin = jnp.clip(hgu[:, F:], -SWIGLU_LIMIT, SWIGLU_LIMIT)
        act = g * jax.nn.sigmoid(SWIGLU_ALPHA * g) * (lin + 1.0)
        return _pack_rows(jnp.dot(act, wdn_f32[slot], preferred_element_type=F32) + bdn_ref[e])

    def load_rows(rs):
        return _unpack_rows(jnp.concatenate([xs_ref[pn, rs, :] for pn in range(xs_ref.shape[0])], axis=1))

    def store_rows(rs, y):
        for pn in range(ys_ref.shape[0]):
            ys_ref[pn, rs, :] = y[:, pn * SC_ROW_WORDS:(pn + 1) * SC_ROW_WORDS]

    for c in range(EXPERT_STEP_BLOCKS // EXPERT_CHAIN_BLOCKS):
        chain = pl.program_id(0) * (EXPERT_STEP_BLOCKS // EXPERT_CHAIN_BLOCKS) + c
        uniform = uni_ref[chain] > 0

        @pl.when(uniform)
        def _():
            rs = slice(c * EXPERT_CHAIN_BLOCKS * EXPERT_ROWS, (c + 1) * EXPERT_CHAIN_BLOCKS * EXPERT_ROWS)
            store_rows(rs, ffn(load_rows(rs), be_ref[chain * EXPERT_CHAIN_BLOCKS]))

        for j in range(EXPERT_CHAIN_BLOCKS):
            b = chain * EXPERT_CHAIN_BLOCKS + j
            r0 = (c * EXPERT_CHAIN_BLOCKS + j) * EXPERT_ROWS
            rs = slice(r0, r0 + EXPERT_ROWS)
            e = be_ref[b]
            nv = nv_ref[b]
            fs = fs_ref[b]

            @pl.when(fs >= 0)
            def _():
                for cp in weight_copies(e, fs):
                    cp.wait()
                cur_slot[0] = fs
                nxt = nxt_ref[b]

                @pl.when(nxt >= 0)
                def _():
                    for cp in weight_copies(nxt, 1 - fs):
                        cp.start()

            @pl.when(jnp.logical_not(uniform) & (nv > 0))
            def _():
                x = load_rows(rs)
                rows = lax.broadcasted_iota(I32, x.shape, 0)
                store_rows(rs, ffn(jnp.where(rows < nv, x, 0.0), e))

            @pl.when(nv == 0)
            def _():
                ys_ref[:, rs, :] = jnp.zeros((ys_ref.shape[0], EXPERT_ROWS, SC_ROW_WORDS), U32)


def _experts(layer, plan, xs, w_gu, b_gu, w_dn, b_dn):
    planes, P, _ = xs.shape
    _, E, D, F2 = w_gu.shape
    F = F2 // 2
    step_rows = EXPERT_STEP_BLOCKS * EXPERT_ROWS
    by_step = lambda i, be, nv, fs, nxt, uni: (0, i, 0)
    layer_all = lambda i, be, nv, fs, nxt, uni: (layer, 0, 0, 0)
    return pl.pallas_call(
        functools.partial(_expert_kernel, layer=layer),
        grid_spec=pltpu.PrefetchScalarGridSpec(
            num_scalar_prefetch=5,
            grid=(P // step_rows,),
            in_specs=[
                pl.BlockSpec((planes, step_rows, SC_ROW_WORDS), by_step),
                pl.BlockSpec(memory_space=pl.ANY),
                pl.BlockSpec((None, E, 1, F2), layer_all),
                pl.BlockSpec(memory_space=pl.ANY),
                pl.BlockSpec((None, E, 1, D), layer_all),
            ],
            out_specs=pl.BlockSpec((planes, step_rows, SC_ROW_WORDS), by_step),
            scratch_shapes=[
                pltpu.VMEM((2, D, F2), F32), pltpu.VMEM((2, F, D), F32),
                pltpu.SMEM((1,), I32),
                pltpu.SemaphoreType.DMA((2, 2)),
            ],
        ),
        out_shape=jax.ShapeDtypeStruct((planes, P, SC_ROW_WORDS), U32),
        compiler_params=pltpu.CompilerParams(
            dimension_semantics=("arbitrary",), vmem_limit_bytes=VMEM_LIMIT),
        name="experts",
    )(plan["block_e"], plan["block_nv"], plan["first_slot"], plan["next_e"], plan["step_uniform"],
      xs, w_gu, b_gu.reshape(-1, E, 1, F2), w_dn, b_dn.reshape(-1, E, 1, D))


def _combine_ple_kernel(h1_ref, yg_ref, gates_ref, gple_ref, wpg_ref, p_ref, wple_ref, gfin_ref, *rest, final):
    o_ref = rest[-1]
    for r in range(h1_ref.shape[0] // COMBINE_ROWS):
        rs = slice(r * COMBINE_ROWS, (r + 1) * COMBINE_ROWS)
        h2 = h1_ref[rs, :]
        gates = gates_ref[rs, :]
        for k in range(TOP_K):
            yk = jnp.concatenate([yg_ref[k, pn, rs, :] for pn in range(yg_ref.shape[1])], axis=1)
            h2 = h2 + gates[:, k:k + 1] * _unpack_rows(yk)
        xn = _rms(h2, gple_ref[...]).astype(BF16)
        gate = jax.nn.sigmoid(jnp.dot(xn, wpg_ref[...], preferred_element_type=F32))
        pe = jnp.dot(p_ref[rs, :].astype(BF16), wple_ref[...], preferred_element_type=F32)
        h3 = h2 + gate * pe
        if final:
            h3 = _rms(h3, gfin_ref[...])
        o_ref[rs, :] = h3


def _combine_ple(layer, group, h1, yg, gates, p, prm, g_final, final, h_prev):
    Tg, D = h1.shape
    T = Tg * MOE_GROUPS
    tm = min(ROW_TILE, Tg)
    PD = p.shape[2]
    planes = yg.shape[1]
    off = group * (Tg // tm)
    const = lambda shape: pl.BlockSpec(shape, lambda i: (0,) * len(shape))
    in_specs = [
        pl.BlockSpec((tm, D), lambda i: (i, 0)),
        pl.BlockSpec((TOP_K, planes, tm, SC_ROW_WORDS), lambda i: (0, 0, i, 0)),
        pl.BlockSpec((tm, 8), lambda i: (i, 0)),
        const((1, D)),
        const((D, D)),
        pl.BlockSpec((None, tm, PD), lambda i: (layer, off + i, 0)),
        const((PD, D)),
        const((1, D)),
    ]
    args = [h1, yg, gates, prm["g_ple"], prm["wpg"], p, prm["wple"], g_final]
    aliases = {}
    if h_prev is not None:
        in_specs.append(pl.BlockSpec(memory_space=pl.ANY))
        args.append(h_prev)
        aliases = {len(args) - 1: 0}
    return pl.pallas_call(
        functools.partial(_combine_ple_kernel, final=final),
        grid=(Tg // tm,),
        in_specs=in_specs,
        out_specs=pl.BlockSpec((tm, D), lambda i: (off + i, 0)),
        out_shape=jax.ShapeDtypeStruct((T, D), F32),
        input_output_aliases=aliases,
        compiler_params=pltpu.CompilerParams(
            dimension_semantics=("parallel",), vmem_limit_bytes=VMEM_LIMIT),
        name="combine_ple",
    )(*args)


def _layer_params(i, w):
    D = w["w_in"].shape[1]
    H = FOX_HEADS
    w_in = w["w_in"][i]
    c1 = 2 * GMLP_WIDTH
    c2, c3, c4 = c1 + FOX_WIDTH, c1 + 2 * FOX_WIDTH, c1 + 3 * FOX_WIDTH
    wk = jnp.zeros((D, H, HEAD_PAD), F32).at[:, :, :FOX_HEAD_DIM].set(
        w_in[:, c2:c3].reshape(D, H, FOX_HEAD_DIM)).reshape(D, H * HEAD_PAD)
    return {
        "g_mix": w["g_mix"][i].reshape(1, D),
        "wuv": w_in[:, :c1].astype(BF16),
        "wqt": (w_in[:, c1:c2] * (FOX_HEAD_DIM ** -0.5 * LOG2E)).T.astype(BF16),
        "wk": wk.astype(BF16),
        "wvt": w_in[:, c3:c4].T.astype(BF16),
        "wft": w_in[:, c4:].T.astype(BF16),
        "b_f": w["b_f"][i].reshape(H, 1),
        "ln_g": w["ln_g"][i].reshape(1, GMLP_WIDTH),
        "ln_b": w["ln_b"][i].reshape(1, GMLP_WIDTH),
        "w_s": w["w_s"][i],
        "b_s": w["b_s"][i].reshape(GMLP_GROUPS, GMLP_BLOCK, 1),
        "g_a": w["g_a"][i].reshape(1, GMLP_WIDTH),
        "woa": w["w_out"][i][:GMLP_WIDTH].astype(BF16),
        "wob": w["w_out"][i][GMLP_WIDTH:].astype(BF16),
        "g_b": w["g_b"][i].reshape(FOX_WIDTH, 1),
        "g_moe": w["g_moe"][i].reshape(1, D),
        "wrt": w["w_router"][i].T.astype(BF16),
        "b_router": w["b_router"][i].reshape(N_EXPERTS, 1),
        "g_ple": w["g_ple"][i].reshape(1, D),
        "wpg": w["w_ple_gate"][i].astype(BF16),
        "wple": w["w_ple"][i].astype(BF16),
    }


def _bias_constants(tm):
    H = FOX_HEADS
    j = jnp.arange(tm)
    tri_incl = (j[:, None] <= j[None, :]).astype(BF16)
    tri_strict = (j[:, None] < j[None, :]).astype(BF16)
    part = jnp.arange(BIAS_PARTS)[:, None]
    head = jnp.arange(H)[None, :]
    rows = (part * H + head).reshape(-1)
    cols = (head * HEAD_PAD + FOX_HEAD_DIM + BIAS_PARTS + part).reshape(-1)
    selk = jnp.zeros((32, H * HEAD_PAD), F32).at[rows, cols].set(-1.0).astype(BF16)
    one_cols = (jnp.arange(H)[:, None] * HEAD_PAD + FOX_HEAD_DIM + jnp.arange(BIAS_PARTS)[None, :]).reshape(-1)
    onesk = jnp.zeros((1, H * HEAD_PAD), F32).at[0, one_cols].set(1.0)
    return tri_incl, tri_strict, selk, onesk


def _moe_dispatch_plan(counts, n_blocks):
    padded = (counts + EXPERT_ROWS - 1) // EXPERT_ROWS * EXPERT_ROWS
    pad_ends = jnp.cumsum(padded)
    pad_starts = (pad_ends - padded).astype(I32)
    block_start = jnp.arange(n_blocks, dtype=I32) * EXPERT_ROWS
    n_before = jnp.sum((pad_ends[None, :] <= block_start[:, None]).astype(I32), axis=1)
    block_e = jnp.minimum(n_before, N_EXPERTS - 1).astype(I32)
    own = block_e[:, None] == jnp.arange(N_EXPERTS, dtype=I32)[None, :]
    seg_end = jnp.sum(jnp.where(own, (pad_starts + counts)[None, :], 0), axis=1)
    block_nv = jnp.clip(seg_end - block_start, 0, EXPERT_ROWS)
    block_nv = jnp.where(block_start < pad_ends[-1], block_nv, 0).astype(I32)
    prev_e = jnp.concatenate([jnp.full((1,), -1, I32), block_e[:-1]])
    first = (block_nv > 0) & (block_e != prev_e)
    run = jnp.cumsum(first.astype(I32)) - 1
    first_slot = jnp.where(first, run % 2, -1).astype(I32)
    ar = jnp.arange(N_EXPERTS, dtype=I32)
    later = (ar[None, :] > ar[:, None]) & (counts > 0)[None, :]
    next_of = jnp.min(jnp.where(later, ar[None, :], N_EXPERTS), axis=1)
    next_of = jnp.where(next_of < N_EXPERTS, next_of, -1)
    next_e = jnp.sum(jnp.where(own, next_of[None, :], 0), axis=1).astype(I32)
    plain = ((block_nv == EXPERT_ROWS) & jnp.logical_not(first)).reshape(-1, EXPERT_CHAIN_BLOCKS)
    step_uniform = jnp.all(plain, axis=1).astype(I32)
    return dict(pad_starts=pad_starts, block_e=block_e, block_nv=block_nv, first_slot=first_slot, next_e=next_e,
                step_uniform=step_uniform)


def kernel(x, p, g_mix, w_in, ln_g, ln_b, w_s, b_s, b_f, g_a, g_b, w_out, g_moe, w_router, b_router,
           w_gu, b_gu, w_dn, b_dn, g_ple, w_ple_gate, w_ple, g_final):
    B, S, D = x.shape
    T = B * S
    depth = w_in.shape[0]
    assert B % MOE_GROUPS == 0 and S % min(ROW_TILE, S) == 0 and S % min(ATTN_TILE, S) == 0
    assert (T // MOE_GROUPS) % SC_WINDOW == 0 and D % (2 * SC_ROW_WORDS) == 0
    w = dict(g_mix=g_mix, w_in=w_in, ln_g=ln_g, ln_b=ln_b, w_s=w_s, b_s=b_s, b_f=b_f, g_a=g_a, g_b=g_b,
             w_out=w_out, g_moe=g_moe, w_router=w_router, b_router=b_router,
             g_ple=g_ple, w_ple_gate=w_ple_gate, w_ple=w_ple)
    tri_incl, tri_strict, selk, onesk = _bias_constants(min(ROW_TILE, S))
    G = MOE_GROUPS
    Tg = T // G
    n_blocks = -(-(Tg * TOP_K) // EXPERT_ROWS) + N_EXPERTS
    n_blocks = -(-n_blocks // EXPERT_STEP_BLOCKS) * EXPERT_STEP_BLOCKS
    P = n_blocks * EXPERT_ROWS
    planes = (D // 2) // SC_ROW_WORDS
    plane_base = (jnp.arange(planes, dtype=I32) * P)[None, :, None]
    p_rows = p.reshape(depth, T, p.shape[-1])

    h = x.reshape(T, D)
    for i in range(depth):
        prm = _layer_params(i, w)
        prm.update(tri_incl=tri_incl, selk=selk, onesk=onesk)
        routed = []
        idx = jnp.zeros((8, 128), I32)
        for g in range(G):
            ya, qt, kx, vt = _mix_in(g, h, B, S, prm, idx)
            ybt = _fox_attn(qt, kx, vt, B // G, S)
            h1, xp, tope, rank, gates, counts = _out_router(g, h, ya, ybt, B, S, prm, tri_strict)
            plan = _moe_dispatch_plan(counts[:, 0], n_blocks)
            dest = _dest_rows(plan["pad_starts"], tope, rank)[:TOP_K]
            idx = (dest[:, None, :] + plane_base).reshape(-1)
            xs = _sc_scatter_rows(xp.reshape(planes * Tg, SC_ROW_WORDS), idx, planes * P)
            routed.append((h1, gates, plan, idx, xs))
        h_out = None
        for g, (h1, gates, plan, idx, xs) in enumerate(routed):
            ys = _experts(i, plan, xs.reshape(planes, P, SC_ROW_WORDS), w_gu, b_gu, w_dn, b_dn)
            yg = _sc_gather_rows(ys.reshape(planes * P, SC_ROW_WORDS), idx)
            h_out = _combine_ple(i, g, h1, yg.reshape(TOP_K, planes, Tg, SC_ROW_WORDS), gates, p_rows, prm,
                                 g_final.reshape(1, D), i == depth - 1, h_out)
        h = h_out
    return h.reshape(B, S, D)
```

```python
import functools

import jax
import jax.numpy as jnp
from jax import lax
from jax.experimental import pallas as pl
from jax.experimental.pallas import tpu as pltpu
from jax.experimental.pallas import tpu_sc as plsc

F32 = jnp.float32
BF16 = jnp.bfloat16
I32 = jnp.int32
U32 = jnp.uint32

CHUNK = 64
GMLP_WIDTH = 512
GMLP_GROUPS = 4
GMLP_GROUP_CH = 128
GMLP_BLOCK = 128
FOX_WIDTH = 512
FOX_HEAD_DIM = 64
FOX_HEADS = 8
N_EXPERTS = 32
TOP_K = 4
EXPERT_ROWS = 256
SWIGLU_LIMIT = 7.0
SWIGLU_ALPHA = 1.702
RMS_EPS = 1e-5
LN_EPS = 1e-5

HEAD_PAD = 128
BIAS_PARTS = 3
ROW_TILE = 512
ATTN_TILE = 512
ATTN_HEADS = 1
ACC_ROWS = 80
LOG2E = 1.4426950408889634
SC_WINDOW = 128
SC_ROW_WORDS = 256
COMBINE_ROWS = 256
EXPERT_STEP_BLOCKS = 4
EXPERT_CHAIN_BLOCKS = 4
WEIGHT_SLOTS = 3
MOE_GROUPS = 2
VMEM_LIMIT = 56 * 1024 * 1024
NEG_BIG = -1e30

_NT = (((1,), (1,)), ((), ()))
_TN = (((0,), (0,)), ((), ()))


def _rms(x, g):
    return x * lax.rsqrt(jnp.mean(x * x, axis=-1, keepdims=True) + RMS_EPS) * g


def _split3(c):
    a1 = c.astype(BF16).astype(F32)
    r1 = c - a1
    a2 = r1.astype(BF16).astype(F32)
    a3 = (r1 - a2).astype(BF16).astype(F32)
    return a1, a2, a3


def _pack_rows(x):
    n = x.shape[1] // 2
    lo = lax.bitcast_convert_type(x[:, :n].astype(BF16).astype(F32), U32) >> 16
    hi = lax.bitcast_convert_type(x[:, n:].astype(BF16).astype(F32), U32) & jnp.uint32(0xFFFF0000)
    return lo | hi


def _unpack_rows(u):
    lo = lax.bitcast_convert_type(u << 16, F32)
    hi = lax.bitcast_convert_type(u & jnp.uint32(0xFFFF0000), F32)
    return jnp.concatenate([lo, hi], axis=1)


def _store_planes(ref, u):
    for j in range(ref.shape[0]):
        ref[j] = u[:, j * SC_ROW_WORDS:(j + 1) * SC_ROW_WORDS]


def _mix_in_kernel(h_ref, gmix_ref, wuv_ref, wqt_ref, wk_ref, wvt_ref, wft_ref, bf_ref, lng_ref, lnb_ref,
                   ws_ref, bs_ref, ga_ref, tri_ref, selk_ref, onesk_ref, after_ref,
                   ya_ref, qt_ref, k_ref, vt_ref, carry_ref, ya_scr):
    del after_ref
    tm = h_ref.shape[0]

    @pl.when(pl.program_id(1) == 0)
    def _():
        carry_ref[...] = jnp.zeros_like(carry_ref)

    xn = _rms(h_ref[...], gmix_ref[...]).astype(BF16)

    z = jnp.dot(xn, wuv_ref[...], preferred_element_type=F32)
    qt = lax.dot_general(wqt_ref[...], xn, _NT, preferred_element_type=F32)
    vt_ref[0] = lax.dot_general(wvt_ref[...], xn, _NT, preferred_element_type=F32).astype(BF16)
    kp = jnp.dot(xn, wk_ref[...], preferred_element_type=F32)
    ft = lax.dot_general(wft_ref[...], xn, _NT, preferred_element_type=F32) + bf_ref[...]

    z = 0.5 * z * (1.0 + lax.erf(z * (0.5 ** 0.5)))
    u = z[:, :GMLP_WIDTH]
    v = z[:, GMLP_WIDTH:]
    mu = jnp.mean(v, axis=-1, keepdims=True)
    vc = v - mu
    var = jnp.mean(vc * vc, axis=-1, keepdims=True)
    vn = (vc * lax.rsqrt(var + LN_EPS) * lng_ref[...] + lnb_ref[...]).astype(BF16)
    pi = lax.broadcasted_iota(I32, (GMLP_BLOCK, GMLP_BLOCK), 0) // CHUNK
    pj = lax.broadcasted_iota(I32, (GMLP_BLOCK, GMLP_BLOCK), 1) // CHUNK
    for g in range(GMLP_GROUPS):
        wm = jnp.where(pj <= pi, ws_ref[g], 0.0).astype(BF16)
        cs = slice(g * GMLP_GROUP_CH, (g + 1) * GMLP_GROUP_CH)
        for n in range(tm // GMLP_BLOCK):
            rs = slice(n * GMLP_BLOCK, (n + 1) * GMLP_BLOCK)
            sv = jnp.dot(wm, vn[rs, cs], preferred_element_type=F32) + bs_ref[g]
            ya_scr[rs, cs] = u[rs, cs] * sv
    ya_ref[...] = _rms(ya_scr[...], ga_ref[...]).astype(BF16)

    lf = (jnp.minimum(ft, 0.0) - jnp.log1p(jnp.exp(-jnp.abs(ft)))) * LOG2E
    zpad = jnp.zeros((32 - BIAS_PARTS * FOX_HEADS, tm), F32)
    stack = jnp.concatenate(list(_split3(lf)) + [zpad], axis=0).astype(BF16)
    cum = jnp.dot(stack, tri_ref[...], preferred_element_type=F32)
    c = cum[0:8] + cum[8:16] + cum[16:24] + carry_ref[:, 0:1]
    carry_ref[...] = jnp.broadcast_to(c[:, tm - 1:tm], carry_ref.shape)
    a1, a2, a3 = _split3(c)

    si = lax.broadcasted_iota(I32, (8, tm), 0)
    zrows = jnp.zeros((HEAD_PAD - FOX_HEAD_DIM - 8, tm), F32)
    for hd in range(FOX_HEADS):
        b1 = jnp.broadcast_to(a1[hd:hd + 1], (8, tm))
        b2 = jnp.broadcast_to(a2[hd:hd + 1], (8, tm))
        b3 = jnp.broadcast_to(a3[hd:hd + 1], (8, tm))
        ext = jnp.where(si == 0, b1, jnp.where(si == 1, b2, jnp.where(si == 2, b3,
                        jnp.where(si < 2 * BIAS_PARTS, 1.0, 0.0))))
        blk = jnp.concatenate([qt[hd * FOX_HEAD_DIM:(hd + 1) * FOX_HEAD_DIM], ext, zrows], axis=0)
        qt_ref[0, hd * HEAD_PAD:(hd + 1) * HEAD_PAD, :] = blk.astype(BF16)

    cstack = jnp.concatenate([a1, a2, a3, zpad], axis=0).astype(BF16)
    kext = lax.dot_general(cstack, selk_ref[...], _TN, preferred_element_type=F32)
    k_ref[...] = (kp + kext + onesk_ref[...]).astype(BF16)


def _mix_in(group, h, B, S, prm, after):
    D = h.shape[1]
    Bg = B // MOE_GROUPS
    Tg = Bg * S
    tm = min(ROW_TILE, S)
    ns = S // tm
    H = FOX_HEADS
    const = lambda shape: pl.BlockSpec(shape, lambda b, s: (0,) * len(shape))
    return pl.pallas_call(
        _mix_in_kernel,
        grid=(Bg, ns),
        in_specs=[
            pl.BlockSpec((tm, D), lambda b, s: ((group * Bg + b) * ns + s, 0)),
            const((1, D)),
            const((D, 2 * GMLP_WIDTH)),
            const((FOX_WIDTH, D)),
            const((D, H * HEAD_PAD)),
            const((FOX_WIDTH, D)),
            const((H, D)),
            const((H, 1)),
            const((1, GMLP_WIDTH)),
            const((1, GMLP_WIDTH)),
            const((GMLP_GROUPS, GMLP_BLOCK, GMLP_BLOCK)),
            const((GMLP_GROUPS, GMLP_BLOCK, 1)),
            const((1, GMLP_WIDTH)),
            const((tm, tm)),
            const((32, H * HEAD_PAD)),
            const((1, H * HEAD_PAD)),
            pl.BlockSpec(memory_space=pl.ANY),
        ],
        out_specs=[
            pl.BlockSpec((tm, GMLP_WIDTH), lambda b, s: (b * ns + s, 0)),
            pl.BlockSpec((1, H * HEAD_PAD, tm), lambda b, s: (b, 0, s)),
            pl.BlockSpec((tm, H * HEAD_PAD), lambda b, s: (b * ns + s, 0)),
            pl.BlockSpec((1, FOX_WIDTH, tm), lambda b, s: (b, 0, s)),
        ],
        out_shape=[
            jax.ShapeDtypeStruct((Tg, GMLP_WIDTH), BF16),
            jax.ShapeDtypeStruct((Bg, H * HEAD_PAD, S), BF16),
            jax.ShapeDtypeStruct((Tg, H * HEAD_PAD), BF16),
            jax.ShapeDtypeStruct((Bg, FOX_WIDTH, S), BF16),
        ],
        scratch_shapes=[pltpu.VMEM((H, 128), F32), pltpu.VMEM((tm, GMLP_WIDTH), F32)],
        compiler_params=pltpu.CompilerParams(
            dimension_semantics=("arbitrary", "arbitrary"), vmem_limit_bytes=VMEM_LIMIT),
        name="mix_in",
    )(h, prm["g_mix"], prm["wuv"], prm["wqt"], prm["wk"], prm["wvt"], prm["wft"], prm["b_f"],
      prm["ln_g"], prm["ln_b"], prm["w_s"], prm["b_s"], prm["g_a"], prm["tri_incl"], prm["selk"], prm["onesk"],
      after)


def _fox_attn_kernel(qt_ref, k_ref, vt_ref, ot_ref):
    S = k_ref.shape[0]
    t = min(ATTN_TILE, S)
    nq = S // t
    hf = t // 2
    ones_rows = jnp.ones((ACC_ROWS - FOX_HEAD_DIM, t), BF16)
    pairs = [(qi, kj) for qi in range(nq) for kj in range(qi + 1)]

    def tri(rows, cols):
        return lax.broadcasted_iota(I32, (rows, cols), 0) <= lax.broadcasted_iota(I32, (rows, cols), 1)

    def scores(qi, kj):
        out = []
        for hh in range(ATTN_HEADS):
            hs = slice(hh * HEAD_PAD, (hh + 1) * HEAD_PAD)
            if kj < qi:
                out.append([jnp.dot(k_ref[kj * t:(kj + 1) * t, hs], qt_ref[0, hs, qi * t:(qi + 1) * t],
                                    preferred_element_type=F32)])
            else:
                top = jnp.dot(k_ref[kj * t:kj * t + hf, hs], qt_ref[0, hs, qi * t:(qi + 1) * t],
                              preferred_element_type=F32)
                bot = jnp.dot(k_ref[kj * t + hf:(kj + 1) * t, hs], qt_ref[0, hs, qi * t + hf:(qi + 1) * t],
                              preferred_element_type=F32)
                out.append([jnp.where(tri(hf, t), top, NEG_BIG), jnp.where(tri(hf, hf), bot, NEG_BIG)])
        return out

    s_next = scores(*pairs[0])
    state = None
    for n, (qi, kj) in enumerate(pairs):
        s_cur = s_next
        if n + 1 < len(pairs):
            s_next = scores(*pairs[n + 1])
        new_state = []
        for hh in range(ATTN_HEADS):
            parts = s_cur[hh]
            v = vt_ref[0, hh * FOX_HEAD_DIM:(hh + 1) * FOX_HEAD_DIM, kj * t:(kj + 1) * t]
            v_aug = jnp.concatenate([v, ones_rows], axis=0)
            mx = jnp.max(parts[0], axis=0, keepdims=True)
            if len(parts) == 2:
                mx = jnp.concatenate(
                    [mx[:, :hf], jnp.maximum(mx[:, hf:], jnp.max(parts[1], axis=0, keepdims=True))], axis=1)
            m_new = mx if kj == 0 else jnp.maximum(state[hh][0], mx)
            if len(parts) == 1:
                pv = jnp.dot(v_aug, jnp.exp2(parts[0] - m_new).astype(BF16), preferred_element_type=F32)
            else:
                pv = jnp.dot(v_aug[:, :hf], jnp.exp2(parts[0] - m_new).astype(BF16), preferred_element_type=F32)
                pv_r = jnp.dot(v_aug[:, hf:], jnp.exp2(parts[1] - m_new[:, hf:]).astype(BF16),
                               preferred_element_type=F32)
                pv = jnp.concatenate([pv[:, :hf], pv[:, hf:] + pv_r], axis=1)
            acc = pv if kj == 0 else jnp.exp2(state[hh][0] - m_new) * state[hh][1] + pv
            new_state.append((m_new, acc))
        state = new_state
        if kj == qi:
            for hh in range(ATTN_HEADS):
                acc = state[hh][1]
                o = acc[:FOX_HEAD_DIM] / acc[FOX_HEAD_DIM:FOX_HEAD_DIM + 1]
                ot_ref[0, hh * FOX_HEAD_DIM:(hh + 1) * FOX_HEAD_DIM, qi * t:(qi + 1) * t] = o.astype(BF16)


def _fox_attn(qt, kx, vt, B, S):
    G = ATTN_HEADS
    return pl.pallas_call(
        _fox_attn_kernel,
        grid=(B, FOX_HEADS // G),
        in_specs=[
            pl.BlockSpec((1, G * HEAD_PAD, S), lambda b, h: (b, h, 0)),
            pl.BlockSpec((S, G * HEAD_PAD), lambda b, h: (b, h)),
            pl.BlockSpec((1, G * FOX_HEAD_DIM, S), lambda b, h: (b, h, 0)),
        ],
        out_specs=pl.BlockSpec((1, G * FOX_HEAD_DIM, S), lambda b, h: (b, h, 0)),
        out_shape=jax.ShapeDtypeStruct((B, FOX_WIDTH, S), BF16),
        compiler_params=pltpu.CompilerParams(
            dimension_semantics=("parallel", "parallel"), vmem_limit_bytes=VMEM_LIMIT),
        name="fox_attn",
    )(qt, kx, vt)


def _out_router_kernel(h_ref, ya_ref, ybt_ref, woa_ref, wob_ref, gb_ref, gmoe_ref, wrt_ref, br_ref, tri_ref,
                       h1_ref, xp_ref, tope_ref, rank_ref, gates_ref, counts_ref, carry_ref):
    tm = h_ref.shape[0]

    @pl.when(pl.program_id(0) == 0)
    def _():
        carry_ref[...] = jnp.zeros_like(carry_ref)

    yb = ybt_ref[0].astype(F32)
    ybn = (yb * lax.rsqrt(jnp.mean(yb * yb, axis=0, keepdims=True) + RMS_EPS) * gb_ref[...]).astype(BF16)
    y = jnp.dot(ya_ref[...], woa_ref[...], preferred_element_type=F32)
    y = y + lax.dot_general(ybn, wob_ref[...], _TN, preferred_element_type=F32)
    h1 = h_ref[...] + y
    h1_ref[...] = h1
    xn = _rms(h1, gmoe_ref[...])
    _store_planes(xp_ref, _pack_rows(xn))

    logits = lax.dot_general(wrt_ref[...], xn.astype(BF16), _NT, preferred_element_type=F32) + br_ref[...]
    ri = lax.broadcasted_iota(I32, (N_EXPERTS, tm), 0)
    vals, idxs = [], []
    l = logits
    for _ in range(TOP_K):
        m = jnp.max(l, axis=0, keepdims=True)
        idx = jnp.min(jnp.where(l == m, ri, N_EXPERTS), axis=0, keepdims=True)
        vals.append(m)
        idxs.append(idx)
        l = jnp.where(ri == idx, -jnp.inf, l)
    es = [jnp.exp(vk - vals[0]) for vk in vals]
    den = es[0] + es[1] + es[2] + es[3]
    zrow = jnp.zeros((8 - TOP_K, tm), F32)
    gates_ref[...] = jnp.concatenate([e / den for e in es] + [zrow], axis=0).T

    sel = jnp.zeros((N_EXPERTS, tm), F32)
    for idx in idxs:
        sel = sel + jnp.where(ri == idx, 1.0, 0.0)
    cnt = jnp.dot(sel.astype(BF16), tri_ref[...], preferred_element_type=F32) + carry_ref[:, 0:1]
    ranks = [jnp.sum(jnp.where(ri == idx, cnt, 0.0), axis=0, keepdims=True) for idx in idxs]
    zi = jnp.zeros((8 - TOP_K, tm), I32)
    rank_ref[...] = jnp.concatenate([r.astype(I32) for r in ranks] + [zi], axis=0)
    tope_ref[...] = jnp.concatenate(idxs + [zi], axis=0)
    total = carry_ref[...] + jnp.sum(sel, axis=1, keepdims=True)
    carry_ref[...] = total
    counts_ref[...] = total.astype(I32)


def _out_router(group, h, ya, ybt, B, S, prm, tri_strict):
    T, D = h.shape
    Tg = T // MOE_GROUPS
    tm = min(ROW_TILE, S)
    ns = S // tm
    E = N_EXPERTS
    planes = (D // 2) // SC_ROW_WORDS
    off = group * (Tg // tm)
    const = lambda shape: pl.BlockSpec(shape, lambda i: (0,) * len(shape))
    return pl.pallas_call(
        _out_router_kernel,
        grid=(Tg // tm,),
        in_specs=[
            pl.BlockSpec((tm, D), lambda i: (off + i, 0)),
            pl.BlockSpec((tm, GMLP_WIDTH), lambda i: (i, 0)),
            pl.BlockSpec((1, FOX_WIDTH, tm), lambda i: (i // ns, 0, i % ns)),
            const((GMLP_WIDTH, D)),
            const((FOX_WIDTH, D)),
            const((FOX_WIDTH, 1)),
            const((1, D)),
            const((E, D)),
            const((E, 1)),
            const((tm, tm)),
        ],
        out_specs=[
            pl.BlockSpec((tm, D), lambda i: (i, 0)),
            pl.BlockSpec((planes, tm, SC_ROW_WORDS), lambda i: (0, i, 0)),
            pl.BlockSpec((8, tm), lambda i: (0, i)),
            pl.BlockSpec((8, tm), lambda i: (0, i)),
            pl.BlockSpec((tm, 8), lambda i: (i, 0)),
            const((E, 128)),
        ],
        out_shape=[
            jax.ShapeDtypeStruct((Tg, D), F32),
            jax.ShapeDtypeStruct((planes, Tg, SC_ROW_WORDS), U32),
            jax.ShapeDtypeStruct((8, Tg), I32),
            jax.ShapeDtypeStruct((8, Tg), I32),
            jax.ShapeDtypeStruct((Tg, 8), F32),
            jax.ShapeDtypeStruct((E, 128), I32),
        ],
        scratch_shapes=[pltpu.VMEM((E, 128), F32)],
        compiler_params=pltpu.CompilerParams(
            dimension_semantics=("arbitrary",), vmem_limit_bytes=VMEM_LIMIT),
        name="out_router",
    )(h, ya, ybt, prm["woa"], prm["wob"], prm["g_b"], prm["g_moe"], prm["wrt"], prm["b_router"], tri_strict)


def _dest_kernel(ps_ref, tope_ref, rank_ref, dest_ref):
    e_sel = tope_ref[...]
    d = rank_ref[...]
    for e in range(N_EXPERTS):
        d = d + jnp.where(e_sel == e, ps_ref[e], 0)
    dest_ref[...] = d


def _dest_rows(pad_starts, tope, rank):
    T = tope.shape[1]
    tc = min(4096, T)
    return pl.pallas_call(
        _dest_kernel,
        grid_spec=pltpu.PrefetchScalarGridSpec(
            num_scalar_prefetch=1,
            grid=(T // tc,),
            in_specs=[pl.BlockSpec((8, tc), lambda i, ps: (0, i)),
                      pl.BlockSpec((8, tc), lambda i, ps: (0, i))],
            out_specs=pl.BlockSpec((8, tc), lambda i, ps: (0, i)),
        ),
        out_shape=jax.ShapeDtypeStruct((8, T), I32),
        name="dest_rows",
    )(pad_starts, tope, rank)


def _sc_mesh():
    return plsc.VectorSubcoreMesh(core_axis_name="c", subcore_axis_name="s")


def _sc_scatter_rows(x, idx, n_out):
    R, W = x.shape
    N = idx.shape[0]
    nsrc = R // SC_WINDOW
    copies = N // R

    @functools.partial(pl.kernel, out_type=jax.ShapeDtypeStruct((n_out, W), x.dtype), mesh=_sc_mesh())
    def k(x_hbm, i_hbm, o_hbm):
        def body(x_vmem, *i_vmems):
            for i_vmem in i_vmems:
                pltpu.sync_copy(x_vmem, o_hbm.at[i_vmem.at[0]])

        idx_spec = lambda c: pl.BlockSpec((1, SC_WINDOW), lambda i: (0, c * nsrc + i))
        pltpu.emit_pipeline(
            body,
            grid=(nsrc,),
            in_specs=[pl.BlockSpec((SC_WINDOW, W), lambda i: (i, 0))] + [idx_spec(c) for c in range(copies)],
            out_specs=[],
            core_axis_name=("c", "s"),
            dimension_semantics=(pltpu.PARALLEL,),
        )(x_hbm, *([i_hbm] * copies))

    return k(x, idx.reshape(1, N))


def _sc_gather_rows(table, idx):
    V, W = table.shape
    N = idx.shape[0]

    @functools.partial(pl.kernel, out_type=jax.ShapeDtypeStruct((N, W), table.dtype), mesh=_sc_mesh())
    def k(t_hbm, i_hbm, o_hbm):
        def body(i_vmem, o_vmem):
            pltpu.sync_copy(t_hbm.at[i_vmem.at[0]], o_vmem)

        pltpu.emit_pipeline(
            body,
            grid=(N // SC_WINDOW,),
            in_specs=[pl.BlockSpec((1, SC_WINDOW), lambda i: (0, i))],
            out_specs=[pl.BlockSpec((SC_WINDOW, W), lambda i: (i, 0))],
            core_axis_name=("c", "s"),
            dimension_semantics=(pltpu.PARALLEL,),
        )(i_hbm, o_hbm)

    return k(table, idx.reshape(1, N))


def _expert_kernel(be_ref, nv_ref, fs_ref, nxt_ref, ahead_ref, uni_ref, xs_ref, wgu_hbm, bgu_ref, wdn_hbm, bdn_ref,
                   ys_ref, wgu_f32, wdn_f32, cur_slot, sem, *, layer):
    F = wdn_f32.shape[1]

    def weight_copies(e, slot):
        return (pltpu.make_async_copy(wgu_hbm.at[layer, e], wgu_f32.at[slot], sem.at[slot, 0]),
                pltpu.make_async_copy(wdn_hbm.at[layer, e], wdn_f32.at[slot], sem.at[slot, 1]))

    @pl.when(pl.program_id(0) == 0)
    def _():
        for cp in weight_copies(be_ref[0], 0):
            cp.start()

        @pl.when(nxt_ref[0] >= 0)
        def _():
            for cp in weight_copies(nxt_ref[0], 1):
                cp.start()

    def ffn(x, e):
        slot = cur_slot[0]
        hgu = jnp.dot(x, wgu_f32[slot], preferred_element_type=F32) + bgu_ref[e]
        g = jnp.minimum(hgu[:, :F], SWIGLU_LIMIT)
        lin = jnp.clip(hgu[:, F:], -SWIGLU_LIMIT, SWIGLU_LIMIT)
        act = g * jax.nn.sigmoid(SWIGLU_ALPHA * g) * (lin + 1.0)
        return _pack_rows(jnp.dot(act, wdn_f32[slot], preferred_element_type=F32) + bdn_ref[e])

    def load_rows(rs):
        return _unpack_rows(jnp.concatenate([xs_ref[pn, rs, :] for pn in range(xs_ref.shape[0])], axis=1))

    def store_rows(rs, y):
        for pn in range(ys_ref.shape[0]):
            ys_ref[pn, rs, :] = y[:, pn * SC_ROW_WORDS:(pn + 1) * SC_ROW_WORDS]

    for c in range(EXPERT_STEP_BLOCKS // EXPERT_CHAIN_BLOCKS):
        chain = pl.program_id(0) * (EXPERT_STEP_BLOCKS // EXPERT_CHAIN_BLOCKS) + c
        uniform = uni_ref[chain] > 0

        @pl.when(uniform)
        def _():
            rs = slice(c * EXPERT_CHAIN_BLOCKS * EXPERT_ROWS, (c + 1) * EXPERT_CHAIN_BLOCKS * EXPERT_ROWS)
            store_rows(rs, ffn(load_rows(rs), be_ref[chain * EXPERT_CHAIN_BLOCKS]))

        for j in range(EXPERT_CHAIN_BLOCKS):
            b = chain * EXPERT_CHAIN_BLOCKS + j
            r0 = (c * EXPERT_CHAIN_BLOCKS + j) * EXPERT_ROWS
            rs = slice(r0, r0 + EXPERT_ROWS)
            e = be_ref[b]
            nv = nv_ref[b]
            fs = fs_ref[b]

            @pl.when(fs >= 0)
            def _():
                for cp in weight_copies(e, fs):
                    cp.wait()
                cur_slot[0] = fs
                ahead = ahead_ref[b]

                @pl.when(ahead >= 0)
                def _():
                    for cp in weight_copies(ahead, (fs + WEIGHT_SLOTS - 1) % WEIGHT_SLOTS):
                        cp.start()

            @pl.when(jnp.logical_not(uniform) & (nv > 0))
            def _():
                x = load_rows(rs)
                rows = lax.broadcasted_iota(I32, x.shape, 0)
                store_rows(rs, ffn(jnp.where(rows < nv, x, 0.0), e))

            @pl.when(nv == 0)
            def _():
                ys_ref[:, rs, :] = jnp.zeros((ys_ref.shape[0], EXPERT_ROWS, SC_ROW_WORDS), U32)


def _experts(layer, plan, xs, w_gu, b_gu, w_dn, b_dn):
    planes, P, _ = xs.shape
    _, E, D, F2 = w_gu.shape
    F = F2 // 2
    step_rows = EXPERT_STEP_BLOCKS * EXPERT_ROWS
    by_step = lambda i, be, nv, fs, nxt, ahead, uni: (0, i, 0)
    layer_all = lambda i, be, nv, fs, nxt, ahead, uni: (layer, 0, 0, 0)
    return pl.pallas_call(
        functools.partial(_expert_kernel, layer=layer),
        grid_spec=pltpu.PrefetchScalarGridSpec(
            num_scalar_prefetch=6,
            grid=(P // step_rows,),
            in_specs=[
                pl.BlockSpec((planes, step_rows, SC_ROW_WORDS), by_step),
                pl.BlockSpec(memory_space=pl.ANY),
                pl.BlockSpec((None, E, 1, F2), layer_all),
                pl.BlockSpec(memory_space=pl.ANY),
                pl.BlockSpec((None, E, 1, D), layer_all),
            ],
            out_specs=pl.BlockSpec((planes, step_rows, SC_ROW_WORDS), by_step),
            scratch_shapes=[
                pltpu.VMEM((WEIGHT_SLOTS, D, F2), F32), pltpu.VMEM((WEIGHT_SLOTS, F, D), F32),
                pltpu.SMEM((1,), I32),
                pltpu.SemaphoreType.DMA((WEIGHT_SLOTS, 2)),
            ],
        ),
        out_shape=jax.ShapeDtypeStruct((planes, P, SC_ROW_WORDS), U32),
        compiler_params=pltpu.CompilerParams(
            dimension_semantics=("arbitrary",), vmem_limit_bytes=VMEM_LIMIT),
        name="experts",
    )(plan["block_e"], plan["block_nv"], plan["first_slot"], plan["next_e"], plan["ahead_e"], plan["step_uniform"],
      xs, w_gu, b_gu.reshape(-1, E, 1, F2), w_dn, b_dn.reshape(-1, E, 1, D))


def _combine_ple_kernel(h1_ref, yg_ref, gates_ref, gple_ref, wpg_ref, p_ref, wple_ref, gfin_ref, *rest, final):
    o_ref = rest[-1]
    for r in range(h1_ref.shape[0] // COMBINE_ROWS):
        rs = slice(r * COMBINE_ROWS, (r + 1) * COMBINE_ROWS)
        h2 = h1_ref[rs, :]
        gates = gates_ref[rs, :]
        for k in range(TOP_K):
            yk = jnp.concatenate([yg_ref[k, pn, rs, :] for pn in range(yg_ref.shape[1])], axis=1)
            h2 = h2 + gates[:, k:k + 1] * _unpack_rows(yk)
        xn = _rms(h2, gple_ref[...]).astype(BF16)
        gate = jax.nn.sigmoid(jnp.dot(xn, wpg_ref[...], preferred_element_type=F32))
        pe = jnp.dot(p_ref[rs, :].astype(BF16), wple_ref[...], preferred_element_type=F32)
        h3 = h2 + gate * pe
        if final:
            h3 = _rms(h3, gfin_ref[...])
        o_ref[rs, :] = h3


def _combine_ple(layer, group, h1, yg, gates, p, prm, g_final, final, h_prev):
    Tg, D = h1.shape
    T = Tg * MOE_GROUPS
    tm = min(ROW_TILE, Tg)
    PD = p.shape[2]
    planes = yg.shape[1]
    off = group * (Tg // tm)
    const = lambda shape: pl.BlockSpec(shape, lambda i: (0,) * len(shape))
    in_specs = [
        pl.BlockSpec((tm, D), lambda i: (i, 0)),
        pl.BlockSpec((TOP_K, planes, tm, SC_ROW_WORDS), lambda i: (0, 0, i, 0)),
        pl.BlockSpec((tm, 8), lambda i: (i, 0)),
        const((1, D)),
        const((D, D)),
        pl.BlockSpec((None, tm, PD), lambda i: (layer, off + i, 0)),
        const((PD, D)),
        const((1, D)),
    ]
    args = [h1, yg, gates, prm["g_ple"], prm["wpg"], p, prm["wple"], g_final]
    aliases = {}
    if h_prev is not None:
        in_specs.append(pl.BlockSpec(memory_space=pl.ANY))
        args.append(h_prev)
        aliases = {len(args) - 1: 0}
    return pl.pallas_call(
        functools.partial(_combine_ple_kernel, final=final),
        grid=(Tg // tm,),
        in_specs=in_specs,
        out_specs=pl.BlockSpec((tm, D), lambda i: (off + i, 0)),
        out_shape=jax.ShapeDtypeStruct((T, D), F32),
        input_output_aliases=aliases,
        compiler_params=pltpu.CompilerParams(
            dimension_semantics=("parallel",), vmem_limit_bytes=VMEM_LIMIT),
        name="combine_ple",
    )(*args)


def _layer_params(i, w):
    D = w["w_in"].shape[1]
    H = FOX_HEADS
    w_in = w["w_in"][i]
    c1 = 2 * GMLP_WIDTH
    c2, c3, c4 = c1 + FOX_WIDTH, c1 + 2 * FOX_WIDTH, c1 + 3 * FOX_WIDTH
    wk = jnp.zeros((D, H, HEAD_PAD), F32).at[:, :, :FOX_HEAD_DIM].set(
        w_in[:, c2:c3].reshape(D, H, FOX_HEAD_DIM)).reshape(D, H * HEAD_PAD)
    return {
        "g_mix": w["g_mix"][i].reshape(1, D),
        "wuv": w_in[:, :c1].astype(BF16),
        "wqt": (w_in[:, c1:c2] * (FOX_HEAD_DIM ** -0.5 * LOG2E)).T.astype(BF16),
        "wk": wk.astype(BF16),
        "wvt": w_in[:, c3:c4].T.astype(BF16),
        "wft": w_in[:, c4:].T.astype(BF16),
        "b_f": w["b_f"][i].reshape(H, 1),
        "ln_g": w["ln_g"][i].reshape(1, GMLP_WIDTH),
        "ln_b": w["ln_b"][i].reshape(1, GMLP_WIDTH),
        "w_s": w["w_s"][i],
        "b_s": w["b_s"][i].reshape(GMLP_GROUPS, GMLP_BLOCK, 1),
        "g_a": w["g_a"][i].reshape(1, GMLP_WIDTH),
        "woa": w["w_out"][i][:GMLP_WIDTH].astype(BF16),
        "wob": w["w_out"][i][GMLP_WIDTH:].astype(BF16),
        "g_b": w["g_b"][i].reshape(FOX_WIDTH, 1),
        "g_moe": w["g_moe"][i].reshape(1, D),
        "wrt": w["w_router"][i].T.astype(BF16),
        "b_router": w["b_router"][i].reshape(N_EXPERTS, 1),
        "g_ple": w["g_ple"][i].reshape(1, D),
        "wpg": w["w_ple_gate"][i].astype(BF16),
        "wple": w["w_ple"][i].astype(BF16),
    }


def _bias_constants(tm):
    H = FOX_HEADS
    j = jnp.arange(tm)
    tri_incl = (j[:, None] <= j[None, :]).astype(BF16)
    tri_strict = (j[:, None] < j[None, :]).astype(BF16)
    part = jnp.arange(BIAS_PARTS)[:, None]
    head = jnp.arange(H)[None, :]
    rows = (part * H + head).reshape(-1)
    cols = (head * HEAD_PAD + FOX_HEAD_DIM + BIAS_PARTS + part).reshape(-1)
    selk = jnp.zeros((32, H * HEAD_PAD), F32).at[rows, cols].set(-1.0).astype(BF16)
    one_cols = (jnp.arange(H)[:, None] * HEAD_PAD + FOX_HEAD_DIM + jnp.arange(BIAS_PARTS)[None, :]).reshape(-1)
    onesk = jnp.zeros((1, H * HEAD_PAD), F32).at[0, one_cols].set(1.0)
    return tri_incl, tri_strict, selk, onesk


def _moe_dispatch_plan(counts, n_blocks):
    padded = (counts + EXPERT_ROWS - 1) // EXPERT_ROWS * EXPERT_ROWS
    pad_ends = jnp.cumsum(padded)
    pad_starts = (pad_ends - padded).astype(I32)
    block_start = jnp.arange(n_blocks, dtype=I32) * EXPERT_ROWS
    n_before = jnp.sum((pad_ends[None, :] <= block_start[:, None]).astype(I32), axis=1)
    block_e = jnp.minimum(n_before, N_EXPERTS - 1).astype(I32)
    own = block_e[:, None] == jnp.arange(N_EXPERTS, dtype=I32)[None, :]
    seg_end = jnp.sum(jnp.where(own, (pad_starts + counts)[None, :], 0), axis=1)
    block_nv = jnp.clip(seg_end - block_start, 0, EXPERT_ROWS)
    block_nv = jnp.where(block_start < pad_ends[-1], block_nv, 0).astype(I32)
    prev_e = jnp.concatenate([jnp.full((1,), -1, I32), block_e[:-1]])
    first = (block_nv > 0) & (block_e != prev_e)
    run = jnp.cumsum(first.astype(I32)) - 1
    first_slot = jnp.where(first, run % WEIGHT_SLOTS, -1).astype(I32)
    ar = jnp.arange(N_EXPERTS, dtype=I32)
    later = (ar[None, :] > ar[:, None]) & (counts > 0)[None, :]
    next_of = jnp.min(jnp.where(later, ar[None, :], N_EXPERTS), axis=1)
    next_of = jnp.where(next_of < N_EXPERTS, next_of, -1)
    next_e = jnp.sum(jnp.where(own, next_of[None, :], 0), axis=1).astype(I32)
    hop = next_of[:, None] == ar[None, :]
    next2_of = jnp.where(next_of >= 0, jnp.sum(jnp.where(hop, next_of[None, :], 0), axis=1), -1)
    ahead_e = jnp.sum(jnp.where(own, next2_of[None, :], 0), axis=1).astype(I32)
    plain = ((block_nv == EXPERT_ROWS) & jnp.logical_not(first)).reshape(-1, EXPERT_CHAIN_BLOCKS)
    step_uniform = jnp.all(plain, axis=1).astype(I32)
    return dict(pad_starts=pad_starts, block_e=block_e, block_nv=block_nv, first_slot=first_slot, next_e=next_e,
                ahead_e=ahead_e, step_uniform=step_uniform)


def kernel(x, p, g_mix, w_in, ln_g, ln_b, w_s, b_s, b_f, g_a, g_b, w_out, g_moe, w_router, b_router,
           w_gu, b_gu, w_dn, b_dn, g_ple, w_ple_gate, w_ple, g_final):
    B, S, D = x.shape
    T = B * S
    depth = w_in.shape[0]
    assert B % MOE_GROUPS == 0 and S % min(ROW_TILE, S) == 0 and S % min(ATTN_TILE, S) == 0
    assert (T // MOE_GROUPS) % SC_WINDOW == 0 and D % (2 * SC_ROW_WORDS) == 0
    w = dict(g_mix=g_mix, w_in=w_in, ln_g=ln_g, ln_b=ln_b, w_s=w_s, b_s=b_s, b_f=b_f, g_a=g_a, g_b=g_b,
             w_out=w_out, g_moe=g_moe, w_router=w_router, b_router=b_router,
             g_ple=g_ple, w_ple_gate=w_ple_gate, w_ple=w_ple)
    tri_incl, tri_strict, selk, onesk = _bias_constants(min(ROW_TILE, S))
    G = MOE_GROUPS
    Tg = T // G
    n_blocks = -(-(Tg * TOP_K) // EXPERT_ROWS) + N_EXPERTS
    n_blocks = -(-n_blocks // EXPERT_STEP_BLOCKS) * EXPERT_STEP_BLOCKS
    P = n_blocks * EXPERT_ROWS
    planes = (D // 2) // SC_ROW_WORDS
    plane_base = (jnp.arange(planes, dtype=I32) * P)[None, :, None]
    p_rows = p.reshape(depth, T, p.shape[-1])

    h = x.reshape(T, D)
    for i in range(depth):
        prm = _layer_params(i, w)
        prm.update(tri_incl=tri_incl, selk=selk, onesk=onesk)
        routed = []
        idx = jnp.zeros((8, 128), I32)
        for g in range(G):
            ya, qt, kx, vt = _mix_in(g, h, B, S, prm, idx)
            ybt = _fox_attn(qt, kx, vt, B // G, S)
            h1, xp, tope, rank, gates, counts = _out_router(g, h, ya, ybt, B, S, prm, tri_strict)
            plan = _moe_dispatch_plan(counts[:, 0], n_blocks)
            dest = _dest_rows(plan["pad_starts"], tope, rank)[:TOP_K]
            idx = (dest[:, None, :] + plane_base).reshape(-1)
            xs = _sc_scatter_rows(xp.reshape(planes * Tg, SC_ROW_WORDS), idx, planes * P)
            routed.append((h1, gates, plan, idx, xs))
        h_out = None
        for g, (h1, gates, plan, idx, xs) in enumerate(routed):
            ys = _experts(i, plan, xs.reshape(planes, P, SC_ROW_WORDS), w_gu, b_gu, w_dn, b_dn)
            yg = _sc_gather_rows(ys.reshape(planes * P, SC_ROW_WORDS), idx)
            h_out = _combine_ple(i, g, h1, yg.reshape(TOP_K, planes, Tg, SC_ROW_WORDS), gates, p_rows, prm,
                                 g_final.reshape(1, D), i == depth - 1, h_out)
        h = h_out
    return h.reshape(B, S, D)
```

```python
import functools

import jax
import jax.numpy as jnp
from jax import lax
from jax.experimental import pallas as pl
from jax.experimental.pallas import tpu as pltpu
from jax.experimental.pallas import tpu_sc as plsc

F32 = jnp.float32
BF16 = jnp.bfloat16
I32 = jnp.int32
U32 = jnp.uint32

CHUNK = 64
GMLP_WIDTH = 512
GMLP_GROUPS = 4
GMLP_GROUP_CH = 128
GMLP_BLOCK = 128
FOX_WIDTH = 512
FOX_HEAD_DIM = 64
FOX_HEADS = 8
N_EXPERTS = 32
TOP_K = 4
EXPERT_ROWS = 256
SWIGLU_LIMIT = 7.0
SWIGLU_ALPHA = 1.702
RMS_EPS = 1e-5
LN_EPS = 1e-5

HEAD_PAD = 128
BIAS_PARTS = 3
ROW_TILE = 512
ATTN_TILE = 512
ATTN_HEADS = 1
ACC_ROWS = 80
LOG2E = 1.4426950408889634
SC_WINDOW = 128
SC_ROW_WORDS = 256
COMBINE_ROWS = 256
EXPERT_STEP_BLOCKS = 2
EXPERT_CHAIN_BLOCKS = 2
MOE_GROUPS = 2
VMEM_LIMIT = 56 * 1024 * 1024
NEG_BIG = -1e30

_NT = (((1,), (1,)), ((), ()))
_TN = (((0,), (0,)), ((), ()))


def _rms(x, g):
    return x * lax.rsqrt(jnp.mean(x * x, axis=-1, keepdims=True) + RMS_EPS) * g


def _split3(c):
    a1 = c.astype(BF16).astype(F32)
    r1 = c - a1
    a2 = r1.astype(BF16).astype(F32)
    a3 = (r1 - a2).astype(BF16).astype(F32)
    return a1, a2, a3


def _pack_rows(x):
    n = x.shape[1] // 2
    lo = lax.bitcast_convert_type(x[:, :n].astype(BF16).astype(F32), U32) >> 16
    hi = lax.bitcast_convert_type(x[:, n:].astype(BF16).astype(F32), U32) & jnp.uint32(0xFFFF0000)
    return lo | hi


def _unpack_rows(u):
    lo = lax.bitcast_convert_type(u << 16, F32)
    hi = lax.bitcast_convert_type(u & jnp.uint32(0xFFFF0000), F32)
    return jnp.concatenate([lo, hi], axis=1)


def _store_planes(ref, u):
    for j in range(ref.shape[0]):
        ref[j] = u[:, j * SC_ROW_WORDS:(j + 1) * SC_ROW_WORDS]


def _mix_in_kernel(h_ref, gmix_ref, wuv_ref, wqt_ref, wk_ref, wvt_ref, wft_ref, bf_ref, lng_ref, lnb_ref,
                   ws_ref, bs_ref, ga_ref, tri_ref, selk_ref, onesk_ref, after_ref,
                   ya_ref, qt_ref, k_ref, vt_ref, carry_ref, ya_scr):
    del after_ref
    tm = h_ref.shape[0]

    @pl.when(pl.program_id(1) == 0)
    def _():
        carry_ref[...] = jnp.zeros_like(carry_ref)

    xn = _rms(h_ref[...], gmix_ref[...]).astype(BF16)

    z = jnp.dot(xn, wuv_ref[...], preferred_element_type=F32)
    qt = lax.dot_general(wqt_ref[...], xn, _NT, preferred_element_type=F32)
    vt_ref[0] = lax.dot_general(wvt_ref[...], xn, _NT, preferred_element_type=F32).astype(BF16)
    kp = jnp.dot(xn, wk_ref[...], preferred_element_type=F32)
    ft = lax.dot_general(wft_ref[...], xn, _NT, preferred_element_type=F32) + bf_ref[...]

    z = 0.5 * z * (1.0 + lax.erf(z * (0.5 ** 0.5)))
    u = z[:, :GMLP_WIDTH]
    v = z[:, GMLP_WIDTH:]
    mu = jnp.mean(v, axis=-1, keepdims=True)
    vc = v - mu
    var = jnp.mean(vc * vc, axis=-1, keepdims=True)
    vn = (vc * lax.rsqrt(var + LN_EPS) * lng_ref[...] + lnb_ref[...]).astype(BF16)
    pi = lax.broadcasted_iota(I32, (GMLP_BLOCK, GMLP_BLOCK), 0) // CHUNK
    pj = lax.broadcasted_iota(I32, (GMLP_BLOCK, GMLP_BLOCK), 1) // CHUNK
    for g in range(GMLP_GROUPS):
        wm = jnp.where(pj <= pi, ws_ref[g], 0.0).astype(BF16)
        cs = slice(g * GMLP_GROUP_CH, (g + 1) * GMLP_GROUP_CH)
        for n in range(tm // GMLP_BLOCK):
            rs = slice(n * GMLP_BLOCK, (n + 1) * GMLP_BLOCK)
            sv = jnp.dot(wm, vn[rs, cs], preferred_element_type=F32) + bs_ref[g]
            ya_scr[rs, cs] = u[rs, cs] * sv
    ya_ref[...] = _rms(ya_scr[...], ga_ref[...]).astype(BF16)

    lf = (jnp.minimum(ft, 0.0) - jnp.log1p(jnp.exp(-jnp.abs(ft)))) * LOG2E
    zpad = jnp.zeros((32 - BIAS_PARTS * FOX_HEADS, tm), F32)
    stack = jnp.concatenate(list(_split3(lf)) + [zpad], axis=0).astype(BF16)
    cum = jnp.dot(stack, tri_ref[...], preferred_element_type=F32)
    c = cum[0:8] + cum[8:16] + cum[16:24] + carry_ref[:, 0:1]
    carry_ref[...] = jnp.broadcast_to(c[:, tm - 1:tm], carry_ref.shape)
    a1, a2, a3 = _split3(c)

    si = lax.broadcasted_iota(I32, (8, tm), 0)
    zrows = jnp.zeros((HEAD_PAD - FOX_HEAD_DIM - 8, tm), F32)
    for hd in range(FOX_HEADS):
        b1 = jnp.broadcast_to(a1[hd:hd + 1], (8, tm))
        b2 = jnp.broadcast_to(a2[hd:hd + 1], (8, tm))
        b3 = jnp.broadcast_to(a3[hd:hd + 1], (8, tm))
        ext = jnp.where(si == 0, b1, jnp.where(si == 1, b2, jnp.where(si == 2, b3,
                        jnp.where(si < 2 * BIAS_PARTS, 1.0, 0.0))))
        blk = jnp.concatenate([qt[hd * FOX_HEAD_DIM:(hd + 1) * FOX_HEAD_DIM], ext, zrows], axis=0)
        qt_ref[0, hd * HEAD_PAD:(hd + 1) * HEAD_PAD, :] = blk.astype(BF16)

    cstack = jnp.concatenate([a1, a2, a3, zpad], axis=0).astype(BF16)
    kext = lax.dot_general(cstack, selk_ref[...], _TN, preferred_element_type=F32)
    k_ref[...] = (kp + kext + onesk_ref[...]).astype(BF16)


def _mix_in(group, h, B, S, prm, after):
    D = h.shape[1]
    Bg = B // MOE_GROUPS
    Tg = Bg * S
    tm = min(ROW_TILE, S)
    ns = S // tm
    H = FOX_HEADS
    const = lambda shape: pl.BlockSpec(shape, lambda b, s: (0,) * len(shape))
    return pl.pallas_call(
        _mix_in_kernel,
        grid=(Bg, ns),
        in_specs=[
            pl.BlockSpec((tm, D), lambda b, s: ((group * Bg + b) * ns + s, 0)),
            const((1, D)),
            const((D, 2 * GMLP_WIDTH)),
            const((FOX_WIDTH, D)),
            const((D, H * HEAD_PAD)),
            const((FOX_WIDTH, D)),
            const((H, D)),
            const((H, 1)),
            const((1, GMLP_WIDTH)),
            const((1, GMLP_WIDTH)),
            const((GMLP_GROUPS, GMLP_BLOCK, GMLP_BLOCK)),
            const((GMLP_GROUPS, GMLP_BLOCK, 1)),
            const((1, GMLP_WIDTH)),
            const((tm, tm)),
            const((32, H * HEAD_PAD)),
            const((1, H * HEAD_PAD)),
            pl.BlockSpec(memory_space=pl.ANY),
        ],
        out_specs=[
            pl.BlockSpec((tm, GMLP_WIDTH), lambda b, s: (b * ns + s, 0)),
            pl.BlockSpec((1, H * HEAD_PAD, tm), lambda b, s: (b, 0, s)),
            pl.BlockSpec((tm, H * HEAD_PAD), lambda b, s: (b * ns + s, 0)),
            pl.BlockSpec((1, FOX_WIDTH, tm), lambda b, s: (b, 0, s)),
        ],
        out_shape=[
            jax.ShapeDtypeStruct((Tg, GMLP_WIDTH), BF16),
            jax.ShapeDtypeStruct((Bg, H * HEAD_PAD, S), BF16),
            jax.ShapeDtypeStruct((Tg, H * HEAD_PAD), BF16),
            jax.ShapeDtypeStruct((Bg, FOX_WIDTH, S), BF16),
        ],
        scratch_shapes=[pltpu.VMEM((H, 128), F32), pltpu.VMEM((tm, GMLP_WIDTH), F32)],
        compiler_params=pltpu.CompilerParams(
            dimension_semantics=("arbitrary", "arbitrary"), vmem_limit_bytes=VMEM_LIMIT),
        name="mix_in",
    )(h, prm["g_mix"], prm["wuv"], prm["wqt"], prm["wk"], prm["wvt"], prm["wft"], prm["b_f"],
      prm["ln_g"], prm["ln_b"], prm["w_s"], prm["b_s"], prm["g_a"], prm["tri_incl"], prm["selk"], prm["onesk"],
      after)


def _fox_attn_kernel(qt_ref, k_ref, vt_ref, ot_ref):
    S = k_ref.shape[0]
    t = min(ATTN_TILE, S)
    nq = S // t
    hf = t // 2
    ones_rows = jnp.ones((ACC_ROWS - FOX_HEAD_DIM, t), BF16)
    pairs = [(qi, kj) for qi in range(nq) for kj in range(qi + 1)]

    def tri(rows, cols):
        return lax.broadcasted_iota(I32, (rows, cols), 0) <= lax.broadcasted_iota(I32, (rows, cols), 1)

    def scores(qi, kj):
        out = []
        for hh in range(ATTN_HEADS):
            hs = slice(hh * HEAD_PAD, (hh + 1) * HEAD_PAD)
            if kj < qi:
                out.append([jnp.dot(k_ref[kj * t:(kj + 1) * t, hs], qt_ref[0, hs, qi * t:(qi + 1) * t],
                                    preferred_element_type=F32)])
            else:
                top = jnp.dot(k_ref[kj * t:kj * t + hf, hs], qt_ref[0, hs, qi * t:(qi + 1) * t],
                              preferred_element_type=F32)
                bot = jnp.dot(k_ref[kj * t + hf:(kj + 1) * t, hs], qt_ref[0, hs, qi * t + hf:(qi + 1) * t],
                              preferred_element_type=F32)
                out.append([jnp.where(tri(hf, t), top, NEG_BIG), jnp.where(tri(hf, hf), bot, NEG_BIG)])
        return out

    s_next = scores(*pairs[0])
    state = None
    for n, (qi, kj) in enumerate(pairs):
        s_cur = s_next
        if n + 1 < len(pairs):
            s_next = scores(*pairs[n + 1])
        new_state = []
        for hh in range(ATTN_HEADS):
            parts = s_cur[hh]
            v = vt_ref[0, hh * FOX_HEAD_DIM:(hh + 1) * FOX_HEAD_DIM, kj * t:(kj + 1) * t]
            v_aug = jnp.concatenate([v, ones_rows], axis=0)
            mx = jnp.max(parts[0], axis=0, keepdims=True)
            if len(parts) == 2:
                mx = jnp.concatenate(
                    [mx[:, :hf], jnp.maximum(mx[:, hf:], jnp.max(parts[1], axis=0, keepdims=True))], axis=1)
            m_new = mx if kj == 0 else jnp.maximum(state[hh][0], mx)
            if len(parts) == 1:
                pv = jnp.dot(v_aug, jnp.exp2(parts[0] - m_new).astype(BF16), preferred_element_type=F32)
            else:
                pv = jnp.dot(v_aug[:, :hf], jnp.exp2(parts[0] - m_new).astype(BF16), preferred_element_type=F32)
                pv_r = jnp.dot(v_aug[:, hf:], jnp.exp2(parts[1] - m_new[:, hf:]).astype(BF16),
                               preferred_element_type=F32)
                pv = jnp.concatenate([pv[:, :hf], pv[:, hf:] + pv_r], axis=1)
            acc = pv if kj == 0 else jnp.exp2(state[hh][0] - m_new) * state[hh][1] + pv
            new_state.append((m_new, acc))
        state = new_state
        if kj == qi:
            for hh in range(ATTN_HEADS):
                acc = state[hh][1]
                o = acc[:FOX_HEAD_DIM] / acc[FOX_HEAD_DIM:FOX_HEAD_DIM + 1]
                ot_ref[0, hh * FOX_HEAD_DIM:(hh + 1) * FOX_HEAD_DIM, qi * t:(qi + 1) * t] = o.astype(BF16)


def _fox_attn(qt, kx, vt, B, S):
    G = ATTN_HEADS
    return pl.pallas_call(
        _fox_attn_kernel,
        grid=(B, FOX_HEADS // G),
        in_specs=[
            pl.BlockSpec((1, G * HEAD_PAD, S), lambda b, h: (b, h, 0)),
            pl.BlockSpec((S, G * HEAD_PAD), lambda b, h: (b, h)),
            pl.BlockSpec((1, G * FOX_HEAD_DIM, S), lambda b, h: (b, h, 0)),
        ],
        out_specs=pl.BlockSpec((1, G * FOX_HEAD_DIM, S), lambda b, h: (b, h, 0)),
        out_shape=jax.ShapeDtypeStruct((B, FOX_WIDTH, S), BF16),
        compiler_params=pltpu.CompilerParams(
            dimension_semantics=("parallel", "parallel"), vmem_limit_bytes=VMEM_LIMIT),
        name="fox_attn",
    )(qt, kx, vt)


def _out_router_kernel(h_ref, ya_ref, ybt_ref, woa_ref, wob_ref, gb_ref, gmoe_ref, wrt_ref, br_ref, tri_ref,
                       h1_ref, xp_ref, tope_ref, rank_ref, gates_ref, counts_ref, carry_ref):
    tm = h_ref.shape[0]

    @pl.when(pl.program_id(0) == 0)
    def _():
        carry_ref[...] = jnp.zeros_like(carry_ref)

    yb = ybt_ref[0].astype(F32)
    ybn = (yb * lax.rsqrt(jnp.mean(yb * yb, axis=0, keepdims=True) + RMS_EPS) * gb_ref[...]).astype(BF16)
    y = jnp.dot(ya_ref[...], woa_ref[...], preferred_element_type=F32)
    y = y + lax.dot_general(ybn, wob_ref[...], _TN, preferred_element_type=F32)
    h1 = h_ref[...] + y
    h1_ref[...] = h1
    xn = _rms(h1, gmoe_ref[...])
    _store_planes(xp_ref, _pack_rows(xn))

    logits = lax.dot_general(wrt_ref[...], xn.astype(BF16), _NT, preferred_element_type=F32) + br_ref[...]
    ri = lax.broadcasted_iota(I32, (N_EXPERTS, tm), 0)
    vals, idxs = [], []
    l = logits
    for _ in range(TOP_K):
        m = jnp.max(l, axis=0, keepdims=True)
        idx = jnp.min(jnp.where(l == m, ri, N_EXPERTS), axis=0, keepdims=True)
        vals.append(m)
        idxs.append(idx)
        l = jnp.where(ri == idx, -jnp.inf, l)
    es = [jnp.exp(vk - vals[0]) for vk in vals]
    den = es[0] + es[1] + es[2] + es[3]
    zrow = jnp.zeros((8 - TOP_K, tm), F32)
    gates_ref[...] = jnp.concatenate([e / den for e in es] + [zrow], axis=0).T

    sel = jnp.zeros((N_EXPERTS, tm), F32)
    for idx in idxs:
        sel = sel + jnp.where(ri == idx, 1.0, 0.0)
    cnt = jnp.dot(sel.astype(BF16), tri_ref[...], preferred_element_type=F32) + carry_ref[:, 0:1]
    ranks = [jnp.sum(jnp.where(ri == idx, cnt, 0.0), axis=0, keepdims=True) for idx in idxs]
    zi = jnp.zeros((8 - TOP_K, tm), I32)
    rank_ref[...] = jnp.concatenate([r.astype(I32) for r in ranks] + [zi], axis=0)
    tope_ref[...] = jnp.concatenate(idxs + [zi], axis=0)
    total = carry_ref[...] + jnp.sum(sel, axis=1, keepdims=True)
    carry_ref[...] = total
    counts_ref[...] = total.astype(I32)


def _out_router(group, h, ya, ybt, B, S, prm, tri_strict):
    T, D = h.shape
    Tg = T // MOE_GROUPS
    tm = min(ROW_TILE, S)
    ns = S // tm
    E = N_EXPERTS
    planes = (D // 2) // SC_ROW_WORDS
    off = group * (Tg // tm)
    const = lambda shape: pl.BlockSpec(shape, lambda i: (0,) * len(shape))
    return pl.pallas_call(
        _out_router_kernel,
        grid=(Tg // tm,),
        in_specs=[
            pl.BlockSpec((tm, D), lambda i: (off + i, 0)),
            pl.BlockSpec((tm, GMLP_WIDTH), lambda i: (i, 0)),
            pl.BlockSpec((1, FOX_WIDTH, tm), lambda i: (i // ns, 0, i % ns)),
            const((GMLP_WIDTH, D)),
            const((FOX_WIDTH, D)),
            const((FOX_WIDTH, 1)),
            const((1, D)),
            const((E, D)),
            const((E, 1)),
            const((tm, tm)),
        ],
        out_specs=[
            pl.BlockSpec((tm, D), lambda i: (i, 0)),
            pl.BlockSpec((planes, tm, SC_ROW_WORDS), lambda i: (0, i, 0)),
            pl.BlockSpec((8, tm), lambda i: (0, i)),
            pl.BlockSpec((8, tm), lambda i: (0, i)),
            pl.BlockSpec((tm, 8), lambda i: (i, 0)),
            const((E, 128)),
        ],
        out_shape=[
            jax.ShapeDtypeStruct((Tg, D), F32),
            jax.ShapeDtypeStruct((planes, Tg, SC_ROW_WORDS), U32),
            jax.ShapeDtypeStruct((8, Tg), I32),
            jax.ShapeDtypeStruct((8, Tg), I32),
            jax.ShapeDtypeStruct((Tg, 8), F32),
            jax.ShapeDtypeStruct((E, 128), I32),
        ],
        scratch_shapes=[pltpu.VMEM((E, 128), F32)],
        compiler_params=pltpu.CompilerParams(
            dimension_semantics=("arbitrary",), vmem_limit_bytes=VMEM_LIMIT),
        name="out_router",
    )(h, ya, ybt, prm["woa"], prm["wob"], prm["g_b"], prm["g_moe"], prm["wrt"], prm["b_router"], tri_strict)


def _dest_kernel(ps_ref, tope_ref, rank_ref, dest_ref):
    e_sel = tope_ref[...]
    d = rank_ref[...]
    for e in range(N_EXPERTS):
        d = d + jnp.where(e_sel == e, ps_ref[e], 0)
    dest_ref[...] = d


def _dest_rows(pad_starts, tope, rank):
    T = tope.shape[1]
    tc = min(4096, T)
    return pl.pallas_call(
        _dest_kernel,
        grid_spec=pltpu.PrefetchScalarGridSpec(
            num_scalar_prefetch=1,
            grid=(T // tc,),
            in_specs=[pl.BlockSpec((8, tc), lambda i, ps: (0, i)),
                      pl.BlockSpec((8, tc), lambda i, ps: (0, i))],
            out_specs=pl.BlockSpec((8, tc), lambda i, ps: (0, i)),
        ),
        out_shape=jax.ShapeDtypeStruct((8, T), I32),
        name="dest_rows",
    )(pad_starts, tope, rank)


def _sc_mesh():
    return plsc.VectorSubcoreMesh(core_axis_name="c", subcore_axis_name="s")


def _sc_scatter_rows(x, idx, n_out):
    R, W = x.shape
    N = idx.shape[0]
    nsrc = R // SC_WINDOW
    copies = N // R

    @functools.partial(pl.kernel, out_type=jax.ShapeDtypeStruct((n_out, W), x.dtype), mesh=_sc_mesh())
    def k(x_hbm, i_hbm, o_hbm):
        def body(x_vmem, *i_vmems):
            for i_vmem in i_vmems:
                pltpu.sync_copy(x_vmem, o_hbm.at[i_vmem.at[0]])

        idx_spec = lambda c: pl.BlockSpec((1, SC_WINDOW), lambda i: (0, c * nsrc + i))
        pltpu.emit_pipeline(
            body,
            grid=(nsrc,),
            in_specs=[pl.BlockSpec((SC_WINDOW, W), lambda i: (i, 0))] + [idx_spec(c) for c in range(copies)],
            out_specs=[],
            core_axis_name=("c", "s"),
            dimension_semantics=(pltpu.PARALLEL,),
        )(x_hbm, *([i_hbm] * copies))

    return k(x, idx.reshape(1, N))


def _sc_gather_rows(table, idx):
    V, W = table.shape
    N = idx.shape[0]

    @functools.partial(pl.kernel, out_type=jax.ShapeDtypeStruct((N, W), table.dtype), mesh=_sc_mesh())
    def k(t_hbm, i_hbm, o_hbm):
        def body(i_vmem, o_vmem):
            pltpu.sync_copy(t_hbm.at[i_vmem.at[0]], o_vmem)

        pltpu.emit_pipeline(
            body,
            grid=(N // SC_WINDOW,),
            in_specs=[pl.BlockSpec((1, SC_WINDOW), lambda i: (0, i))],
            out_specs=[pl.BlockSpec((SC_WINDOW, W), lambda i: (i, 0))],
            core_axis_name=("c", "s"),
            dimension_semantics=(pltpu.PARALLEL,),
        )(i_hbm, o_hbm)

    return k(table, idx.reshape(1, N))


def _expert_kernel(be_ref, nv_ref, fs_ref, nxt_ref, uni_ref, xs_ref, wgu_hbm, bgu_ref, wdn_hbm, bdn_ref, ys_ref,
                   wgu_f32, wdn_f32, cur_slot, sem, *, layer):
    F = wdn_f32.shape[1]

    def weight_copies(e, slot):
        return (pltpu.make_async_copy(wgu_hbm.at[layer, e], wgu_f32.at[slot], sem.at[slot, 0]),
                pltpu.make_async_copy(wdn_hbm.at[layer, e], wdn_f32.at[slot], sem.at[slot, 1]))

    @pl.when(pl.program_id(0) == 0)
    def _():
        for cp in weight_copies(be_ref[0], 0):
            cp.start()

    def ffn(x, e):
        slot = cur_slot[0]
        hgu = jnp.dot(x, wgu_f32[slot], preferred_element_type=F32) + bgu_ref[e]
        g = jnp.minimum(hgu[:, :F], SWIGLU_LIMIT)
        lin = jnp.clip(hgu[:, F:], -SWIGLU_LIMIT, SWIGLU_LIMIT)
        act = g * jax.nn.sigmoid(SWIGLU_ALPHA * g) * (lin + 1.0)
        return _pack_rows(jnp.dot(act, wdn_f32[slot], preferred_element_type=F32) + bdn_ref[e])

    def load_rows(rs):
        return _unpack_rows(jnp.concatenate([xs_ref[pn, rs, :] for pn in range(xs_ref.shape[0])], axis=1))

    def store_rows(rs, y):
        for pn in range(ys_ref.shape[0]):
            ys_ref[pn, rs, :] = y[:, pn * SC_ROW_WORDS:(pn + 1) * SC_ROW_WORDS]

    for c in range(EXPERT_STEP_BLOCKS // EXPERT_CHAIN_BLOCKS):
        chain = pl.program_id(0) * (EXPERT_STEP_BLOCKS // EXPERT_CHAIN_BLOCKS) + c
        uniform = uni_ref[chain] > 0

        @pl.when(uniform)
        def _():
            rs = slice(c * EXPERT_CHAIN_BLOCKS * EXPERT_ROWS, (c + 1) * EXPERT_CHAIN_BLOCKS * EXPERT_ROWS)
            store_rows(rs, ffn(load_rows(rs), be_ref[chain * EXPERT_CHAIN_BLOCKS]))

        for j in range(EXPERT_CHAIN_BLOCKS):
            b = chain * EXPERT_CHAIN_BLOCKS + j
            r0 = (c * EXPERT_CHAIN_BLOCKS + j) * EXPERT_ROWS
            rs = slice(r0, r0 + EXPERT_ROWS)
            e = be_ref[b]
            nv = nv_ref[b]
            fs = fs_ref[b]

            @pl.when(fs >= 0)
            def _():
                for cp in weight_copies(e, fs):
                    cp.wait()
                cur_slot[0] = fs
                nxt = nxt_ref[b]

                @pl.when(nxt >= 0)
                def _():
                    for cp in weight_copies(nxt, 1 - fs):
                        cp.start()

            @pl.when(jnp.logical_not(uniform) & (nv > 0))
            def _():
                x = load_rows(rs)
                rows = lax.broadcasted_iota(I32, x.shape, 0)
                store_rows(rs, ffn(jnp.where(rows < nv, x, 0.0), e))

            @pl.when(nv == 0)
            def _():
                ys_ref[:, rs, :] = jnp.zeros((ys_ref.shape[0], EXPERT_ROWS, SC_ROW_WORDS), U32)


def _experts(layer, plan, xs, w_gu, b_gu, w_dn, b_dn):
    planes, P, _ = xs.shape
    _, E, D, F2 = w_gu.shape
    F = F2 // 2
    step_rows = EXPERT_STEP_BLOCKS * EXPERT_ROWS
    by_step = lambda i, be, nv, fs, nxt, uni: (0, i, 0)
    layer_all = lambda i, be, nv, fs, nxt, uni: (layer, 0, 0, 0)
    return pl.pallas_call(
        functools.partial(_expert_kernel, layer=layer),
        grid_spec=pltpu.PrefetchScalarGridSpec(
            num_scalar_prefetch=5,
            grid=(P // step_rows,),
            in_specs=[
                pl.BlockSpec((planes, step_rows, SC_ROW_WORDS), by_step),
                pl.BlockSpec(memory_space=pl.ANY),
                pl.BlockSpec((None, E, 1, F2), layer_all),
                pl.BlockSpec(memory_space=pl.ANY),
                pl.BlockSpec((None, E, 1, D), layer_all),
            ],
            out_specs=pl.BlockSpec((planes, step_rows, SC_ROW_WORDS), by_step),
            scratch_shapes=[
                pltpu.VMEM((2, D, F2), F32), pltpu.VMEM((2, F, D), F32),
                pltpu.SMEM((1,), I32),
                pltpu.SemaphoreType.DMA((2, 2)),
            ],
        ),
        out_shape=jax.ShapeDtypeStruct((planes, P, SC_ROW_WORDS), U32),
        compiler_params=pltpu.CompilerParams(
            dimension_semantics=("arbitrary",), vmem_limit_bytes=VMEM_LIMIT),
        name="experts",
    )(plan["block_e"], plan["block_nv"], plan["first_slot"], plan["next_e"], plan["step_uniform"],
      xs, w_gu, b_gu.reshape(-1, E, 1, F2), w_dn, b_dn.reshape(-1, E, 1, D))


def _combine_ple_kernel(h1_ref, yg_ref, gates_ref, gple_ref, wpg_ref, p_ref, wple_ref, gfin_ref, *rest, final):
    o_ref = rest[-1]
    for r in range(h1_ref.shape[0] // COMBINE_ROWS):
        rs = slice(r * COMBINE_ROWS, (r + 1) * COMBINE_ROWS)
        h2 = h1_ref[rs, :]
        gates = gates_ref[rs, :]
        for k in range(TOP_K):
            yk = jnp.concatenate([yg_ref[k, pn, rs, :] for pn in range(yg_ref.shape[1])], axis=1)
            h2 = h2 + gates[:, k:k + 1] * _unpack_rows(yk)
        xn = _rms(h2, gple_ref[...]).astype(BF16)
        gate = jax.nn.sigmoid(jnp.dot(xn, wpg_ref[...], preferred_element_type=F32))
        pe = jnp.dot(p_ref[rs, :].astype(BF16), wple_ref[...], preferred_element_type=F32)
        h3 = h2 + gate * pe
        if final:
            h3 = _rms(h3, gfin_ref[...])
        o_ref[rs, :] = h3


def _combine_ple(layer, group, h1, yg, gates, p, prm, g_final, final, h_prev):
    Tg, D = h1.shape
    T = Tg * MOE_GROUPS
    tm = min(ROW_TILE, Tg)
    PD = p.shape[2]
    planes = yg.shape[1]
    off = group * (Tg // tm)
    const = lambda shape: pl.BlockSpec(shape, lambda i: (0,) * len(shape))
    in_specs = [
        pl.BlockSpec((tm, D), lambda i: (i, 0)),
        pl.BlockSpec((TOP_K, planes, tm, SC_ROW_WORDS), lambda i: (0, 0, i, 0)),
        pl.BlockSpec((tm, 8), lambda i: (i, 0)),
        const((1, D)),
        const((D, D)),
        pl.BlockSpec((None, tm, PD), lambda i: (layer, off + i, 0)),
        const((PD, D)),
        const((1, D)),
    ]
    args = [h1, yg, gates, prm["g_ple"], prm["wpg"], p, prm["wple"], g_final]
    aliases = {}
    if h_prev is not None:
        in_specs.append(pl.BlockSpec(memory_space=pl.ANY))
        args.append(h_prev)
        aliases = {len(args) - 1: 0}
    return pl.pallas_call(
        functools.partial(_combine_ple_kernel, final=final),
        grid=(Tg // tm,),
        in_specs=in_specs,
        out_specs=pl.BlockSpec((tm, D), lambda i: (off + i, 0)),
        out_shape=jax.ShapeDtypeStruct((T, D), F32),
        input_output_aliases=aliases,
        compiler_params=pltpu.CompilerParams(
            dimension_semantics=("parallel",), vmem_limit_bytes=VMEM_LIMIT),
        name="combine_ple",
    )(*args)


def _layer_params(i, w):
    D = w["w_in"].shape[1]
    H = FOX_HEADS
    w_in = w["w_in"][i]
    c1 = 2 * GMLP_WIDTH
    c2, c3, c4 = c1 + FOX_WIDTH, c1 + 2 * FOX_WIDTH, c1 + 3 * FOX_WIDTH
    wk = jnp.zeros((D, H, HEAD_PAD), F32).at[:, :, :FOX_HEAD_DIM].set(
        w_in[:, c2:c3].reshape(D, H, FOX_HEAD_DIM)).reshape(D, H * HEAD_PAD)
    return {
        "g_mix": w["g_mix"][i].reshape(1, D),
        "wuv": w_in[:, :c1].astype(BF16),
        "wqt": (w_in[:, c1:c2] * (FOX_HEAD_DIM ** -0.5 * LOG2E)).T.astype(BF16),
        "wk": wk.astype(BF16),
        "wvt": w_in[:, c3:c4].T.astype(BF16),
        "wft": w_in[:, c4:].T.astype(BF16),
        "b_f": w["b_f"][i].reshape(H, 1),
        "ln_g": w["ln_g"][i].reshape(1, GMLP_WIDTH),
        "ln_b": w["ln_b"][i].reshape(1, GMLP_WIDTH),
        "w_s": w["w_s"][i],
        "b_s": w["b_s"][i].reshape(GMLP_GROUPS, GMLP_BLOCK, 1),
        "g_a": w["g_a"][i].reshape(1, GMLP_WIDTH),
        "woa": w["w_out"][i][:GMLP_WIDTH].astype(BF16),
        "wob": w["w_out"][i][GMLP_WIDTH:].astype(BF16),
        "g_b": w["g_b"][i].reshape(FOX_WIDTH, 1),
        "g_moe": w["g_moe"][i].reshape(1, D),
        "wrt": w["w_router"][i].T.astype(BF16),
        "b_router": w["b_router"][i].reshape(N_EXPERTS, 1),
        "g_ple": w["g_ple"][i].reshape(1, D),
        "wpg": w["w_ple_gate"][i].astype(BF16),
        "wple": w["w_ple"][i].astype(BF16),
    }


def _bias_constants(tm):
    H = FOX_HEADS
    j = jnp.arange(tm)
    tri_incl = (j[:, None] <= j[None, :]).astype(BF16)
    tri_strict = (j[:, None] < j[None, :]).astype(BF16)
    part = jnp.arange(BIAS_PARTS)[:, None]
    head = jnp.arange(H)[None, :]
    rows = (part * H + head).reshape(-1)
    cols = (head * HEAD_PAD + FOX_HEAD_DIM + BIAS_PARTS + part).reshape(-1)
    selk = jnp.zeros((32, H * HEAD_PAD), F32).at[rows, cols].set(-1.0).astype(BF16)
    one_cols = (jnp.arange(H)[:, None] * HEAD_PAD + FOX_HEAD_DIM + jnp.arange(BIAS_PARTS)[None, :]).reshape(-1)
    onesk = jnp.zeros((1, H * HEAD_PAD), F32).at[0, one_cols].set(1.0)
    return tri_incl, tri_strict, selk, onesk


def _moe_dispatch_plan(counts, n_blocks):
    padded = (counts + EXPERT_ROWS - 1) // EXPERT_ROWS * EXPERT_ROWS
    pad_ends = jnp.cumsum(padded)
    pad_starts = (pad_ends - padded).astype(I32)
    block_start = jnp.arange(n_blocks, dtype=I32) * EXPERT_ROWS
    n_before = jnp.sum((pad_ends[None, :] <= block_start[:, None]).astype(I32), axis=1)
    block_e = jnp.minimum(n_before, N_EXPERTS - 1).astype(I32)
    own = block_e[:, None] == jnp.arange(N_EXPERTS, dtype=I32)[None, :]
    seg_end = jnp.sum(jnp.where(own, (pad_starts + counts)[None, :], 0), axis=1)
    block_nv = jnp.clip(seg_end - block_start, 0, EXPERT_ROWS)
    block_nv = jnp.where(block_start < pad_ends[-1], block_nv, 0).astype(I32)
    prev_e = jnp.concatenate([jnp.full((1,), -1, I32), block_e[:-1]])
    first = (block_nv > 0) & (block_e != prev_e)
    run = jnp.cumsum(first.astype(I32)) - 1
    first_slot = jnp.where(first, run % 2, -1).astype(I32)
    ar = jnp.arange(N_EXPERTS, dtype=I32)
    later = (ar[None, :] > ar[:, None]) & (counts > 0)[None, :]
    next_of = jnp.min(jnp.where(later, ar[None, :], N_EXPERTS), axis=1)
    next_of = jnp.where(next_of < N_EXPERTS, next_of, -1)
    next_e = jnp.sum(jnp.where(own, next_of[None, :], 0), axis=1).astype(I32)
    plain = ((block_nv == EXPERT_ROWS) & jnp.logical_not(first)).reshape(-1, EXPERT_CHAIN_BLOCKS)
    step_uniform = jnp.all(plain, axis=1).astype(I32)
    return dict(pad_starts=pad_starts, block_e=block_e, block_nv=block_nv, first_slot=first_slot, next_e=next_e,
                step_uniform=step_uniform)


def kernel(x, p, g_mix, w_in, ln_g, ln_b, w_s, b_s, b_f, g_a, g_b, w_out, g_moe, w_router, b_router,
           w_gu, b_gu, w_dn, b_dn, g_ple, w_ple_gate, w_ple, g_final):
    B, S, D = x.shape
    T = B * S
    depth = w_in.shape[0]
    assert B % MOE_GROUPS == 0 and S % min(ROW_TILE, S) == 0 and S % min(ATTN_TILE, S) == 0
    assert (T // MOE_GROUPS) % SC_WINDOW == 0 and D % (2 * SC_ROW_WORDS) == 0
    w = dict(g_mix=g_mix, w_in=w_in, ln_g=ln_g, ln_b=ln_b, w_s=w_s, b_s=b_s, b_f=b_f, g_a=g_a, g_b=g_b,
             w_out=w_out, g_moe=g_moe, w_router=w_router, b_router=b_router,
             g_ple=g_ple, w_ple_gate=w_ple_gate, w_ple=w_ple)
    tri_incl, tri_strict, selk, onesk = _bias_constants(min(ROW_TILE, S))
    G = MOE_GROUPS
    Tg = T // G
    n_blocks = -(-(Tg * TOP_K) // EXPERT_ROWS) + N_EXPERTS
    n_blocks = -(-n_blocks // EXPERT_STEP_BLOCKS) * EXPERT_STEP_BLOCKS
    P = n_blocks * EXPERT_ROWS
    planes = (D // 2) // SC_ROW_WORDS
    plane_base = (jnp.arange(planes, dtype=I32) * P)[None, :, None]
    p_rows = p.reshape(depth, T, p.shape[-1])

    h = x.reshape(T, D)
    for i in range(depth):
        prm = _layer_params(i, w)
        prm.update(tri_incl=tri_incl, selk=selk, onesk=onesk)
        routed = []
        idx = jnp.zeros((8, 128), I32)
        for g in range(G):
            ya, qt, kx, vt = _mix_in(g, h, B, S, prm, idx)
            ybt = _fox_attn(qt, kx, vt, B // G, S)
            h1, xp, tope, rank, gates, counts = _out_router(g, h, ya, ybt, B, S, prm, tri_strict)
            plan = _moe_dispatch_plan(counts[:, 0], n_blocks)
            dest = _dest_rows(plan["pad_starts"], tope, rank)[:TOP_K]
            idx = (dest[:, None, :] + plane_base).reshape(-1)
            xs = _sc_scatter_rows(xp.reshape(planes * Tg, SC_ROW_WORDS), idx, planes * P)
            routed.append((h1, gates, plan, idx, xs))
        h_out = None
        for g, (h1, gates, plan, idx, xs) in enumerate(routed):
            ys = _experts(i, plan, xs.reshape(planes, P, SC_ROW_WORDS), w_gu, b_gu, w_dn, b_dn)
            yg = _sc_gather_rows(ys.reshape(planes * P, SC_ROW_WORDS), idx)
            h_out = _combine_ple(i, g, h1, yg.reshape(TOP_K, planes, Tg, SC_ROW_WORDS), gates, p_rows, prm,
                                 g_final.reshape(1, D), i == depth - 1, h_out)
        h = h_out
    return h.reshape(B, S, D)
```

```python
import functools

import jax
import jax.numpy as jnp
from jax import lax
from jax.experimental import pallas as pl
from jax.experimental.pallas import tpu as pltpu
from jax.experimental.pallas import tpu_sc as plsc

F32 = jnp.float32
BF16 = jnp.bfloat16
I32 = jnp.int32
U32 = jnp.uint32

CHUNK = 64
GMLP_WIDTH = 512
GMLP_GROUPS = 4
GMLP_GROUP_CH = 128
GMLP_BLOCK = 128
FOX_WIDTH = 512
FOX_HEAD_DIM = 64
FOX_HEADS = 8
N_EXPERTS = 32
TOP_K = 4
EXPERT_ROWS = 256
SWIGLU_LIMIT = 7.0
SWIGLU_ALPHA = 1.702
RMS_EPS = 1e-5
LN_EPS = 1e-5

HEAD_PAD = 128
BIAS_PARTS = 3
ROW_TILE = 1024
ATTN_TILE = 512
ATTN_HEADS = 1
ACC_ROWS = 80
LOG2E = 1.4426950408889634
SC_WINDOW = 128
SC_ROW_WORDS = 256
COMBINE_ROWS = 256
EXPERT_STEP_BLOCKS = 4
EXPERT_CHAIN_BLOCKS = 4
MOE_GROUPS = 2
VMEM_LIMIT = 56 * 1024 * 1024
NEG_BIG = -1e30

_NT = (((1,), (1,)), ((), ()))
_TN = (((0,), (0,)), ((), ()))


def _rms(x, g):
    return x * lax.rsqrt(jnp.mean(x * x, axis=-1, keepdims=True) + RMS_EPS) * g


def _split3(c):
    a1 = c.astype(BF16).astype(F32)
    r1 = c - a1
    a2 = r1.astype(BF16).astype(F32)
    a3 = (r1 - a2).astype(BF16).astype(F32)
    return a1, a2, a3


def _pack_rows(x):
    n = x.shape[1] // 2
    lo = lax.bitcast_convert_type(x[:, :n].astype(BF16).astype(F32), U32) >> 16
    hi = lax.bitcast_convert_type(x[:, n:].astype(BF16).astype(F32), U32) & jnp.uint32(0xFFFF0000)
    return lo | hi


def _unpack_rows(u):
    lo = lax.bitcast_convert_type(u << 16, F32)
    hi = lax.bitcast_convert_type(u & jnp.uint32(0xFFFF0000), F32)
    return jnp.concatenate([lo, hi], axis=1)


def _store_planes(ref, u):
    for j in range(ref.shape[0]):
        ref[j] = u[:, j * SC_ROW_WORDS:(j + 1) * SC_ROW_WORDS]


def _mix_in_kernel(h_ref, gmix_ref, wuv_ref, wqt_ref, wk_ref, wvt_ref, wft_ref, bf_ref, lng_ref, lnb_ref,
                   ws_ref, bs_ref, ga_ref, tri_ref, selk_ref, onesk_ref, after_ref,
                   ya_ref, qt_ref, k_ref, vt_ref, carry_ref, ya_scr):
    del after_ref
    tm = h_ref.shape[0]

    @pl.when(pl.program_id(1) == 0)
    def _():
        carry_ref[...] = jnp.zeros_like(carry_ref)

    xn = _rms(h_ref[...], gmix_ref[...]).astype(BF16)

    z = jnp.dot(xn, wuv_ref[...], preferred_element_type=F32)
    qt = lax.dot_general(wqt_ref[...], xn, _NT, preferred_element_type=F32)
    vt_ref[0] = lax.dot_general(wvt_ref[...], xn, _NT, preferred_element_type=F32).astype(BF16)
    kp = jnp.dot(xn, wk_ref[...], preferred_element_type=F32)
    ft = lax.dot_general(wft_ref[...], xn, _NT, preferred_element_type=F32) + bf_ref[...]

    z = 0.5 * z * (1.0 + lax.erf(z * (0.5 ** 0.5)))
    u = z[:, :GMLP_WIDTH]
    v = z[:, GMLP_WIDTH:]
    mu = jnp.mean(v, axis=-1, keepdims=True)
    vc = v - mu
    var = jnp.mean(vc * vc, axis=-1, keepdims=True)
    vn = (vc * lax.rsqrt(var + LN_EPS) * lng_ref[...] + lnb_ref[...]).astype(BF16)
    pi = lax.broadcasted_iota(I32, (GMLP_BLOCK, GMLP_BLOCK), 0) // CHUNK
    pj = lax.broadcasted_iota(I32, (GMLP_BLOCK, GMLP_BLOCK), 1) // CHUNK
    for g in range(GMLP_GROUPS):
        wm = jnp.where(pj <= pi, ws_ref[g], 0.0).astype(BF16)
        cs = slice(g * GMLP_GROUP_CH, (g + 1) * GMLP_GROUP_CH)
        for n in range(tm // GMLP_BLOCK):
            rs = slice(n * GMLP_BLOCK, (n + 1) * GMLP_BLOCK)
            sv = jnp.dot(wm, vn[rs, cs], preferred_element_type=F32) + bs_ref[g]
            ya_scr[rs, cs] = u[rs, cs] * sv
    ya_ref[...] = _rms(ya_scr[...], ga_ref[...]).astype(BF16)

    lf = (jnp.minimum(ft, 0.0) - jnp.log1p(jnp.exp(-jnp.abs(ft)))) * LOG2E
    zpad = jnp.zeros((32 - BIAS_PARTS * FOX_HEADS, tm), F32)
    stack = jnp.concatenate(list(_split3(lf)) + [zpad], axis=0).astype(BF16)
    cum = jnp.dot(stack, tri_ref[...], preferred_element_type=F32)
    c = cum[0:8] + cum[8:16] + cum[16:24] + carry_ref[:, 0:1]
    carry_ref[...] = jnp.broadcast_to(c[:, tm - 1:tm], carry_ref.shape)
    a1, a2, a3 = _split3(c)

    si = lax.broadcasted_iota(I32, (8, tm), 0)
    zrows = jnp.zeros((HEAD_PAD - FOX_HEAD_DIM - 8, tm), F32)
    for hd in range(FOX_HEADS):
        b1 = jnp.broadcast_to(a1[hd:hd + 1], (8, tm))
        b2 = jnp.broadcast_to(a2[hd:hd + 1], (8, tm))
        b3 = jnp.broadcast_to(a3[hd:hd + 1], (8, tm))
        ext = jnp.where(si == 0, b1, jnp.where(si == 1, b2, jnp.where(si == 2, b3,
                        jnp.where(si < 2 * BIAS_PARTS, 1.0, 0.0))))
        blk = jnp.concatenate([qt[hd * FOX_HEAD_DIM:(hd + 1) * FOX_HEAD_DIM], ext, zrows], axis=0)
        qt_ref[0, hd * HEAD_PAD:(hd + 1) * HEAD_PAD, :] = blk.astype(BF16)

    cstack = jnp.concatenate([a1, a2, a3, zpad], axis=0).astype(BF16)
    kext = lax.dot_general(cstack, selk_ref[...], _TN, preferred_element_type=F32)
    k_ref[...] = (kp + kext + onesk_ref[...]).astype(BF16)


def _mix_in(group, h, B, S, prm, after):
    D = h.shape[1]
    Bg = B // MOE_GROUPS
    Tg = Bg * S
    tm = min(ROW_TILE, S)
    ns = S // tm
    H = FOX_HEADS
    const = lambda shape: pl.BlockSpec(shape, lambda b, s: (0,) * len(shape))
    return pl.pallas_call(
        _mix_in_kernel,
        grid=(Bg, ns),
        in_specs=[
            pl.BlockSpec((tm, D), lambda b, s: ((group * Bg + b) * ns + s, 0)),
            const((1, D)),
            const((D, 2 * GMLP_WIDTH)),
            const((FOX_WIDTH, D)),
            const((D, H * HEAD_PAD)),
            const((FOX_WIDTH, D)),
            const((H, D)),
            const((H, 1)),
            const((1, GMLP_WIDTH)),
            const((1, GMLP_WIDTH)),
            const((GMLP_GROUPS, GMLP_BLOCK, GMLP_BLOCK)),
            const((GMLP_GROUPS, GMLP_BLOCK, 1)),
            const((1, GMLP_WIDTH)),
            const((tm, tm)),
            const((32, H * HEAD_PAD)),
            const((1, H * HEAD_PAD)),
            pl.BlockSpec(memory_space=pl.ANY),
        ],
        out_specs=[
            pl.BlockSpec((tm, GMLP_WIDTH), lambda b, s: (b * ns + s, 0)),
            pl.BlockSpec((1, H * HEAD_PAD, tm), lambda b, s: (b, 0, s)),
            pl.BlockSpec((tm, H * HEAD_PAD), lambda b, s: (b * ns + s, 0)),
            pl.BlockSpec((1, FOX_WIDTH, tm), lambda b, s: (b, 0, s)),
        ],
        out_shape=[
            jax.ShapeDtypeStruct((Tg, GMLP_WIDTH), BF16),
            jax.ShapeDtypeStruct((Bg, H * HEAD_PAD, S), BF16),
            jax.ShapeDtypeStruct((Tg, H * HEAD_PAD), BF16),
            jax.ShapeDtypeStruct((Bg, FOX_WIDTH, S), BF16),
        ],
        scratch_shapes=[pltpu.VMEM((H, 128), F32), pltpu.VMEM((tm, GMLP_WIDTH), F32)],
        compiler_params=pltpu.CompilerParams(
            dimension_semantics=("arbitrary", "arbitrary"), vmem_limit_bytes=VMEM_LIMIT),
        name="mix_in",
    )(h, prm["g_mix"], prm["wuv"], prm["wqt"], prm["wk"], prm["wvt"], prm["wft"], prm["b_f"],
      prm["ln_g"], prm["ln_b"], prm["w_s"], prm["b_s"], prm["g_a"], prm["tri_incl"], prm["selk"], prm["onesk"],
      after)


def _fox_attn_kernel(qt_ref, k_ref, vt_ref, ot_ref):
    S = k_ref.shape[0]
    t = min(ATTN_TILE, S)
    nq = S // t
    hf = t // 2
    ones_rows = jnp.ones((ACC_ROWS - FOX_HEAD_DIM, t), BF16)
    pairs = [(qi, kj) for qi in range(nq) for kj in range(qi + 1)]

    def tri(rows, cols):
        return lax.broadcasted_iota(I32, (rows, cols), 0) <= lax.broadcasted_iota(I32, (rows, cols), 1)

    def scores(qi, kj):
        out = []
        for hh in range(ATTN_HEADS):
            hs = slice(hh * HEAD_PAD, (hh + 1) * HEAD_PAD)
            if kj < qi:
                out.append([jnp.dot(k_ref[kj * t:(kj + 1) * t, hs], qt_ref[0, hs, qi * t:(qi + 1) * t],
                                    preferred_element_type=F32)])
            else:
                top = jnp.dot(k_ref[kj * t:kj * t + hf, hs], qt_ref[0, hs, qi * t:(qi + 1) * t],
                              preferred_element_type=F32)
                bot = jnp.dot(k_ref[kj * t + hf:(kj + 1) * t, hs], qt_ref[0, hs, qi * t + hf:(qi + 1) * t],
                              preferred_element_type=F32)
                out.append([jnp.where(tri(hf, t), top, NEG_BIG), jnp.where(tri(hf, hf), bot, NEG_BIG)])
        return out

    s_next = scores(*pairs[0])
    state = None
    for n, (qi, kj) in enumerate(pairs):
        s_cur = s_next
        if n + 1 < len(pairs):
            s_next = scores(*pairs[n + 1])
        new_state = []
        for hh in range(ATTN_HEADS):
            parts = s_cur[hh]
            v = vt_ref[0, hh * FOX_HEAD_DIM:(hh + 1) * FOX_HEAD_DIM, kj * t:(kj + 1) * t]
            v_aug = jnp.concatenate([v, ones_rows], axis=0)
            mx = jnp.max(parts[0], axis=0, keepdims=True)
            if len(parts) == 2:
                mx = jnp.concatenate(
                    [mx[:, :hf], jnp.maximum(mx[:, hf:], jnp.max(parts[1], axis=0, keepdims=True))], axis=1)
            m_new = mx if kj == 0 else jnp.maximum(state[hh][0], mx)
            if len(parts) == 1:
                pv = jnp.dot(v_aug, jnp.exp2(parts[0] - m_new).astype(BF16), preferred_element_type=F32)
            else:
                pv = jnp.dot(v_aug[:, :hf], jnp.exp2(parts[0] - m_new).astype(BF16), preferred_element_type=F32)
                pv_r = jnp.dot(v_aug[:, hf:], jnp.exp2(parts[1] - m_new[:, hf:]).astype(BF16),
                               preferred_element_type=F32)
                pv = jnp.concatenate([pv[:, :hf], pv[:, hf:] + pv_r], axis=1)
            acc = pv if kj == 0 else jnp.exp2(state[hh][0] - m_new) * state[hh][1] + pv
            new_state.append((m_new, acc))
        state = new_state
        if kj == qi:
            for hh in range(ATTN_HEADS):
                acc = state[hh][1]
                o = acc[:FOX_HEAD_DIM] / acc[FOX_HEAD_DIM:FOX_HEAD_DIM + 1]
                ot_ref[0, hh * FOX_HEAD_DIM:(hh + 1) * FOX_HEAD_DIM, qi * t:(qi + 1) * t] = o.astype(BF16)


def _fox_attn(qt, kx, vt, B, S):
    G = ATTN_HEADS
    return pl.pallas_call(
        _fox_attn_kernel,
        grid=(B, FOX_HEADS // G),
        in_specs=[
            pl.BlockSpec((1, G * HEAD_PAD, S), lambda b, h: (b, h, 0)),
            pl.BlockSpec((S, G * HEAD_PAD), lambda b, h: (b, h)),
            pl.BlockSpec((1, G * FOX_HEAD_DIM, S), lambda b, h: (b, h, 0)),
        ],
        out_specs=pl.BlockSpec((1, G * FOX_HEAD_DIM, S), lambda b, h: (b, h, 0)),
        out_shape=jax.ShapeDtypeStruct((B, FOX_WIDTH, S), BF16),
        compiler_params=pltpu.CompilerParams(
            dimension_semantics=("parallel", "parallel"), vmem_limit_bytes=VMEM_LIMIT),
        name="fox_attn",
    )(qt, kx, vt)


def _out_router_kernel(h_ref, ya_ref, ybt_ref, woa_ref, wob_ref, gb_ref, gmoe_ref, wrt_ref, br_ref, tri_ref,
                       h1_ref, xp_ref, tope_ref, rank_ref, gates_ref, counts_ref, carry_ref):
    tm = h_ref.shape[0]

    @pl.when(pl.program_id(0) == 0)
    def _():
        carry_ref[...] = jnp.zeros_like(carry_ref)

    yb = ybt_ref[0].astype(F32)
    ybn = (yb * lax.rsqrt(jnp.mean(yb * yb, axis=0, keepdims=True) + RMS_EPS) * gb_ref[...]).astype(BF16)
    y = jnp.dot(ya_ref[...], woa_ref[...], preferred_element_type=F32)
    y = y + lax.dot_general(ybn, wob_ref[...], _TN, preferred_element_type=F32)
    h1 = h_ref[...] + y
    h1_ref[...] = h1
    xn = _rms(h1, gmoe_ref[...])
    _store_planes(xp_ref, _pack_rows(xn))

    logits = lax.dot_general(wrt_ref[...], xn.astype(BF16), _NT, preferred_element_type=F32) + br_ref[...]
    ri = lax.broadcasted_iota(I32, (N_EXPERTS, tm), 0)
    vals, idxs = [], []
    l = logits
    for _ in range(TOP_K):
        m = jnp.max(l, axis=0, keepdims=True)
        idx = jnp.min(jnp.where(l == m, ri, N_EXPERTS), axis=0, keepdims=True)
        vals.append(m)
        idxs.append(idx)
        l = jnp.where(ri == idx, -jnp.inf, l)
    es = [jnp.exp(vk - vals[0]) for vk in vals]
    den = es[0] + es[1] + es[2] + es[3]
    zrow = jnp.zeros((8 - TOP_K, tm), F32)
    gates_ref[...] = jnp.concatenate([e / den for e in es] + [zrow], axis=0).T

    sel = jnp.zeros((N_EXPERTS, tm), F32)
    for idx in idxs:
        sel = sel + jnp.where(ri == idx, 1.0, 0.0)
    cnt = jnp.dot(sel.astype(BF16), tri_ref[...], preferred_element_type=F32) + carry_ref[:, 0:1]
    ranks = [jnp.sum(jnp.where(ri == idx, cnt, 0.0), axis=0, keepdims=True) for idx in idxs]
    zi = jnp.zeros((8 - TOP_K, tm), I32)
    rank_ref[...] = jnp.concatenate([r.astype(I32) for r in ranks] + [zi], axis=0)
    tope_ref[...] = jnp.concatenate(idxs + [zi], axis=0)
    total = carry_ref[...] + jnp.sum(sel, axis=1, keepdims=True)
    carry_ref[...] = total
    counts_ref[...] = total.astype(I32)


def _out_router(group, h, ya, ybt, B, S, prm, tri_strict):
    T, D = h.shape
    Tg = T // MOE_GROUPS
    tm = min(ROW_TILE, S)
    ns = S // tm
    E = N_EXPERTS
    planes = (D // 2) // SC_ROW_WORDS
    off = group * (Tg // tm)
    const = lambda shape: pl.BlockSpec(shape, lambda i: (0,) * len(shape))
    return pl.pallas_call(
        _out_router_kernel,
        grid=(Tg // tm,),
        in_specs=[
            pl.BlockSpec((tm, D), lambda i: (off + i, 0)),
            pl.BlockSpec((tm, GMLP_WIDTH), lambda i: (i, 0)),
            pl.BlockSpec((1, FOX_WIDTH, tm), lambda i: (i // ns, 0, i % ns)),
            const((GMLP_WIDTH, D)),
            const((FOX_WIDTH, D)),
            const((FOX_WIDTH, 1)),
            const((1, D)),
            const((E, D)),
            const((E, 1)),
            const((tm, tm)),
        ],
        out_specs=[
            pl.BlockSpec((tm, D), lambda i: (i, 0)),
            pl.BlockSpec((planes, tm, SC_ROW_WORDS), lambda i: (0, i, 0)),
            pl.BlockSpec((8, tm), lambda i: (0, i)),
            pl.BlockSpec((8, tm), lambda i: (0, i)),
            pl.BlockSpec((tm, 8), lambda i: (i, 0)),
            const((E, 128)),
        ],
        out_shape=[
            jax.ShapeDtypeStruct((Tg, D), F32),
            jax.ShapeDtypeStruct((planes, Tg, SC_ROW_WORDS), U32),
            jax.ShapeDtypeStruct((8, Tg), I32),
            jax.ShapeDtypeStruct((8, Tg), I32),
            jax.ShapeDtypeStruct((Tg, 8), F32),
            jax.ShapeDtypeStruct((E, 128), I32),
        ],
        scratch_shapes=[pltpu.VMEM((E, 128), F32)],
        compiler_params=pltpu.CompilerParams(
            dimension_semantics=("arbitrary",), vmem_limit_bytes=VMEM_LIMIT),
        name="out_router",
    )(h, ya, ybt, prm["woa"], prm["wob"], prm["g_b"], prm["g_moe"], prm["wrt"], prm["b_router"], tri_strict)


def _dest_kernel(ps_ref, tope_ref, rank_ref, dest_ref):
    e_sel = tope_ref[...]
    d = rank_ref[...]
    for e in range(N_EXPERTS):
        d = d + jnp.where(e_sel == e, ps_ref[e], 0)
    dest_ref[...] = d


def _dest_rows(pad_starts, tope, rank):
    T = tope.shape[1]
    tc = min(4096, T)
    return pl.pallas_call(
        _dest_kernel,
        grid_spec=pltpu.PrefetchScalarGridSpec(
            num_scalar_prefetch=1,
            grid=(T // tc,),
            in_specs=[pl.BlockSpec((8, tc), lambda i, ps: (0, i)),
                      pl.BlockSpec((8, tc), lambda i, ps: (0, i))],
            out_specs=pl.BlockSpec((8, tc), lambda i, ps: (0, i)),
        ),
        out_shape=jax.ShapeDtypeStruct((8, T), I32),
        name="dest_rows",
    )(pad_starts, tope, rank)


def _sc_mesh():
    return plsc.VectorSubcoreMesh(core_axis_name="c", subcore_axis_name="s")


def _sc_scatter_rows(x, idx, n_out):
    R, W = x.shape
    N = idx.shape[0]
    nsrc = R // SC_WINDOW
    copies = N // R

    @functools.partial(pl.kernel, out_type=jax.ShapeDtypeStruct((n_out, W), x.dtype), mesh=_sc_mesh())
    def k(x_hbm, i_hbm, o_hbm):
        def body(x_vmem, *i_vmems):
            for i_vmem in i_vmems:
                pltpu.sync_copy(x_vmem, o_hbm.at[i_vmem.at[0]])

        idx_spec = lambda c: pl.BlockSpec((1, SC_WINDOW), lambda i: (0, c * nsrc + i))
        pltpu.emit_pipeline(
            body,
            grid=(nsrc,),
            in_specs=[pl.BlockSpec((SC_WINDOW, W), lambda i: (i, 0))] + [idx_spec(c) for c in range(copies)],
            out_specs=[],
            core_axis_name=("c", "s"),
            dimension_semantics=(pltpu.PARALLEL,),
        )(x_hbm, *([i_hbm] * copies))

    return k(x, idx.reshape(1, N))


def _sc_gather_rows(table, idx):
    V, W = table.shape
    N = idx.shape[0]

    @functools.partial(pl.kernel, out_type=jax.ShapeDtypeStruct((N, W), table.dtype), mesh=_sc_mesh())
    def k(t_hbm, i_hbm, o_hbm):
        def body(i_vmem, o_vmem):
            pltpu.sync_copy(t_hbm.at[i_vmem.at[0]], o_vmem)

        pltpu.emit_pipeline(
            body,
            grid=(N // SC_WINDOW,),
            in_specs=[pl.BlockSpec((1, SC_WINDOW), lambda i: (0, i))],
            out_specs=[pl.BlockSpec((SC_WINDOW, W), lambda i: (i, 0))],
            core_axis_name=("c", "s"),
            dimension_semantics=(pltpu.PARALLEL,),
        )(i_hbm, o_hbm)

    return k(table, idx.reshape(1, N))


def _expert_kernel(be_ref, nv_ref, fs_ref, nxt_ref, uni_ref, xs_ref, wgu_hbm, bgu_ref, wdn_hbm, bdn_ref, ys_ref,
                   wgu_f32, wdn_f32, cur_slot, sem, *, layer):
    F = wdn_f32.shape[1]

    def weight_copies(e, slot):
        return (pltpu.make_async_copy(wgu_hbm.at[layer, e], wgu_f32.at[slot], sem.at[slot, 0]),
                pltpu.make_async_copy(wdn_hbm.at[layer, e], wdn_f32.at[slot], sem.at[slot, 1]))

    @pl.when(pl.program_id(0) == 0)
    def _():
        for cp in weight_copies(be_ref[0], 0):
            cp.start()

    def ffn(x, e):
        slot = cur_slot[0]
        hgu = jnp.dot(x, wgu_f32[slot], preferred_element_type=F32) + bgu_ref[e]
        g = jnp.minimum(hgu[:, :F], SWIGLU_LIMIT)
        lin = jnp.clip(hgu[:, F:], -SWIGLU_LIMIT, SWIGLU_LIMIT)
        act = g * jax.nn.sigmoid(SWIGLU_ALPHA * g) * (lin + 1.0)
        return _pack_rows(jnp.dot(act, wdn_f32[slot], preferred_element_type=F32) + bdn_ref[e])

    def load_rows(rs):
        return _unpack_rows(jnp.concatenate([xs_ref[pn, rs, :] for pn in range(xs_ref.shape[0])], axis=1))

    def store_rows(rs, y):
        for pn in range(ys_ref.shape[0]):
            ys_ref[pn, rs, :] = y[:, pn * SC_ROW_WORDS:(pn + 1) * SC_ROW_WORDS]

    for c in range(EXPERT_STEP_BLOCKS // EXPERT_CHAIN_BLOCKS):
        chain = pl.program_id(0) * (EXPERT_STEP_BLOCKS // EXPERT_CHAIN_BLOCKS) + c
        uniform = uni_ref[chain] > 0

        @pl.when(uniform)
        def _():
            rs = slice(c * EXPERT_CHAIN_BLOCKS * EXPERT_ROWS, (c + 1) * EXPERT_CHAIN_BLOCKS * EXPERT_ROWS)
            store_rows(rs, ffn(load_rows(rs), be_ref[chain * EXPERT_CHAIN_BLOCKS]))

        for j in range(EXPERT_CHAIN_BLOCKS):
            b = chain * EXPERT_CHAIN_BLOCKS + j
            r0 = (c * EXPERT_CHAIN_BLOCKS + j) * EXPERT_ROWS
            rs = slice(r0, r0 + EXPERT_ROWS)
            e = be_ref[b]
            nv = nv_ref[b]
            fs = fs_ref[b]

            @pl.when(fs >= 0)
            def _():
                for cp in weight_copies(e, fs):
                    cp.wait()
                cur_slot[0] = fs
                nxt = nxt_ref[b]

                @pl.when(nxt >= 0)
                def _():
                    for cp in weight_copies(nxt, 1 - fs):
                        cp.start()

            @pl.when(jnp.logical_not(uniform) & (nv > 0))
            def _():
                x = load_rows(rs)
                rows = lax.broadcasted_iota(I32, x.shape, 0)
                store_rows(rs, ffn(jnp.where(rows < nv, x, 0.0), e))

            @pl.when(nv == 0)
            def _():
                ys_ref[:, rs, :] = jnp.zeros((ys_ref.shape[0], EXPERT_ROWS, SC_ROW_WORDS), U32)


def _experts(layer, plan, xs, w_gu, b_gu, w_dn, b_dn):
    planes, P, _ = xs.shape
    _, E, D, F2 = w_gu.shape
    F = F2 // 2
    step_rows = EXPERT_STEP_BLOCKS * EXPERT_ROWS
    by_step = lambda i, be, nv, fs, nxt, uni: (0, i, 0)
    layer_all = lambda i, be, nv, fs, nxt, uni: (layer, 0, 0, 0)
    return pl.pallas_call(
        functools.partial(_expert_kernel, layer=layer),
        grid_spec=pltpu.PrefetchScalarGridSpec(
            num_scalar_prefetch=5,
            grid=(P // step_rows,),
            in_specs=[
                pl.BlockSpec((planes, step_rows, SC_ROW_WORDS), by_step),
                pl.BlockSpec(memory_space=pl.ANY),
                pl.BlockSpec((None, E, 1, F2), layer_all),
                pl.BlockSpec(memory_space=pl.ANY),
                pl.BlockSpec((None, E, 1, D), layer_all),
            ],
            out_specs=pl.BlockSpec((planes, step_rows, SC_ROW_WORDS), by_step),
            scratch_shapes=[
                pltpu.VMEM((2, D, F2), F32), pltpu.VMEM((2, F, D), F32),
                pltpu.SMEM((1,), I32),
                pltpu.SemaphoreType.DMA((2, 2)),
            ],
        ),
        out_shape=jax.ShapeDtypeStruct((planes, P, SC_ROW_WORDS), U32),
        compiler_params=pltpu.CompilerParams(
            dimension_semantics=("arbitrary",), vmem_limit_bytes=VMEM_LIMIT),
        name="experts",
    )(plan["block_e"], plan["block_nv"], plan["first_slot"], plan["next_e"], plan["step_uniform"],
      xs, w_gu, b_gu.reshape(-1, E, 1, F2), w_dn, b_dn.reshape(-1, E, 1, D))


def _combine_ple_kernel(h1_ref, yg_ref, gates_ref, gple_ref, wpg_ref, p_ref, wple_ref, gfin_ref, *rest, final):
    o_ref = rest[-1]
    for r in range(h1_ref.shape[0] // COMBINE_ROWS):
        rs = slice(r * COMBINE_ROWS, (r + 1) * COMBINE_ROWS)
        h2 = h1_ref[rs, :]
        gates = gates_ref[rs, :]
        for k in range(TOP_K):
            yk = jnp.concatenate([yg_ref[k, pn, rs, :] for pn in range(yg_ref.shape[1])], axis=1)
            h2 = h2 + gates[:, k:k + 1] * _unpack_rows(yk)
        xn = _rms(h2, gple_ref[...]).astype(BF16)
        gate = jax.nn.sigmoid(jnp.dot(xn, wpg_ref[...], preferred_element_type=F32))
        pe = jnp.dot(p_ref[rs, :].astype(BF16), wple_ref[...], preferred_element_type=F32)
        h3 = h2 + gate * pe
        if final:
            h3 = _rms(h3, gfin_ref[...])
        o_ref[rs, :] = h3


def _combine_ple(layer, group, h1, yg, gates, p, prm, g_final, final, h_prev):
    Tg, D = h1.shape
    T = Tg * MOE_GROUPS
    tm = min(ROW_TILE, Tg)
    PD = p.shape[2]
    planes = yg.shape[1]
    off = group * (Tg // tm)
    const = lambda shape: pl.BlockSpec(shape, lambda i: (0,) * len(shape))
    in_specs = [
        pl.BlockSpec((tm, D), lambda i: (i, 0)),
        pl.BlockSpec((TOP_K, planes, tm, SC_ROW_WORDS), lambda i: (0, 0, i, 0)),
        pl.BlockSpec((tm, 8), lambda i: (i, 0)),
        const((1, D)),
        const((D, D)),
        pl.BlockSpec((None, tm, PD), lambda i: (layer, off + i, 0)),
        const((PD, D)),
        const((1, D)),
    ]
    args = [h1, yg, gates, prm["g_ple"], prm["wpg"], p, prm["wple"], g_final]
    aliases = {}
    if h_prev is not None:
        in_specs.append(pl.BlockSpec(memory_space=pl.ANY))
        args.append(h_prev)
        aliases = {len(args) - 1: 0}
    return pl.pallas_call(
        functools.partial(_combine_ple_kernel, final=final),
        grid=(Tg // tm,),
        in_specs=in_specs,
        out_specs=pl.BlockSpec((tm, D), lambda i: (off + i, 0)),
        out_shape=jax.ShapeDtypeStruct((T, D), F32),
        input_output_aliases=aliases,
        compiler_params=pltpu.CompilerParams(
            dimension_semantics=("parallel",), vmem_limit_bytes=VMEM_LIMIT),
        name="combine_ple",
    )(*args)


def _layer_params(i, w):
    D = w["w_in"].shape[1]
    H = FOX_HEADS
    w_in = w["w_in"][i]
    c1 = 2 * GMLP_WIDTH
    c2, c3, c4 = c1 + FOX_WIDTH, c1 + 2 * FOX_WIDTH, c1 + 3 * FOX_WIDTH
    wk = jnp.zeros((D, H, HEAD_PAD), F32).at[:, :, :FOX_HEAD_DIM].set(
        w_in[:, c2:c3].reshape(D, H, FOX_HEAD_DIM)).reshape(D, H * HEAD_PAD)
    return {
        "g_mix": w["g_mix"][i].reshape(1, D),
        "wuv": w_in[:, :c1].astype(BF16),
        "wqt": (w_in[:, c1:c2] * (FOX_HEAD_DIM ** -0.5 * LOG2E)).T.astype(BF16),
        "wk": wk.astype(BF16),
        "wvt": w_in[:, c3:c4].T.astype(BF16),
        "wft": w_in[:, c4:].T.astype(BF16),
        "b_f": w["b_f"][i].reshape(H, 1),
        "ln_g": w["ln_g"][i].reshape(1, GMLP_WIDTH),
        "ln_b": w["ln_b"][i].reshape(1, GMLP_WIDTH),
        "w_s": w["w_s"][i],
        "b_s": w["b_s"][i].reshape(GMLP_GROUPS, GMLP_BLOCK, 1),
        "g_a": w["g_a"][i].reshape(1, GMLP_WIDTH),
        "woa": w["w_out"][i][:GMLP_WIDTH].astype(BF16),
        "wob": w["w_out"][i][GMLP_WIDTH:].astype(BF16),
        "g_b": w["g_b"][i].reshape(FOX_WIDTH, 1),
        "g_moe": w["g_moe"][i].reshape(1, D),
        "wrt": w["w_router"][i].T.astype(BF16),
        "b_router": w["b_router"][i].reshape(N_EXPERTS, 1),
        "g_ple": w["g_ple"][i].reshape(1, D),
        "wpg": w["w_ple_gate"][i].astype(BF16),
        "wple": w["w_ple"][i].astype(BF16),
    }


def _bias_constants(tm):
    H = FOX_HEADS
    j = jnp.arange(tm)
    tri_incl = (j[:, None] <= j[None, :]).astype(BF16)
    tri_strict = (j[:, None] < j[None, :]).astype(BF16)
    part = jnp.arange(BIAS_PARTS)[:, None]
    head = jnp.arange(H)[None, :]
    rows = (part * H + head).reshape(-1)
    cols = (head * HEAD_PAD + FOX_HEAD_DIM + BIAS_PARTS + part).reshape(-1)
    selk = jnp.zeros((32, H * HEAD_PAD), F32).at[rows, cols].set(-1.0).astype(BF16)
    one_cols = (jnp.arange(H)[:, None] * HEAD_PAD + FOX_HEAD_DIM + jnp.arange(BIAS_PARTS)[None, :]).reshape(-1)
    onesk = jnp.zeros((1, H * HEAD_PAD), F32).at[0, one_cols].set(1.0)
    return tri_incl, tri_strict, selk, onesk


def _moe_dispatch_plan(counts, n_blocks):
    padded = (counts + EXPERT_ROWS - 1) // EXPERT_ROWS * EXPERT_ROWS
    pad_ends = jnp.cumsum(padded)
    pad_starts = (pad_ends - padded).astype(I32)
    block_start = jnp.arange(n_blocks, dtype=I32) * EXPERT_ROWS
    n_before = jnp.sum((pad_ends[None, :] <= block_start[:, None]).astype(I32), axis=1)
    block_e = jnp.minimum(n_before, N_EXPERTS - 1).astype(I32)
    own = block_e[:, None] == jnp.arange(N_EXPERTS, dtype=I32)[None, :]
    seg_end = jnp.sum(jnp.where(own, (pad_starts + counts)[None, :], 0), axis=1)
    block_nv = jnp.clip(seg_end - block_start, 0, EXPERT_ROWS)
    block_nv = jnp.where(block_start < pad_ends[-1], block_nv, 0).astype(I32)
    prev_e = jnp.concatenate([jnp.full((1,), -1, I32), block_e[:-1]])
    first = (block_nv > 0) & (block_e != prev_e)
    run = jnp.cumsum(first.astype(I32)) - 1
    first_slot = jnp.where(first, run % 2, -1).astype(I32)
    ar = jnp.arange(N_EXPERTS, dtype=I32)
    later = (ar[None, :] > ar[:, None]) & (counts > 0)[None, :]
    next_of = jnp.min(jnp.where(later, ar[None, :], N_EXPERTS), axis=1)
    next_of = jnp.where(next_of < N_EXPERTS, next_of, -1)
    next_e = jnp.sum(jnp.where(own, next_of[None, :], 0), axis=1).astype(I32)
    plain = ((block_nv == EXPERT_ROWS) & jnp.logical_not(first)).reshape(-1, EXPERT_CHAIN_BLOCKS)
    step_uniform = jnp.all(plain, axis=1).astype(I32)
    return dict(pad_starts=pad_starts, block_e=block_e, block_nv=block_nv, first_slot=first_slot, next_e=next_e,
                step_uniform=step_uniform)


def kernel(x, p, g_mix, w_in, ln_g, ln_b, w_s, b_s, b_f, g_a, g_b, w_out, g_moe, w_router, b_router,
           w_gu, b_gu, w_dn, b_dn, g_ple, w_ple_gate, w_ple, g_final):
    B, S, D = x.shape
    T = B * S
    depth = w_in.shape[0]
    assert B % MOE_GROUPS == 0 and S % min(ROW_TILE, S) == 0 and S % min(ATTN_TILE, S) == 0
    assert (T // MOE_GROUPS) % SC_WINDOW == 0 and D % (2 * SC_ROW_WORDS) == 0
    w = dict(g_mix=g_mix, w_in=w_in, ln_g=ln_g, ln_b=ln_b, w_s=w_s, b_s=b_s, b_f=b_f, g_a=g_a, g_b=g_b,
             w_out=w_out, g_moe=g_moe, w_router=w_router, b_router=b_router,
             g_ple=g_ple, w_ple_gate=w_ple_gate, w_ple=w_ple)
    tri_incl, tri_strict, selk, onesk = _bias_constants(min(ROW_TILE, S))
    G = MOE_GROUPS
    Tg = T // G
    n_blocks = -(-(Tg * TOP_K) // EXPERT_ROWS) + N_EXPERTS
    n_blocks = -(-n_blocks // EXPERT_STEP_BLOCKS) * EXPERT_STEP_BLOCKS
    P = n_blocks * EXPERT_ROWS
    planes = (D // 2) // SC_ROW_WORDS
    plane_base = (jnp.arange(planes, dtype=I32) * P)[None, :, None]
    p_rows = p.reshape(depth, T, p.shape[-1])

    h = x.reshape(T, D)
    for i in range(depth):
        prm = _layer_params(i, w)
        prm.update(tri_incl=tri_incl, selk=selk, onesk=onesk)
        routed = []
        idx = jnp.zeros((8, 128), I32)
        for g in range(G):
            ya, qt, kx, vt = _mix_in(g, h, B, S, prm, idx)
            ybt = _fox_attn(qt, kx, vt, B // G, S)
            h1, xp, tope, rank, gates, counts = _out_router(g, h, ya, ybt, B, S, prm, tri_strict)
            plan = _moe_dispatch_plan(counts[:, 0], n_blocks)
            dest = _dest_rows(plan["pad_starts"], tope, rank)[:TOP_K]
            idx = (dest[:, None, :] + plane_base).reshape(-1)
            xs = _sc_scatter_rows(xp.reshape(planes * Tg, SC_ROW_WORDS), idx, planes * P)
            routed.append((h1, gates, plan, idx, xs))
        h_out = None
        for g, (h1, gates, plan, idx, xs) in enumerate(routed):
            ys = _experts(i, plan, xs.reshape(planes, P, SC_ROW_WORDS), w_gu, b_gu, w_dn, b_dn)
            yg = _sc_gather_rows(ys.reshape(planes * P, SC_ROW_WORDS), idx)
            h_out = _combine_ple(i, g, h1, yg.reshape(TOP_K, planes, Tg, SC_ROW_WORDS), gates, p_rows, prm,
                                 g_final.reshape(1, D), i == depth - 1, h_out)
        h = h_out
    return h.reshape(B, S, D)
```

```python
import functools

import jax
import jax.numpy as jnp
from jax import lax
from jax.experimental import pallas as pl
from jax.experimental.pallas import tpu as pltpu
from jax.experimental.pallas import tpu_sc as plsc

F32 = jnp.float32
BF16 = jnp.bfloat16
I32 = jnp.int32
U32 = jnp.uint32

CHUNK = 64
GMLP_WIDTH = 512
GMLP_GROUPS = 4
GMLP_GROUP_CH = 128
GMLP_BLOCK = 128
FOX_WIDTH = 512
FOX_HEAD_DIM = 64
FOX_HEADS = 8
N_EXPERTS = 32
TOP_K = 4
EXPERT_ROWS = 256
SWIGLU_LIMIT = 7.0
SWIGLU_ALPHA = 1.702
RMS_EPS = 1e-5
LN_EPS = 1e-5

HEAD_PAD = 128
BIAS_PARTS = 3
ROW_TILE = 1024
ATTN_TILE = 512
ATTN_HEADS = 1
ACC_ROWS = 80
LOG2E = 1.4426950408889634
SC_WINDOW = 128
SC_ROW_WORDS = 256
COMBINE_ROWS = 256
EXPERT_STEP_BLOCKS = 4
EXPERT_CHAIN_BLOCKS = 4
MOE_GROUPS = 2
VMEM_LIMIT = 56 * 1024 * 1024
NEG_BIG = -1e30

_NT = (((1,), (1,)), ((), ()))
_TN = (((0,), (0,)), ((), ()))


def _rms(x, g):
    return x * lax.rsqrt(jnp.mean(x * x, axis=-1, keepdims=True) + RMS_EPS) * g


def _split3(c):
    a1 = c.astype(BF16).astype(F32)
    r1 = c - a1
    a2 = r1.astype(BF16).astype(F32)
    a3 = (r1 - a2).astype(BF16).astype(F32)
    return a1, a2, a3


def _pack_rows(x):
    n = x.shape[1] // 2
    lo = lax.bitcast_convert_type(x[:, :n].astype(BF16).astype(F32), U32) >> 16
    hi = lax.bitcast_convert_type(x[:, n:].astype(BF16).astype(F32), U32) & jnp.uint32(0xFFFF0000)
    return lo | hi


def _unpack_rows(u):
    lo = lax.bitcast_convert_type(u << 16, F32)
    hi = lax.bitcast_convert_type(u & jnp.uint32(0xFFFF0000), F32)
    return jnp.concatenate([lo, hi], axis=1)


def _store_planes(ref, u):
    for j in range(ref.shape[0]):
        ref[j] = u[:, j * SC_ROW_WORDS:(j + 1) * SC_ROW_WORDS]


def _mix_in_kernel(h_ref, gmix_ref, wuv_ref, wqt_ref, wk_ref, wvt_ref, wft_ref, bf_ref, lng_ref, lnb_ref,
                   ws_ref, bs_ref, ga_ref, tri_ref, selk_ref, onesk_ref, after_ref,
                   ya_ref, qt_ref, k_ref, vt_ref, carry_ref, ya_scr):
    del after_ref
    tm = h_ref.shape[0]

    @pl.when(pl.program_id(1) == 0)
    def _():
        carry_ref[...] = jnp.zeros_like(carry_ref)

    xn = _rms(h_ref[...], gmix_ref[...]).astype(BF16)

    z = jnp.dot(xn, wuv_ref[...], preferred_element_type=F32)
    qt = lax.dot_general(wqt_ref[...], xn, _NT, preferred_element_type=F32)
    vt_ref[0] = lax.dot_general(wvt_ref[...], xn, _NT, preferred_element_type=F32).astype(BF16)
    kp = jnp.dot(xn, wk_ref[...], preferred_element_type=F32)
    ft = lax.dot_general(wft_ref[...], xn, _NT, preferred_element_type=F32) + bf_ref[...]

    z = 0.5 * z * (1.0 + lax.erf(z * (0.5 ** 0.5)))
    u = z[:, :GMLP_WIDTH]
    v = z[:, GMLP_WIDTH:]
    mu = jnp.mean(v, axis=-1, keepdims=True)
    vc = v - mu
    var = jnp.mean(vc * vc, axis=-1, keepdims=True)
    vn = (vc * lax.rsqrt(var + LN_EPS) * lng_ref[...] + lnb_ref[...]).astype(BF16)
    pi = lax.broadcasted_iota(I32, (GMLP_BLOCK, GMLP_BLOCK), 0) // CHUNK
    pj = lax.broadcasted_iota(I32, (GMLP_BLOCK, GMLP_BLOCK), 1) // CHUNK
    for g in range(GMLP_GROUPS):
        wm = jnp.where(pj <= pi, ws_ref[g], 0.0).astype(BF16)
        cs = slice(g * GMLP_GROUP_CH, (g + 1) * GMLP_GROUP_CH)
        for n in range(tm // GMLP_BLOCK):
            rs = slice(n * GMLP_BLOCK, (n + 1) * GMLP_BLOCK)
            sv = jnp.dot(wm, vn[rs, cs], preferred_element_type=F32) + bs_ref[g]
            ya_scr[rs, cs] = u[rs, cs] * sv
    ya_ref[...] = _rms(ya_scr[...], ga_ref[...]).astype(BF16)

    lf = (jnp.minimum(ft, 0.0) - jnp.log1p(jnp.exp(-jnp.abs(ft)))) * LOG2E
    zpad = jnp.zeros((32 - BIAS_PARTS * FOX_HEADS, tm), F32)
    stack = jnp.concatenate(list(_split3(lf)) + [zpad], axis=0).astype(BF16)
    cum = jnp.dot(stack, tri_ref[...], preferred_element_type=F32)
    c = cum[0:8] + cum[8:16] + cum[16:24] + carry_ref[:, 0:1]
    carry_ref[...] = jnp.broadcast_to(c[:, tm - 1:tm], carry_ref.shape)
    a1, a2, a3 = _split3(c)

    si = lax.broadcasted_iota(I32, (8, tm), 0)
    zrows = jnp.zeros((HEAD_PAD - FOX_HEAD_DIM - 8, tm), F32)
    for hd in range(FOX_HEADS):
        b1 = jnp.broadcast_to(a1[hd:hd + 1], (8, tm))
        b2 = jnp.broadcast_to(a2[hd:hd + 1], (8, tm))
        b3 = jnp.broadcast_to(a3[hd:hd + 1], (8, tm))
        ext = jnp.where(si == 0, b1, jnp.where(si == 1, b2, jnp.where(si == 2, b3,
                        jnp.where(si < 2 * BIAS_PARTS, 1.0, 0.0))))
        blk = jnp.concatenate([qt[hd * FOX_HEAD_DIM:(hd + 1) * FOX_HEAD_DIM], ext, zrows], axis=0)
        qt_ref[0, hd * HEAD_PAD:(hd + 1) * HEAD_PAD, :] = blk.astype(BF16)

    cstack = jnp.concatenate([a1, a2, a3, zpad], axis=0).astype(BF16)
    kext = lax.dot_general(cstack, selk_ref[...], _TN, preferred_element_type=F32)
    k_ref[...] = (kp + kext + onesk_ref[...]).astype(BF16)


def _mix_in(group, h, B, S, prm, after):
    D = h.shape[1]
    Bg = B // MOE_GROUPS
    Tg = Bg * S
    tm = min(ROW_TILE, S)
    ns = S // tm
    H = FOX_HEADS
    const = lambda shape: pl.BlockSpec(shape, lambda b, s: (0,) * len(shape))
    return pl.pallas_call(
        _mix_in_kernel,
        grid=(Bg, ns),
        in_specs=[
            pl.BlockSpec((tm, D), lambda b, s: ((group * Bg + b) * ns + s, 0)),
            const((1, D)),
            const((D, 2 * GMLP_WIDTH)),
            const((FOX_WIDTH, D)),
            const((D, H * HEAD_PAD)),
            const((FOX_WIDTH, D)),
            const((H, D)),
            const((H, 1)),
            const((1, GMLP_WIDTH)),
            const((1, GMLP_WIDTH)),
            const((GMLP_GROUPS, GMLP_BLOCK, GMLP_BLOCK)),
            const((GMLP_GROUPS, GMLP_BLOCK, 1)),
            const((1, GMLP_WIDTH)),
            const((tm, tm)),
            const((32, H * HEAD_PAD)),
            const((1, H * HEAD_PAD)),
            pl.BlockSpec(memory_space=pl.ANY),
        ],
        out_specs=[
            pl.BlockSpec((tm, GMLP_WIDTH), lambda b, s: (b * ns + s, 0)),
            pl.BlockSpec((1, H * HEAD_PAD, tm), lambda b, s: (b, 0, s)),
            pl.BlockSpec((tm, H * HEAD_PAD), lambda b, s: (b * ns + s, 0)),
            pl.BlockSpec((1, FOX_WIDTH, tm), lambda b, s: (b, 0, s)),
        ],
        out_shape=[
            jax.ShapeDtypeStruct((Tg, GMLP_WIDTH), BF16),
            jax.ShapeDtypeStruct((Bg, H * HEAD_PAD, S), BF16),
            jax.ShapeDtypeStruct((Tg, H * HEAD_PAD), BF16),
            jax.ShapeDtypeStruct((Bg, FOX_WIDTH, S), BF16),
        ],
        scratch_shapes=[pltpu.VMEM((H, 128), F32), pltpu.VMEM((tm, GMLP_WIDTH), F32)],
        compiler_params=pltpu.CompilerParams(
            dimension_semantics=("arbitrary", "arbitrary"), vmem_limit_bytes=VMEM_LIMIT),
        name="mix_in",
    )(h, prm["g_mix"], prm["wuv"], prm["wqt"], prm["wk"], prm["wvt"], prm["wft"], prm["b_f"],
      prm["ln_g"], prm["ln_b"], prm["w_s"], prm["b_s"], prm["g_a"], prm["tri_incl"], prm["selk"], prm["onesk"],
      after)


def _fox_attn_kernel(qt_ref, k_ref, vt_ref, ot_ref):
    S = k_ref.shape[0]
    t = min(ATTN_TILE, S)
    nq = S // t
    hf = t // 2
    ones_rows = jnp.ones((ACC_ROWS - FOX_HEAD_DIM, t), BF16)
    pairs = [(qi, kj) for qi in range(nq) for kj in range(qi + 1)]

    def tri(rows, cols):
        return lax.broadcasted_iota(I32, (rows, cols), 0) <= lax.broadcasted_iota(I32, (rows, cols), 1)

    def scores(qi, kj):
        out = []
        for hh in range(ATTN_HEADS):
            hs = slice(hh * HEAD_PAD, (hh + 1) * HEAD_PAD)
            if kj < qi:
                out.append([jnp.dot(k_ref[kj * t:(kj + 1) * t, hs], qt_ref[0, hs, qi * t:(qi + 1) * t],
                                    preferred_element_type=F32)])
            else:
                top = jnp.dot(k_ref[kj * t:kj * t + hf, hs], qt_ref[0, hs, qi * t:(qi + 1) * t],
                              preferred_element_type=F32)
                bot = jnp.dot(k_ref[kj * t + hf:(kj + 1) * t, hs], qt_ref[0, hs, qi * t + hf:(qi + 1) * t],
                              preferred_element_type=F32)
                out.append([jnp.where(tri(hf, t), top, NEG_BIG), jnp.where(tri(hf, hf), bot, NEG_BIG)])
        return out

    s_next = scores(*pairs[0])
    state = None
    for n, (qi, kj) in enumerate(pairs):
        s_cur = s_next
        if n + 1 < len(pairs):
            s_next = scores(*pairs[n + 1])
        new_state = []
        for hh in range(ATTN_HEADS):
            parts = s_cur[hh]
            v = vt_ref[0, hh * FOX_HEAD_DIM:(hh + 1) * FOX_HEAD_DIM, kj * t:(kj + 1) * t]
            v_aug = jnp.concatenate([v, ones_rows], axis=0)
            mx = jnp.max(parts[0], axis=0, keepdims=True)
            if len(parts) == 2:
                mx = jnp.concatenate(
                    [mx[:, :hf], jnp.maximum(mx[:, hf:], jnp.max(parts[1], axis=0, keepdims=True))], axis=1)
            m_new = mx if kj == 0 else jnp.maximum(state[hh][0], mx)
            if len(parts) == 1:
                pv = jnp.dot(v_aug, jnp.exp2(parts[0] - m_new).astype(BF16), preferred_element_type=F32)
            else:
                pv = jnp.dot(v_aug[:, :hf], jnp.exp2(parts[0] - m_new).astype(BF16), preferred_element_type=F32)
                pv_r = jnp.dot(v_aug[:, hf:], jnp.exp2(parts[1] - m_new[:, hf:]).astype(BF16),
                               preferred_element_type=F32)
                pv = jnp.concatenate([pv[:, :hf], pv[:, hf:] + pv_r], axis=1)
            acc = pv if kj == 0 else jnp.exp2(state[hh][0] - m_new) * state[hh][1] + pv
            new_state.append((m_new, acc))
        state = new_state
        if kj == qi:
            for hh in range(ATTN_HEADS):
                acc = state[hh][1]
                o = acc[:FOX_HEAD_DIM] / acc[FOX_HEAD_DIM:FOX_HEAD_DIM + 1]
                ot_ref[0, hh * FOX_HEAD_DIM:(hh + 1) * FOX_HEAD_DIM, qi * t:(qi + 1) * t] = o.astype(BF16)


def _fox_attn(qt, kx, vt, B, S):
    G = ATTN_HEADS
    return pl.pallas_call(
        _fox_attn_kernel,
        grid=(B, FOX_HEADS // G),
        in_specs=[
            pl.BlockSpec((1, G * HEAD_PAD, S), lambda b, h: (b, h, 0)),
            pl.BlockSpec((S, G * HEAD_PAD), lambda b, h: (b, h)),
            pl.BlockSpec((1, G * FOX_HEAD_DIM, S), lambda b, h: (b, h, 0)),
        ],
        out_specs=pl.BlockSpec((1, G * FOX_HEAD_DIM, S), lambda b, h: (b, h, 0)),
        out_shape=jax.ShapeDtypeStruct((B, FOX_WIDTH, S), BF16),
        compiler_params=pltpu.CompilerParams(
            dimension_semantics=("parallel", "parallel"), vmem_limit_bytes=VMEM_LIMIT),
        name="fox_attn",
    )(qt, kx, vt)


def _out_router_kernel(h_ref, ya_ref, ybt_ref, woa_ref, wob_ref, gb_ref, gmoe_ref, wrt_ref, br_ref, tri_ref,
                       h1_ref, xp_ref, tope_ref, rank_ref, gates_ref, counts_ref, carry_ref):
    tm = h_ref.shape[0]

    @pl.when(pl.program_id(0) == 0)
    def _():
        carry_ref[...] = jnp.zeros_like(carry_ref)

    yb = ybt_ref[0].astype(F32)
    ybn = (yb * lax.rsqrt(jnp.mean(yb * yb, axis=0, keepdims=True) + RMS_EPS) * gb_ref[...]).astype(BF16)
    y = jnp.dot(ya_ref[...], woa_ref[...], preferred_element_type=F32)
    y = y + lax.dot_general(ybn, wob_ref[...], _TN, preferred_element_type=F32)
    h1 = h_ref[...] + y
    h1_ref[...] = h1
    xn = _rms(h1, gmoe_ref[...])
    _store_planes(xp_ref, _pack_rows(xn))

    logits = lax.dot_general(wrt_ref[...], xn.astype(BF16), _NT, preferred_element_type=F32) + br_ref[...]
    ri = lax.broadcasted_iota(I32, (N_EXPERTS, tm), 0)
    vals, idxs = [], []
    l = logits
    for _ in range(TOP_K):
        m = jnp.max(l, axis=0, keepdims=True)
        idx = jnp.min(jnp.where(l == m, ri, N_EXPERTS), axis=0, keepdims=True)
        vals.append(m)
        idxs.append(idx)
        l = jnp.where(ri == idx, -jnp.inf, l)
    es = [jnp.exp(vk - vals[0]) for vk in vals]
    den = es[0] + es[1] + es[2] + es[3]
    zrow = jnp.zeros((8 - TOP_K, tm), F32)
    gates_ref[...] = jnp.concatenate([e / den for e in es] + [zrow], axis=0).T

    sel = jnp.zeros((N_EXPERTS, tm), F32)
    for idx in idxs:
        sel = sel + jnp.where(ri == idx, 1.0, 0.0)
    cnt = jnp.dot(sel.astype(BF16), tri_ref[...], preferred_element_type=F32) + carry_ref[:, 0:1]
    ranks = [jnp.sum(jnp.where(ri == idx, cnt, 0.0), axis=0, keepdims=True) for idx in idxs]
    zi = jnp.zeros((8 - TOP_K, tm), I32)
    rank_ref[...] = jnp.concatenate([r.astype(I32) for r in ranks] + [zi], axis=0)
    tope_ref[...] = jnp.concatenate(idxs + [zi], axis=0)
    total = carry_ref[...] + jnp.sum(sel, axis=1, keepdims=True)
    carry_ref[...] = total
    counts_ref[...] = total.astype(I32)


def _out_router(group, h, ya, ybt, B, S, prm, tri_strict):
    T, D = h.shape
    Tg = T // MOE_GROUPS
    tm = min(ROW_TILE, S)
    ns = S // tm
    E = N_EXPERTS
    planes = (D // 2) // SC_ROW_WORDS
    off = group * (Tg // tm)
    const = lambda shape: pl.BlockSpec(shape, lambda i: (0,) * len(shape))
    return pl.pallas_call(
        _out_router_kernel,
        grid=(Tg // tm,),
        in_specs=[
            pl.BlockSpec((tm, D), lambda i: (off + i, 0)),
            pl.BlockSpec((tm, GMLP_WIDTH), lambda i: (i, 0)),
            pl.BlockSpec((1, FOX_WIDTH, tm), lambda i: (i // ns, 0, i % ns)),
            const((GMLP_WIDTH, D)),
            const((FOX_WIDTH, D)),
            const((FOX_WIDTH, 1)),
            const((1, D)),
            const((E, D)),
            const((E, 1)),
            const((tm, tm)),
        ],
        out_specs=[
            pl.BlockSpec((tm, D), lambda i: (i, 0)),
            pl.BlockSpec((planes, tm, SC_ROW_WORDS), lambda i: (0, i, 0)),
            pl.BlockSpec((8, tm), lambda i: (0, i)),
            pl.BlockSpec((8, tm), lambda i: (0, i)),
            pl.BlockSpec((tm, 8), lambda i: (i, 0)),
            const((E, 128)),
        ],
        out_shape=[
            jax.ShapeDtypeStruct((Tg, D), F32),
            jax.ShapeDtypeStruct((planes, Tg, SC_ROW_WORDS), U32),
            jax.ShapeDtypeStruct((8, Tg), I32),
            jax.ShapeDtypeStruct((8, Tg), I32),
            jax.ShapeDtypeStruct((Tg, 8), F32),
            jax.ShapeDtypeStruct((E, 128), I32),
        ],
        scratch_shapes=[pltpu.VMEM((E, 128), F32)],
        compiler_params=pltpu.CompilerParams(
            dimension_semantics=("arbitrary",), vmem_limit_bytes=VMEM_LIMIT),
        name="out_router",
    )(h, ya, ybt, prm["woa"], prm["wob"], prm["g_b"], prm["g_moe"], prm["wrt"], prm["b_router"], tri_strict)


def _dest_kernel(ps_ref, tope_ref, rank_ref, dest_ref):
    e_sel = tope_ref[...]
    d = rank_ref[...]
    for e in range(N_EXPERTS):
        d = d + jnp.where(e_sel == e, ps_ref[e], 0)
    dest_ref[...] = d


def _dest_rows(pad_starts, tope, rank):
    T = tope.shape[1]
    tc = min(4096, T)
    return pl.pallas_call(
        _dest_kernel,
        grid_spec=pltpu.PrefetchScalarGridSpec(
            num_scalar_prefetch=1,
            grid=(T // tc,),
            in_specs=[pl.BlockSpec((8, tc), lambda i, ps: (0, i)),
                      pl.BlockSpec((8, tc), lambda i, ps: (0, i))],
            out_specs=pl.BlockSpec((8, tc), lambda i, ps: (0, i)),
        ),
        out_shape=jax.ShapeDtypeStruct((8, T), I32),
        name="dest_rows",
    )(pad_starts, tope, rank)


def _sc_mesh():
    return plsc.VectorSubcoreMesh(core_axis_name="c", subcore_axis_name="s")


def _sc_scatter_rows(x, idx, n_out):
    R, W = x.shape
    N = idx.shape[0]
    nsrc = R // SC_WINDOW
    copies = N // R

    @functools.partial(pl.kernel, out_type=jax.ShapeDtypeStruct((n_out, W), x.dtype), mesh=_sc_mesh())
    def k(x_hbm, i_hbm, o_hbm):
        def body(x_vmem, *i_vmems):
            for i_vmem in i_vmems:
                pltpu.sync_copy(x_vmem, o_hbm.at[i_vmem.at[0]])

        idx_spec = lambda c: pl.BlockSpec((1, SC_WINDOW), lambda i: (0, c * nsrc + i))
        pltpu.emit_pipeline(
            body,
            grid=(nsrc,),
            in_specs=[pl.BlockSpec((SC_WINDOW, W), lambda i: (i, 0))] + [idx_spec(c) for c in range(copies)],
            out_specs=[],
            core_axis_name=("c", "s"),
            dimension_semantics=(pltpu.PARALLEL,),
        )(x_hbm, *([i_hbm] * copies))

    return k(x, idx.reshape(1, N))


def _sc_gather_rows(table, idx):
    V, W = table.shape
    N = idx.shape[0]

    @functools.partial(pl.kernel, out_type=jax.ShapeDtypeStruct((N, W), table.dtype), mesh=_sc_mesh())
    def k(t_hbm, i_hbm, o_hbm):
        def body(i_vmem, o_vmem):
            pltpu.sync_copy(t_hbm.at[i_vmem.at[0]], o_vmem)

        pltpu.emit_pipeline(
            body,
            grid=(N // SC_WINDOW,),
            in_specs=[pl.BlockSpec((1, SC_WINDOW), lambda i: (0, i))],
            out_specs=[pl.BlockSpec((SC_WINDOW, W), lambda i: (i, 0))],
            core_axis_name=("c", "s"),
            dimension_semantics=(pltpu.PARALLEL,),
        )(i_hbm, o_hbm)

    return k(table, idx.reshape(1, N))


def _expert_kernel(be_ref, nv_ref, fs_ref, nxt_ref, uni_ref, xs_ref, wgu_hbm, bgu_ref, wdn_hbm, bdn_ref, ys_ref,
                   wgu_f32, wdn_f32, cur_slot, sem, *, layer):
    F = wdn_f32.shape[1]

    def weight_copies(e, slot):
        return (pltpu.make_async_copy(wgu_hbm.at[layer, e], wgu_f32.at[slot], sem.at[slot, 0]),
                pltpu.make_async_copy(wdn_hbm.at[layer, e], wdn_f32.at[slot], sem.at[slot, 1]))

    @pl.when(pl.program_id(0) == 0)
    def _():
        for cp in weight_copies(be_ref[0], 0):
            cp.start()

    def ffn(x, e):
        slot = cur_slot[0]
        hgu = jnp.dot(x, wgu_f32[slot], preferred_element_type=F32) + bgu_ref[e]
        g = jnp.minimum(hgu[:, :F], SWIGLU_LIMIT)
        lin = jnp.clip(hgu[:, F:], -SWIGLU_LIMIT, SWIGLU_LIMIT)
        act = g * jax.nn.sigmoid(SWIGLU_ALPHA * g) * (lin + 1.0)
        return _pack_rows(jnp.dot(act, wdn_f32[slot], preferred_element_type=F32) + bdn_ref[e])

    def load_rows(rs):
        return _unpack_rows(jnp.concatenate([xs_ref[pn, rs, :] for pn in range(xs_ref.shape[0])], axis=1))

    def store_rows(rs, y):
        for pn in range(ys_ref.shape[0]):
            ys_ref[pn, rs, :] = y[:, pn * SC_ROW_WORDS:(pn + 1) * SC_ROW_WORDS]

    for c in range(EXPERT_STEP_BLOCKS // EXPERT_CHAIN_BLOCKS):
        chain = pl.program_id(0) * (EXPERT_STEP_BLOCKS // EXPERT_CHAIN_BLOCKS) + c
        uniform = uni_ref[chain] > 0

        @pl.when(uniform)
        def _():
            rs = slice(c * EXPERT_CHAIN_BLOCKS * EXPERT_ROWS, (c + 1) * EXPERT_CHAIN_BLOCKS * EXPERT_ROWS)
            store_rows(rs, ffn(load_rows(rs), be_ref[chain * EXPERT_CHAIN_BLOCKS]))

        for j in range(EXPERT_CHAIN_BLOCKS):
            b = chain * EXPERT_CHAIN_BLOCKS + j
            r0 = (c * EXPERT_CHAIN_BLOCKS + j) * EXPERT_ROWS
            rs = slice(r0, r0 + EXPERT_ROWS)
            e = be_ref[b]
            nv = nv_ref[b]
            fs = fs_ref[b]

            @pl.when(fs >= 0)
            def _():
                for cp in weight_copies(e, fs):
                    cp.wait()
                cur_slot[0] = fs
                nxt = nxt_ref[b]

                @pl.when(nxt >= 0)
                def _():
                    for prio, cp in enumerate(weight_copies(nxt, 1 - fs)):
                        cp.start(priority=prio)

            @pl.when(jnp.logical_not(uniform) & (nv > 0))
            def _():
                x = load_rows(rs)
                rows = lax.broadcasted_iota(I32, x.shape, 0)
                store_rows(rs, ffn(jnp.where(rows < nv, x, 0.0), e))

            @pl.when(nv == 0)
            def _():
                ys_ref[:, rs, :] = jnp.zeros((ys_ref.shape[0], EXPERT_ROWS, SC_ROW_WORDS), U32)


def _experts(layer, plan, xs, w_gu, b_gu, w_dn, b_dn):
    planes, P, _ = xs.shape
    _, E, D, F2 = w_gu.shape
    F = F2 // 2
    step_rows = EXPERT_STEP_BLOCKS * EXPERT_ROWS
    by_step = lambda i, be, nv, fs, nxt, uni: (0, i, 0)
    layer_all = lambda i, be, nv, fs, nxt, uni: (layer, 0, 0, 0)
    return pl.pallas_call(
        functools.partial(_expert_kernel, layer=layer),
        grid_spec=pltpu.PrefetchScalarGridSpec(
            num_scalar_prefetch=5,
            grid=(P // step_rows,),
            in_specs=[
                pl.BlockSpec((planes, step_rows, SC_ROW_WORDS), by_step),
                pl.BlockSpec(memory_space=pl.ANY),
                pl.BlockSpec((None, E, 1, F2), layer_all),
                pl.BlockSpec(memory_space=pl.ANY),
                pl.BlockSpec((None, E, 1, D), layer_all),
            ],
            out_specs=pl.BlockSpec((planes, step_rows, SC_ROW_WORDS), by_step),
            scratch_shapes=[
                pltpu.VMEM((2, D, F2), F32), pltpu.VMEM((2, F, D), F32),
                pltpu.SMEM((1,), I32),
                pltpu.SemaphoreType.DMA((2, 2)),
            ],
        ),
        out_shape=jax.ShapeDtypeStruct((planes, P, SC_ROW_WORDS), U32),
        compiler_params=pltpu.CompilerParams(
            dimension_semantics=("arbitrary",), vmem_limit_bytes=VMEM_LIMIT),
        name="experts",
    )(plan["block_e"], plan["block_nv"], plan["first_slot"], plan["next_e"], plan["step_uniform"],
      xs, w_gu, b_gu.reshape(-1, E, 1, F2), w_dn, b_dn.reshape(-1, E, 1, D))


def _combine_ple_kernel(h1_ref, yg_ref, gates_ref, gple_ref, wpg_ref, p_ref, wple_ref, gfin_ref, *rest, final):
    o_ref = rest[-1]
    for r in range(h1_ref.shape[0] // COMBINE_ROWS):
        rs = slice(r * COMBINE_ROWS, (r + 1) * COMBINE_ROWS)
        h2 = h1_ref[rs, :]
        gates = gates_ref[rs, :]
        for k in range(TOP_K):
            yk = jnp.concatenate([yg_ref[k, pn, rs, :] for pn in range(yg_ref.shape[1])], axis=1)
            h2 = h2 + gates[:, k:k + 1] * _unpack_rows(yk)
        xn = _rms(h2, gple_ref[...]).astype(BF16)
        gate = jax.nn.sigmoid(jnp.dot(xn, wpg_ref[...], preferred_element_type=F32))
        pe = jnp.dot(p_ref[rs, :].astype(BF16), wple_ref[...], preferred_element_type=F32)
        h3 = h2 + gate * pe
        if final:
            h3 = _rms(h3, gfin_ref[...])
        o_ref[rs, :] = h3


def _combine_ple(layer, group, h1, yg, gates, p, prm, g_final, final, h_prev):
    Tg, D = h1.shape
    T = Tg * MOE_GROUPS
    tm = min(ROW_TILE, Tg)
    PD = p.shape[2]
    planes = yg.shape[1]
    off = group * (Tg // tm)
    const = lambda shape: pl.BlockSpec(shape, lambda i: (0,) * len(shape))
    in_specs = [
        pl.BlockSpec((tm, D), lambda i: (i, 0)),
        pl.BlockSpec((TOP_K, planes, tm, SC_ROW_WORDS), lambda i: (0, 0, i, 0)),
        pl.BlockSpec((tm, 8), lambda i: (i, 0)),
        const((1, D)),
        const((D, D)),
        pl.BlockSpec((None, tm, PD), lambda i: (layer, off + i, 0)),
        const((PD, D)),
        const((1, D)),
    ]
    args = [h1, yg, gates, prm["g_ple"], prm["wpg"], p, prm["wple"], g_final]
    aliases = {}
    if h_prev is not None:
        in_specs.append(pl.BlockSpec(memory_space=pl.ANY))
        args.append(h_prev)
        aliases = {len(args) - 1: 0}
    return pl.pallas_call(
        functools.partial(_combine_ple_kernel, final=final),
        grid=(Tg // tm,),
        in_specs=in_specs,
        out_specs=pl.BlockSpec((tm, D), lambda i: (off + i, 0)),
        out_shape=jax.ShapeDtypeStruct((T, D), F32),
        input_output_aliases=aliases,
        compiler_params=pltpu.CompilerParams(
            dimension_semantics=("parallel",), vmem_limit_bytes=VMEM_LIMIT),
        name="combine_ple",
    )(*args)


def _layer_params(i, w):
    D = w["w_in"].shape[1]
    H = FOX_HEADS
    w_in = w["w_in"][i]
    c1 = 2 * GMLP_WIDTH
    c2, c3, c4 = c1 + FOX_WIDTH, c1 + 2 * FOX_WIDTH, c1 + 3 * FOX_WIDTH
    wk = jnp.zeros((D, H, HEAD_PAD), F32).at[:, :, :FOX_HEAD_DIM].set(
        w_in[:, c2:c3].reshape(D, H, FOX_HEAD_DIM)).reshape(D, H * HEAD_PAD)
    return {
        "g_mix": w["g_mix"][i].reshape(1, D),
        "wuv": w_in[:, :c1].astype(BF16),
        "wqt": (w_in[:, c1:c2] * (FOX_HEAD_DIM ** -0.5 * LOG2E)).T.astype(BF16),
        "wk": wk.astype(BF16),
        "wvt": w_in[:, c3:c4].T.astype(BF16),
        "wft": w_in[:, c4:].T.astype(BF16),
        "b_f": w["b_f"][i].reshape(H, 1),
        "ln_g": w["ln_g"][i].reshape(1, GMLP_WIDTH),
        "ln_b": w["ln_b"][i].reshape(1, GMLP_WIDTH),
        "w_s": w["w_s"][i],
        "b_s": w["b_s"][i].reshape(GMLP_GROUPS, GMLP_BLOCK, 1),
        "g_a": w["g_a"][i].reshape(1, GMLP_WIDTH),
        "woa": w["w_out"][i][:GMLP_WIDTH].astype(BF16),
        "wob": w["w_out"][i][GMLP_WIDTH:].astype(BF16),
        "g_b": w["g_b"][i].reshape(FOX_WIDTH, 1),
        "g_moe": w["g_moe"][i].reshape(1, D),
        "wrt": w["w_router"][i].T.astype(BF16),
        "b_router": w["b_router"][i].reshape(N_EXPERTS, 1),
        "g_ple": w["g_ple"][i].reshape(1, D),
        "wpg": w["w_ple_gate"][i].astype(BF16),
        "wple": w["w_ple"][i].astype(BF16),
    }


def _bias_constants(tm):
    H = FOX_HEADS
    j = jnp.arange(tm)
    tri_incl = (j[:, None] <= j[None, :]).astype(BF16)
    tri_strict = (j[:, None] < j[None, :]).astype(BF16)
    part = jnp.arange(BIAS_PARTS)[:, None]
    head = jnp.arange(H)[None, :]
    rows = (part * H + head).reshape(-1)
    cols = (head * HEAD_PAD + FOX_HEAD_DIM + BIAS_PARTS + part).reshape(-1)
    selk = jnp.zeros((32, H * HEAD_PAD), F32).at[rows, cols].set(-1.0).astype(BF16)
    one_cols = (jnp.arange(H)[:, None] * HEAD_PAD + FOX_HEAD_DIM + jnp.arange(BIAS_PARTS)[None, :]).reshape(-1)
    onesk = jnp.zeros((1, H * HEAD_PAD), F32).at[0, one_cols].set(1.0)
    return tri_incl, tri_strict, selk, onesk


def _moe_dispatch_plan(counts, n_blocks):
    padded = (counts + EXPERT_ROWS - 1) // EXPERT_ROWS * EXPERT_ROWS
    pad_ends = jnp.cumsum(padded)
    pad_starts = (pad_ends - padded).astype(I32)
    block_start = jnp.arange(n_blocks, dtype=I32) * EXPERT_ROWS
    n_before = jnp.sum((pad_ends[None, :] <= block_start[:, None]).astype(I32), axis=1)
    block_e = jnp.minimum(n_before, N_EXPERTS - 1).astype(I32)
    own = block_e[:, None] == jnp.arange(N_EXPERTS, dtype=I32)[None, :]
    seg_end = jnp.sum(jnp.where(own, (pad_starts + counts)[None, :], 0), axis=1)
    block_nv = jnp.clip(seg_end - block_start, 0, EXPERT_ROWS)
    block_nv = jnp.where(block_start < pad_ends[-1], block_nv, 0).astype(I32)
    prev_e = jnp.concatenate([jnp.full((1,), -1, I32), block_e[:-1]])
    first = (block_nv > 0) & (block_e != prev_e)
    run = jnp.cumsum(first.astype(I32)) - 1
    first_slot = jnp.where(first, run % 2, -1).astype(I32)
    ar = jnp.arange(N_EXPERTS, dtype=I32)
    later = (ar[None, :] > ar[:, None]) & (counts > 0)[None, :]
    next_of = jnp.min(jnp.where(later, ar[None, :], N_EXPERTS), axis=1)
    next_of = jnp.where(next_of < N_EXPERTS, next_of, -1)
    next_e = jnp.sum(jnp.where(own, next_of[None, :], 0), axis=1).astype(I32)
    plain = ((block_nv == EXPERT_ROWS) & jnp.logical_not(first)).reshape(-1, EXPERT_CHAIN_BLOCKS)
    step_uniform = jnp.all(plain, axis=1).astype(I32)
    return dict(pad_starts=pad_starts, block_e=block_e, block_nv=block_nv, first_slot=first_slot, next_e=next_e,
                step_uniform=step_uniform)


def kernel(x, p, g_mix, w_in, ln_g, ln_b, w_s, b_s, b_f, g_a, g_b, w_out, g_moe, w_router, b_router,
           w_gu, b_gu, w_dn, b_dn, g_ple, w_ple_gate, w_ple, g_final):
    B, S, D = x.shape
    T = B * S
    depth = w_in.shape[0]
    assert B % MOE_GROUPS == 0 and S % min(ROW_TILE, S) == 0 and S % min(ATTN_TILE, S) == 0
    assert (T // MOE_GROUPS) % SC_WINDOW == 0 and D % (2 * SC_ROW_WORDS) == 0
    w = dict(g_mix=g_mix, w_in=w_in, ln_g=ln_g, ln_b=ln_b, w_s=w_s, b_s=b_s, b_f=b_f, g_a=g_a, g_b=g_b,
             w_out=w_out, g_moe=g_moe, w_router=w_router, b_router=b_router,
             g_ple=g_ple, w_ple_gate=w_ple_gate, w_ple=w_ple)
    tri_incl, tri_strict, selk, onesk = _bias_constants(min(ROW_TILE, S))
    G = MOE_GROUPS
    Tg = T // G
    n_blocks = -(-(Tg * TOP_K) // EXPERT_ROWS) + N_EXPERTS
    n_blocks = -(-n_blocks // EXPERT_STEP_BLOCKS) * EXPERT_STEP_BLOCKS
    P = n_blocks * EXPERT_ROWS
    planes = (D // 2) // SC_ROW_WORDS
    plane_base = (jnp.arange(planes, dtype=I32) * P)[None, :, None]
    p_rows = p.reshape(depth, T, p.shape[-1])

    h = x.reshape(T, D)
    for i in range(depth):
        prm = _layer_params(i, w)
        prm.update(tri_incl=tri_incl, selk=selk, onesk=onesk)
        routed = []
        idx = jnp.zeros((8, 128), I32)
        for g in range(G):
            ya, qt, kx, vt = _mix_in(g, h, B, S, prm, idx)
            ybt = _fox_attn(qt, kx, vt, B // G, S)
            h1, xp, tope, rank, gates, counts = _out_router(g, h, ya, ybt, B, S, prm, tri_strict)
            plan = _moe_dispatch_plan(counts[:, 0], n_blocks)
            dest = _dest_rows(plan["pad_starts"], tope, rank)[:TOP_K]
            idx = (dest[:, None, :] + plane_base).reshape(-1)
            xs = _sc_scatter_rows(xp.reshape(planes * Tg, SC_ROW_WORDS), idx, planes * P)
            routed.append((h1, gates, plan, idx, xs))
        h_out = None
        for g, (h1, gates, plan, idx, xs) in enumerate(routed):
            ys = _experts(i, plan, xs.reshape(planes, P, SC_ROW_WORDS), w_gu, b_gu, w_dn, b_dn)
            yg = _sc_gather_rows(ys.reshape(planes * P, SC_ROW_WORDS), idx)
            h_out = _combine_ple(i, g, h1, yg.reshape(TOP_K, planes, Tg, SC_ROW_WORDS), gates, p_rows, prm,
                                 g_final.reshape(1, D), i == depth - 1, h_out)
        h = h_out
    return h.reshape(B, S, D)
```
